```python
import math
import jax, jax.numpy as jnp
from jax import lax
import numpy as np

D_MODEL = 1024
BATCH = 4
SEQ = 4096
DEPTH = 2
DEC_BATCH = 32
DEC_SEQ = 32
PAST_LEN = 1024

CHUNK = 64
N_MIXERS = 2
N_SSM_LAYERS = (DEPTH + 1) // 2
N_RET_LAYERS = DEPTH // 2
SSM_GROUP = 16
SSM_GROUPS = D_MODEL // SSM_GROUP
SSM_STATE = 64
SSM_DT_MIN = 1e-3
SSM_DT_MAX = 1e-1
RET_HEADS = 4
RET_DK = D_MODEL // RET_HEADS
RET_DV = 2 * D_MODEL // RET_HEADS
RET_QK = RET_HEADS * RET_DK
RET_V = RET_HEADS * RET_DV
ROPE_BASE = 10000.0
N_MEM = 256
MEM_HEADS = 4
MEM_HD = D_MODEL // MEM_HEADS
D_FF = 2816
CONV_W = 3
EPS = 1e-6
GN_EPS = 1e-5

kernel_name = 'hybrid_s5_retention_stream_step'

F32 = jnp.float32


def rms_norm(x, g):
    xf = x.astype(F32)
    y = xf * lax.rsqrt(jnp.mean(xf * xf, axis=-1, keepdims=True) + EPS)
    return (y * g.astype(F32)).astype(x.dtype)


def to_chunks(a, c):
    b, l = a.shape[0], a.shape[1]
    return jnp.moveaxis(a.reshape((b, l // c, c) + a.shape[2:]), 1, 0)


def from_chunks(a):
    a = jnp.moveaxis(a, 0, 1)
    return a.reshape((a.shape[0], a.shape[1] * a.shape[2]) + a.shape[3:])


def s5_discretise(a_re, a_im, log_dt, b_re, b_im, c_re, c_im):
    lam = lax.complex(a_re.astype(F32), a_im.astype(F32))
    dt = jnp.exp(log_dt.astype(F32))[:, None]
    a_bar = jnp.exp(lam * dt)
    b = lax.complex(b_re.astype(F32), b_im.astype(F32))
    b_bar = ((a_bar - 1.0) / lam)[..., None] * b
    c = lax.complex(c_re.astype(F32), c_im.astype(F32))
    return a_bar, b_bar, c


def _linear_combine(e1, e2):
    a1, b1 = e1
    a2, b2 = e2
    return a2 * a1, a2 * b1 + b2


def s5_block(state, u, a_bar, b_bar, c):
    bu = jnp.einsum('gpc,blgc->blgp', b_bar, u.astype(jnp.complex64))
    bu = bu.at[:, 0].add(a_bar * state)
    a = jnp.broadcast_to(a_bar, bu.shape)
    _, xs = lax.associative_scan(_linear_combine, (a, bu), axis=1)
    y = jnp.real(jnp.einsum('gcp,blgp->blgc', c, xs))
    return xs[:, -1], y


def s5_mixer(h, st_re, st_im, chunk, a_re, a_im, log_dt, b_re, b_im, c_re, c_im, d, w_glu, b_glu):
    bsz, l, _ = h.shape
    a_bar, b_bar, c = s5_discretise(a_re, a_im, log_dt, b_re, b_im, c_re, c_im)
    hf = h.astype(F32)
    u = hf.reshape(bsz, l, SSM_GROUPS, SSM_GROUP)
    state0 = lax.complex(st_re.astype(F32), st_im.astype(F32))
    step = lambda s, uc: s5_block(s, uc, a_bar, b_bar, c)
    state, y = lax.scan(step, state0, to_chunks(u, chunk))
    y = from_chunks(y).reshape(bsz, l, D_MODEL) + d.astype(F32) * hf
    g = jax.nn.gelu(y)
    out = g * jax.nn.sigmoid(g @ w_glu.astype(F32) + b_glu.astype(F32))
    return out.astype(h.dtype), jnp.real(state), jnp.imag(state)


def rotary(x, pos):
    half = x.shape[-1] // 2
    freqs = ROPE_BASE ** (-jnp.arange(half, dtype=F32) / half)
    ang = pos.astype(F32)[:, None] * freqs[None, :]
    cos = jnp.cos(ang)[None, :, None, :]
    sin = jnp.sin(ang)[None, :, None, :]
    x1, x2 = x[..., :half], x[..., half:]
    return jnp.concatenate([x1 * cos - x2 * sin, x2 * cos + x1 * sin], axis=-1)


def ret_log_decay():
    return jnp.log(1.0 - 2.0 ** (-5.0 - jnp.arange(RET_HEADS, dtype=F32)))


def ret_block(s, qkv, log_g):
    q, k, v = qkv
    c = q.shape[1]
    idx = jnp.arange(c, dtype=F32)
    diff = idx[:, None] - idx[None, :]
    decay = jnp.where(diff >= 0, jnp.exp(jnp.maximum(diff, 0.0)[None] * log_g[:, None, None]), 0.0)
    scores = jnp.einsum('blhd,bmhd->bhlm', q, k) * decay[None]
    intra = jnp.einsum('bhlm,bmhe->blhe', scores, v)
    inner = jnp.exp((idx[:, None] + 1.0) * log_g[None, :])
    cross = jnp.einsum('blhd,bhde->blhe', q, s) * inner[None, :, :, None]
    tail = jnp.exp((c - 1.0 - idx)[:, None] * log_g[None, :])
    s_new = jnp.exp(c * log_g)[None, :, None, None] * s + jnp.einsum('blhd,blhe->bhde', k * tail[None, :, :, None], v)
    return s_new, intra + cross


def retention_mixer(h, state, pos0, chunk, w_qkvg, w_o):
    bsz, l, _ = h.shape
    p = h.astype(F32) @ w_qkvg.astype(F32)
    q, k, v, g = jnp.split(p, [RET_QK, 2 * RET_QK, 2 * RET_QK + RET_V], axis=-1)
    pos = pos0 + jnp.arange(l)
    q = rotary(q.reshape(bsz, l, RET_HEADS, RET_DK), pos)
    k = rotary(k.reshape(bsz, l, RET_HEADS, RET_DK), pos) * (RET_DK ** -0.5)
    v = v.reshape(bsz, l, RET_HEADS, RET_DV)
    log_g = ret_log_decay()
    step = lambda s, qkv: ret_block(s, qkv, log_g)
    state, o = lax.scan(step, state.astype(F32), (to_chunks(q, chunk), to_chunks(k, chunk), to_chunks(v, chunk)))
    o = from_chunks(o)
    mu = jnp.mean(o, axis=-1, keepdims=True)
    var = jnp.mean(jnp.square(o - mu), axis=-1, keepdims=True)
    o = (o - mu) * lax.rsqrt(var + GN_EPS)
    y = jax.nn.silu(g) * o.reshape(bsz, l, RET_V)
    return (y @ w_o.astype(F32)).astype(h.dtype), state


def memory_kv(mem, g_mem, w_kv):
    b, n, _ = mem.shape
    m = rms_norm(mem, g_mem).astype(F32) @ w_kv.astype(F32)
    k, v = jnp.split(m, 2, axis=-1)
    return k.reshape(b, n, MEM_HEADS, MEM_HD), v.reshape(b, n, MEM_HEADS, MEM_HD)


def memory_attn(h, mk, mv, w_q, w_o):
    bsz, l, _ = h.shape
    q = (h.astype(F32) @ w_q.astype(F32)).reshape(bsz, l, MEM_HEADS, MEM_HD)
    s = jnp.einsum('blhd,bmhd->bhlm', q, mk.astype(F32)) * (MEM_HD ** -0.5)
    p = jax.nn.softmax(s, axis=-1)
    o = jnp.einsum('bhlm,bmhd->blhd', p, mv.astype(F32)).reshape(bsz, l, D_MODEL)
    return (o @ w_o.astype(F32)).astype(h.dtype)


def conv_ffn(h, conv_state, w_up, conv_w, conv_b, w_down):
    l = h.shape[1]
    u = h.astype(F32) @ w_up.astype(F32)
    a, g = jnp.split(u, 2, axis=-1)
    ext = jnp.concatenate([conv_state.astype(F32), a], axis=1)
    cw = conv_w.astype(F32)
    conv = cw[0] * ext[:, 0:l] + cw[1] * ext[:, 1:l + 1] + cw[2] * ext[:, 2:l + 2] + conv_b.astype(F32)
    y = (jax.nn.gelu(conv) * g) @ w_down.astype(F32)
    return y.astype(h.dtype), ext[:, l:]


def setup_inputs(seed: int = 0) -> dict:
    key = jax.random.key(seed)
    ks = iter(jax.random.split(key, 48))

    def nrm(shape, scale):
        return jax.random.normal(next(ks), shape, F32) * scale

    def gain(shape):
        return 1.0 + nrm(shape, 0.01)

    n_idx = jnp.arange(SSM_STATE, dtype=F32)
    return {
        'x_prompt': nrm((BATCH, SEQ, D_MODEL), 1.0),
        'x_sample': nrm((DEC_BATCH, DEC_SEQ, D_MODEL), 1.0),
        'mem_prompt': nrm((BATCH, N_MEM, D_MODEL), 1.0),
        'state_ssm_re': nrm((N_SSM_LAYERS, DEC_BATCH, SSM_GROUPS, SSM_STATE), 0.1),
        'state_ssm_im': nrm((N_SSM_LAYERS, DEC_BATCH, SSM_GROUPS, SSM_STATE), 0.1),
        'state_ret': nrm((N_RET_LAYERS, DEC_BATCH, RET_HEADS, RET_DK, RET_DV), 0.1),
        'cache_mem_k': nrm((DEPTH, DEC_BATCH, N_MEM, MEM_HEADS, MEM_HD), 1.0),
        'cache_mem_v': nrm((DEPTH, DEC_BATCH, N_MEM, MEM_HEADS, MEM_HD), 1.0),
        'cache_conv': nrm((DEPTH, DEC_BATCH, CONV_W - 1, D_FF), 1.0),
        'norm_mix': gain((DEPTH, D_MODEL)),
        'norm_mem_q': gain((DEPTH, D_MODEL)),
        'norm_mem_kv': gain((DEPTH, D_MODEL)),
        'norm_ffn': gain((DEPTH, D_MODEL)),
        'norm_final': gain((D_MODEL,)),
        'ssm_a_re': -0.5 + nrm((N_SSM_LAYERS, SSM_GROUPS, SSM_STATE), 0.01),
        'ssm_a_im': math.pi * n_idx + nrm((N_SSM_LAYERS, SSM_GROUPS, SSM_STATE), 0.01),
        'ssm_log_dt': jax.random.uniform(next(ks), (N_SSM_LAYERS, SSM_GROUPS), F32, math.log(SSM_DT_MIN), math.log(SSM_DT_MAX)),
        'ssm_b_re': nrm((N_SSM_LAYERS, SSM_GROUPS, SSM_STATE, SSM_GROUP), (2 * SSM_GROUP) ** -0.5),
        'ssm_b_im': nrm((N_SSM_LAYERS, SSM_GROUPS, SSM_STATE, SSM_GROUP), (2 * SSM_GROUP) ** -0.5),
        'ssm_c_re': nrm((N_SSM_LAYERS, SSM_GROUPS, SSM_GROUP, SSM_STATE), (2 * SSM_STATE) ** -0.5),
        'ssm_c_im': nrm((N_SSM_LAYERS, SSM_GROUPS, SSM_GROUP, SSM_STATE), (2 * SSM_STATE) ** -0.5),
        'ssm_d': nrm((N_SSM_LAYERS, D_MODEL), 1.0),
        'ssm_w_glu': nrm((N_SSM_LAYERS, D_MODEL, D_MODEL), D_MODEL ** -0.5),
        'ssm_b_glu': nrm((N_SSM_LAYERS, D_MODEL), 0.01),
        'ret_w_qkvg': nrm((N_RET_LAYERS, D_MODEL, 2 * RET_QK + 2 * RET_V), D_MODEL ** -0.5),
        'ret_w_o': nrm((N_RET_LAYERS, RET_V, D_MODEL), RET_V ** -0.5),
        'mem_w_q': nrm((DEPTH, D_MODEL, D_MODEL), D_MODEL ** -0.5),
        'mem_w_kv': nrm((DEPTH, D_MODEL, 2 * D_MODEL), D_MODEL ** -0.5),
        'mem_w_o': nrm((DEPTH, D_MODEL, D_MODEL), D_MODEL ** -0.5),
        'ffn_w_up': nrm((DEPTH, D_MODEL, 2 * D_FF), D_MODEL ** -0.5),
        'ffn_conv_w': nrm((DEPTH, CONV_W, D_FF), CONV_W ** -0.5),
        'ffn_conv_b': nrm((DEPTH, D_FF), 0.01),
        'ffn_w_down': nrm((DEPTH, D_FF, D_MODEL), D_FF ** -0.5),
    }


def reference(x_prompt, x_sample, mem_prompt, state_ssm_re, state_ssm_im, state_ret, cache_mem_k, cache_mem_v, cache_conv,
              norm_mix, norm_mem_q, norm_mem_kv, norm_ffn, norm_final,
              ssm_a_re, ssm_a_im, ssm_log_dt, ssm_b_re, ssm_b_im, ssm_c_re, ssm_c_im, ssm_d, ssm_w_glu, ssm_b_glu,
              ret_w_qkvg, ret_w_o,
              mem_w_q, mem_w_kv, mem_w_o,
              ffn_w_up, ffn_conv_w, ffn_conv_b, ffn_w_down):
    xp, xs = x_prompt, x_sample
    bp = xp.shape[0]
    ls = xs.shape[1]
    ssm_re_p, ssm_im_p, ssm_re_s, ssm_im_s = [], [], [], []
    ret_p, ret_s = [], []
    mk_p, mv_p = [], []
    conv_p, conv_s = [], []
    for i in range(DEPTH):
        j = i // N_MIXERS
        hp = rms_norm(xp, norm_mix[i])
        hs = rms_norm(xs, norm_mix[i])
        if i % N_MIXERS == 0:
            w = (ssm_a_re[j], ssm_a_im[j], ssm_log_dt[j], ssm_b_re[j], ssm_b_im[j], ssm_c_re[j], ssm_c_im[j],
                 ssm_d[j], ssm_w_glu[j], ssm_b_glu[j])
            zeros = jnp.zeros((bp, SSM_GROUPS, SSM_STATE), F32)
            yp, re_p, im_p = s5_mixer(hp, zeros, zeros, CHUNK, *w)
            ys, re_s, im_s = s5_mixer(hs, state_ssm_re[j], state_ssm_im[j], ls, *w)
            ssm_re_p.append(re_p)
            ssm_im_p.append(im_p)
            ssm_re_s.append(re_s)
            ssm_im_s.append(im_s)
        else:
            s0 = jnp.zeros((bp, RET_HEADS, RET_DK, RET_DV), F32)
            yp, sp = retention_mixer(hp, s0, 0, CHUNK, ret_w_qkvg[j], ret_w_o[j])
            ys, ss = retention_mixer(hs, state_ret[j], PAST_LEN, ls, ret_w_qkvg[j], ret_w_o[j])
            ret_p.append(sp)
            ret_s.append(ss)
        xp = xp + yp
        xs = xs + ys
        mkp, mvp = memory_kv(mem_prompt, norm_mem_kv[i], mem_w_kv[i])
        mk_p.append(mkp)
        mv_p.append(mvp)
        xp = xp + memory_attn(rms_norm(xp, norm_mem_q[i]), mkp, mvp, mem_w_q[i], mem_w_o[i])
        xs = xs + memory_attn(rms_norm(xs, norm_mem_q[i]), cache_mem_k[i], cache_mem_v[i], mem_w_q[i], mem_w_o[i])
        cz = jnp.zeros((bp, CONV_W - 1, D_FF), F32)
        fp, cp = conv_ffn(rms_norm(xp, norm_ffn[i]), cz, ffn_w_up[i], ffn_conv_w[i], ffn_conv_b[i], ffn_w_down[i])
        fs, cs = conv_ffn(rms_norm(xs, norm_ffn[i]), cache_conv[i], ffn_w_up[i], ffn_conv_w[i], ffn_conv_b[i], ffn_w_down[i])
        xp = xp + fp
        xs = xs + fs
        conv_p.append(cp)
        conv_s.append(cs)
    y_prompt = rms_norm(xp, norm_final)
    y_sample = rms_norm(xs, norm_final)
    return (y_prompt, y_sample,
            jnp.stack(ssm_re_p), jnp.stack(ssm_im_p), jnp.stack(ssm_re_s), jnp.stack(ssm_im_s),
            jnp.stack(ret_p), jnp.stack(ret_s),
            jnp.stack(mk_p), jnp.stack(mv_p),
            jnp.stack(conv_p), jnp.stack(conv_s))
```

```python
import functools
import math

import numpy as np
import jax
import jax.numpy as jnp
from jax import lax
from jax.experimental import pallas as pl
from jax.experimental.pallas import tpu as pltpu

F32 = jnp.float32
BF16 = jnp.bfloat16

D_MODEL = 1024
PAST_LEN = 1024
EPS = 1e-6
GN_EPS = 1e-5
ROPE_BASE = 10000.0
SSM_GROUP = 16
SSM_GROUPS = D_MODEL // SSM_GROUP
SSM_STATE = 64
SSM_FOLD = 16
LANES = 128
GROUPS_PER_TILE = LANES // SSM_GROUP
N_LANE_TILES = D_MODEL // LANES
RET_HEADS = 4
RET_DK = D_MODEL // RET_HEADS
RET_DV = 2 * D_MODEL // RET_HEADS
RET_QK = RET_HEADS * RET_DK
RET_V = RET_HEADS * RET_DV
RET_LOG_G = tuple(math.log(1.0 - 2.0 ** (-5.0 - h)) for h in range(RET_HEADS))
MEM_HEADS = 4
MEM_HD = D_MODEL // MEM_HEADS
VMEM_LIMIT = 56 * 1024 * 1024


def _cparams(*sem):
    return pltpu.CompilerParams(dimension_semantics=sem, vmem_limit_bytes=VMEM_LIMIT)


def _const_spec(shape):
    nd = len(shape)
    return pl.BlockSpec(shape, lambda *_: (0,) * nd, pipeline_mode=pl.Buffered(1))


def _rms_scale(x):
    return lax.rsqrt(jnp.mean(x * x, axis=-1, keepdims=True) + EPS)


def _split_bf16(x):
    hi = x.astype(BF16)
    lo = (x - hi.astype(F32)).astype(BF16)
    return hi, lo


def _dot(a, b):
    return jnp.dot(a, b, preferred_element_type=F32)


def _rms_matmul_kernel(x_ref, g_ref, w_ref, o_ref, *, tn):
    x = x_ref[...]
    h = (x * _rms_scale(x) * g_ref[...]).astype(BF16)
    for n0 in range(0, w_ref.shape[1], tn):
        o_ref[:, n0:n0 + tn] = _dot(h, w_ref[:, n0:n0 + tn]).astype(o_ref.dtype)


def rms_matmul(x, g, w, out_dtype, tm=512, tn=512):
    t, d = x.shape
    n = w.shape[1]
    tm = min(tm, t)
    return pl.pallas_call(
        functools.partial(_rms_matmul_kernel, tn=tn),
        grid=(t // tm,),
        in_specs=[pl.BlockSpec((tm, d), lambda i: (i, 0)),
                  _const_spec((1, d)),
                  _const_spec((d, n))],
        out_specs=pl.BlockSpec((tm, n), lambda i: (i, 0)),
        out_shape=jax.ShapeDtypeStruct((t, n), out_dtype),
        compiler_params=_cparams("parallel"),
        name="rms_matmul",
    )(x, g.reshape(1, d), w)


def _matmul_res_kernel(y_ref, w_ref, x_ref, o_ref):
    o_ref[...] = x_ref[...] + _dot(y_ref[...], w_ref[...])


def matmul_res(y, w, x, tm=512):
    t, k = y.shape
    d = w.shape[1]
    tm = min(tm, t)
    return pl.pallas_call(
        _matmul_res_kernel,
        grid=(t // tm,),
        in_specs=[pl.BlockSpec((tm, k), lambda i: (i, 0)),
                  _const_spec((k, d)),
                  pl.BlockSpec((tm, d), lambda i: (i, 0))],
        out_specs=pl.BlockSpec((tm, d), lambda i: (i, 0)),
        out_shape=jax.ShapeDtypeStruct((t, d), F32),
        compiler_params=_cparams("parallel"),
        name="matmul_res",
    )(y, w, x)


def _attn_kernel(q_ref, k_ref, v_ref, o_ref, *, nseg, seg):
    scale = MEM_HD ** -0.5
    for s in range(nseg):
        rows = slice(s * seg, (s + 1) * seg)
        for h in range(MEM_HEADS):
            cols = slice(h * MEM_HD, (h + 1) * MEM_HD)
            qh = q_ref[rows, cols]
            kh = k_ref[s, :, cols].astype(BF16)
            vh = v_ref[s, :, cols].astype(BF16)
            sc = lax.dot_general(qh, kh, (((1,), (1,)), ((), ())),
                                 preferred_element_type=F32) * scale
            p = jnp.exp(sc - jnp.max(sc, axis=-1, keepdims=True))
            denom = jnp.sum(p, axis=-1, keepdims=True)
            o = _dot(p.astype(BF16), vh) / denom
            o_ref[rows, cols] = o.astype(o_ref.dtype)


def attn_core(q, mk, mv, seq_len, nseg, seg):
    t, d = q.shape
    b, n_mem, _ = mk.shape
    tiles_per_stream = max(seq_len // (nseg * seg), 1)
    n_outer = b // nseg
    tm = nseg * seg
    return pl.pallas_call(
        functools.partial(_attn_kernel, nseg=nseg, seg=seg),
        grid=(n_outer, tiles_per_stream),
        in_specs=[pl.BlockSpec((tm, d), lambda o, i: (o * tiles_per_stream + i, 0)),
                  pl.BlockSpec((nseg, n_mem, d), lambda o, i: (o, 0, 0)),
                  pl.BlockSpec((nseg, n_mem, d), lambda o, i: (o, 0, 0))],
        out_specs=pl.BlockSpec((tm, d), lambda o, i: (o * tiles_per_stream + i, 0)),
        out_shape=jax.ShapeDtypeStruct((t, d), BF16),
        compiler_params=_cparams("parallel", "parallel"),
        name="attn_core",
    )(q, mk, mv)


def _ffn_kernel(x_ref, g_ref, wup_ref, cw_ref, cb_ref, wdn_ref, st_ref, *rest,
                nseg, seg, fc, final_norm):
    if final_norm:
        gf_ref, o_ref, nst_ref, carry_ref = rest
    else:
        o_ref, nst_ref, carry_ref = rest
    d_ff = wdn_ref.shape[0]
    tm = nseg * seg

    @pl.when(pl.program_id(1) == 0)
    def _():
        carry_ref[...] = st_ref[...]

    x = x_ref[...]
    h = (x * _rms_scale(x) * g_ref[...]).astype(BF16)
    row = lax.broadcasted_iota(jnp.int32, (tm, 1), 0) % seg
    acc = jnp.zeros((tm, x.shape[1]), F32)
    for f0 in range(0, d_ff, fc):
        cols = slice(f0, f0 + fc)
        a = _dot(h, wup_ref[:, cols])
        gate = _dot(h, wup_ref[:, d_ff + f0:d_ff + f0 + fc])
        p0 = jnp.broadcast_to(carry_ref[:, 0:1, cols], (nseg, seg, fc)).reshape(tm, fc)
        p1 = jnp.broadcast_to(carry_ref[:, 1:2, cols], (nseg, seg, fc)).reshape(tm, fc)
        a1 = jnp.where(row == 0, p1, pltpu.roll(a, 1, 0))
        a2 = jnp.where(row == 0, p0, jnp.where(row == 1, p1, pltpu.roll(a, 2, 0)))
        conv = (cw_ref[0:1, cols] * a2 + cw_ref[1:2, cols] * a1 + cw_ref[2:3, cols] * a
                + cb_ref[:, cols])
        p = (jax.nn.gelu(conv) * gate).astype(BF16)
        acc = acc + _dot(p, wdn_ref[cols, :])
        carry_ref[:, :, cols] = a.reshape(nseg, seg, fc)[:, seg - 2:seg, :]
    nst_ref[...] = carry_ref[...]
    y = x + acc
    if final_norm:
        y = y * _rms_scale(y) * gf_ref[...]
    o_ref[...] = y


def conv_ffn(x, g, w_up, conv_w, conv_b, w_down, conv_state, seq_len, nseg, seg, g_final=None,
             fc=256):
    t, d = x.shape
    d_ff = w_down.shape[0]
    b = conv_state.shape[0]
    tiles_per_stream = max(seq_len // (nseg * seg), 1)
    n_outer = b // nseg
    tm = nseg * seg
    final_norm = g_final is not None
    in_specs = [pl.BlockSpec((tm, d), lambda o, i: (o * tiles_per_stream + i, 0)),
                _const_spec((1, d)),
                _const_spec((d, 2 * d_ff)),
                _const_spec((3, d_ff)),
                _const_spec((1, d_ff)),
                _const_spec((d_ff, d)),
                pl.BlockSpec((nseg, 2, d_ff), lambda o, i: (o, 0, 0))]
    args = [x, g.reshape(1, d), w_up, conv_w, conv_b.reshape(1, d_ff), w_down, conv_state]
    if final_norm:
        in_specs.append(_const_spec((1, d)))
        args.append(g_final.reshape(1, d))
    return pl.pallas_call(
        functools.partial(_ffn_kernel, nseg=nseg, seg=seg, fc=fc, final_norm=final_norm),
        grid=(n_outer, tiles_per_stream),
        in_specs=in_specs,
        out_specs=[pl.BlockSpec((tm, d), lambda o, i: (o * tiles_per_stream + i, 0)),
                   pl.BlockSpec((nseg, 2, d_ff), lambda o, i: (o, 0, 0))],
        out_shape=[jax.ShapeDtypeStruct((t, d), F32),
                   jax.ShapeDtypeStruct((b, 2, d_ff), F32)],
        scratch_shapes=[pltpu.VMEM((nseg, 2, d_ff), F32)],
        compiler_params=_cparams("parallel", "arbitrary"),
        name="conv_ffn",
    )(*args)


def _rotate(x, cos, sin):
    half = x.shape[1] // 2
    x1, x2 = x[:, :half], x[:, half:]
    return jnp.concatenate([x1 * cos - x2 * sin, x2 * cos + x1 * sin], axis=1)


def _ret_kernel(p_ref, cos_ref, sin_ref, s_in_ref, y_ref, s_out_ref, *, chunk):
    @pl.when(pl.program_id(1) == 0)
    def _():
        s_out_ref[...] = s_in_ref[...]

    cos = cos_ref[...]
    sin = sin_ref[...]
    li = lax.broadcasted_iota(jnp.int32, (chunk, chunk), 0)
    mi = lax.broadcasted_iota(jnp.int32, (chunk, chunk), 1)
    diff = (li - mi).astype(F32)
    pos = lax.broadcasted_iota(jnp.int32, (chunk, 1), 0).astype(F32)
    for h in range(RET_HEADS):
        log_g = RET_LOG_G[h]
        q = p_ref[:, h * RET_DK:(h + 1) * RET_DK].astype(F32)
        k = p_ref[:, RET_QK + h * RET_DK:RET_QK + (h + 1) * RET_DK].astype(F32)
        v = p_ref[:, 2 * RET_QK + h * RET_DV:2 * RET_QK + (h + 1) * RET_DV]
        g = p_ref[:, 2 * RET_QK + RET_V + h * RET_DV:
                  2 * RET_QK + RET_V + (h + 1) * RET_DV].astype(F32)
        qr = _rotate(q, cos, sin).astype(BF16)
        kr = _rotate(k, cos, sin) * (RET_DK ** -0.5)
        decay = jnp.where(diff >= 0, jnp.exp(jnp.maximum(diff, 0.0) * log_g), 0.0)
        scores = lax.dot_general(qr, kr.astype(BF16), (((1,), (1,)), ((), ())),
                                 preferred_element_type=F32) * decay
        intra = _dot(scores.astype(BF16), v)
        s_old = s_out_ref[0, h]
        cross = _dot(qr, s_old.astype(BF16)) * jnp.exp((pos + 1.0) * log_g)
        o = intra + cross
        k_tail = (kr * jnp.exp((chunk - 1.0 - pos) * log_g)).astype(BF16)
        s_out_ref[0, h] = math.exp(chunk * log_g) * s_old + lax.dot_general(
            k_tail, v, (((0,), (0,)), ((), ())), preferred_element_type=F32)
        mu = jnp.mean(o, axis=-1, keepdims=True)
        oc = o - mu
        var = jnp.mean(oc * oc, axis=-1, keepdims=True)
        on = oc * lax.rsqrt(var + GN_EPS)
        y_ref[:, h * RET_DV:(h + 1) * RET_DV] = (jax.nn.silu(g) * on).astype(y_ref.dtype)


def ret_core(p, cos, sin, state, seq_len, chunk):
    t, n = p.shape
    b = state.shape[0]
    nchunks = seq_len // chunk
    half = RET_DK // 2
    return pl.pallas_call(
        functools.partial(_ret_kernel, chunk=chunk),
        grid=(b, nchunks),
        in_specs=[pl.BlockSpec((chunk, n), lambda o, i: (o * nchunks + i, 0)),
                  pl.BlockSpec((chunk, half), lambda o, i: (i, 0)),
                  pl.BlockSpec((chunk, half), lambda o, i: (i, 0)),
                  pl.BlockSpec((1, RET_HEADS, RET_DK, RET_DV), lambda o, i: (o, 0, 0, 0))],
        out_specs=[pl.BlockSpec((chunk, RET_V), lambda o, i: (o * nchunks + i, 0)),
                   pl.BlockSpec((1, RET_HEADS, RET_DK, RET_DV), lambda o, i: (o, 0, 0, 0))],
        out_shape=[jax.ShapeDtypeStruct((t, RET_V), BF16),
                   jax.ShapeDtypeStruct(state.shape, F32)],
        compiler_params=_cparams("parallel", "arbitrary"),
        name="ret_core",
    )(p, cos, sin, state)


def _rope_tables(pos0, length):
    half = RET_DK // 2
    freqs = ROPE_BASE ** (-np.arange(half, dtype=np.float64) / half)
    ang = (pos0 + np.arange(length, dtype=np.float64))[:, None] * freqs[None, :]
    return jnp.asarray(np.cos(ang), F32), jnp.asarray(np.sin(ang), F32)


def _cmul(ar, ai, br, bi):
    return ar * br - ai * bi, ar * bi + ai * br


def _a_bar(a_re, a_im, dt):
    mag = jnp.exp(a_re * dt)
    return mag * jnp.cos(a_im * dt), mag * jnp.sin(a_im * dt)


def _s5_prep_kernel(ab_re_ref, ab_im_ref, ac_re_ref, ac_im_ref, ldt_ref,
                    b_re_ref, b_im_ref, c_re_ref, c_im_ref,
                    bj_re_ref, bj_im_ref, cc_re_ref, cc_im_ref, a16_re_ref, a16_im_ref):
    dt = jnp.exp(ldt_ref[...])
    lam_re, lam_im = ab_re_ref[...], ab_im_ref[...]
    ar, ai = _a_bar(lam_re, lam_im, dt)
    den = lam_re * lam_re + lam_im * lam_im
    nr, ni = ar - 1.0, ai
    coef_re = (nr * lam_re + ni * lam_im) / den
    coef_im = (ni * lam_re - nr * lam_im) / den
    bb_re, bb_im = _cmul(coef_re, coef_im, b_re_ref[...], b_im_ref[...])
    pr, pi = jnp.ones_like(ar), jnp.zeros_like(ar)
    for j in range(SSM_FOLD):
        bj_re_ref[j], bj_im_ref[j] = _cmul(pr, pi, bb_re, bb_im)
        pr, pi = _cmul(pr, pi, ar, ai)
    a16_re_ref[...] = pr
    a16_im_ref[...] = pi
    ar, ai = _a_bar(ac_re_ref[...], ac_im_ref[...], dt)
    c_re, c_im = c_re_ref[...], c_im_ref[...]
    pr, pi = ar, ai
    for l in range(SSM_FOLD):
        zr, zi = _cmul(c_re, c_im, pr, pi)
        cc_re_ref[l] = zr
        cc_im_ref[l] = -zi
        pr, pi = _cmul(pr, pi, ar, ai)


def _s5_kmat_kernel(c_ref, b_ref, k_ref):
    c_hi, c_lo = _split_bf16(c_ref[...])
    b_hi, b_lo = _split_bf16(b_ref[...])
    dn = (((2,), (1,)), ((0,), (0,)))
    k_ref[...] = (lax.dot_general(c_hi, b_hi, dn, preferred_element_type=F32)
                  + lax.dot_general(c_lo, b_hi, dn, preferred_element_type=F32)
                  + lax.dot_general(c_hi, b_lo, dn, preferred_element_type=F32))


def s5_prepare(a_re, a_im, log_dt, b_re, b_im, c_re, c_im):
    g, p, c = SSM_GROUPS, SSM_STATE, SSM_GROUP
    j = SSM_FOLD
    gp = g * p // LANES
    flat = p * c
    vm = lambda n: [pl.BlockSpec(memory_space=pltpu.VMEM)] * n
    bj_re, bj_im, cc_re, cc_im, a16_re, a16_im = pl.pallas_call(
        _s5_prep_kernel,
        in_specs=vm(9),
        out_specs=vm(6),
        out_shape=[jax.ShapeDtypeStruct((j, g, flat), F32)] * 4
        + [jax.ShapeDtypeStruct((g, flat), F32)] * 2,
        name="s5_prep",
    )(jnp.repeat(a_re, c, axis=1), jnp.repeat(a_im, c, axis=1),
      jnp.tile(a_re, (1, c)), jnp.tile(a_im, (1, c)), log_dt.reshape(g, 1),
      b_re.reshape(g, flat), b_im.reshape(g, flat), c_re.reshape(g, flat), c_im.reshape(g, flat))

    bj = jnp.concatenate([bj_re.reshape(j, g, p, c), bj_im.reshape(j, g, p, c)], axis=2)
    bj = bj.transpose(1, 2, 0, 3).reshape(g, 2 * p, j * c)
    cr = jnp.concatenate([c_re, -c_im], axis=2)
    kmat = pl.pallas_call(
        _s5_kmat_kernel,
        in_specs=vm(2),
        out_specs=vm(1)[0],
        out_shape=jax.ShapeDtypeStruct((g, c, j * c), F32),
        name="s5_kmat",
    )(cr, bj)

    nt, gt = N_LANE_TILES, GROUPS_PER_TILE
    eye = jnp.eye(gt, dtype=F32)
    k5 = jnp.pad(kmat.reshape(nt, gt, c, j, c), ((0, 0), (0, 0), (0, 0), (1, 0), (0, 0)))
    d2 = np.arange(j // 2)[:, None, None]
    jidx = 2 * d2 + np.arange(2)[None, None, :] - np.arange(2)[None, :, None] + 1
    kg = k5[:, :, :, jidx, :]
    bt = jnp.einsum("Ggcdmli,gh->Gdmgilhc", kg, eye).reshape(nt, j // 2, 2 * LANES, 2 * LANES)
    half = jnp.asarray(np.arange(gt)[:, None] % 2 == np.arange(2)[None, :], F32)

    def win(x):
        x = x[::-1].reshape(j, nt, gt, p, c)
        x = jnp.einsum("mGgpi,gh->Gmgihp", x, half)
        return x.reshape(nt, j * LANES, 2 * p)
    wc = jnp.stack([win(bj_re), win(bj_im)], axis=1)

    def cross(x):
        x = x.reshape(j, nt, gt, c, p)
        return jnp.einsum("lGgcp,gh->Ggplhc", x, eye).reshape(nt, gt * p, j * LANES)
    cc = jnp.concatenate([cross(cc_re), cross(cc_im)], axis=1).astype(BF16)
    a16 = jnp.stack([a16_re[:, ::c].reshape(gp, LANES), a16_im[:, ::c].reshape(gp, LANES)])
    return bt.astype(BF16), wc, cc, a16


def _s5_tokens(x_ref, g_ref, m):
    xm = x_ref[:, m * D_MODEL:(m + 1) * D_MODEL]
    return xm, xm * _rms_scale(xm) * g_ref[...]


def _s5_state_in_kernel(x_ref, g_ref, wc_ref, vre_ref, vim_ref, uhi_ref, ulo_ref, whi_ref, wlo_ref):
    tile = pl.program_id(1)
    nt = N_LANE_TILES

    @pl.when(tile == 0)
    def _():
        for m in range(SSM_FOLD):
            _, hm = _s5_tokens(x_ref, g_ref, m)
            hi, lo = _split_bf16(hm)
            for k in range(nt):
                uhi_ref[m * nt + k] = hi[:, k * LANES:(k + 1) * LANES]
                ulo_ref[m * nt + k] = lo[:, k * LANES:(k + 1) * LANES]

    k_rows = SSM_FOLD * LANES
    pair = (lax.broadcasted_iota(jnp.int32, (k_rows, 1), 0) // SSM_GROUP) % GROUPS_PER_TILE // 2
    n_half = GROUPS_PER_TILE * SSM_STATE
    for ri in range(2):
        w = wc_ref[0, ri]
        for t in range(GROUPS_PER_TILE // 2):
            hi, lo = _split_bf16(jnp.where(pair == t, w, 0.0))
            cols = slice(ri * n_half + t * LANES, ri * n_half + (t + 1) * LANES)
            whi_ref[:, cols] = hi
            wlo_ref[:, cols] = lo
    lhs_hi = jnp.concatenate([uhi_ref[m * nt + tile] for m in range(SSM_FOLD)], axis=1)
    lhs_lo = jnp.concatenate([ulo_ref[m * nt + tile] for m in range(SSM_FOLD)], axis=1)
    v = _dot(lhs_hi, whi_ref[...]) + _dot(lhs_lo, whi_ref[...]) + _dot(lhs_hi, wlo_ref[...])
    vre_ref[...] = v[:, :n_half]
    vim_ref[...] = v[:, n_half:]


def s5_state_in(x16, g, wc, rt):
    rows = x16.shape[0]
    nt = N_LANE_TILES
    k_rows = SSM_FOLD * LANES
    n_half = GROUPS_PER_TILE * SSM_STATE
    n_state = SSM_GROUPS * SSM_STATE
    out = jax.ShapeDtypeStruct((rows, n_state), F32)
    return pl.pallas_call(
        _s5_state_in_kernel,
        grid=(rows // rt, nt),
        in_specs=[pl.BlockSpec((rt, SSM_FOLD * D_MODEL), lambda i, t: (i, 0)),
                  _const_spec((1, D_MODEL)),
                  pl.BlockSpec((1, 2, k_rows, LANES), lambda i, t: (t, 0, 0, 0))],
        out_specs=[pl.BlockSpec((rt, n_half), lambda i, t: (i, t)),
                   pl.BlockSpec((rt, n_half), lambda i, t: (i, t))],
        out_shape=[out, out],
        scratch_shapes=[pltpu.VMEM((SSM_FOLD * nt, rt, LANES), BF16),
                        pltpu.VMEM((SSM_FOLD * nt, rt, LANES), BF16),
                        pltpu.VMEM((k_rows, 2 * n_half), BF16),
                        pltpu.VMEM((k_rows, 2 * n_half), BF16)],
        compiler_params=_cparams("parallel", "arbitrary"),
        name="s5_state_in",
    )(x16, g.reshape(1, D_MODEL), wc)


def _s5_scan_kernel(vre_ref, vim_ref, a_ref, s0re_ref, s0im_ref,
                    sre_ref, sim_ref, fre_ref, fim_ref, st_re, st_im, *, rb):
    @pl.when(pl.program_id(0) == 0)
    def _():
        st_re[...] = s0re_ref[...]
        st_im[...] = s0im_ref[...]

    ar = a_ref[0]
    ai = a_ref[1]

    def body(n, carry):
        sr, si = carry
        sre_ref[:, n] = sr
        sim_ref[:, n] = si
        return (ar * sr - ai * si + vre_ref[:, n], ar * si + ai * sr + vim_ref[:, n])

    sr, si = lax.fori_loop(0, rb, body, (st_re[...], st_im[...]))
    st_re[...] = sr
    st_im[...] = si
    fre_ref[...] = sr
    fim_ref[...] = si


def s5_scan(v_re, v_im, a16, s0_re, s0_im, rb):
    b, r, gp, _ = v_re.shape
    vspec = pl.BlockSpec((b, rb, gp, LANES), lambda c: (0, c, 0, 0))
    sspec = pl.BlockSpec((b, gp, LANES), lambda c: (0, 0, 0))
    vout = jax.ShapeDtypeStruct(v_re.shape, F32)
    sout = jax.ShapeDtypeStruct(s0_re.shape, F32)
    return pl.pallas_call(
        functools.partial(_s5_scan_kernel, rb=rb),
        grid=(r // rb,),
        in_specs=[vspec, vspec, _const_spec((2, gp, LANES)), sspec, sspec],
        out_specs=[vspec, vspec, sspec, sspec],
        out_shape=[vout, vout, sout, sout],
        scratch_shapes=[pltpu.VMEM((b, gp, LANES), F32), pltpu.VMEM((b, gp, LANES), F32)],
        compiler_params=_cparams("arbitrary"),
        name="s5_scan",
    )(v_re, v_im, a16, s0_re, s0_im)


def _s5_out_kernel(x_ref, g_ref, d_ref, sre_ref, sim_ref, bt_ref, cc_ref, wglu_ref, bglu_ref,
                   o_ref, u_ref, y_ref, *, rt, seg_batch):
    tile = pl.program_id(1)
    nt = N_LANE_TILES

    @pl.when(tile == 0)
    def _():
        for m in range(SSM_FOLD):
            _, hm = _s5_tokens(x_ref, g_ref, m)
            hb = hm.astype(BF16)
            for k in range(nt):
                u_ref[m * nt + k] = hb[:, k * LANES:(k + 1) * LANES]

    npair = SSM_FOLD // 2
    u2 = [jnp.concatenate([u_ref[2 * mm * nt + tile], u_ref[(2 * mm + 1) * nt + tile]], axis=1)
          for mm in range(npair)]
    s_prev = jnp.concatenate([sre_ref[...], sim_ref[...]], axis=1).astype(BF16)
    for ll in range(npair):
        acc = _dot(s_prev, cc_ref[0, :, ll * 2 * LANES:(ll + 1) * 2 * LANES])
        for mm in range(ll + 1):
            acc = acc + _dot(u2[mm], bt_ref[0, ll - mm])
        y_ref[2 * ll * nt + tile] = acc[:, :LANES]
        y_ref[(2 * ll + 1) * nt + tile] = acc[:, LANES:]

    @pl.when(tile == nt - 1)
    def _():
        for m0 in range(0, SSM_FOLD, seg_batch):
            gl = []
            for m in range(m0, m0 + seg_batch):
                _, hm = _s5_tokens(x_ref, g_ref, m)
                ym = jnp.concatenate([y_ref[m * nt + k] for k in range(nt)], axis=1)
                gl.append(jax.nn.gelu(ym + d_ref[...] * hm))
            gl = jnp.concatenate(gl, axis=0)
            out = gl * jax.nn.sigmoid(_dot(gl.astype(BF16), wglu_ref[...]) + bglu_ref[...])
            for j, m in enumerate(range(m0, m0 + seg_batch)):
                cols = slice(m * D_MODEL, (m + 1) * D_MODEL)
                o_ref[:, cols] = x_ref[:, cols] + out[j * rt:(j + 1) * rt]


def s5_out(x16, g, d, s_re, s_im, bt, cc, w_glu, b_glu, rt, seg_batch=4):
    rows = x16.shape[0]
    nt = N_LANE_TILES
    n_half = GROUPS_PER_TILE * SSM_STATE
    width = SSM_FOLD * D_MODEL
    return pl.pallas_call(
        functools.partial(_s5_out_kernel, rt=rt, seg_batch=seg_batch),
        grid=(rows // rt, nt),
        in_specs=[pl.BlockSpec((rt, width), lambda i, t: (i, 0), pipeline_mode=pl.Buffered(1)),
                  _const_spec((1, D_MODEL)),
                  _const_spec((1, D_MODEL)),
                  pl.BlockSpec((rt, n_half), lambda i, t: (i, t)),
                  pl.BlockSpec((rt, n_half), lambda i, t: (i, t)),
                  pl.BlockSpec((1, SSM_FOLD // 2, 2 * LANES, 2 * LANES), lambda i, t: (t, 0, 0, 0)),
                  pl.BlockSpec((1, 2 * n_half, SSM_FOLD * LANES), lambda i, t: (t, 0, 0)),
                  _const_spec((D_MODEL, D_MODEL)),
                  _const_spec((1, D_MODEL))],
        out_specs=pl.BlockSpec((rt, width), lambda i, t: (i, 0), pipeline_mode=pl.Buffered(1)),
        out_shape=jax.ShapeDtypeStruct((rows, width), F32),
        scratch_shapes=[pltpu.VMEM((SSM_FOLD * nt, rt, LANES), BF16),
                        pltpu.VMEM((SSM_FOLD * nt, rt, LANES), F32)],
        compiler_params=_cparams("parallel", "arbitrary"),
        name="s5_out",
    )(x16, g.reshape(1, D_MODEL), d.reshape(1, D_MODEL), s_re, s_im, bt, cc,
      w_glu, b_glu.reshape(1, D_MODEL))


def s5_mixer(x, st_re, st_im, g, ops, d, w_glu, b_glu, rt, rb):
    bt, wc, cc, a16 = ops
    b, l, dm = x.shape
    r = l // SSM_FOLD
    gp = SSM_GROUPS * SSM_STATE // LANES
    x16 = x.reshape(b * r, SSM_FOLD * dm)
    v_re, v_im = s5_state_in(x16, g, wc, rt)
    s_re, s_im, f_re, f_im = s5_scan(v_re.reshape(b, r, gp, LANES), v_im.reshape(b, r, gp, LANES),
                                     a16, st_re.reshape(b, gp, LANES), st_im.reshape(b, gp, LANES),
                                     rb)
    y16 = s5_out(x16, g, d, s_re.reshape(b * r, -1), s_im.reshape(b * r, -1), bt, cc,
                 w_glu, b_glu, rt)
    shape = (b, SSM_GROUPS, SSM_STATE)
    return y16.reshape(b, l, dm), f_re.reshape(shape), f_im.reshape(shape)


def _stream_tiling(seq_len, n_streams):
    if seq_len >= 512:
        return 1, 512
    return min(n_streams, 256 // seq_len), seq_len


def kernel(x_prompt, x_sample, mem_prompt, state_ssm_re, state_ssm_im, state_ret, cache_mem_k, cache_mem_v, cache_conv, norm_mix, norm_mem_q, norm_mem_kv, norm_ffn, norm_final, ssm_a_re, ssm_a_im, ssm_log_dt, ssm_b_re, ssm_b_im, ssm_c_re, ssm_c_im, ssm_d, ssm_w_glu, ssm_b_glu, ret_w_qkvg, ret_w_o, mem_w_q, mem_w_kv, mem_w_o, ffn_w_up, ffn_conv_w, ffn_conv_b, ffn_w_down):
    bp, lp, dm = x_prompt.shape
    bs, ls, _ = x_sample.shape
    depth = norm_mix.shape[0]
    n_mem = mem_prompt.shape[1]
    d_ff = ffn_w_down.shape[1]
    streams = [dict(x=x_prompt, b=bp, l=lp, tiling=_stream_tiling(lp, bp)),
               dict(x=x_sample, b=bs, l=ls, tiling=_stream_tiling(ls, bs))]
    mem_flat = mem_prompt.reshape(bp * n_mem, dm)
    outs = dict(ssm_re=([], []), ssm_im=([], []), ret=([], []), conv=([], []), mk=[], mv=[])

    for i in range(depth):
        j = i // 2
        if i % 2 == 0:
            ops = s5_prepare(ssm_a_re[j], ssm_a_im[j], ssm_log_dt[j], ssm_b_re[j], ssm_b_im[j],
                             ssm_c_re[j], ssm_c_im[j])
            w_glu = ssm_w_glu[j].astype(BF16)
            zeros = jnp.zeros((bp, SSM_GROUPS, SSM_STATE), F32)
            states = [(zeros, zeros), (state_ssm_re[j], state_ssm_im[j])]
            for si, (s, (st_re, st_im)) in enumerate(zip(streams, states)):
                rows = s["b"] * s["l"] // SSM_FOLD
                rt = min(128, rows)
                rb = min(32, s["l"] // SSM_FOLD)
                s["x"], f_re, f_im = s5_mixer(s["x"], st_re, st_im, norm_mix[i], ops, ssm_d[j],
                                              w_glu, ssm_b_glu[j], rt, rb)
                outs["ssm_re"][si].append(f_re)
                outs["ssm_im"][si].append(f_im)
        else:
            w_qkvg = ret_w_qkvg[j].astype(BF16)
            w_o = ret_w_o[j].astype(BF16)
            zeros = jnp.zeros((bp, RET_HEADS, RET_DK, RET_DV), F32)
            for si, (s, st, pos0, chunk) in enumerate(zip(streams, [zeros, state_ret[j]],
                                                          [0, PAST_LEN], [256, ls])):
                x2 = s["x"].reshape(s["b"] * s["l"], dm)
                p = rms_matmul(x2, norm_mix[i], w_qkvg, BF16)
                cos, sin = _rope_tables(pos0, s["l"])
                y, st_new = ret_core(p, cos, sin, st, s["l"], min(chunk, s["l"]))
                s["x"] = matmul_res(y, w_o, x2).reshape(s["x"].shape)
                outs["ret"][si].append(st_new)

        kv = rms_matmul(mem_flat, norm_mem_kv[i], mem_w_kv[i].astype(BF16), F32)
        mk = kv[:, :dm].reshape(bp, n_mem, dm)
        mv = kv[:, dm:].reshape(bp, n_mem, dm)
        outs["mk"].append(mk.reshape(bp, n_mem, MEM_HEADS, MEM_HD))
        outs["mv"].append(mv.reshape(bp, n_mem, MEM_HEADS, MEM_HD))
        w_q = mem_w_q[i].astype(BF16)
        w_o = mem_w_o[i].astype(BF16)
        mems = [(mk, mv), (cache_mem_k[i].reshape(bs, n_mem, dm), cache_mem_v[i].reshape(bs, n_mem, dm))]
        for s, (k_, v_) in zip(streams, mems):
            x2 = s["x"].reshape(s["b"] * s["l"], dm)
            q = rms_matmul(x2, norm_mem_q[i], w_q, BF16)
            o = attn_core(q, k_, v_, s["l"], *s["tiling"])
            s["x"] = matmul_res(o, w_o, x2).reshape(s["x"].shape)

        w_up = ffn_w_up[i].astype(BF16)
        w_dn = ffn_w_down[i].astype(BF16)
        g_final = norm_final if i == depth - 1 else None
        conv_states = [jnp.zeros((bp, 2, d_ff), F32), cache_conv[i]]
        for si, (s, cst) in enumerate(zip(streams, conv_states)):
            x2 = s["x"].reshape(s["b"] * s["l"], dm)
            y, cnew = conv_ffn(x2, norm_ffn[i], w_up, ffn_conv_w[i], ffn_conv_b[i], w_dn, cst,
                               s["l"], *s["tiling"], g_final=g_final)
            s["x"] = y.reshape(s["x"].shape)
            outs["conv"][si].append(cnew)

    return (streams[0]["x"], streams[1]["x"],
            jnp.stack(outs["ssm_re"][0]), jnp.stack(outs["ssm_im"][0]),
            jnp.stack(outs["ssm_re"][1]), jnp.stack(outs["ssm_im"][1]),
            jnp.stack(outs["ret"][0]), jnp.stack(outs["ret"][1]),
            jnp.stack(outs["mk"]), jnp.stack(outs["mv"]),
            jnp.stack(outs["conv"][0]), jnp.stack(outs["conv"][1]))
```

```python
import functools
import math

import numpy as np
import jax
import jax.numpy as jnp
from jax import lax
from jax.experimental import pallas as pl
from jax.experimental.pallas import tpu as pltpu

F32 = jnp.float32
BF16 = jnp.bfloat16

D_MODEL = 1024
PAST_LEN = 1024
EPS = 1e-6
GN_EPS = 1e-5
ROPE_BASE = 10000.0
LANES = 128
SUBLANES = 8
SSM_GROUP = 16
SSM_GROUPS = D_MODEL // SSM_GROUP
SSM_STATE = 64
SSM_FOLD = 16
GROUPS_PER_TILE = LANES // SSM_GROUP
N_LANE_TILES = D_MODEL // LANES
STATE_PER_TILE = GROUPS_PER_TILE * SSM_STATE
N_STATE = SSM_GROUPS * SSM_STATE
RET_HEADS = 4
RET_DK = D_MODEL // RET_HEADS
RET_DV = 2 * D_MODEL // RET_HEADS
RET_QK = RET_HEADS * RET_DK
RET_V = RET_HEADS * RET_DV
RET_LOG_G = tuple(math.log(1.0 - 2.0 ** (-5.0 - h)) for h in range(RET_HEADS))
MEM_HEADS = 4
MEM_HD = D_MODEL // MEM_HEADS
VMEM_LIMIT = 56 * 1024 * 1024


def _cparams(*sem):
    return pltpu.CompilerParams(dimension_semantics=sem, vmem_limit_bytes=VMEM_LIMIT)


def _const_spec(shape):
    nd = len(shape)
    return pl.BlockSpec(shape, lambda *_: (0,) * nd, pipeline_mode=pl.Buffered(1))


def _vmem_specs(n):
    return [pl.BlockSpec(memory_space=pltpu.VMEM)] * n


def _rms_scale(x):
    return lax.rsqrt(jnp.mean(x * x, axis=-1, keepdims=True) + EPS)


def _split_bf16(x):
    hi = x.astype(BF16)
    lo = (x - hi.astype(F32)).astype(BF16)
    return hi, lo


def _dot(a, b):
    return jnp.dot(a, b, preferred_element_type=F32)


def _dot_nt(a, b):
    return lax.dot_general(a, b, (((1,), (1,)), ((), ())), preferred_element_type=F32)


def _rms_matmul_kernel(x_ref, g_ref, w_ref, o_ref, *, tn):
    x = x_ref[...]
    h = (x * _rms_scale(x) * g_ref[...]).astype(BF16)
    for n0 in range(0, w_ref.shape[1], tn):
        o_ref[:, n0:n0 + tn] = _dot(h, w_ref[:, n0:n0 + tn]).astype(o_ref.dtype)


def rms_matmul(x, g, w, out_dtype, tm=512, tn=512):
    t, d = x.shape
    n = w.shape[1]
    tm = min(tm, t)
    return pl.pallas_call(
        functools.partial(_rms_matmul_kernel, tn=tn),
        grid=(t // tm,),
        in_specs=[pl.BlockSpec((tm, d), lambda i: (i, 0)),
                  _const_spec((1, d)),
                  _const_spec((d, n))],
        out_specs=pl.BlockSpec((tm, n), lambda i: (i, 0)),
        out_shape=jax.ShapeDtypeStruct((t, n), out_dtype),
        compiler_params=_cparams("parallel"),
        name="rms_matmul",
    )(x, g.reshape(1, d), w)


def _matmul_res_kernel(y_ref, w_ref, x_ref, o_ref):
    o_ref[...] = x_ref[...] + _dot(y_ref[...], w_ref[...])


def matmul_res(y, w, x, tm=512):
    t, k = y.shape
    d = w.shape[1]
    tm = min(tm, t)
    return pl.pallas_call(
        _matmul_res_kernel,
        grid=(t // tm,),
        in_specs=[pl.BlockSpec((tm, k), lambda i: (i, 0)),
                  _const_spec((k, d)),
                  pl.BlockSpec((tm, d), lambda i: (i, 0))],
        out_specs=pl.BlockSpec((tm, d), lambda i: (i, 0)),
        out_shape=jax.ShapeDtypeStruct((t, d), F32),
        compiler_params=_cparams("parallel"),
        name="matmul_res",
    )(y, w, x)


def _attn_kernel(q_ref, k_ref, v_ref, o_ref, *, nseg, seg):
    scale = MEM_HD ** -0.5
    for s in range(nseg):
        rows = slice(s * seg, (s + 1) * seg)
        for h in range(MEM_HEADS):
            cols = slice(h * MEM_HD, (h + 1) * MEM_HD)
            qh = q_ref[rows, cols]
            kh = k_ref[0, s, :, cols].astype(BF16)
            vh = v_ref[0, s, :, cols].astype(BF16)
            sc = _dot_nt(qh, kh) * scale
            p = jnp.exp(sc - jnp.max(sc, axis=-1, keepdims=True))
            denom = jnp.sum(p, axis=-1, keepdims=True)
            o = _dot(p.astype(BF16), vh) / denom
            o_ref[rows, cols] = o.astype(o_ref.dtype)


def attn_core(q, mk, mv, layer, seq_len, nseg, seg):
    t, d = q.shape
    _, b, n_mem, _ = mk.shape
    tiles_per_stream = max(seq_len // (nseg * seg), 1)
    n_outer = b // nseg
    tm = nseg * seg
    kv_spec = pl.BlockSpec((1, nseg, n_mem, d), lambda o, i: (layer, o, 0, 0))
    return pl.pallas_call(
        functools.partial(_attn_kernel, nseg=nseg, seg=seg),
        grid=(n_outer, tiles_per_stream),
        in_specs=[pl.BlockSpec((tm, d), lambda o, i: (o * tiles_per_stream + i, 0)),
                  kv_spec, kv_spec],
        out_specs=pl.BlockSpec((tm, d), lambda o, i: (o * tiles_per_stream + i, 0)),
        out_shape=jax.ShapeDtypeStruct((t, d), BF16),
        compiler_params=_cparams("parallel", "parallel"),
        name="attn_core",
    )(q, mk, mv)


def _ffn_kernel(x_ref, g_ref, wup_ref, cw_ref, cb_ref, wdn_ref, st_ref, *rest,
                nseg, seg, fc, final_norm):
    if final_norm:
        gf_ref, o_ref, nst_ref, carry_ref = rest
    else:
        o_ref, nst_ref, carry_ref = rest
    d_ff = wdn_ref.shape[0]
    tm = nseg * seg

    @pl.when(pl.program_id(1) == 0)
    def _():
        carry_ref[...] = st_ref[...]

    x = x_ref[...]
    h = (x * _rms_scale(x) * g_ref[...]).astype(BF16)
    row = lax.broadcasted_iota(jnp.int32, (tm, 1), 0) % seg
    acc = jnp.zeros((tm, x.shape[1]), F32)
    for f0 in range(0, d_ff, fc):
        cols = slice(f0, f0 + fc)
        a = _dot(h, wup_ref[:, cols])
        gate = _dot(h, wup_ref[:, d_ff + f0:d_ff + f0 + fc])
        p0 = jnp.broadcast_to(carry_ref[:, 0:1, cols], (nseg, seg, fc)).reshape(tm, fc)
        p1 = jnp.broadcast_to(carry_ref[:, 1:2, cols], (nseg, seg, fc)).reshape(tm, fc)
        a1 = jnp.where(row == 0, p1, pltpu.roll(a, 1, 0))
        a2 = jnp.where(row == 0, p0, jnp.where(row == 1, p1, pltpu.roll(a, 2, 0)))
        conv = (cw_ref[0:1, cols] * a2 + cw_ref[1:2, cols] * a1 + cw_ref[2:3, cols] * a
                + cb_ref[:, cols])
        p = (jax.nn.gelu(conv) * gate).astype(BF16)
        acc = acc + _dot(p, wdn_ref[cols, :])
        carry_ref[:, :, cols] = a.reshape(nseg, seg, fc)[:, seg - 2:seg, :]
    nst_ref[...] = carry_ref[...]
    y = x + acc
    if final_norm:
        y = y * _rms_scale(y) * gf_ref[...]
    o_ref[...] = y


def conv_ffn(x, g, w_up, conv_w, conv_b, w_down, conv_state, seq_len, nseg, seg, g_final=None,
             fc=256):
    t, d = x.shape
    d_ff = w_down.shape[0]
    b = conv_state.shape[0]
    tiles_per_stream = max(seq_len // (nseg * seg), 1)
    n_outer = b // nseg
    tm = nseg * seg
    final_norm = g_final is not None
    in_specs = [pl.BlockSpec((tm, d), lambda o, i: (o * tiles_per_stream + i, 0)),
                _const_spec((1, d)),
                _const_spec((d, 2 * d_ff)),
                _const_spec((3, d_ff)),
                _const_spec((1, d_ff)),
                _const_spec((d_ff, d)),
                pl.BlockSpec((nseg, 2, d_ff), lambda o, i: (o, 0, 0))]
    args = [x, g.reshape(1, d), w_up, conv_w, conv_b.reshape(1, d_ff), w_down, conv_state]
    if final_norm:
        in_specs.append(_const_spec((1, d)))
        args.append(g_final.reshape(1, d))
    return pl.pallas_call(
        functools.partial(_ffn_kernel, nseg=nseg, seg=seg, fc=fc, final_norm=final_norm),
        grid=(n_outer, tiles_per_stream),
        in_specs=in_specs,
        out_specs=[pl.BlockSpec((tm, d), lambda o, i: (o * tiles_per_stream + i, 0)),
                   pl.BlockSpec((nseg, 2, d_ff), lambda o, i: (o, 0, 0))],
        out_shape=[jax.ShapeDtypeStruct((t, d), F32),
                   jax.ShapeDtypeStruct((b, 2, d_ff), F32)],
        scratch_shapes=[pltpu.VMEM((nseg, 2, d_ff), F32)],
        compiler_params=_cparams("parallel", "arbitrary"),
        name="conv_ffn",
    )(*args)


def _rotate(x, cos, sin):
    half = x.shape[1] // 2
    x1, x2 = x[:, :half], x[:, half:]
    return jnp.concatenate([x1 * cos - x2 * sin, x2 * cos + x1 * sin], axis=1)


def _ret_kernel(p_ref, cos_ref, sin_ref, s_in_ref, y_ref, s_out_ref, *, chunk):
    @pl.when(pl.program_id(1) == 0)
    def _():
        s_out_ref[...] = s_in_ref[...]

    cos = cos_ref[...]
    sin = sin_ref[...]
    li = lax.broadcasted_iota(jnp.int32, (chunk, chunk), 0)
    mi = lax.broadcasted_iota(jnp.int32, (chunk, chunk), 1)
    diff = (li - mi).astype(F32)
    pos = lax.broadcasted_iota(jnp.int32, (chunk, 1), 0).astype(F32)
    for h in range(RET_HEADS):
        log_g = RET_LOG_G[h]
        q = p_ref[:, h * RET_DK:(h + 1) * RET_DK].astype(F32)
        k = p_ref[:, RET_QK + h * RET_DK:RET_QK + (h + 1) * RET_DK].astype(F32)
        v = p_ref[:, 2 * RET_QK + h * RET_DV:2 * RET_QK + (h + 1) * RET_DV]
        g = p_ref[:, 2 * RET_QK + RET_V + h * RET_DV:
                  2 * RET_QK + RET_V + (h + 1) * RET_DV].astype(F32)
        qr = _rotate(q, cos, sin).astype(BF16)
        kr = _rotate(k, cos, sin) * (RET_DK ** -0.5)
        decay = jnp.where(diff >= 0, jnp.exp(jnp.maximum(diff, 0.0) * log_g), 0.0)
        scores = _dot_nt(qr, kr.astype(BF16)) * decay
        intra = _dot(scores.astype(BF16), v)
        s_old = s_out_ref[0, h]
        cross = _dot(qr, s_old.astype(BF16)) * jnp.exp((pos + 1.0) * log_g)
        o = intra + cross
        k_tail = (kr * jnp.exp((chunk - 1.0 - pos) * log_g)).astype(BF16)
        s_out_ref[0, h] = math.exp(chunk * log_g) * s_old + lax.dot_general(
            k_tail, v, (((0,), (0,)), ((), ())), preferred_element_type=F32)
        mu = jnp.mean(o, axis=-1, keepdims=True)
        oc = o - mu
        var = jnp.mean(oc * oc, axis=-1, keepdims=True)
        on = oc * lax.rsqrt(var + GN_EPS)
        y_ref[:, h * RET_DV:(h + 1) * RET_DV] = (jax.nn.silu(g) * on).astype(y_ref.dtype)


def ret_core(p, cos, sin, state, seq_len, chunk):
    t, n = p.shape
    b = state.shape[0]
    nchunks = seq_len // chunk
    half = RET_DK // 2
    return pl.pallas_call(
        functools.partial(_ret_kernel, chunk=chunk),
        grid=(b, nchunks),
        in_specs=[pl.BlockSpec((chunk, n), lambda o, i: (o * nchunks + i, 0)),
                  pl.BlockSpec((chunk, half), lambda o, i: (i, 0)),
                  pl.BlockSpec((chunk, half), lambda o, i: (i, 0)),
                  pl.BlockSpec((1, RET_HEADS, RET_DK, RET_DV), lambda o, i: (o, 0, 0, 0))],
        out_specs=[pl.BlockSpec((chunk, RET_V), lambda o, i: (o * nchunks + i, 0)),
                   pl.BlockSpec((1, RET_HEADS, RET_DK, RET_DV), lambda o, i: (o, 0, 0, 0))],
        out_shape=[jax.ShapeDtypeStruct((t, RET_V), BF16),
                   jax.ShapeDtypeStruct(state.shape, F32)],
        compiler_params=_cparams("parallel", "arbitrary"),
        name="ret_core",
    )(p, cos, sin, state)


def _rope_tables(pos0, length):
    half = RET_DK // 2
    freqs = ROPE_BASE ** (-np.arange(half, dtype=np.float64) / half)
    ang = (pos0 + np.arange(length, dtype=np.float64))[:, None] * freqs[None, :]
    return jnp.asarray(np.cos(ang), F32), jnp.asarray(np.sin(ang), F32)


def _cmul(ar, ai, br, bi):
    return ar * br - ai * bi, ar * bi + ai * br


def _s5_prep_kernel(lam_re_ref, lam_im_ref, ldt_ref, b_re_ref, b_im_ref, c_re_ref, c_im_ref,
                    w_re_ref, w_im_ref, bt_ref, cct_ref, a16_ref):
    dt = jnp.exp(ldt_ref[...])
    lam_re, lam_im = lam_re_ref[...], lam_im_ref[...]
    mag = jnp.exp(lam_re * dt)
    ar, ai = mag * jnp.cos(lam_im * dt), mag * jnp.sin(lam_im * dt)
    den = lam_re * lam_re + lam_im * lam_im
    nr, ni = ar - 1.0, ai
    coef_re = (nr * lam_re + ni * lam_im) / den
    coef_im = (ni * lam_re - nr * lam_im) / den
    bb_re, bb_im = _cmul(coef_re, coef_im, b_re_ref[...], b_im_ref[...])
    c_re, c_im = c_re_ref[...], c_im_ref[...]
    lane = lax.broadcasted_iota(jnp.int32, (1, LANES), 1)
    low = lane < SSM_STATE
    rg = lax.broadcasted_iota(jnp.int32, (LANES, 1), 0) // SSM_GROUP
    pair = rg // 2
    own_half = (lane // SSM_STATE) == (rg % 2)
    same_group = rg == lane // SSM_GROUP
    c_hi, c_lo = _split_bf16(jnp.where(low, c_re, -c_im))
    tk = []
    pr, pi = jnp.ones_like(ar), jnp.zeros_like(ar)
    for j in range(SSM_FOLD):
        zr, zi = _cmul(pr, pi, bb_re, bb_im)
        w_re_ref[SSM_FOLD - 1 - j] = jnp.where(own_half, zr, 0.0)
        w_im_ref[SSM_FOLD - 1 - j] = jnp.where(own_half, zi, 0.0)
        b_hi, b_lo = _split_bf16(jnp.where(low, zr, zi))
        kk = _dot_nt(b_hi, c_hi) + _dot_nt(b_lo, c_hi) + _dot_nt(b_hi, c_lo)
        tk.append(jnp.where(same_group, kk, 0.0).astype(BF16))
        pr, pi = _cmul(pr, pi, ar, ai)
        zr, zi = _cmul(c_re, c_im, pr, pi)
        for ri, z in enumerate((zr, -zi)):
            z = jnp.where(own_half, z, 0.0)
            for t in range(GROUPS_PER_TILE // 2):
                c0 = ri * STATE_PER_TILE + t * LANES
                cct_ref[0, j * LANES:(j + 1) * LANES, c0:c0 + LANES] = (
                    jnp.where(pair == t, z, 0.0).astype(BF16))
    a16_ref[0] = pr
    a16_ref[1] = pi
    zero = jnp.zeros((LANES, LANES), BF16)
    for d2 in range(SSM_FOLD // 2):
        top = jnp.concatenate([tk[2 * d2], tk[2 * d2 + 1]], axis=1)
        bot = jnp.concatenate([tk[2 * d2 - 1] if d2 else zero, tk[2 * d2]], axis=1)
        bt_ref[0, d2] = jnp.concatenate([top, bot], axis=0)


def s5_prepare(a_re, a_im, log_dt, b_re, b_im, c_re, c_im):
    g, p, c = SSM_GROUPS, SSM_STATE, SSM_GROUP
    fold, nt = SSM_FOLD, N_LANE_TILES
    rows = g * c
    twice = lambda x: jnp.tile(x, (1, 2))
    lam = lambda x: twice(jnp.repeat(x, c, axis=0))
    in_spec = pl.BlockSpec((LANES, LANES), lambda t: (t, 0))
    w_spec = pl.BlockSpec((fold, LANES, LANES), lambda t: (0, t, 0))
    w_shape = jax.ShapeDtypeStruct((fold, rows, LANES), F32)
    w_re, w_im, bt, cct, a16 = pl.pallas_call(
        _s5_prep_kernel,
        grid=(nt,),
        in_specs=[in_spec, in_spec, pl.BlockSpec((LANES, 1), lambda t: (t, 0))] + [in_spec] * 4,
        out_specs=[w_spec, w_spec,
                   pl.BlockSpec((1, fold // 2, 2 * LANES, 2 * LANES), lambda t: (t, 0, 0, 0)),
                   pl.BlockSpec((1, fold * LANES, 2 * STATE_PER_TILE), lambda t: (t, 0, 0)),
                   pl.BlockSpec((2, LANES, LANES), lambda t: (0, t, 0))],
        out_shape=[w_shape, w_shape,
                   jax.ShapeDtypeStruct((nt, fold // 2, 2 * LANES, 2 * LANES), BF16),
                   jax.ShapeDtypeStruct((nt, fold * LANES, 2 * STATE_PER_TILE), BF16),
                   jax.ShapeDtypeStruct((2, rows, LANES), F32)],
        compiler_params=_cparams("parallel"),
        name="s5_prep",
    )(lam(a_re), lam(a_im), jnp.repeat(log_dt, c).reshape(rows, 1),
      twice(b_re.transpose(0, 2, 1).reshape(rows, p)), twice(b_im.transpose(0, 2, 1).reshape(rows, p)),
      twice(c_re.reshape(rows, p)), twice(c_im.reshape(rows, p)))
    a16 = a16[:, ::c, :p].reshape(2, 1, N_STATE)
    return bt, cct, w_re, w_im, a16


def _s5_fill_tokens(x_ref, g_ref, u_ref):
    for m in range(SSM_FOLD):
        xm = x_ref[:, m * D_MODEL:(m + 1) * D_MODEL]
        hm = (xm * _rms_scale(xm) * g_ref[...]).astype(BF16)
        for k in range(N_LANE_TILES):
            u_ref[m * N_LANE_TILES + k] = hm[:, k * LANES:(k + 1) * LANES]


def _s5_state_in_kernel(x_ref, g_ref, w_re_ref, w_im_ref, vre_ref, vim_ref, u_ref, wd_ref):
    tile = pl.program_id(1)
    nt = N_LANE_TILES

    @pl.when(tile == 0)
    def _():
        _s5_fill_tokens(x_ref, g_ref, u_ref)

    pair = lax.broadcasted_iota(jnp.int32, (LANES, 1), 0) // SSM_GROUP // 2
    for m in range(SSM_FOLD):
        for ri, ref in enumerate((w_re_ref, w_im_ref)):
            w = ref[m]
            for t in range(GROUPS_PER_TILE // 2):
                c0 = ri * STATE_PER_TILE + t * LANES
                wd_ref[m * LANES:(m + 1) * LANES, c0:c0 + LANES] = (
                    jnp.where(pair == t, w, 0.0).astype(BF16))
    lhs = jnp.concatenate([u_ref[m * nt + tile] for m in range(SSM_FOLD)], axis=1)
    v = _dot(lhs, wd_ref[...])
    vre_ref[...] = v[:, :STATE_PER_TILE]
    vim_ref[...] = v[:, STATE_PER_TILE:]


def s5_state_in(x16, g, w_re, w_im, rt):
    rows = x16.shape[0]
    nt = N_LANE_TILES
    k_rows = SSM_FOLD * LANES
    out = jax.ShapeDtypeStruct((rows, N_STATE), F32)
    w_spec = pl.BlockSpec((SSM_FOLD, LANES, LANES), lambda i, t: (0, t, 0))
    return pl.pallas_call(
        _s5_state_in_kernel,
        grid=(rows // rt, nt),
        in_specs=[pl.BlockSpec((rt, SSM_FOLD * D_MODEL), lambda i, t: (i, 0)),
                  _const_spec((1, D_MODEL)), w_spec, w_spec],
        out_specs=[pl.BlockSpec((rt, STATE_PER_TILE), lambda i, t: (i, t)),
                   pl.BlockSpec((rt, STATE_PER_TILE), lambda i, t: (i, t))],
        out_shape=[out, out],
        scratch_shapes=[pltpu.VMEM((SSM_FOLD * nt, rt, LANES), BF16),
                        pltpu.VMEM((k_rows, 2 * STATE_PER_TILE), BF16)],
        compiler_params=_cparams("parallel", "arbitrary"),
        name="s5_state_in",
    )(x16, g.reshape(1, D_MODEL), w_re, w_im)


def _s5_scan_kernel(vre_ref, vim_ref, a_ref, s0re_ref, s0im_ref,
                    sre_ref, sim_ref, fre_ref, fim_ref, *, streams, steps):
    ar = a_ref[0]
    ai = a_ref[1]
    if streams % SUBLANES == 0:
        def body(n, carry):
            sr, si = carry
            rows = pl.ds(pl.multiple_of(n * streams, SUBLANES), streams)
            sre_ref[rows, :] = sr
            sim_ref[rows, :] = si
            return (ar * sr - ai * si + vre_ref[rows, :], ar * si + ai * sr + vim_ref[rows, :])

        sr, si = lax.fori_loop(0, steps, body, (s0re_ref[...], s0im_ref[...]))
        fre_ref[...] = sr
        fim_ref[...] = si
    else:
        half = SUBLANES // 2
        low = lax.broadcasted_iota(jnp.int32, (SUBLANES, 1), 0) < half

        def body(k, carry):
            sr, si = carry
            rows = pl.ds(pl.multiple_of(k * SUBLANES, SUBLANES), SUBLANES)
            vr, vi = vre_ref[rows, :], vim_ref[rows, :]
            tr, ti = ar * sr - ai * si + vr, ar * si + ai * sr + vi
            wr, wi = pltpu.roll(tr, half, 0), pltpu.roll(ti, half, 0)
            sre_ref[rows, :] = jnp.where(low, sr, wr)
            sim_ref[rows, :] = jnp.where(low, si, wi)
            ur, ui = ar * wr - ai * wi + vr, ar * wi + ai * wr + vi
            return (jnp.where(low, pltpu.roll(ur, half, 0), ur),
                    jnp.where(low, pltpu.roll(ui, half, 0), ui))

        s0r, s0i = s0re_ref[...], s0im_ref[...]
        sr, si = lax.fori_loop(0, steps // 2, body, (jnp.concatenate([s0r, s0r], axis=0),
                                                     jnp.concatenate([s0i, s0i], axis=0)))
        fre_ref[...] = sr[:half]
        fim_ref[...] = si[:half]


def s5_scan(v_re, v_im, a16, s0_re, s0_im, width):
    rows, n = v_re.shape
    streams = s0_re.shape[0]
    assert streams % SUBLANES == 0 or 2 * streams == SUBLANES
    steps = rows // streams
    vspec = pl.BlockSpec((rows, width), lambda c: (0, c))
    sspec = pl.BlockSpec((streams, width), lambda c: (0, c))
    vout = jax.ShapeDtypeStruct((rows, n), F32)
    sout = jax.ShapeDtypeStruct((streams, n), F32)
    return pl.pallas_call(
        functools.partial(_s5_scan_kernel, streams=streams, steps=steps),
        grid=(n // width,),
        in_specs=[vspec, vspec, pl.BlockSpec((2, 1, width), lambda c: (0, 0, c)), sspec, sspec],
        out_specs=[vspec, vspec, sspec, sspec],
        out_shape=[vout, vout, sout, sout],
        compiler_params=_cparams("parallel"),
        name="s5_scan",
    )(v_re, v_im, a16, s0_re, s0_im)


def _s5_out_kernel(x_ref, g_ref, d_ref, sre_ref, sim_ref, bt_ref, cct_ref, wglu_ref, bglu_ref,
                   o_ref, u_ref, y_ref, *, rt, seg_batch):
    tile = pl.program_id(1)
    nt = N_LANE_TILES

    @pl.when(tile == 0)
    def _():
        _s5_fill_tokens(x_ref, g_ref, u_ref)

    npair = SSM_FOLD // 2
    u2 = [jnp.concatenate([u_ref[2 * mm * nt + tile], u_ref[(2 * mm + 1) * nt + tile]], axis=1)
          for mm in range(npair)]
    s_prev = jnp.concatenate([sre_ref[...], sim_ref[...]], axis=1).astype(BF16)
    for ll in range(npair):
        acc = _dot_nt(s_prev, cct_ref[0, ll * 2 * LANES:(ll + 1) * 2 * LANES, :])
        for mm in range(ll + 1):
            acc = acc + _dot(u2[mm], bt_ref[0, ll - mm])
        y_ref[2 * ll * nt + tile] = acc[:, :LANES]
        y_ref[(2 * ll + 1) * nt + tile] = acc[:, LANES:]

    @pl.when(tile == nt - 1)
    def _():
        for m0 in range(0, SSM_FOLD, seg_batch):
            gl = []
            for m in range(m0, m0 + seg_batch):
                xm = x_ref[:, m * D_MODEL:(m + 1) * D_MODEL]
                hm = xm * _rms_scale(xm) * g_ref[...]
                ym = jnp.concatenate([y_ref[m * nt + k] for k in range(nt)], axis=1)
                gl.append(jax.nn.gelu(ym + d_ref[...] * hm))
            gl = jnp.concatenate(gl, axis=0)
            out = gl * jax.nn.sigmoid(_dot(gl.astype(BF16), wglu_ref[...]) + bglu_ref[...])
            for j, m in enumerate(range(m0, m0 + seg_batch)):
                cols = slice(m * D_MODEL, (m + 1) * D_MODEL)
                o_ref[:, cols] = x_ref[:, cols] + out[j * rt:(j + 1) * rt]


def s5_out(x16, g, d, s_re, s_im, bt, cct, w_glu, b_glu, rt, seg_batch=4):
    rows = x16.shape[0]
    nt = N_LANE_TILES
    width = SSM_FOLD * D_MODEL
    return pl.pallas_call(
        functools.partial(_s5_out_kernel, rt=rt, seg_batch=seg_batch),
        grid=(rows // rt, nt),
        in_specs=[pl.BlockSpec((rt, width), lambda i, t: (i, 0), pipeline_mode=pl.Buffered(1)),
                  _const_spec((1, D_MODEL)),
                  _const_spec((1, D_MODEL)),
                  pl.BlockSpec((rt, STATE_PER_TILE), lambda i, t: (i, t)),
                  pl.BlockSpec((rt, STATE_PER_TILE), lambda i, t: (i, t)),
                  pl.BlockSpec((1, SSM_FOLD // 2, 2 * LANES, 2 * LANES), lambda i, t: (t, 0, 0, 0)),
                  pl.BlockSpec((1, SSM_FOLD * LANES, 2 * STATE_PER_TILE), lambda i, t: (t, 0, 0)),
                  _const_spec((D_MODEL, D_MODEL)),
                  _const_spec((1, D_MODEL))],
        out_specs=pl.BlockSpec((rt, width), lambda i, t: (i, 0), pipeline_mode=pl.Buffered(1)),
        out_shape=jax.ShapeDtypeStruct((rows, width), F32),
        scratch_shapes=[pltpu.VMEM((SSM_FOLD * nt, rt, LANES), BF16),
                        pltpu.VMEM((SSM_FOLD * nt, rt, LANES), F32)],
        compiler_params=_cparams("parallel", "arbitrary"),
        name="s5_out",
    )(x16, g.reshape(1, D_MODEL), d.reshape(1, D_MODEL), s_re, s_im, bt, cct,
      w_glu, b_glu.reshape(1, D_MODEL))


def s5_mixer(x, st_re, st_im, g, ops, d, w_glu, b_glu, rt, scan_width):
    bt, cct, w_re, w_im, a16 = ops
    b, l, dm = x.shape
    r = l // SSM_FOLD
    x16 = x.reshape(b, r, SSM_FOLD * dm).transpose(1, 0, 2).reshape(r * b, SSM_FOLD * dm)
    v_re, v_im = s5_state_in(x16, g, w_re, w_im, rt)
    s_re, s_im, f_re, f_im = s5_scan(v_re, v_im, a16, st_re.reshape(b, N_STATE),
                                     st_im.reshape(b, N_STATE), scan_width)
    y16 = s5_out(x16, g, d, s_re, s_im, bt, cct, w_glu, b_glu, rt)
    y = y16.reshape(r, b, l // r, dm).transpose(1, 0, 2, 3).reshape(b, l, dm)
    shape = (b, SSM_GROUPS, SSM_STATE)
    return y, f_re.reshape(shape), f_im.reshape(shape)


def _stream_tiling(seq_len, n_streams):
    if seq_len >= 512:
        return 1, 512
    return min(n_streams, 256 // seq_len), seq_len


def kernel(x_prompt, x_sample, mem_prompt, state_ssm_re, state_ssm_im, state_ret, cache_mem_k, cache_mem_v, cache_conv, norm_mix, norm_mem_q, norm_mem_kv, norm_ffn, norm_final, ssm_a_re, ssm_a_im, ssm_log_dt, ssm_b_re, ssm_b_im, ssm_c_re, ssm_c_im, ssm_d, ssm_w_glu, ssm_b_glu, ret_w_qkvg, ret_w_o, mem_w_q, mem_w_kv, mem_w_o, ffn_w_up, ffn_conv_w, ffn_conv_b, ffn_w_down):
    bp, lp, dm = x_prompt.shape
    bs, ls, _ = x_sample.shape
    depth = norm_mix.shape[0]
    n_mem = mem_prompt.shape[1]
    d_ff = ffn_w_down.shape[1]
    streams = [dict(x=x_prompt, b=bp, l=lp, tiling=_stream_tiling(lp, bp)),
               dict(x=x_sample, b=bs, l=ls, tiling=_stream_tiling(ls, bs))]
    mem_flat = mem_prompt.reshape(bp * n_mem, dm)
    cache_k = cache_mem_k.reshape(depth, bs, n_mem, dm).astype(BF16)
    cache_v = cache_mem_v.reshape(depth, bs, n_mem, dm).astype(BF16)
    outs = dict(ssm_re=([], []), ssm_im=([], []), ret=([], []), conv=([], []), mk=[], mv=[])

    for i in range(depth):
        j = i // 2
        if i % 2 == 0:
            ops = s5_prepare(ssm_a_re[j], ssm_a_im[j], ssm_log_dt[j], ssm_b_re[j], ssm_b_im[j],
                             ssm_c_re[j], ssm_c_im[j])
            w_glu = ssm_w_glu[j].astype(BF16)
            zeros = jnp.zeros((bp, SSM_GROUPS, SSM_STATE), F32)
            states = [(zeros, zeros), (state_ssm_re[j], state_ssm_im[j])]
            for si, (s, (st_re, st_im)) in enumerate(zip(streams, states)):
                rows = s["b"] * s["l"] // SSM_FOLD
                rt = min(128, rows)
                scan_width = 1024 if s["b"] < SUBLANES else 512
                s["x"], f_re, f_im = s5_mixer(s["x"], st_re, st_im, norm_mix[i], ops, ssm_d[j],
                                              w_glu, ssm_b_glu[j], rt, scan_width)
                outs["ssm_re"][si].append(f_re)
                outs["ssm_im"][si].append(f_im)
        else:
            w_qkvg = ret_w_qkvg[j].astype(BF16)
            w_o = ret_w_o[j].astype(BF16)
            zeros = jnp.zeros((bp, RET_HEADS, RET_DK, RET_DV), F32)
            for si, (s, st, pos0, chunk) in enumerate(zip(streams, [zeros, state_ret[j]],
                                                          [0, PAST_LEN], [256, ls])):
                x2 = s["x"].reshape(s["b"] * s["l"], dm)
                p = rms_matmul(x2, norm_mix[i], w_qkvg, BF16)
                cos, sin = _rope_tables(pos0, s["l"])
                y, st_new = ret_core(p, cos, sin, st, s["l"], min(chunk, s["l"]))
                s["x"] = matmul_res(y, w_o, x2).reshape(s["x"].shape)
                outs["ret"][si].append(st_new)

        kv = rms_matmul(mem_flat, norm_mem_kv[i], mem_w_kv[i].astype(BF16), F32)
        mk = kv[:, :dm].reshape(1, bp, n_mem, dm)
        mv = kv[:, dm:].reshape(1, bp, n_mem, dm)
        outs["mk"].append(mk.reshape(bp, n_mem, MEM_HEADS, MEM_HD))
        outs["mv"].append(mv.reshape(bp, n_mem, MEM_HEADS, MEM_HD))
        w_q = mem_w_q[i].astype(BF16)
        w_o = mem_w_o[i].astype(BF16)
        for s, (k_, v_, layer) in zip(streams, [(mk, mv, 0), (cache_k, cache_v, i)]):
            x2 = s["x"].reshape(s["b"] * s["l"], dm)
            q = rms_matmul(x2, norm_mem_q[i], w_q, BF16)
            o = attn_core(q, k_, v_, layer, s["l"], *s["tiling"])
            s["x"] = matmul_res(o, w_o, x2).reshape(s["x"].shape)

        w_up = ffn_w_up[i].astype(BF16)
        w_dn = ffn_w_down[i].astype(BF16)
        g_final = norm_final if i == depth - 1 else None
        conv_states = [jnp.zeros((bp, 2, d_ff), F32), cache_conv[i]]
        for si, (s, cst) in enumerate(zip(streams, conv_states)):
            x2 = s["x"].reshape(s["b"] * s["l"], dm)
            y, cnew = conv_ffn(x2, norm_ffn[i], w_up, ffn_conv_w[i], ffn_conv_b[i], w_dn, cst,
                               s["l"], *s["tiling"], g_final=g_final)
            s["x"] = y.reshape(s["x"].shape)
            outs["conv"][si].append(cnew)

    return (streams[0]["x"], streams[1]["x"],
            jnp.stack(outs["ssm_re"][0]), jnp.stack(outs["ssm_im"][0]),
            jnp.stack(outs["ssm_re"][1]), jnp.stack(outs["ssm_im"][1]),
            jnp.stack(outs["ret"][0]), jnp.stack(outs["ret"][1]),
            jnp.stack(outs["mk"]), jnp.stack(outs["mv"]),
            jnp.stack(outs["conv"][0]), jnp.stack(outs["conv"][1]))
```

```python
import functools
import math

import numpy as np
import jax
import jax.numpy as jnp
from jax import lax
from jax.experimental import pallas as pl
from jax.experimental.pallas import tpu as pltpu

F32 = jnp.float32
BF16 = jnp.bfloat16

D_MODEL = 1024
PAST_LEN = 1024
EPS = 1e-6
GN_EPS = 1e-5
ROPE_BASE = 10000.0
LANES = 128
SUBLANES = 8
SSM_GROUP = 16
SSM_GROUPS = D_MODEL // SSM_GROUP
SSM_STATE = 64
SSM_FOLD = 16
GROUPS_PER_TILE = LANES // SSM_GROUP
N_LANE_TILES = D_MODEL // LANES
STATE_PER_TILE = GROUPS_PER_TILE * SSM_STATE
N_STATE = SSM_GROUPS * SSM_STATE
RET_HEADS = 4
RET_DK = D_MODEL // RET_HEADS
RET_DV = 2 * D_MODEL // RET_HEADS
RET_QK = RET_HEADS * RET_DK
RET_V = RET_HEADS * RET_DV
RET_LOG_G = tuple(math.log(1.0 - 2.0 ** (-5.0 - h)) for h in range(RET_HEADS))
MEM_HEADS = 4
MEM_HD = D_MODEL // MEM_HEADS
VMEM_LIMIT = 56 * 1024 * 1024
ATTN_TILE_ROWS = 512
FFN_TILE_ROWS = 256


def _cparams(*sem):
    return pltpu.CompilerParams(dimension_semantics=sem, vmem_limit_bytes=VMEM_LIMIT)


def _const_spec(shape):
    nd = len(shape)
    return pl.BlockSpec(shape, lambda *_: (0,) * nd, pipeline_mode=pl.Buffered(1))


def _vmem_specs(n):
    return [pl.BlockSpec(memory_space=pltpu.VMEM)] * n


def _rms_scale(x):
    return lax.rsqrt(jnp.mean(x * x, axis=-1, keepdims=True) + EPS)


def _split_bf16(x):
    hi = x.astype(BF16)
    lo = (x - hi.astype(F32)).astype(BF16)
    return hi, lo


def _dot(a, b):
    return jnp.dot(a, b, preferred_element_type=F32)


def _dot_nt(a, b):
    return lax.dot_general(a, b, (((1,), (1,)), ((), ())), preferred_element_type=F32)


def _rms_matmul_kernel(x_ref, g_ref, w_ref, o_ref, *, tn):
    x = x_ref[...]
    h = (x * _rms_scale(x) * g_ref[...]).astype(BF16)
    for n0 in range(0, w_ref.shape[1], tn):
        o_ref[:, n0:n0 + tn] = _dot(h, w_ref[:, n0:n0 + tn]).astype(o_ref.dtype)


def rms_matmul(x, g, w, out_dtype, tm=512, tn=512):
    t, d = x.shape
    n = w.shape[1]
    tm = min(tm, t)
    return pl.pallas_call(
        functools.partial(_rms_matmul_kernel, tn=tn),
        grid=(t // tm,),
        in_specs=[pl.BlockSpec((tm, d), lambda i: (i, 0)),
                  _const_spec((1, d)),
                  _const_spec((d, n))],
        out_specs=pl.BlockSpec((tm, n), lambda i: (i, 0)),
        out_shape=jax.ShapeDtypeStruct((t, n), out_dtype),
        compiler_params=_cparams("parallel"),
        name="rms_matmul",
    )(x, g.reshape(1, d), w)


def _matmul_res_kernel(y_ref, w_ref, x_ref, o_ref):
    o_ref[...] = x_ref[...] + _dot(y_ref[...], w_ref[...])


def matmul_res(y, w, x, tm=512):
    t, k = y.shape
    d = w.shape[1]
    tm = min(tm, t)
    return pl.pallas_call(
        _matmul_res_kernel,
        grid=(t // tm,),
        in_specs=[pl.BlockSpec((tm, k), lambda i: (i, 0)),
                  _const_spec((k, d)),
                  pl.BlockSpec((tm, d), lambda i: (i, 0))],
        out_specs=pl.BlockSpec((tm, d), lambda i: (i, 0)),
        out_shape=jax.ShapeDtypeStruct((t, d), F32),
        compiler_params=_cparams("parallel"),
        name="matmul_res",
    )(y, w, x)


def _attn_kernel(q_ref, k_ref, v_ref, o_ref, *, nseg, seg):
    scale = MEM_HD ** -0.5
    for s in range(nseg):
        rows = slice(s * seg, (s + 1) * seg)
        for h in range(MEM_HEADS):
            cols = slice(h * MEM_HD, (h + 1) * MEM_HD)
            qh = q_ref[rows, cols]
            kh = k_ref[0, s, :, cols].astype(BF16)
            vh = v_ref[0, s, :, cols].astype(BF16)
            sc = _dot_nt(qh, kh) * scale
            p = jnp.exp(sc - jnp.max(sc, axis=-1, keepdims=True))
            denom = jnp.sum(p, axis=-1, keepdims=True)
            o = _dot(p.astype(BF16), vh) / denom
            o_ref[rows, cols] = o.astype(o_ref.dtype)


def attn_core(q, mk, mv, layer, seq_len, nseg, seg):
    t, d = q.shape
    _, b, n_mem, _ = mk.shape
    tiles_per_stream = max(seq_len // (nseg * seg), 1)
    n_outer = b // nseg
    tm = nseg * seg
    kv_spec = pl.BlockSpec((1, nseg, n_mem, d), lambda o, i: (layer, o, 0, 0))
    return pl.pallas_call(
        functools.partial(_attn_kernel, nseg=nseg, seg=seg),
        grid=(n_outer, tiles_per_stream),
        in_specs=[pl.BlockSpec((tm, d), lambda o, i: (o * tiles_per_stream + i, 0)),
                  kv_spec, kv_spec],
        out_specs=pl.BlockSpec((tm, d), lambda o, i: (o * tiles_per_stream + i, 0)),
        out_shape=jax.ShapeDtypeStruct((t, d), BF16),
        compiler_params=_cparams("parallel", "parallel"),
        name="attn_core",
    )(q, mk, mv)


def _ffn_kernel(x_ref, g_ref, wup_ref, cw_ref, cb_ref, wdn_ref, st_ref, *rest,
                nseg, seg, fc, final_norm):
    if final_norm:
        gf_ref, o_ref, nst_ref, carry_ref = rest
    else:
        o_ref, nst_ref, carry_ref = rest
    d_ff = wdn_ref.shape[0]
    tm = nseg * seg

    @pl.when(pl.program_id(1) == 0)
    def _():
        carry_ref[...] = st_ref[...]

    x = x_ref[...]
    h = (x * _rms_scale(x) * g_ref[...]).astype(BF16)
    row = lax.broadcasted_iota(jnp.int32, (tm, 1), 0) % seg
    o_ref[...] = x

    def up_proj(f0):
        return (_dot(h, wup_ref[:, f0:f0 + fc]), _dot(h, wup_ref[:, d_ff + f0:d_ff + f0 + fc]))

    nxt = up_proj(0)
    pending = None
    for f0 in range(0, d_ff, fc):
        cols = slice(f0, f0 + fc)
        a, gate = nxt
        if f0 + fc < d_ff:
            nxt = up_proj(f0 + fc)
        if pending is not None:
            o_ref[...] += _dot(pending, wdn_ref[f0 - fc:f0, :])
        p0 = jnp.broadcast_to(carry_ref[:, 0:1, cols], (nseg, seg, fc)).reshape(tm, fc)
        p1 = jnp.broadcast_to(carry_ref[:, 1:2, cols], (nseg, seg, fc)).reshape(tm, fc)
        a1 = jnp.where(row == 0, p1, pltpu.roll(a, 1, 0))
        a2 = jnp.where(row == 0, p0, jnp.where(row == 1, p1, pltpu.roll(a, 2, 0)))
        conv = (cw_ref[0:1, cols] * a2 + cw_ref[1:2, cols] * a1 + cw_ref[2:3, cols] * a
                + cb_ref[:, cols])
        p = jax.nn.gelu(conv) * gate
        pending = p.astype(BF16)
        carry_ref[:, :, cols] = a.reshape(nseg, seg, fc)[:, seg - 2:seg, :]
    y = o_ref[...] + _dot(pending, wdn_ref[d_ff - fc:d_ff, :])
    nst_ref[...] = carry_ref[...]
    if final_norm:
        y = y * _rms_scale(y) * gf_ref[...]
    o_ref[...] = y


def conv_ffn(x, g, w_up, conv_w, conv_b, w_down, conv_state, seq_len, nseg, seg, g_final=None,
             fc=256):
    t, d = x.shape
    d_ff = w_down.shape[0]
    b = conv_state.shape[0]
    tiles_per_stream = max(seq_len // (nseg * seg), 1)
    n_outer = b // nseg
    tm = nseg * seg
    final_norm = g_final is not None
    in_specs = [pl.BlockSpec((tm, d), lambda o, i: (o * tiles_per_stream + i, 0)),
                _const_spec((1, d)),
                _const_spec((d, 2 * d_ff)),
                _const_spec((3, d_ff)),
                _const_spec((1, d_ff)),
                _const_spec((d_ff, d)),
                pl.BlockSpec((nseg, 2, d_ff), lambda o, i: (o, 0, 0))]
    args = [x, g.reshape(1, d), w_up, conv_w, conv_b.reshape(1, d_ff), w_down, conv_state]
    if final_norm:
        in_specs.append(_const_spec((1, d)))
        args.append(g_final.reshape(1, d))
    return pl.pallas_call(
        functools.partial(_ffn_kernel, nseg=nseg, seg=seg, fc=fc, final_norm=final_norm),
        grid=(n_outer, tiles_per_stream),
        in_specs=in_specs,
        out_specs=[pl.BlockSpec((tm, d), lambda o, i: (o * tiles_per_stream + i, 0)),
                   pl.BlockSpec((nseg, 2, d_ff), lambda o, i: (o, 0, 0))],
        out_shape=[jax.ShapeDtypeStruct((t, d), F32),
                   jax.ShapeDtypeStruct((b, 2, d_ff), F32)],
        scratch_shapes=[pltpu.VMEM((nseg, 2, d_ff), F32)],
        compiler_params=_cparams("parallel", "arbitrary"),
        name="conv_ffn",
    )(*args)


def _rotate(x, cos, sin):
    half = x.shape[1] // 2
    x1, x2 = x[:, :half], x[:, half:]
    return jnp.concatenate([x1 * cos - x2 * sin, x2 * cos + x1 * sin], axis=1)


def _ret_kernel(p_ref, cos_ref, sin_ref, s_in_ref, y_ref, s_out_ref, *, chunk):
    @pl.when(pl.program_id(1) == 0)
    def _():
        s_out_ref[...] = s_in_ref[...]

    cos = cos_ref[...]
    sin = sin_ref[...]
    li = lax.broadcasted_iota(jnp.int32, (chunk, chunk), 0)
    mi = lax.broadcasted_iota(jnp.int32, (chunk, chunk), 1)
    diff = (li - mi).astype(F32)
    pos = lax.broadcasted_iota(jnp.int32, (chunk, 1), 0).astype(F32)
    for h in range(RET_HEADS):
        log_g = RET_LOG_G[h]
        q = p_ref[:, h * RET_DK:(h + 1) * RET_DK].astype(F32)
        k = p_ref[:, RET_QK + h * RET_DK:RET_QK + (h + 1) * RET_DK].astype(F32)
        v = p_ref[:, 2 * RET_QK + h * RET_DV:2 * RET_QK + (h + 1) * RET_DV]
        g = p_ref[:, 2 * RET_QK + RET_V + h * RET_DV:
                  2 * RET_QK + RET_V + (h + 1) * RET_DV].astype(F32)
        qr = _rotate(q, cos, sin).astype(BF16)
        kr = _rotate(k, cos, sin) * (RET_DK ** -0.5)
        decay = jnp.where(diff >= 0, jnp.exp(jnp.maximum(diff, 0.0) * log_g), 0.0)
        scores = _dot_nt(qr, kr.astype(BF16)) * decay
        intra = _dot(scores.astype(BF16), v)
        s_old = s_out_ref[0, h]
        cross = _dot(qr, s_old.astype(BF16)) * jnp.exp((pos + 1.0) * log_g)
        o = intra + cross
        k_tail = (kr * jnp.exp((chunk - 1.0 - pos) * log_g)).astype(BF16)
        s_out_ref[0, h] = math.exp(chunk * log_g) * s_old + lax.dot_general(
            k_tail, v, (((0,), (0,)), ((), ())), preferred_element_type=F32)
        mu = jnp.mean(o, axis=-1, keepdims=True)
        oc = o - mu
        var = jnp.mean(oc * oc, axis=-1, keepdims=True)
        on = oc * lax.rsqrt(var + GN_EPS)
        y_ref[:, h * RET_DV:(h + 1) * RET_DV] = (jax.nn.silu(g) * on).astype(y_ref.dtype)


def ret_core(p, cos, sin, state, seq_len, chunk):
    t, n = p.shape
    b = state.shape[0]
    nchunks = seq_len // chunk
    half = RET_DK // 2
    return pl.pallas_call(
        functools.partial(_ret_kernel, chunk=chunk),
        grid=(b, nchunks),
        in_specs=[pl.BlockSpec((chunk, n), lambda o, i: (o * nchunks + i, 0)),
                  pl.BlockSpec((chunk, half), lambda o, i: (i, 0)),
                  pl.BlockSpec((chunk, half), lambda o, i: (i, 0)),
                  pl.BlockSpec((1, RET_HEADS, RET_DK, RET_DV), lambda o, i: (o, 0, 0, 0))],
        out_specs=[pl.BlockSpec((chunk, RET_V), lambda o, i: (o * nchunks + i, 0)),
                   pl.BlockSpec((1, RET_HEADS, RET_DK, RET_DV), lambda o, i: (o, 0, 0, 0))],
        out_shape=[jax.ShapeDtypeStruct((t, RET_V), BF16),
                   jax.ShapeDtypeStruct(state.shape, F32)],
        compiler_params=_cparams("parallel", "arbitrary"),
        name="ret_core",
    )(p, cos, sin, state)


def _rope_tables(pos0, length):
    half = RET_DK // 2
    freqs = ROPE_BASE ** (-np.arange(half, dtype=np.float64) / half)
    ang = (pos0 + np.arange(length, dtype=np.float64))[:, None] * freqs[None, :]
    return jnp.asarray(np.cos(ang), F32), jnp.asarray(np.sin(ang), F32)


def _cmul(ar, ai, br, bi):
    return ar * br - ai * bi, ar * bi + ai * br


def _s5_prep_kernel(lam_re_ref, lam_im_ref, ldt_ref, b_re_ref, b_im_ref, c_re_ref, c_im_ref,
                    w_re_ref, w_im_ref, bt_ref, cct_ref, a16_ref):
    dt = jnp.exp(ldt_ref[...])
    lam_re, lam_im = lam_re_ref[...], lam_im_ref[...]
    mag = jnp.exp(lam_re * dt)
    ar, ai = mag * jnp.cos(lam_im * dt), mag * jnp.sin(lam_im * dt)
    den = lam_re * lam_re + lam_im * lam_im
    nr, ni = ar - 1.0, ai
    coef_re = (nr * lam_re + ni * lam_im) / den
    coef_im = (ni * lam_re - nr * lam_im) / den
    bb_re, bb_im = _cmul(coef_re, coef_im, b_re_ref[...], b_im_ref[...])
    c_re, c_im = c_re_ref[...], c_im_ref[...]
    lane = lax.broadcasted_iota(jnp.int32, (1, LANES), 1)
    low = lane < SSM_STATE
    rg = lax.broadcasted_iota(jnp.int32, (LANES, 1), 0) // SSM_GROUP
    pair = rg // 2
    own_half = (lane // SSM_STATE) == (rg % 2)
    same_group = rg == lane // SSM_GROUP
    c_hi, c_lo = _split_bf16(jnp.where(low, c_re, -c_im))
    tk = []
    pr, pi = jnp.ones_like(ar), jnp.zeros_like(ar)
    for j in range(SSM_FOLD):
        zr, zi = _cmul(pr, pi, bb_re, bb_im)
        w_re_ref[SSM_FOLD - 1 - j] = jnp.where(own_half, zr, 0.0)
        w_im_ref[SSM_FOLD - 1 - j] = jnp.where(own_half, zi, 0.0)
        b_hi, b_lo = _split_bf16(jnp.where(low, zr, zi))
        kk = _dot_nt(b_hi, c_hi) + _dot_nt(b_lo, c_hi) + _dot_nt(b_hi, c_lo)
        tk.append(jnp.where(same_group, kk, 0.0).astype(BF16))
        pr, pi = _cmul(pr, pi, ar, ai)
        zr, zi = _cmul(c_re, c_im, pr, pi)
        for ri, z in enumerate((zr, -zi)):
            z = jnp.where(own_half, z, 0.0)
            for t in range(GROUPS_PER_TILE // 2):
                c0 = ri * STATE_PER_TILE + t * LANES
                cct_ref[0, j * LANES:(j + 1) * LANES, c0:c0 + LANES] = (
                    jnp.where(pair == t, z, 0.0).astype(BF16))
    a16_ref[0] = pr
    a16_ref[1] = pi
    zero = jnp.zeros((LANES, LANES), BF16)
    for d2 in range(SSM_FOLD // 2):
        top = jnp.concatenate([tk[2 * d2], tk[2 * d2 + 1]], axis=1)
        bot = jnp.concatenate([tk[2 * d2 - 1] if d2 else zero, tk[2 * d2]], axis=1)
        bt_ref[0, d2] = jnp.concatenate([top, bot], axis=0)


def s5_prepare(a_re, a_im, log_dt, b_re, b_im, c_re, c_im):
    g, p, c = SSM_GROUPS, SSM_STATE, SSM_GROUP
    fold, nt = SSM_FOLD, N_LANE_TILES
    rows = g * c
    twice = lambda x: jnp.tile(x, (1, 2))
    lam = lambda x: twice(jnp.repeat(x, c, axis=0))
    in_spec = pl.BlockSpec((LANES, LANES), lambda t: (t, 0))
    w_spec = pl.BlockSpec((fold, LANES, LANES), lambda t: (0, t, 0))
    w_shape = jax.ShapeDtypeStruct((fold, rows, LANES), F32)
    w_re, w_im, bt, cct, a16 = pl.pallas_call(
        _s5_prep_kernel,
        grid=(nt,),
        in_specs=[in_spec, in_spec, pl.BlockSpec((LANES, 1), lambda t: (t, 0))] + [in_spec] * 4,
        out_specs=[w_spec, w_spec,
                   pl.BlockSpec((1, fold // 2, 2 * LANES, 2 * LANES), lambda t: (t, 0, 0, 0)),
                   pl.BlockSpec((1, fold * LANES, 2 * STATE_PER_TILE), lambda t: (t, 0, 0)),
                   pl.BlockSpec((2, LANES, LANES), lambda t: (0, t, 0))],
        out_shape=[w_shape, w_shape,
                   jax.ShapeDtypeStruct((nt, fold // 2, 2 * LANES, 2 * LANES), BF16),
                   jax.ShapeDtypeStruct((nt, fold * LANES, 2 * STATE_PER_TILE), BF16),
                   jax.ShapeDtypeStruct((2, rows, LANES), F32)],
        compiler_params=_cparams("parallel"),
        name="s5_prep",
    )(lam(a_re), lam(a_im), jnp.repeat(log_dt, c).reshape(rows, 1),
      twice(b_re.transpose(0, 2, 1).reshape(rows, p)), twice(b_im.transpose(0, 2, 1).reshape(rows, p)),
      twice(c_re.reshape(rows, p)), twice(c_im.reshape(rows, p)))
    a16 = a16[:, ::c, :p].reshape(2, 1, N_STATE)
    return bt, cct, w_re, w_im, a16


def _s5_fill_tokens(x_ref, g_ref, u_ref):
    for m in range(SSM_FOLD):
        xm = x_ref[:, m * D_MODEL:(m + 1) * D_MODEL]
        hm = (xm * _rms_scale(xm) * g_ref[...]).astype(BF16)
        for k in range(N_LANE_TILES):
            u_ref[m * N_LANE_TILES + k] = hm[:, k * LANES:(k + 1) * LANES]


def _s5_state_in_kernel(x_ref, g_ref, w_re_ref, w_im_ref, vre_ref, vim_ref, u_ref, wd_ref):
    tile = pl.program_id(1)
    nt = N_LANE_TILES

    @pl.when(tile == 0)
    def _():
        _s5_fill_tokens(x_ref, g_ref, u_ref)

    pair = lax.broadcasted_iota(jnp.int32, (LANES, 1), 0) // SSM_GROUP // 2
    for m in range(SSM_FOLD):
        for ri, ref in enumerate((w_re_ref, w_im_ref)):
            w = ref[m]
            for t in range(GROUPS_PER_TILE // 2):
                c0 = ri * STATE_PER_TILE + t * LANES
                wd_ref[m * LANES:(m + 1) * LANES, c0:c0 + LANES] = (
                    jnp.where(pair == t, w, 0.0).astype(BF16))
    lhs = jnp.concatenate([u_ref[m * nt + tile] for m in range(SSM_FOLD)], axis=1)
    v = _dot(lhs, wd_ref[...])
    vre_ref[...] = v[:, :STATE_PER_TILE]
    vim_ref[...] = v[:, STATE_PER_TILE:]


def s5_state_in(x16, g, w_re, w_im, rt):
    rows = x16.shape[0]
    nt = N_LANE_TILES
    k_rows = SSM_FOLD * LANES
    out = jax.ShapeDtypeStruct((rows, N_STATE), F32)
    w_spec = pl.BlockSpec((SSM_FOLD, LANES, LANES), lambda i, t: (0, t, 0))
    return pl.pallas_call(
        _s5_state_in_kernel,
        grid=(rows // rt, nt),
        in_specs=[pl.BlockSpec((rt, SSM_FOLD * D_MODEL), lambda i, t: (i, 0)),
                  _const_spec((1, D_MODEL)), w_spec, w_spec],
        out_specs=[pl.BlockSpec((rt, STATE_PER_TILE), lambda i, t: (i, t)),
                   pl.BlockSpec((rt, STATE_PER_TILE), lambda i, t: (i, t))],
        out_shape=[out, out],
        scratch_shapes=[pltpu.VMEM((SSM_FOLD * nt, rt, LANES), BF16),
                        pltpu.VMEM((k_rows, 2 * STATE_PER_TILE), BF16)],
        compiler_params=_cparams("parallel", "arbitrary"),
        name="s5_state_in",
    )(x16, g.reshape(1, D_MODEL), w_re, w_im)


def _s5_scan_steps_kernel(vre_ref, vim_ref, a_ref, s0re_ref, s0im_ref,
                          sre_ref, sim_ref, fre_ref, fim_ref, *, streams, steps):
    ar = a_ref[0]
    ai = a_ref[1]

    def body(n, carry):
        sr, si = carry
        rows = pl.ds(pl.multiple_of(n * streams, SUBLANES), streams)
        sre_ref[rows, :] = sr
        sim_ref[rows, :] = si
        return (ar * sr - ai * si + vre_ref[rows, :], ar * si + ai * sr + vim_ref[rows, :])

    sr, si = lax.fori_loop(0, steps, body, (s0re_ref[...], s0im_ref[...]))
    fre_ref[...] = sr
    fim_ref[...] = si


def _s5_scan_stream_kernel(vre_ref, vim_ref, a_ref, s0re_ref, s0im_ref,
                           sre_ref, sim_ref, fre_ref, fim_ref, *, steps):
    q = lax.broadcasted_iota(jnp.int32, (SUBLANES, 1), 0)
    a1 = (a_ref[0], a_ref[1])
    a2 = _cmul(*a1, *a1)
    a4 = _cmul(*a2, *a2)
    a8 = _cmul(*a4, *a4)
    strides = ((1, a1), (2, a2), (4, a4))
    width = a1[0].shape[1]
    pw = (jnp.ones((SUBLANES, width), F32), jnp.zeros((SUBLANES, width), F32))
    for k, ak in strides:
        nxt = _cmul(*pw, *ak)
        bit = (q & k) != 0
        pw = (jnp.where(bit, nxt[0], pw[0]), jnp.where(bit, nxt[1], pw[1]))

    def shifted(x, k):
        return jnp.where(q >= k, pltpu.roll(x, k, 0), 0.0)

    def body(t, carry):
        cr, ci = carry
        rows = pl.ds(pl.multiple_of(t * SUBLANES, SUBLANES), SUBLANES)
        pr, pi = vre_ref[rows, :], vim_ref[rows, :]
        for k, ak in strides:
            dr, di = _cmul(shifted(pr, k), shifted(pi, k), *ak)
            pr, pi = pr + dr, pi + di
        xr, xi = _cmul(*pw, cr, ci)
        sre_ref[rows, :] = xr + shifted(pr, 1)
        sim_ref[rows, :] = xi + shifted(pi, 1)
        nr, ni = _cmul(*a8, cr, ci)
        return nr + pr[SUBLANES - 1:], ni + pi[SUBLANES - 1:]

    cr, ci = lax.fori_loop(0, steps // SUBLANES, body, (s0re_ref[0], s0im_ref[0]))
    fre_ref[0] = cr
    fim_ref[0] = ci


def s5_scan(v_re, v_im, a16, s0_re, s0_im, stream_major, width):
    rows, n = v_re.shape
    streams = s0_re.shape[0]
    steps = rows // streams
    a_spec = pl.BlockSpec((2, 1, width), lambda *i: (0, 0, i[-1]))
    vout = jax.ShapeDtypeStruct((rows, n), F32)
    if stream_major:
        assert steps % SUBLANES == 0
        kern = functools.partial(_s5_scan_stream_kernel, steps=steps)
        grid = (streams, n // width)
        vspec = pl.BlockSpec((steps, width), lambda b, c: (b, c))
        sspec = pl.BlockSpec((1, 1, width), lambda b, c: (b, 0, c))
        s0_re, s0_im = s0_re.reshape(streams, 1, n), s0_im.reshape(streams, 1, n)
        sout = jax.ShapeDtypeStruct((streams, 1, n), F32)
    else:
        assert streams % SUBLANES == 0
        kern = functools.partial(_s5_scan_steps_kernel, streams=streams, steps=steps)
        grid = (n // width,)
        vspec = pl.BlockSpec((rows, width), lambda c: (0, c))
        sspec = pl.BlockSpec((streams, width), lambda c: (0, c))
        sout = jax.ShapeDtypeStruct((streams, n), F32)
    s_re, s_im, f_re, f_im = pl.pallas_call(
        kern,
        grid=grid,
        in_specs=[vspec, vspec, a_spec, sspec, sspec],
        out_specs=[vspec, vspec, sspec, sspec],
        out_shape=[vout, vout, sout, sout],
        compiler_params=_cparams(*(["parallel"] * len(grid))),
        name="s5_scan",
    )(v_re, v_im, a16, s0_re, s0_im)
    return s_re, s_im, f_re.reshape(streams, n), f_im.reshape(streams, n)


def _s5_out_kernel(x_ref, g_ref, d_ref, sre_ref, sim_ref, bt_ref, cct_ref, wglu_ref, bglu_ref,
                   o_ref, u_ref, y_ref, *, rt, seg_batch):
    tile = pl.program_id(1)
    nt = N_LANE_TILES

    @pl.when(tile == 0)
    def _():
        _s5_fill_tokens(x_ref, g_ref, u_ref)

    npair = SSM_FOLD // 2
    u2 = [jnp.concatenate([u_ref[2 * mm * nt + tile], u_ref[(2 * mm + 1) * nt + tile]], axis=1)
          for mm in range(npair)]
    s_prev = jnp.concatenate([sre_ref[...], sim_ref[...]], axis=1).astype(BF16)
    for ll in range(npair):
        acc = _dot_nt(s_prev, cct_ref[0, ll * 2 * LANES:(ll + 1) * 2 * LANES, :])
        for mm in range(ll + 1):
            acc = acc + _dot(u2[mm], bt_ref[0, ll - mm])
        y_ref[2 * ll * nt + tile] = acc[:, :LANES]
        y_ref[(2 * ll + 1) * nt + tile] = acc[:, LANES:]

    @pl.when(tile == nt - 1)
    def _():
        for m0 in range(0, SSM_FOLD, seg_batch):
            gl = []
            for m in range(m0, m0 + seg_batch):
                xm = x_ref[:, m * D_MODEL:(m + 1) * D_MODEL]
                hm = xm * _rms_scale(xm) * g_ref[...]
                ym = jnp.concatenate([y_ref[m * nt + k] for k in range(nt)], axis=1)
                gl.append(jax.nn.gelu(ym + d_ref[...] * hm))
            gl = jnp.concatenate(gl, axis=0)
            out = gl * jax.nn.sigmoid(_dot(gl.astype(BF16), wglu_ref[...]) + bglu_ref[...])
            for j, m in enumerate(range(m0, m0 + seg_batch)):
                cols = slice(m * D_MODEL, (m + 1) * D_MODEL)
                o_ref[:, cols] = x_ref[:, cols] + out[j * rt:(j + 1) * rt]


def s5_out(x16, g, d, s_re, s_im, bt, cct, w_glu, b_glu, rt, seg_batch=4):
    rows = x16.shape[0]
    nt = N_LANE_TILES
    width = SSM_FOLD * D_MODEL
    return pl.pallas_call(
        functools.partial(_s5_out_kernel, rt=rt, seg_batch=seg_batch),
        grid=(rows // rt, nt),
        in_specs=[pl.BlockSpec((rt, width), lambda i, t: (i, 0), pipeline_mode=pl.Buffered(1)),
                  _const_spec((1, D_MODEL)),
                  _const_spec((1, D_MODEL)),
                  pl.BlockSpec((rt, STATE_PER_TILE), lambda i, t: (i, t)),
                  pl.BlockSpec((rt, STATE_PER_TILE), lambda i, t: (i, t)),
                  pl.BlockSpec((1, SSM_FOLD // 2, 2 * LANES, 2 * LANES), lambda i, t: (t, 0, 0, 0)),
                  pl.BlockSpec((1, SSM_FOLD * LANES, 2 * STATE_PER_TILE), lambda i, t: (t, 0, 0)),
                  _const_spec((D_MODEL, D_MODEL)),
                  _const_spec((1, D_MODEL))],
        out_specs=pl.BlockSpec((rt, width), lambda i, t: (i, 0), pipeline_mode=pl.Buffered(1)),
        out_shape=jax.ShapeDtypeStruct((rows, width), F32),
        scratch_shapes=[pltpu.VMEM((SSM_FOLD * nt, rt, LANES), BF16),
                        pltpu.VMEM((SSM_FOLD * nt, rt, LANES), F32)],
        compiler_params=_cparams("parallel", "arbitrary"),
        name="s5_out",
    )(x16, g.reshape(1, D_MODEL), d.reshape(1, D_MODEL), s_re, s_im, bt, cct,
      w_glu, b_glu.reshape(1, D_MODEL))


def s5_mixer(x, st_re, st_im, g, ops, d, w_glu, b_glu, rt, scan_width):
    bt, cct, w_re, w_im, a16 = ops
    b, l, dm = x.shape
    r = l // SSM_FOLD
    stream_major = r % SUBLANES == 0
    x16 = x.reshape(b, r, SSM_FOLD * dm)
    if not stream_major:
        x16 = x16.transpose(1, 0, 2)
    x16 = x16.reshape(r * b, SSM_FOLD * dm)
    v_re, v_im = s5_state_in(x16, g, w_re, w_im, rt)
    s_re, s_im, f_re, f_im = s5_scan(v_re, v_im, a16, st_re.reshape(b, N_STATE),
                                     st_im.reshape(b, N_STATE), stream_major, scan_width)
    y16 = s5_out(x16, g, d, s_re, s_im, bt, cct, w_glu, b_glu, rt)
    if stream_major:
        y = y16.reshape(b, l, dm)
    else:
        y = y16.reshape(r, b, SSM_FOLD, dm).transpose(1, 0, 2, 3).reshape(b, l, dm)
    shape = (b, SSM_GROUPS, SSM_STATE)
    return y, f_re.reshape(shape), f_im.reshape(shape)


def _stream_tiling(seq_len, n_streams, tile_rows):
    if seq_len >= tile_rows:
        return 1, tile_rows
    return min(n_streams, 256 // seq_len), seq_len


def kernel(x_prompt, x_sample, mem_prompt, state_ssm_re, state_ssm_im, state_ret, cache_mem_k, cache_mem_v, cache_conv, norm_mix, norm_mem_q, norm_mem_kv, norm_ffn, norm_final, ssm_a_re, ssm_a_im, ssm_log_dt, ssm_b_re, ssm_b_im, ssm_c_re, ssm_c_im, ssm_d, ssm_w_glu, ssm_b_glu, ret_w_qkvg, ret_w_o, mem_w_q, mem_w_kv, mem_w_o, ffn_w_up, ffn_conv_w, ffn_conv_b, ffn_w_down):
    bp, lp, dm = x_prompt.shape
    bs, ls, _ = x_sample.shape
    depth = norm_mix.shape[0]
    n_mem = mem_prompt.shape[1]
    d_ff = ffn_w_down.shape[1]
    streams = [dict(x=x, b=x.shape[0], l=x.shape[1],
                    attn_tiling=_stream_tiling(x.shape[1], x.shape[0], ATTN_TILE_ROWS),
                    ffn_tiling=_stream_tiling(x.shape[1], x.shape[0], FFN_TILE_ROWS))
               for x in (x_prompt, x_sample)]
    mem_flat = mem_prompt.reshape(bp * n_mem, dm)
    cache_k = cache_mem_k.reshape(depth, bs, n_mem, dm).astype(BF16)
    cache_v = cache_mem_v.reshape(depth, bs, n_mem, dm).astype(BF16)
    outs = dict(ssm_re=([], []), ssm_im=([], []), ret=([], []), conv=([], []), mk=[], mv=[])

    for i in range(depth):
        j = i // 2
        if i % 2 == 0:
            ops = s5_prepare(ssm_a_re[j], ssm_a_im[j], ssm_log_dt[j], ssm_b_re[j], ssm_b_im[j],
                             ssm_c_re[j], ssm_c_im[j])
            w_glu = ssm_w_glu[j].astype(BF16)
            zeros = jnp.zeros((bp, SSM_GROUPS, SSM_STATE), F32)
            states = [(zeros, zeros), (state_ssm_re[j], state_ssm_im[j])]
            for si, (s, (st_re, st_im)) in enumerate(zip(streams, states)):
                rows = s["b"] * s["l"] // SSM_FOLD
                rt = min(128, rows)
                scan_width = 1024 if s["b"] < SUBLANES else 512
                s["x"], f_re, f_im = s5_mixer(s["x"], st_re, st_im, norm_mix[i], ops, ssm_d[j],
                                              w_glu, ssm_b_glu[j], rt, scan_width)
                outs["ssm_re"][si].append(f_re)
                outs["ssm_im"][si].append(f_im)
        else:
            w_qkvg = ret_w_qkvg[j].astype(BF16)
            w_o = ret_w_o[j].astype(BF16)
            zeros = jnp.zeros((bp, RET_HEADS, RET_DK, RET_DV), F32)
            for si, (s, st, pos0, chunk) in enumerate(zip(streams, [zeros, state_ret[j]],
                                                          [0, PAST_LEN], [256, ls])):
                x2 = s["x"].reshape(s["b"] * s["l"], dm)
                p = rms_matmul(x2, norm_mix[i], w_qkvg, BF16)
                cos, sin = _rope_tables(pos0, s["l"])
                y, st_new = ret_core(p, cos, sin, st, s["l"], min(chunk, s["l"]))
                s["x"] = matmul_res(y, w_o, x2).reshape(s["x"].shape)
                outs["ret"][si].append(st_new)

        kv = rms_matmul(mem_flat, norm_mem_kv[i], mem_w_kv[i].astype(BF16), F32)
        mk = kv[:, :dm].reshape(1, bp, n_mem, dm)
        mv = kv[:, dm:].reshape(1, bp, n_mem, dm)
        outs["mk"].append(mk.reshape(bp, n_mem, MEM_HEADS, MEM_HD))
        outs["mv"].append(mv.reshape(bp, n_mem, MEM_HEADS, MEM_HD))
        w_q = mem_w_q[i].astype(BF16)
        w_o = mem_w_o[i].astype(BF16)
        for s, (k_, v_, layer) in zip(streams, [(mk, mv, 0), (cache_k, cache_v, i)]):
            x2 = s["x"].reshape(s["b"] * s["l"], dm)
            q = rms_matmul(x2, norm_mem_q[i], w_q, BF16)
            o = attn_core(q, k_, v_, layer, s["l"], *s["attn_tiling"])
            s["x"] = matmul_res(o, w_o, x2).reshape(s["x"].shape)

        w_up = ffn_w_up[i].astype(BF16)
        w_dn = ffn_w_down[i].astype(BF16)
        g_final = norm_final if i == depth - 1 else None
        conv_states = [jnp.zeros((bp, 2, d_ff), F32), cache_conv[i]]
        for si, (s, cst) in enumerate(zip(streams, conv_states)):
            x2 = s["x"].reshape(s["b"] * s["l"], dm)
            y, cnew = conv_ffn(x2, norm_ffn[i], w_up, ffn_conv_w[i], ffn_conv_b[i], w_dn, cst,
                               s["l"], *s["ffn_tiling"], g_final=g_final)
            s["x"] = y.reshape(s["x"].shape)
            outs["conv"][si].append(cnew)

    return (streams[0]["x"], streams[1]["x"],
            jnp.stack(outs["ssm_re"][0]), jnp.stack(outs["ssm_im"][0]),
            jnp.stack(outs["ssm_re"][1]), jnp.stack(outs["ssm_im"][1]),
            jnp.stack(outs["ret"][0]), jnp.stack(outs["ret"][1]),
            jnp.stack(outs["mk"]), jnp.stack(outs["mv"]),
            jnp.stack(outs["conv"][0]), jnp.stack(outs["conv"][1]))
```

```python
import functools
import math

import numpy as np
import jax
import jax.numpy as jnp
from jax import lax
from jax.experimental import pallas as pl
from jax.experimental.pallas import tpu as pltpu

F32 = jnp.float32
BF16 = jnp.bfloat16

D_MODEL = 1024
PAST_LEN = 1024
EPS = 1e-6
GN_EPS = 1e-5
ROPE_BASE = 10000.0
LANES = 128
SUBLANES = 8
SSM_GROUP = 16
SSM_GROUPS = D_MODEL // SSM_GROUP
SSM_STATE = 64
SSM_FOLD = 16
GROUPS_PER_TILE = LANES // SSM_GROUP
N_LANE_TILES = D_MODEL // LANES
STATE_PER_TILE = GROUPS_PER_TILE * SSM_STATE
N_STATE = SSM_GROUPS * SSM_STATE
RET_HEADS = 4
RET_DK = D_MODEL // RET_HEADS
RET_DV = 2 * D_MODEL // RET_HEADS
RET_QK = RET_HEADS * RET_DK
RET_V = RET_HEADS * RET_DV
RET_LOG_G = tuple(math.log(1.0 - 2.0 ** (-5.0 - h)) for h in range(RET_HEADS))
MEM_HEADS = 4
MEM_HD = D_MODEL // MEM_HEADS
VMEM_LIMIT = 56 * 1024 * 1024
ATTN_TILE_ROWS = 512
FFN_TILE_ROWS = 256
RET_CHUNK_ROWS = 256


def _cparams(*sem):
    return pltpu.CompilerParams(dimension_semantics=sem, vmem_limit_bytes=VMEM_LIMIT)


def _const_spec(shape):
    nd = len(shape)
    return pl.BlockSpec(shape, lambda *_: (0,) * nd, pipeline_mode=pl.Buffered(1))


def _vmem_specs(n):
    return [pl.BlockSpec(memory_space=pltpu.VMEM)] * n


def _rms_scale(x):
    return lax.rsqrt(jnp.mean(x * x, axis=-1, keepdims=True) + EPS)


def _split_bf16(x):
    hi = x.astype(BF16)
    lo = (x - hi.astype(F32)).astype(BF16)
    return hi, lo


def _dot(a, b):
    return jnp.dot(a, b, preferred_element_type=F32)


def _dot_nt(a, b):
    return lax.dot_general(a, b, (((1,), (1,)), ((), ())), preferred_element_type=F32)


def _rms_matmul_kernel(x_ref, g_ref, w_ref, o_ref, *, tn):
    x = x_ref[...]
    h = (x * _rms_scale(x) * g_ref[...]).astype(BF16)
    for n0 in range(0, w_ref.shape[1], tn):
        o_ref[:, n0:n0 + tn] = _dot(h, w_ref[:, n0:n0 + tn]).astype(o_ref.dtype)


def rms_matmul(x, g, w, out_dtype, tm=512, tn=512):
    t, d = x.shape
    n = w.shape[1]
    tm = min(tm, t)
    return pl.pallas_call(
        functools.partial(_rms_matmul_kernel, tn=tn),
        grid=(t // tm,),
        in_specs=[pl.BlockSpec((tm, d), lambda i: (i, 0)),
                  _const_spec((1, d)),
                  _const_spec((d, n))],
        out_specs=pl.BlockSpec((tm, n), lambda i: (i, 0)),
        out_shape=jax.ShapeDtypeStruct((t, n), out_dtype),
        compiler_params=_cparams("parallel"),
        name="rms_matmul",
    )(x, g.reshape(1, d), w)


def _matmul_res_kernel(y_ref, w_ref, x_ref, o_ref):
    o_ref[...] = x_ref[...] + _dot(y_ref[...], w_ref[...])


def matmul_res(y, w, x, tm=512):
    t, k = y.shape
    d = w.shape[1]
    tm = min(tm, t)
    return pl.pallas_call(
        _matmul_res_kernel,
        grid=(t // tm,),
        in_specs=[pl.BlockSpec((tm, k), lambda i: (i, 0)),
                  _const_spec((k, d)),
                  pl.BlockSpec((tm, d), lambda i: (i, 0))],
        out_specs=pl.BlockSpec((tm, d), lambda i: (i, 0)),
        out_shape=jax.ShapeDtypeStruct((t, d), F32),
        compiler_params=_cparams("parallel"),
        name="matmul_res",
    )(y, w, x)


def _mem_attn_kernel(x_ref, g_ref, wq_ref, k_ref, v_ref, wo_ref, o_ref, *, nseg, seg):
    scale = MEM_HD ** -0.5
    x = x_ref[...]
    h = (x * _rms_scale(x) * g_ref[...]).astype(BF16)
    o_ref[...] = x

    def q_proj(hd):
        return _dot(h, wq_ref[:, hd * MEM_HD:(hd + 1) * MEM_HD]).astype(BF16)

    nxt = q_proj(0)
    pending = None
    for hd in range(MEM_HEADS):
        cols = slice(hd * MEM_HD, (hd + 1) * MEM_HD)
        q = nxt
        if hd + 1 < MEM_HEADS:
            nxt = q_proj(hd + 1)
        if pending is not None:
            o_ref[...] += _dot(pending, wo_ref[(hd - 1) * MEM_HD:hd * MEM_HD, :])
        outs = []
        for s in range(nseg):
            kh = k_ref[0, s, :, cols].astype(BF16)
            vh = v_ref[0, s, :, cols].astype(BF16)
            sc = _dot_nt(q[s * seg:(s + 1) * seg], kh) * scale
            p = jnp.exp(sc - jnp.max(sc, axis=-1, keepdims=True))
            denom = jnp.sum(p, axis=-1, keepdims=True)
            outs.append((_dot(p.astype(BF16), vh) / denom).astype(BF16))
        pending = outs[0] if nseg == 1 else jnp.concatenate(outs, axis=0)
    o_ref[...] += _dot(pending, wo_ref[(MEM_HEADS - 1) * MEM_HD:, :])


def mem_attn(x, g, w_q, mk, mv, layer, w_o, seq_len, nseg, seg):
    t, d = x.shape
    _, b, n_mem, _ = mk.shape
    tiles_per_stream = max(seq_len // (nseg * seg), 1)
    n_outer = b // nseg
    tm = nseg * seg
    x_spec = pl.BlockSpec((tm, d), lambda o, i: (o * tiles_per_stream + i, 0))
    kv_spec = pl.BlockSpec((1, nseg, n_mem, d), lambda o, i: (layer, o, 0, 0))
    return pl.pallas_call(
        functools.partial(_mem_attn_kernel, nseg=nseg, seg=seg),
        grid=(n_outer, tiles_per_stream),
        in_specs=[x_spec, _const_spec((1, d)), _const_spec((d, d)), kv_spec, kv_spec,
                  _const_spec((d, d))],
        out_specs=x_spec,
        out_shape=jax.ShapeDtypeStruct((t, d), F32),
        compiler_params=_cparams("parallel", "parallel"),
        name="mem_attn",
    )(x, g.reshape(1, d), w_q, mk, mv, w_o)


def _ffn_kernel(x_ref, g_ref, wup_ref, cw_ref, cb_ref, wdn_ref, st_ref, *rest,
                nseg, seg, fc, final_norm):
    if final_norm:
        gf_ref, o_ref, nst_ref, carry_ref = rest
    else:
        o_ref, nst_ref, carry_ref = rest
    d_ff = wdn_ref.shape[0]
    tm = nseg * seg

    @pl.when(pl.program_id(1) == 0)
    def _():
        carry_ref[...] = st_ref[...]

    x = x_ref[...]
    h = (x * _rms_scale(x) * g_ref[...]).astype(BF16)
    row = lax.broadcasted_iota(jnp.int32, (tm, 1), 0) % seg
    o_ref[...] = x

    def up_proj(f0):
        return (_dot(h, wup_ref[:, f0:f0 + fc]), _dot(h, wup_ref[:, d_ff + f0:d_ff + f0 + fc]))

    nxt = up_proj(0)
    pending = None
    for f0 in range(0, d_ff, fc):
        cols = slice(f0, f0 + fc)
        a, gate = nxt
        if f0 + fc < d_ff:
            nxt = up_proj(f0 + fc)
        if pending is not None:
            o_ref[...] += _dot(pending, wdn_ref[f0 - fc:f0, :])
        p0 = jnp.broadcast_to(carry_ref[:, 0:1, cols], (nseg, seg, fc)).reshape(tm, fc)
        p1 = jnp.broadcast_to(carry_ref[:, 1:2, cols], (nseg, seg, fc)).reshape(tm, fc)
        a1 = jnp.where(row == 0, p1, pltpu.roll(a, 1, 0))
        a2 = jnp.where(row == 0, p0, jnp.where(row == 1, p1, pltpu.roll(a, 2, 0)))
        conv = (cw_ref[0:1, cols] * a2 + cw_ref[1:2, cols] * a1 + cw_ref[2:3, cols] * a
                + cb_ref[:, cols])
        p = jax.nn.gelu(conv) * gate
        pending = p.astype(BF16)
        carry_ref[:, :, cols] = a.reshape(nseg, seg, fc)[:, seg - 2:seg, :]
    y = o_ref[...] + _dot(pending, wdn_ref[d_ff - fc:d_ff, :])
    nst_ref[...] = carry_ref[...]
    if final_norm:
        y = y * _rms_scale(y) * gf_ref[...]
    o_ref[...] = y


def conv_ffn(x, g, w_up, conv_w, conv_b, w_down, conv_state, seq_len, nseg, seg, g_final=None,
             fc=256):
    t, d = x.shape
    d_ff = w_down.shape[0]
    b = conv_state.shape[0]
    tiles_per_stream = max(seq_len // (nseg * seg), 1)
    n_outer = b // nseg
    tm = nseg * seg
    final_norm = g_final is not None
    in_specs = [pl.BlockSpec((tm, d), lambda o, i: (o * tiles_per_stream + i, 0)),
                _const_spec((1, d)),
                _const_spec((d, 2 * d_ff)),
                _const_spec((3, d_ff)),
                _const_spec((1, d_ff)),
                _const_spec((d_ff, d)),
                pl.BlockSpec((nseg, 2, d_ff), lambda o, i: (o, 0, 0))]
    args = [x, g.reshape(1, d), w_up, conv_w, conv_b.reshape(1, d_ff), w_down, conv_state]
    if final_norm:
        in_specs.append(_const_spec((1, d)))
        args.append(g_final.reshape(1, d))
    return pl.pallas_call(
        functools.partial(_ffn_kernel, nseg=nseg, seg=seg, fc=fc, final_norm=final_norm),
        grid=(n_outer, tiles_per_stream),
        in_specs=in_specs,
        out_specs=[pl.BlockSpec((tm, d), lambda o, i: (o * tiles_per_stream + i, 0)),
                   pl.BlockSpec((nseg, 2, d_ff), lambda o, i: (o, 0, 0))],
        out_shape=[jax.ShapeDtypeStruct((t, d), F32),
                   jax.ShapeDtypeStruct((b, 2, d_ff), F32)],
        scratch_shapes=[pltpu.VMEM((nseg, 2, d_ff), F32)],
        compiler_params=_cparams("parallel", "arbitrary"),
        name="conv_ffn",
    )(*args)


def _rotate(x, cos, sin):
    half = x.shape[1] // 2
    x1, x2 = x[:, :half], x[:, half:]
    return jnp.concatenate([x1 * cos - x2 * sin, x2 * cos + x1 * sin], axis=1)


def _ret_head(h, q, k, v, g, cos, sin, s_ref, chunk):
    log_g = RET_LOG_G[h]
    li = lax.broadcasted_iota(jnp.int32, (chunk, chunk), 0)
    mi = lax.broadcasted_iota(jnp.int32, (chunk, chunk), 1)
    diff = (li - mi).astype(F32)
    pos = lax.broadcasted_iota(jnp.int32, (chunk, 1), 0).astype(F32)
    qr = _rotate(q, cos, sin).astype(BF16)
    kr = _rotate(k, cos, sin) * (RET_DK ** -0.5)
    decay = jnp.where(diff >= 0, jnp.exp(jnp.maximum(diff, 0.0) * log_g), 0.0)
    scores = _dot_nt(qr, kr.astype(BF16)) * decay
    intra = _dot(scores.astype(BF16), v)
    s_old = s_ref[0, h]
    cross = _dot(qr, s_old.astype(BF16)) * jnp.exp((pos + 1.0) * log_g)
    o = intra + cross
    k_tail = (kr * jnp.exp((chunk - 1.0 - pos) * log_g)).astype(BF16)
    s_ref[0, h] = math.exp(chunk * log_g) * s_old + lax.dot_general(
        k_tail, v, (((0,), (0,)), ((), ())), preferred_element_type=F32)
    mu = jnp.mean(o, axis=-1, keepdims=True)
    oc = o - mu
    var = jnp.mean(oc * oc, axis=-1, keepdims=True)
    return jax.nn.silu(g) * (oc * lax.rsqrt(var + GN_EPS))


def _ret_kernel(p_ref, cos_ref, sin_ref, s_in_ref, y_ref, s_out_ref, *, chunk):
    @pl.when(pl.program_id(1) == 0)
    def _():
        s_out_ref[...] = s_in_ref[...]

    cos = cos_ref[...]
    sin = sin_ref[...]
    for h in range(RET_HEADS):
        q = p_ref[:, h * RET_DK:(h + 1) * RET_DK].astype(F32)
        k = p_ref[:, RET_QK + h * RET_DK:RET_QK + (h + 1) * RET_DK].astype(F32)
        v = p_ref[:, 2 * RET_QK + h * RET_DV:2 * RET_QK + (h + 1) * RET_DV]
        g = p_ref[:, 2 * RET_QK + RET_V + h * RET_DV:
                  2 * RET_QK + RET_V + (h + 1) * RET_DV].astype(F32)
        y = _ret_head(h, q, k, v, g, cos, sin, s_out_ref, chunk)
        y_ref[:, h * RET_DV:(h + 1) * RET_DV] = y.astype(y_ref.dtype)


def _ret_fused_kernel(x_ref, g_ref, w_ref, wo_ref, cos_ref, sin_ref, s_in_ref, o_ref, s_out_ref,
                      *, chunk):
    @pl.when(pl.program_id(1) == 0)
    def _():
        s_out_ref[...] = s_in_ref[...]

    cos = cos_ref[...]
    sin = sin_ref[...]
    x = x_ref[...]
    hn = (x * _rms_scale(x) * g_ref[...]).astype(BF16)
    o_ref[...] = x

    def proj(h):
        c = [h * RET_DK, RET_QK + h * RET_DK, 2 * RET_QK + h * RET_DV, 2 * RET_QK + RET_V + h * RET_DV]
        return (_dot(hn, w_ref[:, c[0]:c[0] + RET_DK]), _dot(hn, w_ref[:, c[1]:c[1] + RET_DK]),
                _dot(hn, w_ref[:, c[2]:c[2] + RET_DV]).astype(BF16),
                _dot(hn, w_ref[:, c[3]:c[3] + RET_DV]))

    nxt = proj(0)
    pending = None
    for h in range(RET_HEADS):
        q, k, v, g = nxt
        if h + 1 < RET_HEADS:
            nxt = proj(h + 1)
        if pending is not None:
            o_ref[...] += _dot(pending, wo_ref[(h - 1) * RET_DV:h * RET_DV, :])
        pending = _ret_head(h, q, k, v, g, cos, sin, s_out_ref, chunk).astype(BF16)
    o_ref[...] += _dot(pending, wo_ref[(RET_HEADS - 1) * RET_DV:, :])


def _ret_specs(seq_len, chunk):
    nchunks = seq_len // chunk
    half = RET_DK // 2
    row_spec = lambda n: pl.BlockSpec((chunk, n), lambda o, i: (o * nchunks + i, 0))
    rope_spec = pl.BlockSpec((chunk, half), lambda o, i: (i, 0))
    state_spec = pl.BlockSpec((1, RET_HEADS, RET_DK, RET_DV), lambda o, i: (o, 0, 0, 0))
    return nchunks, row_spec, rope_spec, state_spec


def ret_core(p, cos, sin, state, seq_len, chunk):
    t, n = p.shape
    nchunks, row_spec, rope_spec, state_spec = _ret_specs(seq_len, chunk)
    return pl.pallas_call(
        functools.partial(_ret_kernel, chunk=chunk),
        grid=(state.shape[0], nchunks),
        in_specs=[row_spec(n), rope_spec, rope_spec, state_spec],
        out_specs=[row_spec(RET_V), state_spec],
        out_shape=[jax.ShapeDtypeStruct((t, RET_V), BF16),
                   jax.ShapeDtypeStruct(state.shape, F32)],
        compiler_params=_cparams("parallel", "arbitrary"),
        name="ret_core",
    )(p, cos, sin, state)


def ret_mixer(x, g, w_qkvg, w_o, cos, sin, state, seq_len, chunk):
    t, d = x.shape
    nchunks, row_spec, rope_spec, state_spec = _ret_specs(seq_len, chunk)
    return pl.pallas_call(
        functools.partial(_ret_fused_kernel, chunk=chunk),
        grid=(state.shape[0], nchunks),
        in_specs=[row_spec(d), _const_spec((1, d)), _const_spec(w_qkvg.shape),
                  _const_spec(w_o.shape), rope_spec, rope_spec, state_spec],
        out_specs=[row_spec(d), state_spec],
        out_shape=[jax.ShapeDtypeStruct((t, d), F32), jax.ShapeDtypeStruct(state.shape, F32)],
        compiler_params=_cparams("parallel", "arbitrary"),
        name="ret_mixer",
    )(x, g.reshape(1, d), w_qkvg, w_o, cos, sin, state)


def _rope_tables(pos0, length):
    half = RET_DK // 2
    freqs = ROPE_BASE ** (-np.arange(half, dtype=np.float64) / half)
    ang = (pos0 + np.arange(length, dtype=np.float64))[:, None] * freqs[None, :]
    return jnp.asarray(np.cos(ang), F32), jnp.asarray(np.sin(ang), F32)


def _cmul(ar, ai, br, bi):
    return ar * br - ai * bi, ar * bi + ai * br


def _s5_prep_kernel(lam_re_ref, lam_im_ref, ldt_ref, b_re_ref, b_im_ref, c_re_ref, c_im_ref,
                    w_re_ref, w_im_ref, bt_ref, cct_ref, a16_ref):
    dt = jnp.exp(ldt_ref[...])
    lam_re, lam_im = lam_re_ref[...], lam_im_ref[...]
    mag = jnp.exp(lam_re * dt)
    ar, ai = mag * jnp.cos(lam_im * dt), mag * jnp.sin(lam_im * dt)
    den = lam_re * lam_re + lam_im * lam_im
    nr, ni = ar - 1.0, ai
    coef_re = (nr * lam_re + ni * lam_im) / den
    coef_im = (ni * lam_re - nr * lam_im) / den
    bb_re, bb_im = _cmul(coef_re, coef_im, b_re_ref[...], b_im_ref[...])
    c_re, c_im = c_re_ref[...], c_im_ref[...]
    lane = lax.broadcasted_iota(jnp.int32, (1, LANES), 1)
    low = lane < SSM_STATE
    rg = lax.broadcasted_iota(jnp.int32, (LANES, 1), 0) // SSM_GROUP
    pair = rg // 2
    own_half = (lane // SSM_STATE) == (rg % 2)
    same_group = rg == lane // SSM_GROUP
    c_hi, c_lo = _split_bf16(jnp.where(low, c_re, -c_im))
    tk = []
    pr, pi = jnp.ones_like(ar), jnp.zeros_like(ar)
    for j in range(SSM_FOLD):
        zr, zi = _cmul(pr, pi, bb_re, bb_im)
        w_re_ref[SSM_FOLD - 1 - j] = jnp.where(own_half, zr, 0.0)
        w_im_ref[SSM_FOLD - 1 - j] = jnp.where(own_half, zi, 0.0)
        b_hi, b_lo = _split_bf16(jnp.where(low, zr, zi))
        kk = _dot_nt(b_hi, c_hi) + _dot_nt(b_lo, c_hi) + _dot_nt(b_hi, c_lo)
        tk.append(jnp.where(same_group, kk, 0.0).astype(BF16))
        pr, pi = _cmul(pr, pi, ar, ai)
        zr, zi = _cmul(c_re, c_im, pr, pi)
        for ri, z in enumerate((zr, -zi)):
            z = jnp.where(own_half, z, 0.0)
            for t in range(GROUPS_PER_TILE // 2):
                c0 = ri * STATE_PER_TILE + t * LANES
                cct_ref[0, j * LANES:(j + 1) * LANES, c0:c0 + LANES] = (
                    jnp.where(pair == t, z, 0.0).astype(BF16))
    a16_ref[0] = pr
    a16_ref[1] = pi
    zero = jnp.zeros((LANES, LANES), BF16)
    for d2 in range(SSM_FOLD // 2):
        top = jnp.concatenate([tk[2 * d2], tk[2 * d2 + 1]], axis=1)
        bot = jnp.concatenate([tk[2 * d2 - 1] if d2 else zero, tk[2 * d2]], axis=1)
        bt_ref[0, d2] = jnp.concatenate([top, bot], axis=0)


def s5_prepare(a_re, a_im, log_dt, b_re, b_im, c_re, c_im):
    g, p, c = SSM_GROUPS, SSM_STATE, SSM_GROUP
    fold, nt = SSM_FOLD, N_LANE_TILES
    rows = g * c
    twice = lambda x: jnp.tile(x, (1, 2))
    lam = lambda x: twice(jnp.repeat(x, c, axis=0))
    in_spec = pl.BlockSpec((LANES, LANES), lambda t: (t, 0))
    w_spec = pl.BlockSpec((fold, LANES, LANES), lambda t: (0, t, 0))
    w_shape = jax.ShapeDtypeStruct((fold, rows, LANES), F32)
    w_re, w_im, bt, cct, a16 = pl.pallas_call(
        _s5_prep_kernel,
        grid=(nt,),
        in_specs=[in_spec, in_spec, pl.BlockSpec((LANES, 1), lambda t: (t, 0))] + [in_spec] * 4,
        out_specs=[w_spec, w_spec,
                   pl.BlockSpec((1, fold // 2, 2 * LANES, 2 * LANES), lambda t: (t, 0, 0, 0)),
                   pl.BlockSpec((1, fold * LANES, 2 * STATE_PER_TILE), lambda t: (t, 0, 0)),
                   pl.BlockSpec((2, LANES, LANES), lambda t: (0, t, 0))],
        out_shape=[w_shape, w_shape,
                   jax.ShapeDtypeStruct((nt, fold // 2, 2 * LANES, 2 * LANES), BF16),
                   jax.ShapeDtypeStruct((nt, fold * LANES, 2 * STATE_PER_TILE), BF16),
                   jax.ShapeDtypeStruct((2, rows, LANES), F32)],
        compiler_params=_cparams("parallel"),
        name="s5_prep",
    )(lam(a_re), lam(a_im), jnp.repeat(log_dt, c).reshape(rows, 1),
      twice(b_re.transpose(0, 2, 1).reshape(rows, p)), twice(b_im.transpose(0, 2, 1).reshape(rows, p)),
      twice(c_re.reshape(rows, p)), twice(c_im.reshape(rows, p)))
    a16 = a16[:, ::c, :p].reshape(2, 1, N_STATE)
    return bt, cct, w_re, w_im, a16


def _s5_fill_tokens(x_ref, g_ref, u_ref):
    for m in range(SSM_FOLD):
        xm = x_ref[:, m * D_MODEL:(m + 1) * D_MODEL]
        hm = (xm * _rms_scale(xm) * g_ref[...]).astype(BF16)
        for k in range(N_LANE_TILES):
            u_ref[m * N_LANE_TILES + k] = hm[:, k * LANES:(k + 1) * LANES]


def _s5_state_in_kernel(x_ref, g_ref, w_re_ref, w_im_ref, vre_ref, vim_ref, u_ref, wd_ref):
    tile = pl.program_id(1)
    nt = N_LANE_TILES

    @pl.when(tile == 0)
    def _():
        _s5_fill_tokens(x_ref, g_ref, u_ref)

    pair = lax.broadcasted_iota(jnp.int32, (LANES, 1), 0) // SSM_GROUP // 2
    for m in range(SSM_FOLD):
        for ri, ref in enumerate((w_re_ref, w_im_ref)):
            w = ref[m]
            for t in range(GROUPS_PER_TILE // 2):
                c0 = ri * STATE_PER_TILE + t * LANES
                wd_ref[m * LANES:(m + 1) * LANES, c0:c0 + LANES] = (
                    jnp.where(pair == t, w, 0.0).astype(BF16))
    lhs = jnp.concatenate([u_ref[m * nt + tile] for m in range(SSM_FOLD)], axis=1)
    v = _dot(lhs, wd_ref[...])
    vre_ref[...] = v[:, :STATE_PER_TILE]
    vim_ref[...] = v[:, STATE_PER_TILE:]


def s5_state_in(x16, g, w_re, w_im, rt):
    rows = x16.shape[0]
    nt = N_LANE_TILES
    k_rows = SSM_FOLD * LANES
    out = jax.ShapeDtypeStruct((rows, N_STATE), F32)
    w_spec = pl.BlockSpec((SSM_FOLD, LANES, LANES), lambda i, t: (0, t, 0))
    return pl.pallas_call(
        _s5_state_in_kernel,
        grid=(rows // rt, nt),
        in_specs=[pl.BlockSpec((rt, SSM_FOLD * D_MODEL), lambda i, t: (i, 0)),
                  _const_spec((1, D_MODEL)), w_spec, w_spec],
        out_specs=[pl.BlockSpec((rt, STATE_PER_TILE), lambda i, t: (i, t)),
                   pl.BlockSpec((rt, STATE_PER_TILE), lambda i, t: (i, t))],
        out_shape=[out, out],
        scratch_shapes=[pltpu.VMEM((SSM_FOLD * nt, rt, LANES), BF16),
                        pltpu.VMEM((k_rows, 2 * STATE_PER_TILE), BF16)],
        compiler_params=_cparams("parallel", "arbitrary"),
        name="s5_state_in",
    )(x16, g.reshape(1, D_MODEL), w_re, w_im)


def _s5_scan_steps_kernel(vre_ref, vim_ref, a_ref, s0re_ref, s0im_ref,
                          sre_ref, sim_ref, fre_ref, fim_ref, *, streams, steps):
    ar = a_ref[0]
    ai = a_ref[1]

    def body(n, carry):
        sr, si = carry
        rows = pl.ds(pl.multiple_of(n * streams, SUBLANES), streams)
        sre_ref[rows, :] = sr
        sim_ref[rows, :] = si
        return (ar * sr - ai * si + vre_ref[rows, :], ar * si + ai * sr + vim_ref[rows, :])

    sr, si = lax.fori_loop(0, steps, body, (s0re_ref[...], s0im_ref[...]))
    fre_ref[...] = sr
    fim_ref[...] = si


def _s5_scan_stream_kernel(vre_ref, vim_ref, a_ref, s0re_ref, s0im_ref,
                           sre_ref, sim_ref, fre_ref, fim_ref, *, steps):
    q = lax.broadcasted_iota(jnp.int32, (SUBLANES, 1), 0)
    a1 = (a_ref[0], a_ref[1])
    a2 = _cmul(*a1, *a1)
    a4 = _cmul(*a2, *a2)
    a8 = _cmul(*a4, *a4)
    strides = ((1, a1), (2, a2), (4, a4))
    width = a1[0].shape[1]
    pw = (jnp.ones((SUBLANES, width), F32), jnp.zeros((SUBLANES, width), F32))
    for k, ak in strides:
        nxt = _cmul(*pw, *ak)
        bit = (q & k) != 0
        pw = (jnp.where(bit, nxt[0], pw[0]), jnp.where(bit, nxt[1], pw[1]))

    def shifted(x, k):
        return jnp.where(q >= k, pltpu.roll(x, k, 0), 0.0)

    def body(t, carry):
        cr, ci = carry
        rows = pl.ds(pl.multiple_of(t * SUBLANES, SUBLANES), SUBLANES)
        pr, pi = vre_ref[rows, :], vim_ref[rows, :]
        for k, ak in strides:
            dr, di = _cmul(shifted(pr, k), shifted(pi, k), *ak)
            pr, pi = pr + dr, pi + di
        xr, xi = _cmul(*pw, cr, ci)
        sre_ref[rows, :] = xr + shifted(pr, 1)
        sim_ref[rows, :] = xi + shifted(pi, 1)
        nr, ni = _cmul(*a8, cr, ci)
        return nr + pr[SUBLANES - 1:], ni + pi[SUBLANES - 1:]

    cr, ci = lax.fori_loop(0, steps // SUBLANES, body, (s0re_ref[0], s0im_ref[0]))
    fre_ref[0] = cr
    fim_ref[0] = ci


def s5_scan(v_re, v_im, a16, s0_re, s0_im, stream_major, width):
    rows, n = v_re.shape
    streams = s0_re.shape[0]
    steps = rows // streams
    a_spec = pl.BlockSpec((2, 1, width), lambda *i: (0, 0, i[-1]))
    vout = jax.ShapeDtypeStruct((rows, n), F32)
    if stream_major:
        assert steps % SUBLANES == 0
        kern = functools.partial(_s5_scan_stream_kernel, steps=steps)
        grid = (streams, n // width)
        vspec = pl.BlockSpec((steps, width), lambda b, c: (b, c))
        sspec = pl.BlockSpec((1, 1, width), lambda b, c: (b, 0, c))
        s0_re, s0_im = s0_re.reshape(streams, 1, n), s0_im.reshape(streams, 1, n)
        sout = jax.ShapeDtypeStruct((streams, 1, n), F32)
    else:
        assert streams % SUBLANES == 0
        kern = functools.partial(_s5_scan_steps_kernel, streams=streams, steps=steps)
        grid = (n // width,)
        vspec = pl.BlockSpec((rows, width), lambda c: (0, c))
        sspec = pl.BlockSpec((streams, width), lambda c: (0, c))
        sout = jax.ShapeDtypeStruct((streams, n), F32)
    s_re, s_im, f_re, f_im = pl.pallas_call(
        kern,
        grid=grid,
        in_specs=[vspec, vspec, a_spec, sspec, sspec],
        out_specs=[vspec, vspec, sspec, sspec],
        out_shape=[vout, vout, sout, sout],
        compiler_params=_cparams(*(["parallel"] * len(grid))),
        name="s5_scan",
    )(v_re, v_im, a16, s0_re, s0_im)
    return s_re, s_im, f_re.reshape(streams, n), f_im.reshape(streams, n)


def _s5_out_kernel(x_ref, g_ref, d_ref, sre_ref, sim_ref, bt_ref, cct_ref, wglu_ref, bglu_ref,
                   o_ref, u_ref, y_ref, *, rt, seg_batch):
    tile = pl.program_id(1)
    nt = N_LANE_TILES

    @pl.when(tile == 0)
    def _():
        _s5_fill_tokens(x_ref, g_ref, u_ref)

    npair = SSM_FOLD // 2
    u2 = [jnp.concatenate([u_ref[2 * mm * nt + tile], u_ref[(2 * mm + 1) * nt + tile]], axis=1)
          for mm in range(npair)]
    s_prev = jnp.concatenate([sre_ref[...], sim_ref[...]], axis=1).astype(BF16)
    for ll in range(npair):
        acc = _dot_nt(s_prev, cct_ref[0, ll * 2 * LANES:(ll + 1) * 2 * LANES, :])
        for mm in range(ll + 1):
            acc = acc + _dot(u2[mm], bt_ref[0, ll - mm])
        y_ref[2 * ll * nt + tile] = acc[:, :LANES]
        y_ref[(2 * ll + 1) * nt + tile] = acc[:, LANES:]

    @pl.when(tile == nt - 1)
    def _():
        for m0 in range(0, SSM_FOLD, seg_batch):
            gl = []
            for m in range(m0, m0 + seg_batch):
                xm = x_ref[:, m * D_MODEL:(m + 1) * D_MODEL]
                hm = xm * _rms_scale(xm) * g_ref[...]
                ym = jnp.concatenate([y_ref[m * nt + k] for k in range(nt)], axis=1)
                gl.append(jax.nn.gelu(ym + d_ref[...] * hm))
            gl = jnp.concatenate(gl, axis=0)
            out = gl * jax.nn.sigmoid(_dot(gl.astype(BF16), wglu_ref[...]) + bglu_ref[...])
            for j, m in enumerate(range(m0, m0 + seg_batch)):
                cols = slice(m * D_MODEL, (m + 1) * D_MODEL)
                o_ref[:, cols] = x_ref[:, cols] + out[j * rt:(j + 1) * rt]


def s5_out(x16, g, d, s_re, s_im, bt, cct, w_glu, b_glu, rt, seg_batch=4):
    rows = x16.shape[0]
    nt = N_LANE_TILES
    width = SSM_FOLD * D_MODEL
    return pl.pallas_call(
        functools.partial(_s5_out_kernel, rt=rt, seg_batch=seg_batch),
        grid=(rows // rt, nt),
        in_specs=[pl.BlockSpec((rt, width), lambda i, t: (i, 0), pipeline_mode=pl.Buffered(1)),
                  _const_spec((1, D_MODEL)),
                  _const_spec((1, D_MODEL)),
                  pl.BlockSpec((rt, STATE_PER_TILE), lambda i, t: (i, t)),
                  pl.BlockSpec((rt, STATE_PER_TILE), lambda i, t: (i, t)),
                  pl.BlockSpec((1, SSM_FOLD // 2, 2 * LANES, 2 * LANES), lambda i, t: (t, 0, 0, 0)),
                  pl.BlockSpec((1, SSM_FOLD * LANES, 2 * STATE_PER_TILE), lambda i, t: (t, 0, 0)),
                  _const_spec((D_MODEL, D_MODEL)),
                  _const_spec((1, D_MODEL))],
        out_specs=pl.BlockSpec((rt, width), lambda i, t: (i, 0), pipeline_mode=pl.Buffered(1)),
        out_shape=jax.ShapeDtypeStruct((rows, width), F32),
        scratch_shapes=[pltpu.VMEM((SSM_FOLD * nt, rt, LANES), BF16),
                        pltpu.VMEM((SSM_FOLD * nt, rt, LANES), F32)],
        compiler_params=_cparams("parallel", "arbitrary"),
        name="s5_out",
    )(x16, g.reshape(1, D_MODEL), d.reshape(1, D_MODEL), s_re, s_im, bt, cct,
      w_glu, b_glu.reshape(1, D_MODEL))


def s5_mixer(x, st_re, st_im, g, ops, d, w_glu, b_glu, rt, scan_width):
    bt, cct, w_re, w_im, a16 = ops
    b, l, dm = x.shape
    r = l // SSM_FOLD
    stream_major = r % SUBLANES == 0
    x16 = x.reshape(b, r, SSM_FOLD * dm)
    if not stream_major:
        x16 = x16.transpose(1, 0, 2)
    x16 = x16.reshape(r * b, SSM_FOLD * dm)
    v_re, v_im = s5_state_in(x16, g, w_re, w_im, rt)
    s_re, s_im, f_re, f_im = s5_scan(v_re, v_im, a16, st_re.reshape(b, N_STATE),
                                     st_im.reshape(b, N_STATE), stream_major, scan_width)
    y16 = s5_out(x16, g, d, s_re, s_im, bt, cct, w_glu, b_glu, rt)
    if stream_major:
        y = y16.reshape(b, l, dm)
    else:
        y = y16.reshape(r, b, SSM_FOLD, dm).transpose(1, 0, 2, 3).reshape(b, l, dm)
    shape = (b, SSM_GROUPS, SSM_STATE)
    return y, f_re.reshape(shape), f_im.reshape(shape)


def _stream_tiling(seq_len, n_streams, tile_rows):
    if seq_len >= tile_rows:
        return 1, tile_rows
    return min(n_streams, 256 // seq_len), seq_len


def kernel(x_prompt, x_sample, mem_prompt, state_ssm_re, state_ssm_im, state_ret, cache_mem_k, cache_mem_v, cache_conv, norm_mix, norm_mem_q, norm_mem_kv, norm_ffn, norm_final, ssm_a_re, ssm_a_im, ssm_log_dt, ssm_b_re, ssm_b_im, ssm_c_re, ssm_c_im, ssm_d, ssm_w_glu, ssm_b_glu, ret_w_qkvg, ret_w_o, mem_w_q, mem_w_kv, mem_w_o, ffn_w_up, ffn_conv_w, ffn_conv_b, ffn_w_down):
    bp, lp, dm = x_prompt.shape
    bs, ls, _ = x_sample.shape
    depth = norm_mix.shape[0]
    n_mem = mem_prompt.shape[1]
    d_ff = ffn_w_down.shape[1]
    streams = [dict(x=x, b=x.shape[0], l=x.shape[1],
                    attn_tiling=_stream_tiling(x.shape[1], x.shape[0], ATTN_TILE_ROWS),
                    ffn_tiling=_stream_tiling(x.shape[1], x.shape[0], FFN_TILE_ROWS))
               for x in (x_prompt, x_sample)]
    mem_flat = mem_prompt.reshape(bp * n_mem, dm)
    cache_k = cache_mem_k.reshape(depth, bs, n_mem, dm).astype(BF16)
    cache_v = cache_mem_v.reshape(depth, bs, n_mem, dm).astype(BF16)
    outs = dict(ssm_re=([], []), ssm_im=([], []), ret=([], []), conv=([], []), mk=[], mv=[])

    for i in range(depth):
        j = i // 2
        if i % 2 == 0:
            ops = s5_prepare(ssm_a_re[j], ssm_a_im[j], ssm_log_dt[j], ssm_b_re[j], ssm_b_im[j],
                             ssm_c_re[j], ssm_c_im[j])
            w_glu = ssm_w_glu[j].astype(BF16)
            zeros = jnp.zeros((bp, SSM_GROUPS, SSM_STATE), F32)
            states = [(zeros, zeros), (state_ssm_re[j], state_ssm_im[j])]
            for si, (s, (st_re, st_im)) in enumerate(zip(streams, states)):
                rows = s["b"] * s["l"] // SSM_FOLD
                rt = min(128, rows)
                scan_width = 1024 if s["b"] < SUBLANES else 512
                s["x"], f_re, f_im = s5_mixer(s["x"], st_re, st_im, norm_mix[i], ops, ssm_d[j],
                                              w_glu, ssm_b_glu[j], rt, scan_width)
                outs["ssm_re"][si].append(f_re)
                outs["ssm_im"][si].append(f_im)
        else:
            w_qkvg = ret_w_qkvg[j].astype(BF16)
            w_o = ret_w_o[j].astype(BF16)
            zeros = jnp.zeros((bp, RET_HEADS, RET_DK, RET_DV), F32)
            chunk = RET_CHUNK_ROWS
            for si, (s, st, pos0) in enumerate(zip(streams, [zeros, state_ret[j]], [0, PAST_LEN])):
                x2 = s["x"].reshape(s["b"] * s["l"], dm)
                cos, sin = _rope_tables(pos0, s["l"])
                if s["l"] >= chunk:
                    y, st_new = ret_mixer(x2, norm_mix[i], w_qkvg, w_o, cos, sin, st, s["l"], chunk)
                else:
                    p = rms_matmul(x2, norm_mix[i], w_qkvg, BF16)
                    y, st_new = ret_core(p, cos, sin, st, s["l"], s["l"])
                    y = matmul_res(y, w_o, x2)
                s["x"] = y.reshape(s["x"].shape)
                outs["ret"][si].append(st_new)

        kv = rms_matmul(mem_flat, norm_mem_kv[i], mem_w_kv[i].astype(BF16), F32)
        mk = kv[:, :dm].reshape(1, bp, n_mem, dm)
        mv = kv[:, dm:].reshape(1, bp, n_mem, dm)
        outs["mk"].append(mk.reshape(bp, n_mem, MEM_HEADS, MEM_HD))
        outs["mv"].append(mv.reshape(bp, n_mem, MEM_HEADS, MEM_HD))
        w_q = mem_w_q[i].astype(BF16)
        w_o = mem_w_o[i].astype(BF16)
        for s, (k_, v_, layer) in zip(streams, [(mk, mv, 0), (cache_k, cache_v, i)]):
            x2 = s["x"].reshape(s["b"] * s["l"], dm)
            y = mem_attn(x2, norm_mem_q[i], w_q, k_, v_, layer, w_o, s["l"], *s["attn_tiling"])
            s["x"] = y.reshape(s["x"].shape)

        w_up = ffn_w_up[i].astype(BF16)
        w_dn = ffn_w_down[i].astype(BF16)
        g_final = norm_final if i == depth - 1 else None
        conv_states = [jnp.zeros((bp, 2, d_ff), F32), cache_conv[i]]
        for si, (s, cst) in enumerate(zip(streams, conv_states)):
            x2 = s["x"].reshape(s["b"] * s["l"], dm)
            y, cnew = conv_ffn(x2, norm_ffn[i], w_up, ffn_conv_w[i], ffn_conv_b[i], w_dn, cst,
                               s["l"], *s["ffn_tiling"], g_final=g_final)
            s["x"] = y.reshape(s["x"].shape)
            outs["conv"][si].append(cnew)

    return (streams[0]["x"], streams[1]["x"],
            jnp.stack(outs["ssm_re"][0]), jnp.stack(outs["ssm_im"][0]),
            jnp.stack(outs["ssm_re"][1]), jnp.stack(outs["ssm_im"][1]),
            jnp.stack(outs["ret"][0]), jnp.stack(outs["ret"][1]),
            jnp.stack(outs["mk"]), jnp.stack(outs["mv"]),
            jnp.stack(outs["conv"][0]), jnp.stack(outs["conv"][1]))
```

```python
import functools
import math

import numpy as np
import jax
import jax.numpy as jnp
from jax import lax
from jax.experimental import pallas as pl
from jax.experimental.pallas import tpu as pltpu

F32 = jnp.float32
BF16 = jnp.bfloat16

D_MODEL = 1024
PAST_LEN = 1024
EPS = 1e-6
GN_EPS = 1e-5
ROPE_BASE = 10000.0
LANES = 128
SUBLANES = 8
SSM_GROUP = 16
SSM_GROUPS = D_MODEL // SSM_GROUP
SSM_STATE = 64
SSM_FOLD = 16
GROUPS_PER_TILE = LANES // SSM_GROUP
N_LANE_TILES = D_MODEL // LANES
STATE_PER_TILE = GROUPS_PER_TILE * SSM_STATE
N_STATE = SSM_GROUPS * SSM_STATE
RET_HEADS = 4
RET_DK = D_MODEL // RET_HEADS
RET_DV = 2 * D_MODEL // RET_HEADS
RET_QK = RET_HEADS * RET_DK
RET_V = RET_HEADS * RET_DV
RET_LOG_G = tuple(math.log(1.0 - 2.0 ** (-5.0 - h)) for h in range(RET_HEADS))
MEM_HEADS = 4
MEM_HD = D_MODEL // MEM_HEADS
VMEM_LIMIT = 56 * 1024 * 1024
ATTN_TILE_ROWS = 512
FFN_TILE_ROWS = 256
RET_CHUNK_ROWS = 256
S5_TOKEN_TILE = 512
S5_ROW_TILE = 512
S5_SCAN_WIDTH = 1024


def _cparams(*sem):
    return pltpu.CompilerParams(dimension_semantics=sem, vmem_limit_bytes=VMEM_LIMIT)


def _const_spec(shape):
    nd = len(shape)
    return pl.BlockSpec(shape, lambda *_: (0,) * nd, pipeline_mode=pl.Buffered(1))


def _vmem_specs(n):
    return [pl.BlockSpec(memory_space=pltpu.VMEM)] * n


def _rms_scale(x):
    return lax.rsqrt(jnp.mean(x * x, axis=-1, keepdims=True) + EPS)


def _split_bf16(x):
    hi = x.astype(BF16)
    lo = (x - hi.astype(F32)).astype(BF16)
    return hi, lo


def _dot(a, b):
    return jnp.dot(a, b, preferred_element_type=F32)


def _dot_nt(a, b):
    return lax.dot_general(a, b, (((1,), (1,)), ((), ())), preferred_element_type=F32)


def _rms_matmul_kernel(x_ref, g_ref, w_ref, o_ref, *, tn):
    x = x_ref[...]
    h = (x * _rms_scale(x) * g_ref[...]).astype(BF16)
    for n0 in range(0, w_ref.shape[1], tn):
        o_ref[:, n0:n0 + tn] = _dot(h, w_ref[:, n0:n0 + tn]).astype(o_ref.dtype)


def rms_matmul(x, g, w, out_dtype, tm=512, tn=512):
    t, d = x.shape
    n = w.shape[1]
    tm = min(tm, t)
    return pl.pallas_call(
        functools.partial(_rms_matmul_kernel, tn=tn),
        grid=(t // tm,),
        in_specs=[pl.BlockSpec((tm, d), lambda i: (i, 0)),
                  _const_spec((1, d)),
                  _const_spec((d, n))],
        out_specs=pl.BlockSpec((tm, n), lambda i: (i, 0)),
        out_shape=jax.ShapeDtypeStruct((t, n), out_dtype),
        compiler_params=_cparams("parallel"),
        name="rms_matmul",
    )(x, g.reshape(1, d), w)


def _matmul_res_kernel(y_ref, w_ref, x_ref, o_ref):
    o_ref[...] = x_ref[...] + _dot(y_ref[...], w_ref[...])


def matmul_res(y, w, x, tm=512):
    t, k = y.shape
    d = w.shape[1]
    tm = min(tm, t)
    return pl.pallas_call(
        _matmul_res_kernel,
        grid=(t // tm,),
        in_specs=[pl.BlockSpec((tm, k), lambda i: (i, 0)),
                  _const_spec((k, d)),
                  pl.BlockSpec((tm, d), lambda i: (i, 0))],
        out_specs=pl.BlockSpec((tm, d), lambda i: (i, 0)),
        out_shape=jax.ShapeDtypeStruct((t, d), F32),
        compiler_params=_cparams("parallel"),
        name="matmul_res",
    )(y, w, x)


def _mem_attn_kernel(x_ref, g_ref, wq_ref, k_ref, v_ref, wo_ref, o_ref, *, nseg, seg):
    scale = MEM_HD ** -0.5
    x = x_ref[...]
    h = (x * _rms_scale(x) * g_ref[...]).astype(BF16)
    o_ref[...] = x

    def q_proj(hd):
        return _dot(h, wq_ref[:, hd * MEM_HD:(hd + 1) * MEM_HD]).astype(BF16)

    nxt = q_proj(0)
    pending = None
    for hd in range(MEM_HEADS):
        cols = slice(hd * MEM_HD, (hd + 1) * MEM_HD)
        q = nxt
        if hd + 1 < MEM_HEADS:
            nxt = q_proj(hd + 1)
        if pending is not None:
            o_ref[...] += _dot(pending, wo_ref[(hd - 1) * MEM_HD:hd * MEM_HD, :])
        outs = []
        for s in range(nseg):
            kh = k_ref[0, s, :, cols].astype(BF16)
            vh = v_ref[0, s, :, cols].astype(BF16)
            sc = _dot_nt(q[s * seg:(s + 1) * seg], kh) * scale
            p = jnp.exp(sc - jnp.max(sc, axis=-1, keepdims=True))
            denom = jnp.sum(p, axis=-1, keepdims=True)
            outs.append((_dot(p.astype(BF16), vh) / denom).astype(BF16))
        pending = outs[0] if nseg == 1 else jnp.concatenate(outs, axis=0)
    o_ref[...] += _dot(pending, wo_ref[(MEM_HEADS - 1) * MEM_HD:, :])


def mem_attn(x, g, w_q, mk, mv, layer, w_o, seq_len, nseg, seg):
    t, d = x.shape
    _, b, n_mem, _ = mk.shape
    tiles_per_stream = max(seq_len // (nseg * seg), 1)
    n_outer = b // nseg
    tm = nseg * seg
    x_spec = pl.BlockSpec((tm, d), lambda o, i: (o * tiles_per_stream + i, 0))
    kv_spec = pl.BlockSpec((1, nseg, n_mem, d), lambda o, i: (layer, o, 0, 0))
    return pl.pallas_call(
        functools.partial(_mem_attn_kernel, nseg=nseg, seg=seg),
        grid=(n_outer, tiles_per_stream),
        in_specs=[x_spec, _const_spec((1, d)), _const_spec((d, d)), kv_spec, kv_spec,
                  _const_spec((d, d))],
        out_specs=x_spec,
        out_shape=jax.ShapeDtypeStruct((t, d), F32),
        compiler_params=_cparams("parallel", "parallel"),
        name="mem_attn",
    )(x, g.reshape(1, d), w_q, mk, mv, w_o)


def _ffn_kernel(x_ref, g_ref, wup_ref, cw_ref, cb_ref, wdn_ref, st_ref, *rest,
                nseg, seg, fc, final_norm):
    if final_norm:
        gf_ref, o_ref, nst_ref, carry_ref = rest
    else:
        o_ref, nst_ref, carry_ref = rest
    d_ff = wdn_ref.shape[0]
    tm = nseg * seg

    @pl.when(pl.program_id(1) == 0)
    def _():
        carry_ref[...] = st_ref[...]

    x = x_ref[...]
    h = (x * _rms_scale(x) * g_ref[...]).astype(BF16)
    row = lax.broadcasted_iota(jnp.int32, (tm, 1), 0) % seg
    o_ref[...] = x

    def up_proj(f0):
        return (_dot(h, wup_ref[:, f0:f0 + fc]), _dot(h, wup_ref[:, d_ff + f0:d_ff + f0 + fc]))

    nxt = up_proj(0)
    pending = None
    for f0 in range(0, d_ff, fc):
        cols = slice(f0, f0 + fc)
        a, gate = nxt
        if f0 + fc < d_ff:
            nxt = up_proj(f0 + fc)
        if pending is not None:
            o_ref[...] += _dot(pending, wdn_ref[f0 - fc:f0, :])
        p0 = jnp.broadcast_to(carry_ref[:, 0:1, cols], (nseg, seg, fc)).reshape(tm, fc)
        p1 = jnp.broadcast_to(carry_ref[:, 1:2, cols], (nseg, seg, fc)).reshape(tm, fc)
        a1 = jnp.where(row == 0, p1, pltpu.roll(a, 1, 0))
        a2 = jnp.where(row == 0, p0, jnp.where(row == 1, p1, pltpu.roll(a, 2, 0)))
        conv = (cw_ref[0:1, cols] * a2 + cw_ref[1:2, cols] * a1 + cw_ref[2:3, cols] * a
                + cb_ref[:, cols])
        p = jax.nn.gelu(conv) * gate
        pending = p.astype(BF16)
        carry_ref[:, :, cols] = a.reshape(nseg, seg, fc)[:, seg - 2:seg, :]
    y = o_ref[...] + _dot(pending, wdn_ref[d_ff - fc:d_ff, :])
    nst_ref[...] = carry_ref[...]
    if final_norm:
        y = y * _rms_scale(y) * gf_ref[...]
    o_ref[...] = y


def conv_ffn(x, g, w_up, conv_w, conv_b, w_down, conv_state, seq_len, nseg, seg, g_final=None,
             fc=256):
    t, d = x.shape
    d_ff = w_down.shape[0]
    b = conv_state.shape[0]
    tiles_per_stream = max(seq_len // (nseg * seg), 1)
    n_outer = b // nseg
    tm = nseg * seg
    final_norm = g_final is not None
    in_specs = [pl.BlockSpec((tm, d), lambda o, i: (o * tiles_per_stream + i, 0)),
                _const_spec((1, d)),
                _const_spec((d, 2 * d_ff)),
                _const_spec((3, d_ff)),
                _const_spec((1, d_ff)),
                _const_spec((d_ff, d)),
                pl.BlockSpec((nseg, 2, d_ff), lambda o, i: (o, 0, 0))]
    args = [x, g.reshape(1, d), w_up, conv_w, conv_b.reshape(1, d_ff), w_down, conv_state]
    if final_norm:
        in_specs.append(_const_spec((1, d)))
        args.append(g_final.reshape(1, d))
    return pl.pallas_call(
        functools.partial(_ffn_kernel, nseg=nseg, seg=seg, fc=fc, final_norm=final_norm),
        grid=(n_outer, tiles_per_stream),
        in_specs=in_specs,
        out_specs=[pl.BlockSpec((tm, d), lambda o, i: (o * tiles_per_stream + i, 0)),
                   pl.BlockSpec((nseg, 2, d_ff), lambda o, i: (o, 0, 0))],
        out_shape=[jax.ShapeDtypeStruct((t, d), F32),
                   jax.ShapeDtypeStruct((b, 2, d_ff), F32)],
        scratch_shapes=[pltpu.VMEM((nseg, 2, d_ff), F32)],
        compiler_params=_cparams("parallel", "arbitrary"),
        name="conv_ffn",
    )(*args)


def _rotate(x, cos, sin):
    half = x.shape[1] // 2
    x1, x2 = x[:, :half], x[:, half:]
    return jnp.concatenate([x1 * cos - x2 * sin, x2 * cos + x1 * sin], axis=1)


def _ret_head(h, q, k, v, g, cos, sin, s_ref, chunk):
    log_g = RET_LOG_G[h]
    li = lax.broadcasted_iota(jnp.int32, (chunk, chunk), 0)
    mi = lax.broadcasted_iota(jnp.int32, (chunk, chunk), 1)
    diff = (li - mi).astype(F32)
    pos = lax.broadcasted_iota(jnp.int32, (chunk, 1), 0).astype(F32)
    qr = _rotate(q, cos, sin).astype(BF16)
    kr = _rotate(k, cos, sin) * (RET_DK ** -0.5)
    decay = jnp.where(diff >= 0, jnp.exp(jnp.maximum(diff, 0.0) * log_g), 0.0)
    scores = _dot_nt(qr, kr.astype(BF16)) * decay
    intra = _dot(scores.astype(BF16), v)
    s_old = s_ref[0, h]
    cross = _dot(qr, s_old.astype(BF16)) * jnp.exp((pos + 1.0) * log_g)
    o = intra + cross
    k_tail = (kr * jnp.exp((chunk - 1.0 - pos) * log_g)).astype(BF16)
    s_ref[0, h] = math.exp(chunk * log_g) * s_old + lax.dot_general(
        k_tail, v, (((0,), (0,)), ((), ())), preferred_element_type=F32)
    mu = jnp.mean(o, axis=-1, keepdims=True)
    oc = o - mu
    var = jnp.mean(oc * oc, axis=-1, keepdims=True)
    return jax.nn.silu(g) * (oc * lax.rsqrt(var + GN_EPS))


def _ret_kernel(p_ref, cos_ref, sin_ref, s_in_ref, y_ref, s_out_ref, *, chunk):
    @pl.when(pl.program_id(1) == 0)
    def _():
        s_out_ref[...] = s_in_ref[...]

    cos = cos_ref[...]
    sin = sin_ref[...]
    for h in range(RET_HEADS):
        q = p_ref[:, h * RET_DK:(h + 1) * RET_DK].astype(F32)
        k = p_ref[:, RET_QK + h * RET_DK:RET_QK + (h + 1) * RET_DK].astype(F32)
        v = p_ref[:, 2 * RET_QK + h * RET_DV:2 * RET_QK + (h + 1) * RET_DV]
        g = p_ref[:, 2 * RET_QK + RET_V + h * RET_DV:
                  2 * RET_QK + RET_V + (h + 1) * RET_DV].astype(F32)
        y = _ret_head(h, q, k, v, g, cos, sin, s_out_ref, chunk)
        y_ref[:, h * RET_DV:(h + 1) * RET_DV] = y.astype(y_ref.dtype)


def _ret_fused_kernel(x_ref, g_ref, w_ref, wo_ref, cos_ref, sin_ref, s_in_ref, o_ref, s_out_ref,
                      *, chunk):
    @pl.when(pl.program_id(1) == 0)
    def _():
        s_out_ref[...] = s_in_ref[...]

    cos = cos_ref[...]
    sin = sin_ref[...]
    x = x_ref[...]
    hn = (x * _rms_scale(x) * g_ref[...]).astype(BF16)
    o_ref[...] = x

    def proj(h):
        c = [h * RET_DK, RET_QK + h * RET_DK, 2 * RET_QK + h * RET_DV, 2 * RET_QK + RET_V + h * RET_DV]
        return (_dot(hn, w_ref[:, c[0]:c[0] + RET_DK]), _dot(hn, w_ref[:, c[1]:c[1] + RET_DK]),
                _dot(hn, w_ref[:, c[2]:c[2] + RET_DV]).astype(BF16),
                _dot(hn, w_ref[:, c[3]:c[3] + RET_DV]))

    nxt = proj(0)
    pending = None
    for h in range(RET_HEADS):
        q, k, v, g = nxt
        if h + 1 < RET_HEADS:
            nxt = proj(h + 1)
        if pending is not None:
            o_ref[...] += _dot(pending, wo_ref[(h - 1) * RET_DV:h * RET_DV, :])
        pending = _ret_head(h, q, k, v, g, cos, sin, s_out_ref, chunk).astype(BF16)
    o_ref[...] += _dot(pending, wo_ref[(RET_HEADS - 1) * RET_DV:, :])


def _ret_specs(seq_len, chunk):
    nchunks = seq_len // chunk
    half = RET_DK // 2
    row_spec = lambda n: pl.BlockSpec((chunk, n), lambda o, i: (o * nchunks + i, 0))
    rope_spec = pl.BlockSpec((chunk, half), lambda o, i: (i, 0))
    state_spec = pl.BlockSpec((1, RET_HEADS, RET_DK, RET_DV), lambda o, i: (o, 0, 0, 0))
    return nchunks, row_spec, rope_spec, state_spec


def ret_core(p, cos, sin, state, seq_len, chunk):
    t, n = p.shape
    nchunks, row_spec, rope_spec, state_spec = _ret_specs(seq_len, chunk)
    return pl.pallas_call(
        functools.partial(_ret_kernel, chunk=chunk),
        grid=(state.shape[0], nchunks),
        in_specs=[row_spec(n), rope_spec, rope_spec, state_spec],
        out_specs=[row_spec(RET_V), state_spec],
        out_shape=[jax.ShapeDtypeStruct((t, RET_V), BF16),
                   jax.ShapeDtypeStruct(state.shape, F32)],
        compiler_params=_cparams("parallel", "arbitrary"),
        name="ret_core",
    )(p, cos, sin, state)


def ret_mixer(x, g, w_qkvg, w_o, cos, sin, state, seq_len, chunk):
    t, d = x.shape
    nchunks, row_spec, rope_spec, state_spec = _ret_specs(seq_len, chunk)
    return pl.pallas_call(
        functools.partial(_ret_fused_kernel, chunk=chunk),
        grid=(state.shape[0], nchunks),
        in_specs=[row_spec(d), _const_spec((1, d)), _const_spec(w_qkvg.shape),
                  _const_spec(w_o.shape), rope_spec, rope_spec, state_spec],
        out_specs=[row_spec(d), state_spec],
        out_shape=[jax.ShapeDtypeStruct((t, d), F32), jax.ShapeDtypeStruct(state.shape, F32)],
        compiler_params=_cparams("parallel", "arbitrary"),
        name="ret_mixer",
    )(x, g.reshape(1, d), w_qkvg, w_o, cos, sin, state)


def _rope_tables(pos0, length):
    half = RET_DK // 2
    freqs = ROPE_BASE ** (-np.arange(half, dtype=np.float64) / half)
    ang = (pos0 + np.arange(length, dtype=np.float64))[:, None] * freqs[None, :]
    return jnp.asarray(np.cos(ang), F32), jnp.asarray(np.sin(ang), F32)


def _cmul(ar, ai, br, bi):
    return ar * br - ai * bi, ar * bi + ai * br


def _s5_prep_kernel(lam_re_ref, lam_im_ref, ldt_ref, b_re_ref, b_im_ref, c_re_ref, c_im_ref,
                    w_re_ref, w_im_ref, bt_ref, cct_ref, a16_ref):
    dt = jnp.exp(ldt_ref[...])
    lam_re, lam_im = lam_re_ref[...], lam_im_ref[...]
    mag = jnp.exp(lam_re * dt)
    ar, ai = mag * jnp.cos(lam_im * dt), mag * jnp.sin(lam_im * dt)
    den = lam_re * lam_re + lam_im * lam_im
    nr, ni = ar - 1.0, ai
    coef_re = (nr * lam_re + ni * lam_im) / den
    coef_im = (ni * lam_re - nr * lam_im) / den
    bb_re, bb_im = _cmul(coef_re, coef_im, b_re_ref[...], b_im_ref[...])
    c_re, c_im = c_re_ref[...], c_im_ref[...]
    lane = lax.broadcasted_iota(jnp.int32, (1, LANES), 1)
    low = lane < SSM_STATE
    rg = lax.broadcasted_iota(jnp.int32, (LANES, 1), 0) // SSM_GROUP
    pair = rg // 2
    own_half = (lane // SSM_STATE) == (rg % 2)
    same_group = rg == lane // SSM_GROUP
    c_hi, c_lo = _split_bf16(jnp.where(low, c_re, -c_im))
    tk = []
    pr, pi = jnp.ones_like(ar), jnp.zeros_like(ar)
    for j in range(SSM_FOLD):
        zr, zi = _cmul(pr, pi, bb_re, bb_im)
        w_re_ref[SSM_FOLD - 1 - j] = jnp.where(own_half, zr, 0.0)
        w_im_ref[SSM_FOLD - 1 - j] = jnp.where(own_half, zi, 0.0)
        b_hi, b_lo = _split_bf16(jnp.where(low, zr, zi))
        kk = _dot_nt(b_hi, c_hi) + _dot_nt(b_lo, c_hi) + _dot_nt(b_hi, c_lo)
        tk.append(jnp.where(same_group, kk, 0.0).astype(BF16))
        pr, pi = _cmul(pr, pi, ar, ai)
        zr, zi = _cmul(c_re, c_im, pr, pi)
        for ri, z in enumerate((zr, -zi)):
            z = jnp.where(own_half, z, 0.0)
            for t in range(GROUPS_PER_TILE // 2):
                c0 = ri * STATE_PER_TILE + t * LANES
                cct_ref[0, j * LANES:(j + 1) * LANES, c0:c0 + LANES] = (
                    jnp.where(pair == t, z, 0.0).astype(BF16))
    a16_ref[0] = pr
    a16_ref[1] = pi
    zero = jnp.zeros((LANES, LANES), BF16)
    for d2 in range(SSM_FOLD // 2):
        top = jnp.concatenate([tk[2 * d2], tk[2 * d2 + 1]], axis=1)
        bot = jnp.concatenate([tk[2 * d2 - 1] if d2 else zero, tk[2 * d2]], axis=1)
        bt_ref[0, d2] = jnp.concatenate([top, bot], axis=0)


def s5_prepare(a_re, a_im, log_dt, b_re, b_im, c_re, c_im):
    g, p, c = SSM_GROUPS, SSM_STATE, SSM_GROUP
    fold, nt = SSM_FOLD, N_LANE_TILES
    rows = g * c
    twice = lambda x: jnp.tile(x, (1, 2))
    lam = lambda x: twice(jnp.repeat(x, c, axis=0))
    in_spec = pl.BlockSpec((LANES, LANES), lambda t: (t, 0))
    w_spec = pl.BlockSpec((fold, LANES, LANES), lambda t: (0, t, 0))
    w_shape = jax.ShapeDtypeStruct((fold, rows, LANES), F32)
    w_re, w_im, bt, cct, a16 = pl.pallas_call(
        _s5_prep_kernel,
        grid=(nt,),
        in_specs=[in_spec, in_spec, pl.BlockSpec((LANES, 1), lambda t: (t, 0))] + [in_spec] * 4,
        out_specs=[w_spec, w_spec,
                   pl.BlockSpec((1, fold // 2, 2 * LANES, 2 * LANES), lambda t: (t, 0, 0, 0)),
                   pl.BlockSpec((1, fold * LANES, 2 * STATE_PER_TILE), lambda t: (t, 0, 0)),
                   pl.BlockSpec((2, LANES, LANES), lambda t: (0, t, 0))],
        out_shape=[w_shape, w_shape,
                   jax.ShapeDtypeStruct((nt, fold // 2, 2 * LANES, 2 * LANES), BF16),
                   jax.ShapeDtypeStruct((nt, fold * LANES, 2 * STATE_PER_TILE), BF16),
                   jax.ShapeDtypeStruct((2, rows, LANES), F32)],
        compiler_params=_cparams("parallel"),
        name="s5_prep",
    )(lam(a_re), lam(a_im), jnp.repeat(log_dt, c).reshape(rows, 1),
      twice(b_re.transpose(0, 2, 1).reshape(rows, p)), twice(b_im.transpose(0, 2, 1).reshape(rows, p)),
      twice(c_re.reshape(rows, p)), twice(c_im.reshape(rows, p)))
    a16 = a16[:, ::c, :p].reshape(2, 1, N_STATE)
    return bt, cct, w_re, w_im, a16


FOLD_BLOCK = SSM_FOLD * SSM_FOLD


def _fold_perm():
    idx = np.arange(FOLD_BLOCK)
    perm = np.zeros((FOLD_BLOCK, FOLD_BLOCK), np.float32)
    perm[(idx % SSM_FOLD) * SSM_FOLD + idx // SSM_FOLD, idx] = 1.0
    return jnp.asarray(perm, BF16)


def _s5_fold_kernel(x_ref, g_ref, perm_ref, u_ref):
    for blk in range(x_ref.shape[0] // FOLD_BLOCK):
        x = x_ref[blk * FOLD_BLOCK:(blk + 1) * FOLD_BLOCK, :]
        h = (x * _rms_scale(x) * g_ref[...]).astype(BF16)
        f = _dot(perm_ref[...], h).astype(BF16)
        for m in range(SSM_FOLD):
            for k in range(N_LANE_TILES):
                u_ref[m, k, blk * SSM_FOLD:(blk + 1) * SSM_FOLD, :] = (
                    f[m * SSM_FOLD:(m + 1) * SSM_FOLD, k * LANES:(k + 1) * LANES])


def s5_fold(x, g, perm, tokens):
    t, d = x.shape
    rows = t // SSM_FOLD
    return pl.pallas_call(
        _s5_fold_kernel,
        grid=(t // tokens,),
        in_specs=[pl.BlockSpec((tokens, d), lambda i: (i, 0)), _const_spec((1, d)),
                  _const_spec((FOLD_BLOCK, FOLD_BLOCK))],
        out_specs=pl.BlockSpec((SSM_FOLD, N_LANE_TILES, tokens // SSM_FOLD, LANES),
                               lambda i: (0, 0, i, 0)),
        out_shape=jax.ShapeDtypeStruct((SSM_FOLD, N_LANE_TILES, rows, LANES), BF16),
        compiler_params=_cparams("parallel"),
        name="s5_fold",
    )(x, g.reshape(1, d), perm)


def _s5_state_in_kernel(u_ref, w_re_ref, w_im_ref, vre_ref, vim_ref, wd_ref):
    @pl.when(pl.program_id(1) == 0)
    def _():
        pair = lax.broadcasted_iota(jnp.int32, (LANES, 1), 0) // SSM_GROUP // 2
        for m in range(SSM_FOLD):
            for ri, ref in enumerate((w_re_ref, w_im_ref)):
                w = ref[m]
                for t in range(GROUPS_PER_TILE // 2):
                    c0 = ri * STATE_PER_TILE + t * LANES
                    wd_ref[m * LANES:(m + 1) * LANES, c0:c0 + LANES] = (
                        jnp.where(pair == t, w, 0.0).astype(BF16))

    lhs = jnp.concatenate([u_ref[m, 0] for m in range(SSM_FOLD)], axis=1)
    v = _dot(lhs, wd_ref[...])
    vre_ref[...] = v[:, :STATE_PER_TILE]
    vim_ref[...] = v[:, STATE_PER_TILE:]


def _s5_tile_specs(rt):
    u_spec = pl.BlockSpec((SSM_FOLD, 1, rt, LANES), lambda t, i: (0, t, i, 0))
    s_spec = pl.BlockSpec((rt, STATE_PER_TILE), lambda t, i: (i, t))
    return u_spec, s_spec


def s5_state_in(u, w_re, w_im, rt):
    rows = u.shape[2]
    u_spec, s_spec = _s5_tile_specs(rt)
    out = jax.ShapeDtypeStruct((rows, N_STATE), F32)
    w_spec = pl.BlockSpec((SSM_FOLD, LANES, LANES), lambda t, i: (0, t, 0))
    return pl.pallas_call(
        _s5_state_in_kernel,
        grid=(N_LANE_TILES, rows // rt),
        in_specs=[u_spec, w_spec, w_spec],
        out_specs=[s_spec, s_spec],
        out_shape=[out, out],
        scratch_shapes=[pltpu.VMEM((SSM_FOLD * LANES, 2 * STATE_PER_TILE), BF16)],
        compiler_params=_cparams("parallel", "arbitrary"),
        name="s5_state_in",
    )(u, w_re, w_im)


def _s5_scan_pairs_kernel(vre_ref, vim_ref, a_ref, s0re_ref, s0im_ref,
                          sre_ref, sim_ref, fre_ref, fim_ref):
    ar, ai = a_ref[0], a_ref[1]
    s0r, s0i = s0re_ref[...], s0im_ref[...]
    vr, vi = vre_ref[...], vim_ref[...]
    first = lax.broadcasted_iota(jnp.int32, (s0r.shape[0], 1), 0) % 2 == 0
    tr, ti = ar * s0r - ai * s0i + vr, ar * s0i + ai * s0r + vi
    sr = jnp.where(first, s0r, pltpu.roll(tr, 1, 0))
    si = jnp.where(first, s0i, pltpu.roll(ti, 1, 0))
    sre_ref[...] = sr
    sim_ref[...] = si
    fre_ref[...] = ar * sr - ai * si + vr
    fim_ref[...] = ar * si + ai * sr + vi


def _s5_scan_stream_kernel(vre_ref, vim_ref, a_ref, s0re_ref, s0im_ref,
                           sre_ref, sim_ref, fre_ref, fim_ref, *, steps):
    q = lax.broadcasted_iota(jnp.int32, (SUBLANES, 1), 0)
    a1 = (a_ref[0], a_ref[1])
    a2 = _cmul(*a1, *a1)
    a4 = _cmul(*a2, *a2)
    a8 = _cmul(*a4, *a4)
    strides = ((1, a1), (2, a2), (4, a4))
    width = a1[0].shape[1]
    pw = (jnp.ones((SUBLANES, width), F32), jnp.zeros((SUBLANES, width), F32))
    for k, ak in strides:
        nxt = _cmul(*pw, *ak)
        bit = (q & k) != 0
        pw = (jnp.where(bit, nxt[0], pw[0]), jnp.where(bit, nxt[1], pw[1]))

    def shifted(x, k):
        return jnp.where(q >= k, pltpu.roll(x, k, 0), 0.0)

    def body(t, carry):
        cr, ci = carry
        rows = pl.ds(pl.multiple_of(t * SUBLANES, SUBLANES), SUBLANES)
        pr, pi = vre_ref[rows, :], vim_ref[rows, :]
        for k, ak in strides:
            dr, di = _cmul(shifted(pr, k), shifted(pi, k), *ak)
            pr, pi = pr + dr, pi + di
        xr, xi = _cmul(*pw, cr, ci)
        sre_ref[rows, :] = xr + shifted(pr, 1)
        sim_ref[rows, :] = xi + shifted(pi, 1)
        nr, ni = _cmul(*a8, cr, ci)
        return nr + pr[SUBLANES - 1:], ni + pi[SUBLANES - 1:]

    cr, ci = lax.fori_loop(0, steps // SUBLANES, body, (s0re_ref[0], s0im_ref[0]))
    fre_ref[0] = cr
    fim_ref[0] = ci


def s5_scan(v_re, v_im, a16, s0_re, s0_im, width):
    rows, n = v_re.shape
    streams = s0_re.shape[0]
    steps = rows // streams
    a_spec = pl.BlockSpec((2, 1, width), lambda *i: (0, 0, i[-1]))
    vout = jax.ShapeDtypeStruct((rows, n), F32)
    long_streams = steps % SUBLANES == 0
    if long_streams:
        kern = functools.partial(_s5_scan_stream_kernel, steps=steps)
        grid = (streams, n // width)
        vspec = pl.BlockSpec((steps, width), lambda b, c: (b, c))
        sspec = pl.BlockSpec((1, 1, width), lambda b, c: (b, 0, c))
        s0_re, s0_im = s0_re.reshape(streams, 1, n), s0_im.reshape(streams, 1, n)
        sout = jax.ShapeDtypeStruct((streams, 1, n), F32)
    else:
        assert steps == 2 and rows % SUBLANES == 0
        kern = _s5_scan_pairs_kernel
        grid = (n // width,)
        vspec = pl.BlockSpec((rows, width), lambda c: (0, c))
        sspec = vspec
        s0_re, s0_im = jnp.repeat(s0_re, steps, axis=0), jnp.repeat(s0_im, steps, axis=0)
        sout = vout
    s_re, s_im, f_re, f_im = pl.pallas_call(
        kern,
        grid=grid,
        in_specs=[vspec, vspec, a_spec, sspec, sspec],
        out_specs=[vspec, vspec, sspec, sspec],
        out_shape=[vout, vout, sout, sout],
        compiler_params=_cparams(*(["parallel"] * len(grid))),
        name="s5_scan",
    )(v_re, v_im, a16, s0_re, s0_im)
    if long_streams:
        return s_re, s_im, f_re.reshape(streams, n), f_im.reshape(streams, n)
    return s_re, s_im, f_re[steps - 1::steps], f_im[steps - 1::steps]


def _s5_toeplitz_kernel(u_ref, sre_ref, sim_ref, bt_ref, cct_ref, y_ref):
    npair = SSM_FOLD // 2
    u2 = [jnp.concatenate([u_ref[2 * mm, 0], u_ref[2 * mm + 1, 0]], axis=1) for mm in range(npair)]
    s_prev = jnp.concatenate([sre_ref[...], sim_ref[...]], axis=1).astype(BF16)
    for ll in range(npair):
        acc = _dot_nt(s_prev, cct_ref[0, ll * 2 * LANES:(ll + 1) * 2 * LANES, :])
        for mm in range(ll + 1):
            acc = acc + _dot(u2[mm], bt_ref[0, ll - mm])
        y_ref[2 * ll, 0] = acc[:, :LANES]
        y_ref[2 * ll + 1, 0] = acc[:, LANES:]


def s5_toeplitz(u, s_re, s_im, bt, cct, rt):
    rows = u.shape[2]
    u_spec, s_spec = _s5_tile_specs(rt)
    return pl.pallas_call(
        _s5_toeplitz_kernel,
        grid=(N_LANE_TILES, rows // rt),
        in_specs=[u_spec, s_spec, s_spec,
                  pl.BlockSpec((1, SSM_FOLD // 2, 2 * LANES, 2 * LANES), lambda t, i: (t, 0, 0, 0)),
                  pl.BlockSpec((1, SSM_FOLD * LANES, 2 * STATE_PER_TILE), lambda t, i: (t, 0, 0))],
        out_specs=u_spec,
        out_shape=jax.ShapeDtypeStruct(u.shape, F32),
        compiler_params=_cparams("parallel", "parallel"),
        name="s5_toeplitz",
    )(u, s_re, s_im, bt, cct)


def _s5_glu_kernel(x_ref, u_ref, y_ref, d_ref, wglu_ref, bglu_ref, perm_ref, o_ref):
    nblk = x_ref.shape[0] // FOLD_BLOCK
    gl = []
    for blk in range(nblk):
        rows = slice(blk * SSM_FOLD, (blk + 1) * SSM_FOLD)
        for m in range(SSM_FOLD):
            ym = jnp.concatenate([y_ref[m, k, rows, :] for k in range(N_LANE_TILES)], axis=1)
            hm = jnp.concatenate([u_ref[m, k, rows, :] for k in range(N_LANE_TILES)], axis=1)
            gl.append(jax.nn.gelu(ym + d_ref[...] * hm.astype(F32)))
    gl = jnp.concatenate(gl, axis=0)
    out = gl * jax.nn.sigmoid(_dot(gl.astype(BF16), wglu_ref[...]) + bglu_ref[...])
    for blk in range(nblk):
        tok = slice(blk * FOLD_BLOCK, (blk + 1) * FOLD_BLOCK)
        hi, lo = _split_bf16(out[tok])
        o_ref[tok, :] = x_ref[tok, :] + (_dot(perm_ref[...], hi) + _dot(perm_ref[...], lo))


def s5_glu(x, u, y, d, w_glu, b_glu, perm, tokens):
    t, dm = x.shape
    fold_spec = pl.BlockSpec((SSM_FOLD, N_LANE_TILES, tokens // SSM_FOLD, LANES),
                             lambda i: (0, 0, i, 0))
    x_spec = pl.BlockSpec((tokens, dm), lambda i: (i, 0))
    return pl.pallas_call(
        _s5_glu_kernel,
        grid=(t // tokens,),
        in_specs=[x_spec, fold_spec, fold_spec, _const_spec((1, dm)), _const_spec((dm, dm)),
                  _const_spec((1, dm)), _const_spec((FOLD_BLOCK, FOLD_BLOCK))],
        out_specs=x_spec,
        out_shape=jax.ShapeDtypeStruct((t, dm), F32),
        compiler_params=_cparams("parallel"),
        name="s5_glu",
    )(x, u, y, d.reshape(1, dm), w_glu, b_glu.reshape(1, dm), perm)


def s5_mixer(x, st_re, st_im, g, ops, d, w_glu, b_glu):
    bt, cct, w_re, w_im, a16 = ops
    b, l, dm = x.shape
    rows = b * l // SSM_FOLD
    x2 = x.reshape(b * l, dm)
    perm = _fold_perm()
    tokens = min(S5_TOKEN_TILE, b * l)
    rt = min(S5_ROW_TILE, rows)
    u = s5_fold(x2, g, perm, tokens)
    v_re, v_im = s5_state_in(u, w_re, w_im, rt)
    s_re, s_im, f_re, f_im = s5_scan(v_re, v_im, a16, st_re.reshape(b, N_STATE),
                                     st_im.reshape(b, N_STATE), S5_SCAN_WIDTH)
    y = s5_toeplitz(u, s_re, s_im, bt, cct, rt)
    out = s5_glu(x2, u, y, d, w_glu, b_glu, perm, tokens)
    shape = (b, SSM_GROUPS, SSM_STATE)
    return out.reshape(b, l, dm), f_re.reshape(shape), f_im.reshape(shape)


def _stream_tiling(seq_len, n_streams, tile_rows):
    if seq_len >= tile_rows:
        return 1, tile_rows
    return min(n_streams, 256 // seq_len), seq_len


def kernel(x_prompt, x_sample, mem_prompt, state_ssm_re, state_ssm_im, state_ret, cache_mem_k, cache_mem_v, cache_conv, norm_mix, norm_mem_q, norm_mem_kv, norm_ffn, norm_final, ssm_a_re, ssm_a_im, ssm_log_dt, ssm_b_re, ssm_b_im, ssm_c_re, ssm_c_im, ssm_d, ssm_w_glu, ssm_b_glu, ret_w_qkvg, ret_w_o, mem_w_q, mem_w_kv, mem_w_o, ffn_w_up, ffn_conv_w, ffn_conv_b, ffn_w_down):
    bp, lp, dm = x_prompt.shape
    bs, ls, _ = x_sample.shape
    depth = norm_mix.shape[0]
    n_mem = mem_prompt.shape[1]
    d_ff = ffn_w_down.shape[1]
    streams = [dict(x=x, b=x.shape[0], l=x.shape[1],
                    attn_tiling=_stream_tiling(x.shape[1], x.shape[0], ATTN_TILE_ROWS),
                    ffn_tiling=_stream_tiling(x.shape[1], x.shape[0], FFN_TILE_ROWS))
               for x in (x_prompt, x_sample)]
    mem_flat = mem_prompt.reshape(bp * n_mem, dm)
    cache_k = cache_mem_k.reshape(depth, bs, n_mem, dm).astype(BF16)
    cache_v = cache_mem_v.reshape(depth, bs, n_mem, dm).astype(BF16)
    outs = dict(ssm_re=([], []), ssm_im=([], []), ret=([], []), conv=([], []), mk=[], mv=[])

    for i in range(depth):
        j = i // 2
        if i % 2 == 0:
            ops = s5_prepare(ssm_a_re[j], ssm_a_im[j], ssm_log_dt[j], ssm_b_re[j], ssm_b_im[j],
                             ssm_c_re[j], ssm_c_im[j])
            w_glu = ssm_w_glu[j].astype(BF16)
            zeros = jnp.zeros((bp, SSM_GROUPS, SSM_STATE), F32)
            states = [(zeros, zeros), (state_ssm_re[j], state_ssm_im[j])]
            for si, (s, (st_re, st_im)) in enumerate(zip(streams, states)):
                s["x"], f_re, f_im = s5_mixer(s["x"], st_re, st_im, norm_mix[i], ops, ssm_d[j],
                                              w_glu, ssm_b_glu[j])
                outs["ssm_re"][si].append(f_re)
                outs["ssm_im"][si].append(f_im)
        else:
            w_qkvg = ret_w_qkvg[j].astype(BF16)
            w_o = ret_w_o[j].astype(BF16)
            zeros = jnp.zeros((bp, RET_HEADS, RET_DK, RET_DV), F32)
            chunk = RET_CHUNK_ROWS
            for si, (s, st, pos0) in enumerate(zip(streams, [zeros, state_ret[j]], [0, PAST_LEN])):
                x2 = s["x"].reshape(s["b"] * s["l"], dm)
                cos, sin = _rope_tables(pos0, s["l"])
                if s["l"] >= chunk:
                    y, st_new = ret_mixer(x2, norm_mix[i], w_qkvg, w_o, cos, sin, st, s["l"], chunk)
                else:
                    p = rms_matmul(x2, norm_mix[i], w_qkvg, BF16)
                    y, st_new = ret_core(p, cos, sin, st, s["l"], s["l"])
                    y = matmul_res(y, w_o, x2)
                s["x"] = y.reshape(s["x"].shape)
                outs["ret"][si].append(st_new)

        kv = rms_matmul(mem_flat, norm_mem_kv[i], mem_w_kv[i].astype(BF16), F32)
        mk = kv[:, :dm].reshape(1, bp, n_mem, dm)
        mv = kv[:, dm:].reshape(1, bp, n_mem, dm)
        outs["mk"].append(mk.reshape(bp, n_mem, MEM_HEADS, MEM_HD))
        outs["mv"].append(mv.reshape(bp, n_mem, MEM_HEADS, MEM_HD))
        w_q = mem_w_q[i].astype(BF16)
        w_o = mem_w_o[i].astype(BF16)
        for s, (k_, v_, layer) in zip(streams, [(mk, mv, 0), (cache_k, cache_v, i)]):
            x2 = s["x"].reshape(s["b"] * s["l"], dm)
            y = mem_attn(x2, norm_mem_q[i], w_q, k_, v_, layer, w_o, s["l"], *s["attn_tiling"])
            s["x"] = y.reshape(s["x"].shape)

        w_up = ffn_w_up[i].astype(BF16)
        w_dn = ffn_w_down[i].astype(BF16)
        g_final = norm_final if i == depth - 1 else None
        conv_states = [jnp.zeros((bp, 2, d_ff), F32), cache_conv[i]]
        for si, (s, cst) in enumerate(zip(streams, conv_states)):
            x2 = s["x"].reshape(s["b"] * s["l"], dm)
            y, cnew = conv_ffn(x2, norm_ffn[i], w_up, ffn_conv_w[i], ffn_conv_b[i], w_dn, cst,
                               s["l"], *s["ffn_tiling"], g_final=g_final)
            s["x"] = y.reshape(s["x"].shape)
            outs["conv"][si].append(cnew)

    return (streams[0]["x"], streams[1]["x"],
            jnp.stack(outs["ssm_re"][0]), jnp.stack(outs["ssm_im"][0]),
            jnp.stack(outs["ssm_re"][1]), jnp.stack(outs["ssm_im"][1]),
            jnp.stack(outs["ret"][0]), jnp.stack(outs["ret"][1]),
            jnp.stack(outs["mk"]), jnp.stack(outs["mv"]),
            jnp.stack(outs["conv"][0]), jnp.stack(outs["conv"][1]))
```

```python
import functools
import math

import numpy as np
import jax
import jax.numpy as jnp
from jax import lax
from jax.experimental import pallas as pl
from jax.experimental.pallas import tpu as pltpu

F32 = jnp.float32
BF16 = jnp.bfloat16

D_MODEL = 1024
PAST_LEN = 1024
EPS = 1e-6
GN_EPS = 1e-5
ROPE_BASE = 10000.0
LANES = 128
SUBLANES = 8
SSM_GROUP = 16
SSM_GROUPS = D_MODEL // SSM_GROUP
SSM_STATE = 64
SSM_FOLD = 16
GROUPS_PER_TILE = LANES // SSM_GROUP
N_LANE_TILES = D_MODEL // LANES
STATE_PER_TILE = GROUPS_PER_TILE * SSM_STATE
N_STATE = SSM_GROUPS * SSM_STATE
RET_HEADS = 4
RET_DK = D_MODEL // RET_HEADS
RET_DV = 2 * D_MODEL // RET_HEADS
RET_QK = RET_HEADS * RET_DK
RET_V = RET_HEADS * RET_DV
RET_LOG_G = tuple(math.log(1.0 - 2.0 ** (-5.0 - h)) for h in range(RET_HEADS))
MEM_HEADS = 4
MEM_HD = D_MODEL // MEM_HEADS
VMEM_LIMIT = 56 * 1024 * 1024
ATTN_TILE_ROWS = 512
FFN_TILE_ROWS = 256
RET_CHUNK_ROWS = 256
S5_TOKEN_TILE = 512
S5_ROW_TILE = 512
S5_SCAN_WIDTH = 1024


def _cparams(*sem):
    return pltpu.CompilerParams(dimension_semantics=sem, vmem_limit_bytes=VMEM_LIMIT)


def _const_spec(shape):
    nd = len(shape)
    return pl.BlockSpec(shape, lambda *_: (0,) * nd, pipeline_mode=pl.Buffered(1))


def _vmem_specs(n):
    return [pl.BlockSpec(memory_space=pltpu.VMEM)] * n


def _rms_scale(x):
    return lax.rsqrt(jnp.mean(x * x, axis=-1, keepdims=True) + EPS)


def _split_bf16(x):
    hi = x.astype(BF16)
    lo = (x - hi.astype(F32)).astype(BF16)
    return hi, lo


def _dot(a, b):
    return jnp.dot(a, b, preferred_element_type=F32)


def _dot_nt(a, b):
    return lax.dot_general(a, b, (((1,), (1,)), ((), ())), preferred_element_type=F32)


def _rms_matmul_kernel(x_ref, g_ref, w_ref, o_ref, *, tn):
    x = x_ref[...]
    h = (x * _rms_scale(x) * g_ref[...]).astype(BF16)
    for n0 in range(0, w_ref.shape[1], tn):
        o_ref[:, n0:n0 + tn] = _dot(h, w_ref[:, n0:n0 + tn]).astype(o_ref.dtype)


def rms_matmul(x, g, w, out_dtype, tm=512, tn=512):
    t, d = x.shape
    n = w.shape[1]
    tm = min(tm, t)
    return pl.pallas_call(
        functools.partial(_rms_matmul_kernel, tn=tn),
        grid=(t // tm,),
        in_specs=[pl.BlockSpec((tm, d), lambda i: (i, 0)),
                  _const_spec((1, d)),
                  _const_spec((d, n))],
        out_specs=pl.BlockSpec((tm, n), lambda i: (i, 0)),
        out_shape=jax.ShapeDtypeStruct((t, n), out_dtype),
        compiler_params=_cparams("parallel"),
        name="rms_matmul",
    )(x, g.reshape(1, d), w)


def _matmul_res_kernel(y_ref, w_ref, x_ref, o_ref):
    o_ref[...] = x_ref[...] + _dot(y_ref[...], w_ref[...])


def matmul_res(y, w, x, tm=512):
    t, k = y.shape
    d = w.shape[1]
    tm = min(tm, t)
    return pl.pallas_call(
        _matmul_res_kernel,
        grid=(t // tm,),
        in_specs=[pl.BlockSpec((tm, k), lambda i: (i, 0)),
                  _const_spec((k, d)),
                  pl.BlockSpec((tm, d), lambda i: (i, 0))],
        out_specs=pl.BlockSpec((tm, d), lambda i: (i, 0)),
        out_shape=jax.ShapeDtypeStruct((t, d), F32),
        compiler_params=_cparams("parallel"),
        name="matmul_res",
    )(y, w, x)


MEM_LANE_TILES = MEM_HD // LANES


def _kv_rows(x):
    *lead, n, h, hd = x.shape
    x = x.reshape(*lead, n, h, MEM_LANE_TILES, LANES)
    return jnp.swapaxes(x, -3, -2).reshape(*lead, n * MEM_LANE_TILES * h, LANES)


def _head_rows(kv_ref, s, hd):
    stride = MEM_LANE_TILES * MEM_HEADS
    n_mem = kv_ref.shape[2] // stride
    return jnp.concatenate(
        [kv_ref[0, s, pl.ds(j * MEM_HEADS + hd, n_mem, stride=stride), :]
         for j in range(MEM_LANE_TILES)], axis=1).astype(BF16)


def _mem_attn_kernel(x_ref, g_ref, wq_ref, k_ref, v_ref, wo_ref, o_ref, *, nseg, seg):
    scale = MEM_HD ** -0.5
    x = x_ref[...]
    h = (x * _rms_scale(x) * g_ref[...]).astype(BF16)
    o_ref[...] = x

    def q_proj(hd):
        return _dot(h, wq_ref[:, hd * MEM_HD:(hd + 1) * MEM_HD]).astype(BF16)

    nxt = q_proj(0)
    pending = None
    for hd in range(MEM_HEADS):
        cols = slice(hd * MEM_HD, (hd + 1) * MEM_HD)
        q = nxt
        if hd + 1 < MEM_HEADS:
            nxt = q_proj(hd + 1)
        if pending is not None:
            o_ref[...] += _dot(pending, wo_ref[(hd - 1) * MEM_HD:hd * MEM_HD, :])
        outs = []
        for s in range(nseg):
            kh = _head_rows(k_ref, s, hd)
            vh = _head_rows(v_ref, s, hd)
            sc = _dot_nt(q[s * seg:(s + 1) * seg], kh) * scale
            p = jnp.exp(sc - jnp.max(sc, axis=-1, keepdims=True))
            denom = jnp.sum(p, axis=-1, keepdims=True)
            outs.append((_dot(p.astype(BF16), vh) / denom).astype(BF16))
        pending = outs[0] if nseg == 1 else jnp.concatenate(outs, axis=0)
    o_ref[...] += _dot(pending, wo_ref[(MEM_HEADS - 1) * MEM_HD:, :])


def mem_attn(x, g, w_q, mk, mv, layer, w_o, seq_len, nseg, seg):
    t, d = x.shape
    b = mk.shape[1]
    mk, mv = _kv_rows(mk), _kv_rows(mv)
    tiles_per_stream = max(seq_len // (nseg * seg), 1)
    n_outer = b // nseg
    tm = nseg * seg
    x_spec = pl.BlockSpec((tm, d), lambda o, i: (o * tiles_per_stream + i, 0))
    kv_spec = pl.BlockSpec((1, nseg) + mk.shape[2:], lambda o, i: (layer, o, 0, 0))
    return pl.pallas_call(
        functools.partial(_mem_attn_kernel, nseg=nseg, seg=seg),
        grid=(n_outer, tiles_per_stream),
        in_specs=[x_spec, _const_spec((1, d)), _const_spec((d, d)), kv_spec, kv_spec,
                  _const_spec((d, d))],
        out_specs=x_spec,
        out_shape=jax.ShapeDtypeStruct((t, d), F32),
        compiler_params=_cparams("parallel", "parallel"),
        name="mem_attn",
    )(x, g.reshape(1, d), w_q, mk, mv, w_o)


def _ffn_kernel(x_ref, g_ref, wup_ref, cw_ref, cb_ref, wdn_ref, st_ref, *rest,
                nseg, seg, fc, final_norm):
    if final_norm:
        gf_ref, o_ref, nst_ref, carry_ref = rest
    else:
        o_ref, nst_ref, carry_ref = rest
    d_ff = wdn_ref.shape[0]
    tm = nseg * seg

    @pl.when(pl.program_id(1) == 0)
    def _():
        carry_ref[...] = st_ref[...]

    x = x_ref[...]
    h = (x * _rms_scale(x) * g_ref[...]).astype(BF16)
    row = lax.broadcasted_iota(jnp.int32, (tm, 1), 0) % seg
    o_ref[...] = x

    def up_proj(f0):
        return (_dot(h, wup_ref[:, f0:f0 + fc]), _dot(h, wup_ref[:, d_ff + f0:d_ff + f0 + fc]))

    nxt = up_proj(0)
    pending = None
    for f0 in range(0, d_ff, fc):
        cols = slice(f0, f0 + fc)
        a, gate = nxt
        if f0 + fc < d_ff:
            nxt = up_proj(f0 + fc)
        if pending is not None:
            o_ref[...] += _dot(pending, wdn_ref[f0 - fc:f0, :])
        p0 = jnp.broadcast_to(carry_ref[:, 0:1, cols], (nseg, seg, fc)).reshape(tm, fc)
        p1 = jnp.broadcast_to(carry_ref[:, 1:2, cols], (nseg, seg, fc)).reshape(tm, fc)
        a1 = jnp.where(row == 0, p1, pltpu.roll(a, 1, 0))
        a2 = jnp.where(row == 0, p0, jnp.where(row == 1, p1, pltpu.roll(a, 2, 0)))
        conv = (cw_ref[0:1, cols] * a2 + cw_ref[1:2, cols] * a1 + cw_ref[2:3, cols] * a
                + cb_ref[:, cols])
        p = jax.nn.gelu(conv) * gate
        pending = p.astype(BF16)
        carry_ref[:, :, cols] = a.reshape(nseg, seg, fc)[:, seg - 2:seg, :]
    y = o_ref[...] + _dot(pending, wdn_ref[d_ff - fc:d_ff, :])
    nst_ref[...] = carry_ref[...]
    if final_norm:
        y = y * _rms_scale(y) * gf_ref[...]
    o_ref[...] = y


def conv_ffn(x, g, w_up, conv_w, conv_b, w_down, conv_state, seq_len, nseg, seg, g_final=None,
             fc=256):
    t, d = x.shape
    d_ff = w_down.shape[0]
    b = conv_state.shape[0]
    tiles_per_stream = max(seq_len // (nseg * seg), 1)
    n_outer = b // nseg
    tm = nseg * seg
    final_norm = g_final is not None
    in_specs = [pl.BlockSpec((tm, d), lambda o, i: (o * tiles_per_stream + i, 0)),
                _const_spec((1, d)),
                _const_spec((d, 2 * d_ff)),
                _const_spec((3, d_ff)),
                _const_spec((1, d_ff)),
                _const_spec((d_ff, d)),
                pl.BlockSpec((nseg, 2, d_ff), lambda o, i: (o, 0, 0))]
    args = [x, g.reshape(1, d), w_up, conv_w, conv_b.reshape(1, d_ff), w_down, conv_state]
    if final_norm:
        in_specs.append(_const_spec((1, d)))
        args.append(g_final.reshape(1, d))
    return pl.pallas_call(
        functools.partial(_ffn_kernel, nseg=nseg, seg=seg, fc=fc, final_norm=final_norm),
        grid=(n_outer, tiles_per_stream),
        in_specs=in_specs,
        out_specs=[pl.BlockSpec((tm, d), lambda o, i: (o * tiles_per_stream + i, 0)),
                   pl.BlockSpec((nseg, 2, d_ff), lambda o, i: (o, 0, 0))],
        out_shape=[jax.ShapeDtypeStruct((t, d), F32),
                   jax.ShapeDtypeStruct((b, 2, d_ff), F32)],
        scratch_shapes=[pltpu.VMEM((nseg, 2, d_ff), F32)],
        compiler_params=_cparams("parallel", "arbitrary"),
        name="conv_ffn",
    )(*args)


def _rotate(x, cos, sin):
    half = x.shape[1] // 2
    x1, x2 = x[:, :half], x[:, half:]
    return jnp.concatenate([x1 * cos - x2 * sin, x2 * cos + x1 * sin], axis=1)


def _ret_head(h, q, k, v, g, cos, sin, s_ref, chunk):
    log_g = RET_LOG_G[h]
    li = lax.broadcasted_iota(jnp.int32, (chunk, chunk), 0)
    mi = lax.broadcasted_iota(jnp.int32, (chunk, chunk), 1)
    diff = (li - mi).astype(F32)
    pos = lax.broadcasted_iota(jnp.int32, (chunk, 1), 0).astype(F32)
    qr = _rotate(q, cos, sin).astype(BF16)
    kr = _rotate(k, cos, sin) * (RET_DK ** -0.5)
    decay = jnp.where(diff >= 0, jnp.exp(jnp.maximum(diff, 0.0) * log_g), 0.0)
    scores = _dot_nt(qr, kr.astype(BF16)) * decay
    intra = _dot(scores.astype(BF16), v)
    s_old = s_ref[0, h]
    cross = _dot(qr, s_old.astype(BF16)) * jnp.exp((pos + 1.0) * log_g)
    o = intra + cross
    k_tail = (kr * jnp.exp((chunk - 1.0 - pos) * log_g)).astype(BF16)
    s_ref[0, h] = math.exp(chunk * log_g) * s_old + lax.dot_general(
        k_tail, v, (((0,), (0,)), ((), ())), preferred_element_type=F32)
    mu = jnp.mean(o, axis=-1, keepdims=True)
    oc = o - mu
    var = jnp.mean(oc * oc, axis=-1, keepdims=True)
    return jax.nn.silu(g) * (oc * lax.rsqrt(var + GN_EPS))


def _ret_kernel(p_ref, cos_ref, sin_ref, s_in_ref, y_ref, s_out_ref, *, chunk):
    @pl.when(pl.program_id(1) == 0)
    def _():
        s_out_ref[...] = s_in_ref[...]

    cos = cos_ref[...]
    sin = sin_ref[...]
    for h in range(RET_HEADS):
        q = p_ref[:, h * RET_DK:(h + 1) * RET_DK].astype(F32)
        k = p_ref[:, RET_QK + h * RET_DK:RET_QK + (h + 1) * RET_DK].astype(F32)
        v = p_ref[:, 2 * RET_QK + h * RET_DV:2 * RET_QK + (h + 1) * RET_DV]
        g = p_ref[:, 2 * RET_QK + RET_V + h * RET_DV:
                  2 * RET_QK + RET_V + (h + 1) * RET_DV].astype(F32)
        y = _ret_head(h, q, k, v, g, cos, sin, s_out_ref, chunk)
        y_ref[:, h * RET_DV:(h + 1) * RET_DV] = y.astype(y_ref.dtype)


def _ret_fused_kernel(x_ref, g_ref, w_ref, wo_ref, cos_ref, sin_ref, s_in_ref, o_ref, s_out_ref,
                      *, chunk):
    @pl.when(pl.program_id(1) == 0)
    def _():
        s_out_ref[...] = s_in_ref[...]

    cos = cos_ref[...]
    sin = sin_ref[...]
    x = x_ref[...]
    hn = (x * _rms_scale(x) * g_ref[...]).astype(BF16)
    o_ref[...] = x

    def proj(h):
        c = [h * RET_DK, RET_QK + h * RET_DK, 2 * RET_QK + h * RET_DV, 2 * RET_QK + RET_V + h * RET_DV]
        return (_dot(hn, w_ref[:, c[0]:c[0] + RET_DK]), _dot(hn, w_ref[:, c[1]:c[1] + RET_DK]),
                _dot(hn, w_ref[:, c[2]:c[2] + RET_DV]).astype(BF16),
                _dot(hn, w_ref[:, c[3]:c[3] + RET_DV]))

    nxt = proj(0)
    pending = None
    for h in range(RET_HEADS):
        q, k, v, g = nxt
        if h + 1 < RET_HEADS:
            nxt = proj(h + 1)
        if pending is not None:
            o_ref[...] += _dot(pending, wo_ref[(h - 1) * RET_DV:h * RET_DV, :])
        pending = _ret_head(h, q, k, v, g, cos, sin, s_out_ref, chunk).astype(BF16)
    o_ref[...] += _dot(pending, wo_ref[(RET_HEADS - 1) * RET_DV:, :])


def _ret_specs(seq_len, chunk):
    nchunks = seq_len // chunk
    half = RET_DK // 2
    row_spec = lambda n: pl.BlockSpec((chunk, n), lambda o, i: (o * nchunks + i, 0))
    rope_spec = pl.BlockSpec((chunk, half), lambda o, i: (i, 0))
    state_spec = pl.BlockSpec((1, RET_HEADS, RET_DK, RET_DV), lambda o, i: (o, 0, 0, 0))
    return nchunks, row_spec, rope_spec, state_spec


def ret_core(p, cos, sin, state, seq_len, chunk):
    t, n = p.shape
    nchunks, row_spec, rope_spec, state_spec = _ret_specs(seq_len, chunk)
    return pl.pallas_call(
        functools.partial(_ret_kernel, chunk=chunk),
        grid=(state.shape[0], nchunks),
        in_specs=[row_spec(n), rope_spec, rope_spec, state_spec],
        out_specs=[row_spec(RET_V), state_spec],
        out_shape=[jax.ShapeDtypeStruct((t, RET_V), BF16),
                   jax.ShapeDtypeStruct(state.shape, F32)],
        compiler_params=_cparams("parallel", "arbitrary"),
        name="ret_core",
    )(p, cos, sin, state)


def ret_mixer(x, g, w_qkvg, w_o, cos, sin, state, seq_len, chunk):
    t, d = x.shape
    nchunks, row_spec, rope_spec, state_spec = _ret_specs(seq_len, chunk)
    return pl.pallas_call(
        functools.partial(_ret_fused_kernel, chunk=chunk),
        grid=(state.shape[0], nchunks),
        in_specs=[row_spec(d), _const_spec((1, d)), _const_spec(w_qkvg.shape),
                  _const_spec(w_o.shape), rope_spec, rope_spec, state_spec],
        out_specs=[row_spec(d), state_spec],
        out_shape=[jax.ShapeDtypeStruct((t, d), F32), jax.ShapeDtypeStruct(state.shape, F32)],
        compiler_params=_cparams("parallel", "arbitrary"),
        name="ret_mixer",
    )(x, g.reshape(1, d), w_qkvg, w_o, cos, sin, state)


def _rope_tables(pos0, length):
    half = RET_DK // 2
    freqs = ROPE_BASE ** (-np.arange(half, dtype=np.float64) / half)
    ang = (pos0 + np.arange(length, dtype=np.float64))[:, None] * freqs[None, :]
    return jnp.asarray(np.cos(ang), F32), jnp.asarray(np.sin(ang), F32)


def _cmul(ar, ai, br, bi):
    return ar * br - ai * bi, ar * bi + ai * br


def _s5_prep_kernel(lam_re_ref, lam_im_ref, ldt_ref, b_re_ref, b_im_ref, c_re_ref, c_im_ref,
                    w_re_ref, w_im_ref, bt_ref, cct_ref, a16_ref):
    dt = jnp.exp(ldt_ref[...])
    lam_re, lam_im = lam_re_ref[...], lam_im_ref[...]
    mag = jnp.exp(lam_re * dt)
    ar, ai = mag * jnp.cos(lam_im * dt), mag * jnp.sin(lam_im * dt)
    den = lam_re * lam_re + lam_im * lam_im
    nr, ni = ar - 1.0, ai
    coef_re = (nr * lam_re + ni * lam_im) / den
    coef_im = (ni * lam_re - nr * lam_im) / den
    bb_re, bb_im = _cmul(coef_re, coef_im, b_re_ref[...], b_im_ref[...])
    c_re, c_im = c_re_ref[...], c_im_ref[...]
    lane = lax.broadcasted_iota(jnp.int32, (1, LANES), 1)
    low = lane < SSM_STATE
    rg = lax.broadcasted_iota(jnp.int32, (LANES, 1), 0) // SSM_GROUP
    pair = rg // 2
    own_half = (lane // SSM_STATE) == (rg % 2)
    same_group = rg == lane // SSM_GROUP
    c_hi, c_lo = _split_bf16(jnp.where(low, c_re, -c_im))
    tk = []
    pr, pi = jnp.ones_like(ar), jnp.zeros_like(ar)
    for j in range(SSM_FOLD):
        zr, zi = _cmul(pr, pi, bb_re, bb_im)
        w_re_ref[SSM_FOLD - 1 - j] = jnp.where(own_half, zr, 0.0)
        w_im_ref[SSM_FOLD - 1 - j] = jnp.where(own_half, zi, 0.0)
        b_hi, b_lo = _split_bf16(jnp.where(low, zr, zi))
        kk = _dot_nt(b_hi, c_hi) + _dot_nt(b_lo, c_hi) + _dot_nt(b_hi, c_lo)
        tk.append(jnp.where(same_group, kk, 0.0).astype(BF16))
        pr, pi = _cmul(pr, pi, ar, ai)
        zr, zi = _cmul(c_re, c_im, pr, pi)
        for ri, z in enumerate((zr, -zi)):
            z = jnp.where(own_half, z, 0.0)
            for t in range(GROUPS_PER_TILE // 2):
                c0 = ri * STATE_PER_TILE + t * LANES
                cct_ref[0, j * LANES:(j + 1) * LANES, c0:c0 + LANES] = (
                    jnp.where(pair == t, z, 0.0).astype(BF16))
    a16_ref[0] = pr
    a16_ref[1] = pi
    zero = jnp.zeros((LANES, LANES), BF16)
    for d2 in range(SSM_FOLD // 2):
        top = jnp.concatenate([tk[2 * d2], tk[2 * d2 + 1]], axis=1)
        bot = jnp.concatenate([tk[2 * d2 - 1] if d2 else zero, tk[2 * d2]], axis=1)
        bt_ref[0, d2] = jnp.concatenate([top, bot], axis=0)


def s5_prepare(a_re, a_im, log_dt, b_re, b_im, c_re, c_im):
    g, p, c = SSM_GROUPS, SSM_STATE, SSM_GROUP
    fold, nt = SSM_FOLD, N_LANE_TILES
    rows = g * c
    twice = lambda x: jnp.tile(x, (1, 2))
    lam = lambda x: twice(jnp.repeat(x, c, axis=0))
    in_spec = pl.BlockSpec((LANES, LANES), lambda t: (t, 0))
    w_spec = pl.BlockSpec((fold, LANES, LANES), lambda t: (0, t, 0))
    w_shape = jax.ShapeDtypeStruct((fold, rows, LANES), F32)
    w_re, w_im, bt, cct, a16 = pl.pallas_call(
        _s5_prep_kernel,
        grid=(nt,),
        in_specs=[in_spec, in_spec, pl.BlockSpec((LANES, 1), lambda t: (t, 0))] + [in_spec] * 4,
        out_specs=[w_spec, w_spec,
                   pl.BlockSpec((1, fold // 2, 2 * LANES, 2 * LANES), lambda t: (t, 0, 0, 0)),
                   pl.BlockSpec((1, fold * LANES, 2 * STATE_PER_TILE), lambda t: (t, 0, 0)),
                   pl.BlockSpec((2, LANES, LANES), lambda t: (0, t, 0))],
        out_shape=[w_shape, w_shape,
                   jax.ShapeDtypeStruct((nt, fold // 2, 2 * LANES, 2 * LANES), BF16),
                   jax.ShapeDtypeStruct((nt, fold * LANES, 2 * STATE_PER_TILE), BF16),
                   jax.ShapeDtypeStruct((2, rows, LANES), F32)],
        compiler_params=_cparams("parallel"),
        name="s5_prep",
    )(lam(a_re), lam(a_im), jnp.repeat(log_dt, c).reshape(rows, 1),
      twice(b_re.transpose(0, 2, 1).reshape(rows, p)), twice(b_im.transpose(0, 2, 1).reshape(rows, p)),
      twice(c_re.reshape(rows, p)), twice(c_im.reshape(rows, p)))
    a16 = a16[:, ::c, :p].reshape(2, 1, N_STATE)
    return bt, cct, w_re, w_im, a16


FOLD_BLOCK = SSM_FOLD * SSM_FOLD


def _fold_perm():
    idx = np.arange(FOLD_BLOCK)
    perm = np.zeros((FOLD_BLOCK, FOLD_BLOCK), np.float32)
    perm[(idx % SSM_FOLD) * SSM_FOLD + idx // SSM_FOLD, idx] = 1.0
    return jnp.asarray(perm, BF16)


def _s5_fold_kernel(x_ref, g_ref, perm_ref, u_ref):
    for blk in range(x_ref.shape[0] // FOLD_BLOCK):
        x = x_ref[blk * FOLD_BLOCK:(blk + 1) * FOLD_BLOCK, :]
        h = (x * _rms_scale(x) * g_ref[...]).astype(BF16)
        f = _dot(perm_ref[...], h).astype(BF16)
        for m in range(SSM_FOLD):
            for k in range(N_LANE_TILES):
                u_ref[m, k, blk * SSM_FOLD:(blk + 1) * SSM_FOLD, :] = (
                    f[m * SSM_FOLD:(m + 1) * SSM_FOLD, k * LANES:(k + 1) * LANES])


def s5_fold(x, g, perm, tokens):
    t, d = x.shape
    rows = t // SSM_FOLD
    return pl.pallas_call(
        _s5_fold_kernel,
        grid=(t // tokens,),
        in_specs=[pl.BlockSpec((tokens, d), lambda i: (i, 0)), _const_spec((1, d)),
                  _const_spec((FOLD_BLOCK, FOLD_BLOCK))],
        out_specs=pl.BlockSpec((SSM_FOLD, N_LANE_TILES, tokens // SSM_FOLD, LANES),
                               lambda i: (0, 0, i, 0)),
        out_shape=jax.ShapeDtypeStruct((SSM_FOLD, N_LANE_TILES, rows, LANES), BF16),
        compiler_params=_cparams("parallel"),
        name="s5_fold",
    )(x, g.reshape(1, d), perm)


def _s5_state_in_kernel(u_ref, w_re_ref, w_im_ref, vre_ref, vim_ref, wd_ref):
    @pl.when(pl.program_id(1) == 0)
    def _():
        pair = lax.broadcasted_iota(jnp.int32, (LANES, 1), 0) // SSM_GROUP // 2
        for m in range(SSM_FOLD):
            for ri, ref in enumerate((w_re_ref, w_im_ref)):
                w = ref[m]
                for t in range(GROUPS_PER_TILE // 2):
                    c0 = ri * STATE_PER_TILE + t * LANES
                    wd_ref[m * LANES:(m + 1) * LANES, c0:c0 + LANES] = (
                        jnp.where(pair == t, w, 0.0).astype(BF16))

    lhs = jnp.concatenate([u_ref[m, 0] for m in range(SSM_FOLD)], axis=1)
    v = _dot(lhs, wd_ref[...])
    vre_ref[...] = v[:, :STATE_PER_TILE]
    vim_ref[...] = v[:, STATE_PER_TILE:]


def _s5_tile_specs(rt):
    u_spec = pl.BlockSpec((SSM_FOLD, 1, rt, LANES), lambda t, i: (0, t, i, 0))
    s_spec = pl.BlockSpec((rt, STATE_PER_TILE), lambda t, i: (i, t))
    return u_spec, s_spec


def s5_state_in(u, w_re, w_im, rt):
    rows = u.shape[2]
    u_spec, s_spec = _s5_tile_specs(rt)
    out = jax.ShapeDtypeStruct((rows, N_STATE), F32)
    w_spec = pl.BlockSpec((SSM_FOLD, LANES, LANES), lambda t, i: (0, t, 0))
    return pl.pallas_call(
        _s5_state_in_kernel,
        grid=(N_LANE_TILES, rows // rt),
        in_specs=[u_spec, w_spec, w_spec],
        out_specs=[s_spec, s_spec],
        out_shape=[out, out],
        scratch_shapes=[pltpu.VMEM((SSM_FOLD * LANES, 2 * STATE_PER_TILE), BF16)],
        compiler_params=_cparams("parallel", "arbitrary"),
        name="s5_state_in",
    )(u, w_re, w_im)


def _s5_scan_pairs_kernel(vre_ref, vim_ref, a_ref, s0re_ref, s0im_ref,
                          sre_ref, sim_ref, fre_ref, fim_ref):
    ar, ai = a_ref[0], a_ref[1]
    s0r, s0i = s0re_ref[...], s0im_ref[...]
    vr, vi = vre_ref[...], vim_ref[...]
    first = lax.broadcasted_iota(jnp.int32, (s0r.shape[0], 1), 0) % 2 == 0
    tr, ti = ar * s0r - ai * s0i + vr, ar * s0i + ai * s0r + vi
    sr = jnp.where(first, s0r, pltpu.roll(tr, 1, 0))
    si = jnp.where(first, s0i, pltpu.roll(ti, 1, 0))
    sre_ref[...] = sr
    sim_ref[...] = si
    fre_ref[...] = ar * sr - ai * si + vr
    fim_ref[...] = ar * si + ai * sr + vi


def _s5_scan_stream_kernel(vre_ref, vim_ref, a_ref, s0re_ref, s0im_ref,
                           sre_ref, sim_ref, fre_ref, fim_ref, *, steps):
    q = lax.broadcasted_iota(jnp.int32, (SUBLANES, 1), 0)
    a1 = (a_ref[0], a_ref[1])
    a2 = _cmul(*a1, *a1)
    a4 = _cmul(*a2, *a2)
    a8 = _cmul(*a4, *a4)
    strides = ((1, a1), (2, a2), (4, a4))
    width = a1[0].shape[1]
    pw = (jnp.ones((SUBLANES, width), F32), jnp.zeros((SUBLANES, width), F32))
    for k, ak in strides:
        nxt = _cmul(*pw, *ak)
        bit = (q & k) != 0
        pw = (jnp.where(bit, nxt[0], pw[0]), jnp.where(bit, nxt[1], pw[1]))

    def shifted(x, k):
        return jnp.where(q >= k, pltpu.roll(x, k, 0), 0.0)

    def body(t, carry):
        cr, ci = carry
        rows = pl.ds(pl.multiple_of(t * SUBLANES, SUBLANES), SUBLANES)
        pr, pi = vre_ref[rows, :], vim_ref[rows, :]
        for k, ak in strides:
            dr, di = _cmul(shifted(pr, k), shifted(pi, k), *ak)
            pr, pi = pr + dr, pi + di
        xr, xi = _cmul(*pw, cr, ci)
        sre_ref[rows, :] = xr + shifted(pr, 1)
        sim_ref[rows, :] = xi + shifted(pi, 1)
        nr, ni = _cmul(*a8, cr, ci)
        return nr + pr[SUBLANES - 1:], ni + pi[SUBLANES - 1:]

    cr, ci = lax.fori_loop(0, steps // SUBLANES, body, (s0re_ref[0], s0im_ref[0]))
    fre_ref[0] = cr
    fim_ref[0] = ci


def s5_scan(v_re, v_im, a16, s0_re, s0_im, width):
    rows, n = v_re.shape
    streams = s0_re.shape[0]
    steps = rows // streams
    a_spec = pl.BlockSpec((2, 1, width), lambda *i: (0, 0, i[-1]))
    vout = jax.ShapeDtypeStruct((rows, n), F32)
    long_streams = steps % SUBLANES == 0
    if long_streams:
        kern = functools.partial(_s5_scan_stream_kernel, steps=steps)
        grid = (streams, n // width)
        vspec = pl.BlockSpec((steps, width), lambda b, c: (b, c))
        sspec = pl.BlockSpec((1, 1, width), lambda b, c: (b, 0, c))
        s0_re, s0_im = s0_re.reshape(streams, 1, n), s0_im.reshape(streams, 1, n)
        sout = jax.ShapeDtypeStruct((streams, 1, n), F32)
    else:
        assert steps == 2 and rows % SUBLANES == 0
        kern = _s5_scan_pairs_kernel
        grid = (n // width,)
        vspec = pl.BlockSpec((rows, width), lambda c: (0, c))
        sspec = vspec
        s0_re, s0_im = jnp.repeat(s0_re, steps, axis=0), jnp.repeat(s0_im, steps, axis=0)
        sout = vout
    s_re, s_im, f_re, f_im = pl.pallas_call(
        kern,
        grid=grid,
        in_specs=[vspec, vspec, a_spec, sspec, sspec],
        out_specs=[vspec, vspec, sspec, sspec],
        out_shape=[vout, vout, sout, sout],
        compiler_params=_cparams(*(["parallel"] * len(grid))),
        name="s5_scan",
    )(v_re, v_im, a16, s0_re, s0_im)
    if long_streams:
        return s_re, s_im, f_re.reshape(streams, n), f_im.reshape(streams, n)
    return s_re, s_im, f_re[steps - 1::steps], f_im[steps - 1::steps]


def _s5_toeplitz_kernel(u_ref, sre_ref, sim_ref, bt_ref, cct_ref, y_ref):
    npair = SSM_FOLD // 2
    u2 = [jnp.concatenate([u_ref[2 * mm, 0], u_ref[2 * mm + 1, 0]], axis=1) for mm in range(npair)]
    s_prev = jnp.concatenate([sre_ref[...], sim_ref[...]], axis=1).astype(BF16)
    for ll in range(npair):
        acc = _dot_nt(s_prev, cct_ref[0, ll * 2 * LANES:(ll + 1) * 2 * LANES, :])
        for mm in range(ll + 1):
            acc = acc + _dot(u2[mm], bt_ref[0, ll - mm])
        y_ref[2 * ll, 0] = acc[:, :LANES]
        y_ref[2 * ll + 1, 0] = acc[:, LANES:]


def s5_toeplitz(u, s_re, s_im, bt, cct, rt):
    rows = u.shape[2]
    u_spec, s_spec = _s5_tile_specs(rt)
    return pl.pallas_call(
        _s5_toeplitz_kernel,
        grid=(N_LANE_TILES, rows // rt),
        in_specs=[u_spec, s_spec, s_spec,
                  pl.BlockSpec((1, SSM_FOLD // 2, 2 * LANES, 2 * LANES), lambda t, i: (t, 0, 0, 0)),
                  pl.BlockSpec((1, SSM_FOLD * LANES, 2 * STATE_PER_TILE), lambda t, i: (t, 0, 0))],
        out_specs=u_spec,
        out_shape=jax.ShapeDtypeStruct(u.shape, F32),
        compiler_params=_cparams("parallel", "parallel"),
        name="s5_toeplitz",
    )(u, s_re, s_im, bt, cct)


def _s5_glu_kernel(x_ref, u_ref, y_ref, d_ref, wglu_ref, bglu_ref, perm_ref, o_ref):
    nblk = x_ref.shape[0] // FOLD_BLOCK
    gl = []
    for blk in range(nblk):
        rows = slice(blk * SSM_FOLD, (blk + 1) * SSM_FOLD)
        for m in range(SSM_FOLD):
            ym = jnp.concatenate([y_ref[m, k, rows, :] for k in range(N_LANE_TILES)], axis=1)
            hm = jnp.concatenate([u_ref[m, k, rows, :] for k in range(N_LANE_TILES)], axis=1)
            gl.append(jax.nn.gelu(ym + d_ref[...] * hm.astype(F32)))
    gl = jnp.concatenate(gl, axis=0)
    out = gl * jax.nn.sigmoid(_dot(gl.astype(BF16), wglu_ref[...]) + bglu_ref[...])
    for blk in range(nblk):
        tok = slice(blk * FOLD_BLOCK, (blk + 1) * FOLD_BLOCK)
        hi, lo = _split_bf16(out[tok])
        o_ref[tok, :] = x_ref[tok, :] + (_dot(perm_ref[...], hi) + _dot(perm_ref[...], lo))


def s5_glu(x, u, y, d, w_glu, b_glu, perm, tokens):
    t, dm = x.shape
    fold_spec = pl.BlockSpec((SSM_FOLD, N_LANE_TILES, tokens // SSM_FOLD, LANES),
                             lambda i: (0, 0, i, 0))
    x_spec = pl.BlockSpec((tokens, dm), lambda i: (i, 0))
    return pl.pallas_call(
        _s5_glu_kernel,
        grid=(t // tokens,),
        in_specs=[x_spec, fold_spec, fold_spec, _const_spec((1, dm)), _const_spec((dm, dm)),
                  _const_spec((1, dm)), _const_spec((FOLD_BLOCK, FOLD_BLOCK))],
        out_specs=x_spec,
        out_shape=jax.ShapeDtypeStruct((t, dm), F32),
        compiler_params=_cparams("parallel"),
        name="s5_glu",
    )(x, u, y, d.reshape(1, dm), w_glu, b_glu.reshape(1, dm), perm)


def s5_mixer(x, st_re, st_im, g, ops, d, w_glu, b_glu):
    bt, cct, w_re, w_im, a16 = ops
    b, l, dm = x.shape
    rows = b * l // SSM_FOLD
    x2 = x.reshape(b * l, dm)
    perm = _fold_perm()
    tokens = min(S5_TOKEN_TILE, b * l)
    rt = min(S5_ROW_TILE, rows)
    u = s5_fold(x2, g, perm, tokens)
    v_re, v_im = s5_state_in(u, w_re, w_im, rt)
    s_re, s_im, f_re, f_im = s5_scan(v_re, v_im, a16, st_re.reshape(b, N_STATE),
                                     st_im.reshape(b, N_STATE), S5_SCAN_WIDTH)
    y = s5_toeplitz(u, s_re, s_im, bt, cct, rt)
    out = s5_glu(x2, u, y, d, w_glu, b_glu, perm, tokens)
    shape = (b, SSM_GROUPS, SSM_STATE)
    return out.reshape(b, l, dm), f_re.reshape(shape), f_im.reshape(shape)


def _stream_tiling(seq_len, n_streams, tile_rows):
    if seq_len >= tile_rows:
        return 1, tile_rows
    return min(n_streams, 256 // seq_len), seq_len


def kernel(x_prompt, x_sample, mem_prompt, state_ssm_re, state_ssm_im, state_ret, cache_mem_k, cache_mem_v, cache_conv, norm_mix, norm_mem_q, norm_mem_kv, norm_ffn, norm_final, ssm_a_re, ssm_a_im, ssm_log_dt, ssm_b_re, ssm_b_im, ssm_c_re, ssm_c_im, ssm_d, ssm_w_glu, ssm_b_glu, ret_w_qkvg, ret_w_o, mem_w_q, mem_w_kv, mem_w_o, ffn_w_up, ffn_conv_w, ffn_conv_b, ffn_w_down):
    bp, lp, dm = x_prompt.shape
    bs, ls, _ = x_sample.shape
    depth = norm_mix.shape[0]
    n_mem = mem_prompt.shape[1]
    d_ff = ffn_w_down.shape[1]
    streams = [dict(x=x, b=x.shape[0], l=x.shape[1],
                    attn_tiling=_stream_tiling(x.shape[1], x.shape[0], ATTN_TILE_ROWS),
                    ffn_tiling=_stream_tiling(x.shape[1], x.shape[0], FFN_TILE_ROWS))
               for x in (x_prompt, x_sample)]
    mem_flat = mem_prompt.reshape(bp * n_mem, dm)
    outs = dict(ssm_re=([], []), ssm_im=([], []), ret=([], []), conv=([], []), mk=[], mv=[])

    for i in range(depth):
        j = i // 2
        if i % 2 == 0:
            ops = s5_prepare(ssm_a_re[j], ssm_a_im[j], ssm_log_dt[j], ssm_b_re[j], ssm_b_im[j],
                             ssm_c_re[j], ssm_c_im[j])
            w_glu = ssm_w_glu[j].astype(BF16)
            zeros = jnp.zeros((bp, SSM_GROUPS, SSM_STATE), F32)
            states = [(zeros, zeros), (state_ssm_re[j], state_ssm_im[j])]
            for si, (s, (st_re, st_im)) in enumerate(zip(streams, states)):
                s["x"], f_re, f_im = s5_mixer(s["x"], st_re, st_im, norm_mix[i], ops, ssm_d[j],
                                              w_glu, ssm_b_glu[j])
                outs["ssm_re"][si].append(f_re)
                outs["ssm_im"][si].append(f_im)
        else:
            w_qkvg = ret_w_qkvg[j].astype(BF16)
            w_o = ret_w_o[j].astype(BF16)
            zeros = jnp.zeros((bp, RET_HEADS, RET_DK, RET_DV), F32)
            chunk = RET_CHUNK_ROWS
            for si, (s, st, pos0) in enumerate(zip(streams, [zeros, state_ret[j]], [0, PAST_LEN])):
                x2 = s["x"].reshape(s["b"] * s["l"], dm)
                cos, sin = _rope_tables(pos0, s["l"])
                if s["l"] >= chunk:
                    y, st_new = ret_mixer(x2, norm_mix[i], w_qkvg, w_o, cos, sin, st, s["l"], chunk)
                else:
                    p = rms_matmul(x2, norm_mix[i], w_qkvg, BF16)
                    y, st_new = ret_core(p, cos, sin, st, s["l"], s["l"])
                    y = matmul_res(y, w_o, x2)
                s["x"] = y.reshape(s["x"].shape)
                outs["ret"][si].append(st_new)

        kv = rms_matmul(mem_flat, norm_mem_kv[i], mem_w_kv[i].astype(BF16), F32)
        mk = kv[:, :dm].reshape(bp, n_mem, MEM_HEADS, MEM_HD)
        mv = kv[:, dm:].reshape(bp, n_mem, MEM_HEADS, MEM_HD)
        outs["mk"].append(mk)
        outs["mv"].append(mv)
        w_q = mem_w_q[i].astype(BF16)
        w_o = mem_w_o[i].astype(BF16)
        for s, (k_, v_, layer) in zip(streams, [(mk[None], mv[None], 0),
                                                (cache_mem_k, cache_mem_v, i)]):
            x2 = s["x"].reshape(s["b"] * s["l"], dm)
            y = mem_attn(x2, norm_mem_q[i], w_q, k_, v_, layer, w_o, s["l"], *s["attn_tiling"])
            s["x"] = y.reshape(s["x"].shape)

        w_up = ffn_w_up[i].astype(BF16)
        w_dn = ffn_w_down[i].astype(BF16)
        g_final = norm_final if i == depth - 1 else None
        conv_states = [jnp.zeros((bp, 2, d_ff), F32), cache_conv[i]]
        for si, (s, cst) in enumerate(zip(streams, conv_states)):
            x2 = s["x"].reshape(s["b"] * s["l"], dm)
            y, cnew = conv_ffn(x2, norm_ffn[i], w_up, ffn_conv_w[i], ffn_conv_b[i], w_dn, cst,
                               s["l"], *s["ffn_tiling"], g_final=g_final)
            s["x"] = y.reshape(s["x"].shape)
            outs["conv"][si].append(cnew)

    return (streams[0]["x"], streams[1]["x"],
            jnp.stack(outs["ssm_re"][0]), jnp.stack(outs["ssm_im"][0]),
            jnp.stack(outs["ssm_re"][1]), jnp.stack(outs["ssm_im"][1]),
            jnp.stack(outs["ret"][0]), jnp.stack(outs["ret"][1]),
            jnp.stack(outs["mk"]), jnp.stack(outs["mv"]),
            jnp.stack(outs["conv"][0]), jnp.stack(outs["conv"][1]))
```

```python
import functools
import math

import numpy as np
import jax
import jax.numpy as jnp
from jax import lax
from jax.experimental import pallas as pl
from jax.experimental.pallas import tpu as pltpu

F32 = jnp.float32
BF16 = jnp.bfloat16

D_MODEL = 1024
PAST_LEN = 1024
EPS = 1e-6
GN_EPS = 1e-5
ROPE_BASE = 10000.0
LANES = 128
SUBLANES = 8
SSM_GROUP = 16
SSM_GROUPS = D_MODEL // SSM_GROUP
SSM_STATE = 64
SSM_FOLD = 16
GROUPS_PER_TILE = LANES // SSM_GROUP
N_LANE_TILES = D_MODEL // LANES
STATE_PER_TILE = GROUPS_PER_TILE * SSM_STATE
N_STATE = SSM_GROUPS * SSM_STATE
RET_HEADS = 4
RET_DK = D_MODEL // RET_HEADS
RET_DV = 2 * D_MODEL // RET_HEADS
RET_QK = RET_HEADS * RET_DK
RET_V = RET_HEADS * RET_DV
RET_LOG_G = tuple(math.log(1.0 - 2.0 ** (-5.0 - h)) for h in range(RET_HEADS))
MEM_HEADS = 4
MEM_HD = D_MODEL // MEM_HEADS
VMEM_LIMIT = 56 * 1024 * 1024
ATTN_TILE_ROWS = 512
FFN_TILE_ROWS = 256
RET_CHUNK_ROWS = 256
S5_TOKEN_TILE = 512
S5_ROW_TILE = 512
S5_SCAN_WIDTH = 1024


def _cparams(*sem):
    return pltpu.CompilerParams(dimension_semantics=sem, vmem_limit_bytes=VMEM_LIMIT)


def _const_spec(shape):
    nd = len(shape)
    return pl.BlockSpec(shape, lambda *_: (0,) * nd, pipeline_mode=pl.Buffered(1))


def _weight_spec(w):
    stack, layer = w
    return pl.BlockSpec((None,) + stack.shape[1:], lambda *_: (layer, 0, 0),
                        pipeline_mode=pl.Buffered(1))


def _rms_scale(x):
    return lax.rsqrt(jnp.mean(x * x, axis=-1, keepdims=True) + EPS)


def _split_bf16(x):
    hi = x.astype(BF16)
    lo = (x - hi.astype(F32)).astype(BF16)
    return hi, lo


def _dot(a, b):
    return jnp.dot(a, b, preferred_element_type=F32)


def _dot_nt(a, b):
    return lax.dot_general(a, b, (((1,), (1,)), ((), ())), preferred_element_type=F32)


def _rms_matmul_kernel(x_ref, g_ref, w_ref, o_ref, *, tn):
    x = x_ref[...]
    h = (x * _rms_scale(x) * g_ref[...]).astype(BF16)
    for n0 in range(0, w_ref.shape[1], tn):
        o_ref[:, n0:n0 + tn] = _dot(h, w_ref[:, n0:n0 + tn]).astype(o_ref.dtype)


def rms_matmul(x, g, w, out_dtype, tm=512, tn=512):
    t, d = x.shape
    n = w[0].shape[2]
    tm = min(tm, t)
    return pl.pallas_call(
        functools.partial(_rms_matmul_kernel, tn=tn),
        grid=(t // tm,),
        in_specs=[pl.BlockSpec((tm, d), lambda i: (i, 0)),
                  _const_spec((1, d)),
                  _weight_spec(w)],
        out_specs=pl.BlockSpec((tm, n), lambda i: (i, 0)),
        out_shape=jax.ShapeDtypeStruct((t, n), out_dtype),
        compiler_params=_cparams("parallel"),
        name="rms_matmul",
    )(x, g.reshape(1, d), w[0])


def _matmul_res_kernel(y_ref, w_ref, x_ref, o_ref):
    o_ref[...] = x_ref[...] + _dot(y_ref[...], w_ref[...])


def matmul_res(y, w, x, tm=512):
    t, k = y.shape
    d = w[0].shape[2]
    tm = min(tm, t)
    return pl.pallas_call(
        _matmul_res_kernel,
        grid=(t // tm,),
        in_specs=[pl.BlockSpec((tm, k), lambda i: (i, 0)),
                  _weight_spec(w),
                  pl.BlockSpec((tm, d), lambda i: (i, 0))],
        out_specs=pl.BlockSpec((tm, d), lambda i: (i, 0)),
        out_shape=jax.ShapeDtypeStruct((t, d), F32),
        compiler_params=_cparams("parallel"),
        name="matmul_res",
    )(y, w[0], x)


MEM_LANE_TILES = MEM_HD // LANES


def _kv_rows(x):
    *lead, n, h, hd = x.shape
    x = x.reshape(*lead, n, h, MEM_LANE_TILES, LANES)
    return jnp.swapaxes(x, -3, -2).reshape(*lead, n * MEM_LANE_TILES * h, LANES)


def _head_rows(kv_ref, s, hd):
    stride = MEM_LANE_TILES * MEM_HEADS
    n_mem = kv_ref.shape[2] // stride
    return jnp.concatenate(
        [kv_ref[0, s, pl.ds(j * MEM_HEADS + hd, n_mem, stride=stride), :]
         for j in range(MEM_LANE_TILES)], axis=1).astype(BF16)


def _mem_attn_kernel(x_ref, g_ref, wq_ref, k_ref, v_ref, wo_ref, o_ref, *, nseg, seg, head_group):
    scale = MEM_HD ** -0.5
    x = x_ref[...]
    h = (x * _rms_scale(x) * g_ref[...]).astype(BF16)
    o_ref[...] = x

    width = head_group * MEM_HD

    def q_proj(grp):
        return _dot(h, wq_ref[:, grp * width:(grp + 1) * width]).astype(BF16)

    nxt = q_proj(0)
    pending = None
    for grp in range(MEM_HEADS // head_group):
        q = nxt
        if (grp + 1) * head_group < MEM_HEADS:
            nxt = q_proj(grp + 1)
        if pending is not None:
            o_ref[...] += _dot(pending, wo_ref[(grp - 1) * width:grp * width, :])
        heads = []
        for hg in range(head_group):
            hd = grp * head_group + hg
            outs = []
            for s in range(nseg):
                kh = _head_rows(k_ref, s, hd)
                vh = _head_rows(v_ref, s, hd)
                sc = _dot_nt(q[s * seg:(s + 1) * seg, hg * MEM_HD:(hg + 1) * MEM_HD], kh) * scale
                p = jnp.exp(sc - jnp.max(sc, axis=-1, keepdims=True))
                denom = jnp.sum(p, axis=-1, keepdims=True)
                outs.append((_dot(p.astype(BF16), vh) / denom).astype(BF16))
            heads.append(outs[0] if nseg == 1 else jnp.concatenate(outs, axis=0))
        pending = heads[0] if head_group == 1 else jnp.concatenate(heads, axis=1)
    o_ref[...] += _dot(pending, wo_ref[MEM_HEADS * MEM_HD - width:, :])


def mem_attn(x, g, w_q, mk, mv, layer, w_o, seq_len, nseg, seg):
    t, d = x.shape
    b = mk.shape[1]
    mk, mv = _kv_rows(mk), _kv_rows(mv)
    tiles_per_stream = max(seq_len // (nseg * seg), 1)
    n_outer = b // nseg
    tm = nseg * seg
    x_spec = pl.BlockSpec((tm, d), lambda o, i: (o * tiles_per_stream + i, 0))
    kv_spec = pl.BlockSpec((1, nseg) + mk.shape[2:], lambda o, i: (layer, o, 0, 0))
    return pl.pallas_call(
        functools.partial(_mem_attn_kernel, nseg=nseg, seg=seg, head_group=2 if nseg == 1 else 1),
        grid=(n_outer, tiles_per_stream),
        in_specs=[x_spec, _const_spec((1, d)), _weight_spec(w_q), kv_spec, kv_spec,
                  _weight_spec(w_o)],
        out_specs=x_spec,
        out_shape=jax.ShapeDtypeStruct((t, d), F32),
        compiler_params=_cparams("parallel", "parallel"),
        name="mem_attn",
    )(x, g.reshape(1, d), w_q[0], mk, mv, w_o[0])


def _ffn_kernel(x_ref, g_ref, wup_ref, cw_ref, cb_ref, wdn_ref, st_ref, *rest,
                nseg, seg, fc, final_norm):
    if final_norm:
        gf_ref, o_ref, nst_ref, carry_ref = rest
    else:
        o_ref, nst_ref, carry_ref = rest
    d_ff = wdn_ref.shape[0]
    tm = nseg * seg

    @pl.when(pl.program_id(1) == 0)
    def _():
        carry_ref[...] = st_ref[...]

    x = x_ref[...]
    h = (x * _rms_scale(x) * g_ref[...]).astype(BF16)
    row = lax.broadcasted_iota(jnp.int32, (tm, 1), 0) % seg
    o_ref[...] = x

    def up_proj(f0):
        return (_dot(h, wup_ref[:, f0:f0 + fc]), _dot(h, wup_ref[:, d_ff + f0:d_ff + f0 + fc]))

    nxt = up_proj(0)
    pending = None
    for f0 in range(0, d_ff, fc):
        cols = slice(f0, f0 + fc)
        a, gate = nxt
        if f0 + fc < d_ff:
            nxt = up_proj(f0 + fc)
        if pending is not None:
            o_ref[...] += _dot(pending, wdn_ref[f0 - fc:f0, :])
        p0 = jnp.broadcast_to(carry_ref[:, 0:1, cols], (nseg, seg, fc)).reshape(tm, fc)
        p1 = jnp.broadcast_to(carry_ref[:, 1:2, cols], (nseg, seg, fc)).reshape(tm, fc)
        a1 = jnp.where(row == 0, p1, pltpu.roll(a, 1, 0))
        a2 = jnp.where(row == 0, p0, jnp.where(row == 1, p1, pltpu.roll(a, 2, 0)))
        conv = (cw_ref[0:1, cols] * a2 + cw_ref[1:2, cols] * a1 + cw_ref[2:3, cols] * a
                + cb_ref[:, cols])
        p = jax.nn.gelu(conv) * gate
        pending = p.astype(BF16)
        carry_ref[:, :, cols] = a.reshape(nseg, seg, fc)[:, seg - 2:seg, :]
    y = o_ref[...] + _dot(pending, wdn_ref[d_ff - fc:d_ff, :])
    nst_ref[...] = carry_ref[...]
    if final_norm:
        y = y * _rms_scale(y) * gf_ref[...]
    o_ref[...] = y


def conv_ffn(x, g, w_up, conv_w, conv_b, w_down, conv_state, seq_len, nseg, seg, g_final=None,
             fc=256):
    t, d = x.shape
    d_ff = w_down[0].shape[1]
    b = conv_state.shape[0]
    tiles_per_stream = max(seq_len // (nseg * seg), 1)
    n_outer = b // nseg
    tm = nseg * seg
    final_norm = g_final is not None
    in_specs = [pl.BlockSpec((tm, d), lambda o, i: (o * tiles_per_stream + i, 0)),
                _const_spec((1, d)),
                _weight_spec(w_up),
                _const_spec((3, d_ff)),
                _const_spec((1, d_ff)),
                _weight_spec(w_down),
                pl.BlockSpec((nseg, 2, d_ff), lambda o, i: (o, 0, 0))]
    args = [x, g.reshape(1, d), w_up[0], conv_w, conv_b.reshape(1, d_ff), w_down[0], conv_state]
    if final_norm:
        in_specs.append(_const_spec((1, d)))
        args.append(g_final.reshape(1, d))
    return pl.pallas_call(
        functools.partial(_ffn_kernel, nseg=nseg, seg=seg, fc=fc, final_norm=final_norm),
        grid=(n_outer, tiles_per_stream),
        in_specs=in_specs,
        out_specs=[pl.BlockSpec((tm, d), lambda o, i: (o * tiles_per_stream + i, 0)),
                   pl.BlockSpec((nseg, 2, d_ff), lambda o, i: (o, 0, 0))],
        out_shape=[jax.ShapeDtypeStruct((t, d), F32),
                   jax.ShapeDtypeStruct((b, 2, d_ff), F32)],
        scratch_shapes=[pltpu.VMEM((nseg, 2, d_ff), F32)],
        compiler_params=_cparams("parallel", "arbitrary"),
        name="conv_ffn",
    )(*args)


def _rotate(x, cos, sin):
    half = x.shape[1] // 2
    x1, x2 = x[:, :half], x[:, half:]
    return jnp.concatenate([x1 * cos - x2 * sin, x2 * cos + x1 * sin], axis=1)


def _ret_head(h, q, k, v, g, cos, sin, s_ref, chunk):
    log_g = RET_LOG_G[h]
    li = lax.broadcasted_iota(jnp.int32, (chunk, chunk), 0)
    mi = lax.broadcasted_iota(jnp.int32, (chunk, chunk), 1)
    diff = (li - mi).astype(F32)
    pos = lax.broadcasted_iota(jnp.int32, (chunk, 1), 0).astype(F32)
    qr = _rotate(q, cos, sin).astype(BF16)
    kr = _rotate(k, cos, sin) * (RET_DK ** -0.5)
    decay = jnp.where(diff >= 0, jnp.exp(jnp.maximum(diff, 0.0) * log_g), 0.0)
    scores = _dot_nt(qr, kr.astype(BF16)) * decay
    intra = _dot(scores.astype(BF16), v)
    s_old = s_ref[0, h]
    cross = _dot(qr, s_old.astype(BF16)) * jnp.exp((pos + 1.0) * log_g)
    o = intra + cross
    k_tail = (kr * jnp.exp((chunk - 1.0 - pos) * log_g)).astype(BF16)
    s_ref[0, h] = math.exp(chunk * log_g) * s_old + lax.dot_general(
        k_tail, v, (((0,), (0,)), ((), ())), preferred_element_type=F32)
    mu = jnp.mean(o, axis=-1, keepdims=True)
    oc = o - mu
    var = jnp.mean(oc * oc, axis=-1, keepdims=True)
    return jax.nn.silu(g) * (oc * lax.rsqrt(var + GN_EPS))


def _ret_kernel(p_ref, cos_ref, sin_ref, s_in_ref, y_ref, s_out_ref, *, chunk):
    @pl.when(pl.program_id(1) == 0)
    def _():
        s_out_ref[...] = s_in_ref[...]

    cos = cos_ref[...]
    sin = sin_ref[...]
    for h in range(RET_HEADS):
        q = p_ref[:, h * RET_DK:(h + 1) * RET_DK].astype(F32)
        k = p_ref[:, RET_QK + h * RET_DK:RET_QK + (h + 1) * RET_DK].astype(F32)
        v = p_ref[:, 2 * RET_QK + h * RET_DV:2 * RET_QK + (h + 1) * RET_DV]
        g = p_ref[:, 2 * RET_QK + RET_V + h * RET_DV:
                  2 * RET_QK + RET_V + (h + 1) * RET_DV].astype(F32)
        y = _ret_head(h, q, k, v, g, cos, sin, s_out_ref, chunk)
        y_ref[:, h * RET_DV:(h + 1) * RET_DV] = y.astype(y_ref.dtype)


def _ret_fused_kernel(x_ref, g_ref, w_ref, wo_ref, cos_ref, sin_ref, s_in_ref, o_ref, s_out_ref,
                      *, chunk):
    @pl.when(pl.program_id(1) == 0)
    def _():
        s_out_ref[...] = s_in_ref[...]

    cos = cos_ref[...]
    sin = sin_ref[...]
    x = x_ref[...]
    hn = (x * _rms_scale(x) * g_ref[...]).astype(BF16)
    o_ref[...] = x

    def proj(h):
        c = [h * RET_DK, RET_QK + h * RET_DK, 2 * RET_QK + h * RET_DV, 2 * RET_QK + RET_V + h * RET_DV]
        return (_dot(hn, w_ref[:, c[0]:c[0] + RET_DK]), _dot(hn, w_ref[:, c[1]:c[1] + RET_DK]),
                _dot(hn, w_ref[:, c[2]:c[2] + RET_DV]).astype(BF16),
                _dot(hn, w_ref[:, c[3]:c[3] + RET_DV]))

    nxt = proj(0)
    pending = None
    for h in range(RET_HEADS):
        q, k, v, g = nxt
        if h + 1 < RET_HEADS:
            nxt = proj(h + 1)
        if pending is not None:
            o_ref[...] += _dot(pending, wo_ref[(h - 1) * RET_DV:h * RET_DV, :])
        pending = _ret_head(h, q, k, v, g, cos, sin, s_out_ref, chunk).astype(BF16)
    o_ref[...] += _dot(pending, wo_ref[(RET_HEADS - 1) * RET_DV:, :])


def _ret_specs(seq_len, chunk):
    nchunks = seq_len // chunk
    half = RET_DK // 2
    row_spec = lambda n: pl.BlockSpec((chunk, n), lambda o, i: (o * nchunks + i, 0))
    rope_spec = pl.BlockSpec((chunk, half), lambda o, i: (i, 0))
    state_spec = pl.BlockSpec((1, RET_HEADS, RET_DK, RET_DV), lambda o, i: (o, 0, 0, 0))
    return nchunks, row_spec, rope_spec, state_spec


def ret_core(p, cos, sin, state, seq_len, chunk):
    t, n = p.shape
    nchunks, row_spec, rope_spec, state_spec = _ret_specs(seq_len, chunk)
    return pl.pallas_call(
        functools.partial(_ret_kernel, chunk=chunk),
        grid=(state.shape[0], nchunks),
        in_specs=[row_spec(n), rope_spec, rope_spec, state_spec],
        out_specs=[row_spec(RET_V), state_spec],
        out_shape=[jax.ShapeDtypeStruct((t, RET_V), BF16),
                   jax.ShapeDtypeStruct(state.shape, F32)],
        compiler_params=_cparams("parallel", "arbitrary"),
        name="ret_core",
    )(p, cos, sin, state)


def ret_mixer(x, g, w_qkvg, w_o, cos, sin, state, seq_len, chunk):
    t, d = x.shape
    nchunks, row_spec, rope_spec, state_spec = _ret_specs(seq_len, chunk)
    return pl.pallas_call(
        functools.partial(_ret_fused_kernel, chunk=chunk),
        grid=(state.shape[0], nchunks),
        in_specs=[row_spec(d), _const_spec((1, d)), _weight_spec(w_qkvg),
                  _weight_spec(w_o), rope_spec, rope_spec, state_spec],
        out_specs=[row_spec(d), state_spec],
        out_shape=[jax.ShapeDtypeStruct((t, d), F32), jax.ShapeDtypeStruct(state.shape, F32)],
        compiler_params=_cparams("parallel", "arbitrary"),
        name="ret_mixer",
    )(x, g.reshape(1, d), w_qkvg[0], w_o[0], cos, sin, state)


def _rope_tables(pos0, length):
    half = RET_DK // 2
    freqs = ROPE_BASE ** (-np.arange(half, dtype=np.float64) / half)
    ang = (pos0 + np.arange(length, dtype=np.float64))[:, None] * freqs[None, :]
    return jnp.asarray(np.cos(ang), F32), jnp.asarray(np.sin(ang), F32)


def _cmul(ar, ai, br, bi):
    return ar * br - ai * bi, ar * bi + ai * br


def _s5_prep_kernel(lam_re_ref, lam_im_ref, ldt_ref, b_re_ref, b_im_ref, c_re_ref, c_im_ref,
                    w_re_ref, w_im_ref, bt_ref, cct_ref, a16_ref):
    dt = jnp.exp(ldt_ref[...])
    lam_re, lam_im = lam_re_ref[...], lam_im_ref[...]
    mag = jnp.exp(lam_re * dt)
    ar, ai = mag * jnp.cos(lam_im * dt), mag * jnp.sin(lam_im * dt)
    den = lam_re * lam_re + lam_im * lam_im
    nr, ni = ar - 1.0, ai
    coef_re = (nr * lam_re + ni * lam_im) / den
    coef_im = (ni * lam_re - nr * lam_im) / den
    bb_re, bb_im = _cmul(coef_re, coef_im, b_re_ref[...], b_im_ref[...])
    c_re, c_im = c_re_ref[...], c_im_ref[...]
    lane = lax.broadcasted_iota(jnp.int32, (1, LANES), 1)
    low = lane < SSM_STATE
    rg = lax.broadcasted_iota(jnp.int32, (LANES, 1), 0) // SSM_GROUP
    pair = rg // 2
    own_half = (lane // SSM_STATE) == (rg % 2)
    same_group = rg == lane // SSM_GROUP
    c_hi, c_lo = _split_bf16(jnp.where(low, c_re, -c_im))
    tk = []
    pr, pi = jnp.ones_like(ar), jnp.zeros_like(ar)
    for j in range(SSM_FOLD):
        zr, zi = _cmul(pr, pi, bb_re, bb_im)
        w_re_ref[SSM_FOLD - 1 - j] = jnp.where(own_half, zr, 0.0)
        w_im_ref[SSM_FOLD - 1 - j] = jnp.where(own_half, zi, 0.0)
        b_hi, b_lo = _split_bf16(jnp.where(low, zr, zi))
        kk = _dot_nt(b_hi, c_hi) + _dot_nt(b_lo, c_hi) + _dot_nt(b_hi, c_lo)
        tk.append(jnp.where(same_group, kk, 0.0).astype(BF16))
        pr, pi = _cmul(pr, pi, ar, ai)
        zr, zi = _cmul(c_re, c_im, pr, pi)
        for ri, z in enumerate((zr, -zi)):
            z = jnp.where(own_half, z, 0.0)
            for t in range(GROUPS_PER_TILE // 2):
                c0 = ri * STATE_PER_TILE + t * LANES
                cct_ref[0, j * LANES:(j + 1) * LANES, c0:c0 + LANES] = (
                    jnp.where(pair == t, z, 0.0).astype(BF16))
    a16_ref[0] = pr
    a16_ref[1] = pi
    zero = jnp.zeros((LANES, LANES), BF16)
    for d2 in range(SSM_FOLD // 2):
        top = jnp.concatenate([tk[2 * d2], tk[2 * d2 + 1]], axis=1)
        bot = jnp.concatenate([tk[2 * d2 - 1] if d2 else zero, tk[2 * d2]], axis=1)
        bt_ref[0, d2] = jnp.concatenate([top, bot], axis=0)


def s5_prepare(a_re, a_im, log_dt, b_re, b_im, c_re, c_im):
    g, p, c = SSM_GROUPS, SSM_STATE, SSM_GROUP
    fold, nt = SSM_FOLD, N_LANE_TILES
    rows = g * c
    twice = lambda x: jnp.tile(x, (1, 2))
    lam = lambda x: twice(jnp.repeat(x, c, axis=0))
    in_spec = pl.BlockSpec((LANES, LANES), lambda t: (t, 0))
    w_spec = pl.BlockSpec((fold, LANES, LANES), lambda t: (0, t, 0))
    w_shape = jax.ShapeDtypeStruct((fold, rows, LANES), F32)
    w_re, w_im, bt, cct, a16 = pl.pallas_call(
        _s5_prep_kernel,
        grid=(nt,),
        in_specs=[in_spec, in_spec, pl.BlockSpec((LANES, 1), lambda t: (t, 0))] + [in_spec] * 4,
        out_specs=[w_spec, w_spec,
                   pl.BlockSpec((1, fold // 2, 2 * LANES, 2 * LANES), lambda t: (t, 0, 0, 0)),
                   pl.BlockSpec((1, fold * LANES, 2 * STATE_PER_TILE), lambda t: (t, 0, 0)),
                   pl.BlockSpec((2, LANES, LANES), lambda t: (0, t, 0))],
        out_shape=[w_shape, w_shape,
                   jax.ShapeDtypeStruct((nt, fold // 2, 2 * LANES, 2 * LANES), BF16),
                   jax.ShapeDtypeStruct((nt, fold * LANES, 2 * STATE_PER_TILE), BF16),
                   jax.ShapeDtypeStruct((2, rows, LANES), F32)],
        compiler_params=_cparams("parallel"),
        name="s5_prep",
    )(lam(a_re), lam(a_im), jnp.repeat(log_dt, c).reshape(rows, 1),
      twice(b_re.transpose(0, 2, 1).reshape(rows, p)), twice(b_im.transpose(0, 2, 1).reshape(rows, p)),
      twice(c_re.reshape(rows, p)), twice(c_im.reshape(rows, p)))
    a16 = a16[:, ::c, :p].reshape(2, 1, N_STATE)
    return bt, cct, w_re, w_im, a16


FOLD_BLOCK = SSM_FOLD * SSM_FOLD


def _fold_perm():
    idx = np.arange(FOLD_BLOCK)
    perm = np.zeros((FOLD_BLOCK, FOLD_BLOCK), np.float32)
    perm[(idx % SSM_FOLD) * SSM_FOLD + idx // SSM_FOLD, idx] = 1.0
    return jnp.asarray(perm, BF16)


def _s5_fold_kernel(x_ref, g_ref, perm_ref, u_ref):
    for blk in range(x_ref.shape[0] // FOLD_BLOCK):
        x = x_ref[blk * FOLD_BLOCK:(blk + 1) * FOLD_BLOCK, :]
        h = (x * _rms_scale(x) * g_ref[...]).astype(BF16)
        f = _dot(perm_ref[...], h).astype(BF16)
        for m in range(SSM_FOLD):
            for k in range(N_LANE_TILES):
                u_ref[m, k, blk * SSM_FOLD:(blk + 1) * SSM_FOLD, :] = (
                    f[m * SSM_FOLD:(m + 1) * SSM_FOLD, k * LANES:(k + 1) * LANES])


def s5_fold(x, g, perm, tokens):
    t, d = x.shape
    rows = t // SSM_FOLD
    return pl.pallas_call(
        _s5_fold_kernel,
        grid=(t // tokens,),
        in_specs=[pl.BlockSpec((tokens, d), lambda i: (i, 0)), _const_spec((1, d)),
                  _const_spec((FOLD_BLOCK, FOLD_BLOCK))],
        out_specs=pl.BlockSpec((SSM_FOLD, N_LANE_TILES, tokens // SSM_FOLD, LANES),
                               lambda i: (0, 0, i, 0)),
        out_shape=jax.ShapeDtypeStruct((SSM_FOLD, N_LANE_TILES, rows, LANES), BF16),
        compiler_params=_cparams("parallel"),
        name="s5_fold",
    )(x, g.reshape(1, d), perm)


def _s5_state_in_kernel(u_ref, w_re_ref, w_im_ref, vre_ref, vim_ref, wd_ref):
    @pl.when(pl.program_id(1) == 0)
    def _():
        pair = lax.broadcasted_iota(jnp.int32, (LANES, 1), 0) // SSM_GROUP // 2
        for m in range(SSM_FOLD):
            for ri, ref in enumerate((w_re_ref, w_im_ref)):
                w = ref[m]
                for t in range(GROUPS_PER_TILE // 2):
                    c0 = ri * STATE_PER_TILE + t * LANES
                    wd_ref[m * LANES:(m + 1) * LANES, c0:c0 + LANES] = (
                        jnp.where(pair == t, w, 0.0).astype(BF16))

    lhs = jnp.concatenate([u_ref[m, 0] for m in range(SSM_FOLD)], axis=1)
    v = _dot(lhs, wd_ref[...])
    vre_ref[...] = v[:, :STATE_PER_TILE]
    vim_ref[...] = v[:, STATE_PER_TILE:]


def _s5_tile_specs(rt):
    u_spec = pl.BlockSpec((SSM_FOLD, 1, rt, LANES), lambda t, i: (0, t, i, 0))
    s_spec = pl.BlockSpec((rt, STATE_PER_TILE), lambda t, i: (i, t))
    return u_spec, s_spec


def s5_state_in(u, w_re, w_im, rt):
    rows = u.shape[2]
    u_spec, s_spec = _s5_tile_specs(rt)
    out = jax.ShapeDtypeStruct((rows, N_STATE), F32)
    w_spec = pl.BlockSpec((SSM_FOLD, LANES, LANES), lambda t, i: (0, t, 0))
    return pl.pallas_call(
        _s5_state_in_kernel,
        grid=(N_LANE_TILES, rows // rt),
        in_specs=[u_spec, w_spec, w_spec],
        out_specs=[s_spec, s_spec],
        out_shape=[out, out],
        scratch_shapes=[pltpu.VMEM((SSM_FOLD * LANES, 2 * STATE_PER_TILE), BF16)],
        compiler_params=_cparams("parallel", "arbitrary"),
        name="s5_state_in",
    )(u, w_re, w_im)


def _s5_scan_pairs_kernel(vre_ref, vim_ref, a_ref, s0re_ref, s0im_ref,
                          sre_ref, sim_ref, fre_ref, fim_ref):
    ar, ai = a_ref[0], a_ref[1]
    s0r, s0i = s0re_ref[...], s0im_ref[...]
    vr, vi = vre_ref[...], vim_ref[...]
    first = lax.broadcasted_iota(jnp.int32, (s0r.shape[0], 1), 0) % 2 == 0
    tr, ti = ar * s0r - ai * s0i + vr, ar * s0i + ai * s0r + vi
    sr = jnp.where(first, s0r, pltpu.roll(tr, 1, 0))
    si = jnp.where(first, s0i, pltpu.roll(ti, 1, 0))
    sre_ref[...] = sr
    sim_ref[...] = si
    fre_ref[...] = ar * sr - ai * si + vr
    fim_ref[...] = ar * si + ai * sr + vi


def _s5_scan_stream_kernel(vre_ref, vim_ref, a_ref, s0re_ref, s0im_ref,
                           sre_ref, sim_ref, fre_ref, fim_ref, *, steps):
    q = lax.broadcasted_iota(jnp.int32, (SUBLANES, 1), 0)
    a1 = (a_ref[0], a_ref[1])
    a2 = _cmul(*a1, *a1)
    a4 = _cmul(*a2, *a2)
    a8 = _cmul(*a4, *a4)
    strides = ((1, a1), (2, a2), (4, a4))
    width = a1[0].shape[1]
    pw = (jnp.ones((SUBLANES, width), F32), jnp.zeros((SUBLANES, width), F32))
    for k, ak in strides:
        nxt = _cmul(*pw, *ak)
        bit = (q & k) != 0
        pw = (jnp.where(bit, nxt[0], pw[0]), jnp.where(bit, nxt[1], pw[1]))

    def shifted(x, k):
        return jnp.where(q >= k, pltpu.roll(x, k, 0), 0.0)

    def body(t, carry):
        cr, ci = carry
        rows = pl.ds(pl.multiple_of(t * SUBLANES, SUBLANES), SUBLANES)
        pr, pi = vre_ref[rows, :], vim_ref[rows, :]
        for k, ak in strides:
            dr, di = _cmul(shifted(pr, k), shifted(pi, k), *ak)
            pr, pi = pr + dr, pi + di
        xr, xi = _cmul(*pw, cr, ci)
        sre_ref[rows, :] = xr + shifted(pr, 1)
        sim_ref[rows, :] = xi + shifted(pi, 1)
        nr, ni = _cmul(*a8, cr, ci)
        return nr + pr[SUBLANES - 1:], ni + pi[SUBLANES - 1:]

    cr, ci = lax.fori_loop(0, steps // SUBLANES, body, (s0re_ref[0], s0im_ref[0]))
    fre_ref[0] = cr
    fim_ref[0] = ci


def s5_scan(v_re, v_im, a16, s0_re, s0_im, width):
    rows, n = v_re.shape
    streams = s0_re.shape[0]
    steps = rows // streams
    a_spec = pl.BlockSpec((2, 1, width), lambda *i: (0, 0, i[-1]))
    vout = jax.ShapeDtypeStruct((rows, n), F32)
    long_streams = steps % SUBLANES == 0
    if long_streams:
        kern = functools.partial(_s5_scan_stream_kernel, steps=steps)
        grid = (streams, n // width)
        vspec = pl.BlockSpec((steps, width), lambda b, c: (b, c))
        sspec = pl.BlockSpec((1, 1, width), lambda b, c: (b, 0, c))
        s0_re, s0_im = s0_re.reshape(streams, 1, n), s0_im.reshape(streams, 1, n)
        sout = jax.ShapeDtypeStruct((streams, 1, n), F32)
    else:
        assert steps == 2 and rows % SUBLANES == 0
        kern = _s5_scan_pairs_kernel
        grid = (n // width,)
        vspec = pl.BlockSpec((rows, width), lambda c: (0, c))
        sspec = vspec
        s0_re, s0_im = jnp.repeat(s0_re, steps, axis=0), jnp.repeat(s0_im, steps, axis=0)
        sout = vout
    s_re, s_im, f_re, f_im = pl.pallas_call(
        kern,
        grid=grid,
        in_specs=[vspec, vspec, a_spec, sspec, sspec],
        out_specs=[vspec, vspec, sspec, sspec],
        out_shape=[vout, vout, sout, sout],
        compiler_params=_cparams(*(["parallel"] * len(grid))),
        name="s5_scan",
    )(v_re, v_im, a16, s0_re, s0_im)
    if long_streams:
        return s_re, s_im, f_re.reshape(streams, n), f_im.reshape(streams, n)
    return s_re, s_im, f_re[steps - 1::steps], f_im[steps - 1::steps]


def _s5_toeplitz_kernel(u_ref, sre_ref, sim_ref, bt_ref, cct_ref, y_ref):
    npair = SSM_FOLD // 2
    u2 = [jnp.concatenate([u_ref[2 * mm, 0], u_ref[2 * mm + 1, 0]], axis=1) for mm in range(npair)]
    s_prev = jnp.concatenate([sre_ref[...], sim_ref[...]], axis=1).astype(BF16)
    for ll in range(npair):
        acc = _dot_nt(s_prev, cct_ref[0, ll * 2 * LANES:(ll + 1) * 2 * LANES, :])
        for mm in range(ll + 1):
            acc = acc + _dot(u2[mm], bt_ref[0, ll - mm])
        y_ref[2 * ll, 0] = acc[:, :LANES]
        y_ref[2 * ll + 1, 0] = acc[:, LANES:]


def s5_toeplitz(u, s_re, s_im, bt, cct, rt):
    rows = u.shape[2]
    u_spec, s_spec = _s5_tile_specs(rt)
    return pl.pallas_call(
        _s5_toeplitz_kernel,
        grid=(N_LANE_TILES, rows // rt),
        in_specs=[u_spec, s_spec, s_spec,
                  pl.BlockSpec((1, SSM_FOLD // 2, 2 * LANES, 2 * LANES), lambda t, i: (t, 0, 0, 0)),
                  pl.BlockSpec((1, SSM_FOLD * LANES, 2 * STATE_PER_TILE), lambda t, i: (t, 0, 0))],
        out_specs=u_spec,
        out_shape=jax.ShapeDtypeStruct(u.shape, F32),
        compiler_params=_cparams("parallel", "parallel"),
        name="s5_toeplitz",
    )(u, s_re, s_im, bt, cct)


def _s5_glu_kernel(x_ref, u_ref, y_ref, d_ref, wglu_ref, bglu_ref, perm_ref, o_ref):
    nblk = x_ref.shape[0] // FOLD_BLOCK
    gl = []
    for blk in range(nblk):
        rows = slice(blk * SSM_FOLD, (blk + 1) * SSM_FOLD)
        for m in range(SSM_FOLD):
            ym = jnp.concatenate([y_ref[m, k, rows, :] for k in range(N_LANE_TILES)], axis=1)
            hm = jnp.concatenate([u_ref[m, k, rows, :] for k in range(N_LANE_TILES)], axis=1)
            gl.append(jax.nn.gelu(ym + d_ref[...] * hm.astype(F32)))
    gl = jnp.concatenate(gl, axis=0)
    out = gl * jax.nn.sigmoid(_dot(gl.astype(BF16), wglu_ref[...]) + bglu_ref[...])
    for blk in range(nblk):
        tok = slice(blk * FOLD_BLOCK, (blk + 1) * FOLD_BLOCK)
        hi, lo = _split_bf16(out[tok])
        o_ref[tok, :] = x_ref[tok, :] + (_dot(perm_ref[...], hi) + _dot(perm_ref[...], lo))


def s5_glu(x, u, y, d, w_glu, b_glu, perm, tokens):
    t, dm = x.shape
    fold_spec = pl.BlockSpec((SSM_FOLD, N_LANE_TILES, tokens // SSM_FOLD, LANES),
                             lambda i: (0, 0, i, 0))
    x_spec = pl.BlockSpec((tokens, dm), lambda i: (i, 0))
    return pl.pallas_call(
        _s5_glu_kernel,
        grid=(t // tokens,),
        in_specs=[x_spec, fold_spec, fold_spec, _const_spec((1, dm)), _weight_spec(w_glu),
                  _const_spec((1, dm)), _const_spec((FOLD_BLOCK, FOLD_BLOCK))],
        out_specs=x_spec,
        out_shape=jax.ShapeDtypeStruct((t, dm), F32),
        compiler_params=_cparams("parallel"),
        name="s5_glu",
    )(x, u, y, d.reshape(1, dm), w_glu[0], b_glu.reshape(1, dm), perm)


def s5_mixer(x, st_re, st_im, g, ops, d, w_glu, b_glu):
    bt, cct, w_re, w_im, a16 = ops
    b, l, dm = x.shape
    rows = b * l // SSM_FOLD
    x2 = x.reshape(b * l, dm)
    perm = _fold_perm()
    tokens = min(S5_TOKEN_TILE, b * l)
    rt = min(S5_ROW_TILE, rows)
    u = s5_fold(x2, g, perm, tokens)
    v_re, v_im = s5_state_in(u, w_re, w_im, rt)
    s_re, s_im, f_re, f_im = s5_scan(v_re, v_im, a16, st_re.reshape(b, N_STATE),
                                     st_im.reshape(b, N_STATE), S5_SCAN_WIDTH)
    y = s5_toeplitz(u, s_re, s_im, bt, cct, rt)
    out = s5_glu(x2, u, y, d, w_glu, b_glu, perm, tokens)
    shape = (b, SSM_GROUPS, SSM_STATE)
    return out.reshape(b, l, dm), f_re.reshape(shape), f_im.reshape(shape)


def _stream_tiling(seq_len, n_streams, tile_rows):
    if seq_len >= tile_rows:
        return 1, tile_rows
    return min(n_streams, 256 // seq_len), seq_len


def kernel(x_prompt, x_sample, mem_prompt, state_ssm_re, state_ssm_im, state_ret, cache_mem_k, cache_mem_v, cache_conv, norm_mix, norm_mem_q, norm_mem_kv, norm_ffn, norm_final, ssm_a_re, ssm_a_im, ssm_log_dt, ssm_b_re, ssm_b_im, ssm_c_re, ssm_c_im, ssm_d, ssm_w_glu, ssm_b_glu, ret_w_qkvg, ret_w_o, mem_w_q, mem_w_kv, mem_w_o, ffn_w_up, ffn_conv_w, ffn_conv_b, ffn_w_down):
    bp, lp, dm = x_prompt.shape
    bs, ls, _ = x_sample.shape
    depth = norm_mix.shape[0]
    n_mem = mem_prompt.shape[1]
    d_ff = ffn_w_down.shape[1]
    streams = [dict(x=x, b=x.shape[0], l=x.shape[1],
                    attn_tiling=_stream_tiling(x.shape[1], x.shape[0], ATTN_TILE_ROWS),
                    ffn_tiling=_stream_tiling(x.shape[1], x.shape[0], FFN_TILE_ROWS))
               for x in (x_prompt, x_sample)]
    mem_flat = mem_prompt.reshape(bp * n_mem, dm)
    ssm_w_glu, ret_w_qkvg, ret_w_o, mem_w_q, mem_w_kv, mem_w_o, ffn_w_up, ffn_w_down = (
        w.astype(BF16) for w in (ssm_w_glu, ret_w_qkvg, ret_w_o, mem_w_q, mem_w_kv, mem_w_o,
                                 ffn_w_up, ffn_w_down))
    outs = dict(ssm_re=([], []), ssm_im=([], []), ret=([], []), conv=([], []), mk=[], mv=[])

    for i in range(depth):
        j = i // 2
        if i % 2 == 0:
            ops = s5_prepare(ssm_a_re[j], ssm_a_im[j], ssm_log_dt[j], ssm_b_re[j], ssm_b_im[j],
                             ssm_c_re[j], ssm_c_im[j])
            w_glu = (ssm_w_glu, j)
            zeros = jnp.zeros((bp, SSM_GROUPS, SSM_STATE), F32)
            states = [(zeros, zeros), (state_ssm_re[j], state_ssm_im[j])]
            for si, (s, (st_re, st_im)) in enumerate(zip(streams, states)):
                s["x"], f_re, f_im = s5_mixer(s["x"], st_re, st_im, norm_mix[i], ops, ssm_d[j],
                                              w_glu, ssm_b_glu[j])
                outs["ssm_re"][si].append(f_re)
                outs["ssm_im"][si].append(f_im)
        else:
            w_qkvg = (ret_w_qkvg, j)
            w_o = (ret_w_o, j)
            zeros = jnp.zeros((bp, RET_HEADS, RET_DK, RET_DV), F32)
            chunk = RET_CHUNK_ROWS
            for si, (s, st, pos0) in enumerate(zip(streams, [zeros, state_ret[j]], [0, PAST_LEN])):
                x2 = s["x"].reshape(s["b"] * s["l"], dm)
                cos, sin = _rope_tables(pos0, s["l"])
                if s["l"] >= chunk:
                    y, st_new = ret_mixer(x2, norm_mix[i], w_qkvg, w_o, cos, sin, st, s["l"], chunk)
                else:
                    p = rms_matmul(x2, norm_mix[i], w_qkvg, BF16)
                    y, st_new = ret_core(p, cos, sin, st, s["l"], s["l"])
                    y = matmul_res(y, w_o, x2)
                s["x"] = y.reshape(s["x"].shape)
                outs["ret"][si].append(st_new)

        kv = rms_matmul(mem_flat, norm_mem_kv[i], (mem_w_kv, i), F32)
        mk = kv[:, :dm].reshape(bp, n_mem, MEM_HEADS, MEM_HD)
        mv = kv[:, dm:].reshape(bp, n_mem, MEM_HEADS, MEM_HD)
        outs["mk"].append(mk)
        outs["mv"].append(mv)
        w_q = (mem_w_q, i)
        w_o = (mem_w_o, i)
        for s, (k_, v_, layer) in zip(streams, [(mk[None], mv[None], 0),
                                                (cache_mem_k, cache_mem_v, i)]):
            x2 = s["x"].reshape(s["b"] * s["l"], dm)
            y = mem_attn(x2, norm_mem_q[i], w_q, k_, v_, layer, w_o, s["l"], *s["attn_tiling"])
            s["x"] = y.reshape(s["x"].shape)

        w_up = (ffn_w_up, i)
        w_dn = (ffn_w_down, i)
        g_final = norm_final if i == depth - 1 else None
        conv_states = [jnp.zeros((bp, 2, d_ff), F32), cache_conv[i]]
        for si, (s, cst) in enumerate(zip(streams, conv_states)):
            x2 = s["x"].reshape(s["b"] * s["l"], dm)
            y, cnew = conv_ffn(x2, norm_ffn[i], w_up, ffn_conv_w[i], ffn_conv_b[i], w_dn, cst,
                               s["l"], *s["ffn_tiling"], g_final=g_final)
            s["x"] = y.reshape(s["x"].shape)
            outs["conv"][si].append(cnew)

    return (streams[0]["x"], streams[1]["x"],
            jnp.stack(outs["ssm_re"][0]), jnp.stack(outs["ssm_im"][0]),
            jnp.stack(outs["ssm_re"][1]), jnp.stack(outs["ssm_im"][1]),
            jnp.stack(outs["ret"][0]), jnp.stack(outs["ret"][1]),
            jnp.stack(outs["mk"]), jnp.stack(outs["mv"]),
            jnp.stack(outs["conv"][0]), jnp.stack(outs["conv"][1]))
```

```python
import functools
import math

import numpy as np
import jax
import jax.numpy as jnp
from jax import lax
from jax.experimental import pallas as pl
from jax.experimental.pallas import tpu as pltpu

F32 = jnp.float32
BF16 = jnp.bfloat16

D_MODEL = 1024
PAST_LEN = 1024
EPS = 1e-6
GN_EPS = 1e-5
ROPE_BASE = 10000.0
LANES = 128
SUBLANES = 8
SSM_GROUP = 16
SSM_GROUPS = D_MODEL // SSM_GROUP
SSM_STATE = 64
SSM_FOLD = 16
GROUPS_PER_TILE = LANES // SSM_GROUP
N_LANE_TILES = D_MODEL // LANES
STATE_PER_TILE = GROUPS_PER_TILE * SSM_STATE
N_STATE = SSM_GROUPS * SSM_STATE
RET_HEADS = 4
RET_DK = D_MODEL // RET_HEADS
RET_DV = 2 * D_MODEL // RET_HEADS
RET_QK = RET_HEADS * RET_DK
RET_V = RET_HEADS * RET_DV
RET_HEAD_GROUP = 2
RET_LOG_G = tuple(math.log(1.0 - 2.0 ** (-5.0 - h)) for h in range(RET_HEADS))
MEM_HEADS = 4
MEM_HD = D_MODEL // MEM_HEADS
VMEM_LIMIT = 56 * 1024 * 1024
ATTN_TILE_ROWS = 512
FFN_TILE_ROWS = 256
RET_CHUNK_ROWS = 256
S5_TOKEN_TILE = 512
S5_ROW_TILE = 512
S5_SCAN_WIDTH = 1024


def _cparams(*sem):
    return pltpu.CompilerParams(dimension_semantics=sem, vmem_limit_bytes=VMEM_LIMIT)


def _const_spec(shape):
    nd = len(shape)
    return pl.BlockSpec(shape, lambda *_: (0,) * nd, pipeline_mode=pl.Buffered(1))


def _weight_spec(w):
    stack, layer = w
    return pl.BlockSpec((None,) + stack.shape[1:], lambda *_: (layer, 0, 0),
                        pipeline_mode=pl.Buffered(1))


def _rms_scale(x):
    return lax.rsqrt(jnp.mean(x * x, axis=-1, keepdims=True) + EPS)


def _split_bf16(x):
    hi = x.astype(BF16)
    lo = (x - hi.astype(F32)).astype(BF16)
    return hi, lo


def _dot(a, b):
    return jnp.dot(a, b, preferred_element_type=F32)


def _dot_nt(a, b):
    return lax.dot_general(a, b, (((1,), (1,)), ((), ())), preferred_element_type=F32)


def _rms_matmul_kernel(x_ref, g_ref, w_ref, o_ref, *, tn):
    x = x_ref[...]
    h = (x * _rms_scale(x) * g_ref[...]).astype(BF16)
    for n0 in range(0, w_ref.shape[1], tn):
        o_ref[:, n0:n0 + tn] = _dot(h, w_ref[:, n0:n0 + tn]).astype(o_ref.dtype)


def rms_matmul(x, g, w, out_dtype, tm=512, tn=512):
    t, d = x.shape
    n = w[0].shape[2]
    tm = min(tm, t)
    return pl.pallas_call(
        functools.partial(_rms_matmul_kernel, tn=tn),
        grid=(t // tm,),
        in_specs=[pl.BlockSpec((tm, d), lambda i: (i, 0)),
                  _const_spec((1, d)),
                  _weight_spec(w)],
        out_specs=pl.BlockSpec((tm, n), lambda i: (i, 0)),
        out_shape=jax.ShapeDtypeStruct((t, n), out_dtype),
        compiler_params=_cparams("parallel"),
        name="rms_matmul",
    )(x, g.reshape(1, d), w[0])


def _matmul_res_kernel(y_ref, w_ref, x_ref, o_ref):
    o_ref[...] = x_ref[...] + _dot(y_ref[...], w_ref[...])


def matmul_res(y, w, x, tm=512):
    t, k = y.shape
    d = w[0].shape[2]
    tm = min(tm, t)
    return pl.pallas_call(
        _matmul_res_kernel,
        grid=(t // tm,),
        in_specs=[pl.BlockSpec((tm, k), lambda i: (i, 0)),
                  _weight_spec(w),
                  pl.BlockSpec((tm, d), lambda i: (i, 0))],
        out_specs=pl.BlockSpec((tm, d), lambda i: (i, 0)),
        out_shape=jax.ShapeDtypeStruct((t, d), F32),
        compiler_params=_cparams("parallel"),
        name="matmul_res",
    )(y, w[0], x)


MEM_LANE_TILES = MEM_HD // LANES


def _kv_rows(x):
    *lead, n, h, hd = x.shape
    x = x.reshape(*lead, n, h, MEM_LANE_TILES, LANES)
    return jnp.swapaxes(x, -3, -2).reshape(*lead, n * MEM_LANE_TILES * h, LANES)


def _kv_unrows(x, heads):
    *lead, rows, _ = x.shape
    n = rows // (MEM_LANE_TILES * heads)
    x = x.reshape(*lead, n, MEM_LANE_TILES, heads, LANES)
    return jnp.swapaxes(x, -3, -2).reshape(*lead, n, heads, MEM_LANE_TILES * LANES)


def _mem_kv_kernel(x_ref, g_ref, w_ref, k_ref, v_ref):
    x = x_ref[...]
    h = (x * _rms_scale(x) * g_ref[...]).astype(BF16)
    n_mem, d = x.shape
    stride = MEM_LANE_TILES * MEM_HEADS
    for which, ref in enumerate((k_ref, v_ref)):
        for hd in range(MEM_HEADS):
            c0 = which * d + hd * MEM_HD
            res = _dot(h, w_ref[:, c0:c0 + MEM_HD])
            for j in range(MEM_LANE_TILES):
                ref[pl.ds(j * MEM_HEADS + hd, n_mem, stride=stride), :] = (
                    res[:, j * LANES:(j + 1) * LANES])


def mem_kv(mem, g, w_kv):
    b, n_mem, d = mem.shape
    rows = n_mem * MEM_LANE_TILES * MEM_HEADS
    out_spec = pl.BlockSpec((None, rows, LANES), lambda i: (i, 0, 0))
    out = jax.ShapeDtypeStruct((b, rows, LANES), F32)
    return pl.pallas_call(
        _mem_kv_kernel,
        grid=(b,),
        in_specs=[pl.BlockSpec((None, n_mem, d), lambda i: (i, 0, 0)), _const_spec((1, d)),
                  _weight_spec(w_kv)],
        out_specs=[out_spec, out_spec],
        out_shape=[out, out],
        compiler_params=_cparams("parallel"),
        name="mem_kv",
    )(mem, g.reshape(1, d), w_kv[0])


def _head_rows(kv_ref, s, hd):
    stride = MEM_LANE_TILES * MEM_HEADS
    n_mem = kv_ref.shape[2] // stride
    return jnp.concatenate(
        [kv_ref[0, s, pl.ds(j * MEM_HEADS + hd, n_mem, stride=stride), :]
         for j in range(MEM_LANE_TILES)], axis=1).astype(BF16)


def _mem_attn_kernel(x_ref, g_ref, wq_ref, k_ref, v_ref, wo_ref, o_ref, *, nseg, seg, head_group):
    scale = MEM_HD ** -0.5
    x = x_ref[...]
    h = (x * _rms_scale(x) * g_ref[...]).astype(BF16)
    o_ref[...] = x

    width = head_group * MEM_HD

    def q_proj(grp):
        return _dot(h, wq_ref[:, grp * width:(grp + 1) * width]).astype(BF16)

    nxt = q_proj(0)
    pending = None
    for grp in range(MEM_HEADS // head_group):
        q = nxt
        if (grp + 1) * head_group < MEM_HEADS:
            nxt = q_proj(grp + 1)
        if pending is not None:
            o_ref[...] += _dot(pending, wo_ref[(grp - 1) * width:grp * width, :])
        heads = []
        for hg in range(head_group):
            hd = grp * head_group + hg
            outs = []
            for s in range(nseg):
                kh = _head_rows(k_ref, s, hd)
                vh = _head_rows(v_ref, s, hd)
                sc = _dot_nt(q[s * seg:(s + 1) * seg, hg * MEM_HD:(hg + 1) * MEM_HD], kh) * scale
                p = jnp.exp(sc - jnp.max(sc, axis=-1, keepdims=True))
                denom = jnp.sum(p, axis=-1, keepdims=True)
                outs.append((_dot(p.astype(BF16), vh) / denom).astype(BF16))
            heads.append(outs[0] if nseg == 1 else jnp.concatenate(outs, axis=0))
        pending = heads[0] if head_group == 1 else jnp.concatenate(heads, axis=1)
    o_ref[...] += _dot(pending, wo_ref[MEM_HEADS * MEM_HD - width:, :])


def mem_attn(x, g, w_q, mk, mv, layer, w_o, seq_len, nseg, seg):
    t, d = x.shape
    b = mk.shape[1]
    tiles_per_stream = max(seq_len // (nseg * seg), 1)
    n_outer = b // nseg
    tm = nseg * seg
    x_spec = pl.BlockSpec((tm, d), lambda o, i: (o * tiles_per_stream + i, 0))
    kv_spec = pl.BlockSpec((1, nseg) + mk.shape[2:], lambda o, i: (layer, o, 0, 0))
    return pl.pallas_call(
        functools.partial(_mem_attn_kernel, nseg=nseg, seg=seg, head_group=2 if nseg == 1 else 1),
        grid=(n_outer, tiles_per_stream),
        in_specs=[x_spec, _const_spec((1, d)), _weight_spec(w_q), kv_spec, kv_spec,
                  _weight_spec(w_o)],
        out_specs=x_spec,
        out_shape=jax.ShapeDtypeStruct((t, d), F32),
        compiler_params=_cparams("parallel", "parallel"),
        name="mem_attn",
    )(x, g.reshape(1, d), w_q[0], mk, mv, w_o[0])


def _ffn_kernel(x_ref, g_ref, wup_ref, cw_ref, cb_ref, wdn_ref, st_ref, *rest,
                nseg, seg, fc, final_norm):
    if final_norm:
        gf_ref, o_ref, nst_ref, carry_ref = rest
    else:
        o_ref, nst_ref, carry_ref = rest
    d_ff = wdn_ref.shape[0]
    tm = nseg * seg

    @pl.when(pl.program_id(1) == 0)
    def _():
        carry_ref[...] = st_ref[...]

    x = x_ref[...]
    h = (x * _rms_scale(x) * g_ref[...]).astype(BF16)
    row = lax.broadcasted_iota(jnp.int32, (tm, 1), 0) % seg
    o_ref[...] = x

    def up_proj(f0):
        return (_dot(h, wup_ref[:, f0:f0 + fc]), _dot(h, wup_ref[:, d_ff + f0:d_ff + f0 + fc]))

    nxt = up_proj(0)
    pending = None
    for f0 in range(0, d_ff, fc):
        cols = slice(f0, f0 + fc)
        a, gate = nxt
        if f0 + fc < d_ff:
            nxt = up_proj(f0 + fc)
        if pending is not None:
            o_ref[...] += _dot(pending, wdn_ref[f0 - fc:f0, :])
        p0 = jnp.broadcast_to(carry_ref[:, 0:1, cols], (nseg, seg, fc)).reshape(tm, fc)
        p1 = jnp.broadcast_to(carry_ref[:, 1:2, cols], (nseg, seg, fc)).reshape(tm, fc)
        a1 = jnp.where(row == 0, p1, pltpu.roll(a, 1, 0))
        a2 = jnp.where(row == 0, p0, jnp.where(row == 1, p1, pltpu.roll(a, 2, 0)))
        conv = (cw_ref[0:1, cols] * a2 + cw_ref[1:2, cols] * a1 + cw_ref[2:3, cols] * a
                + cb_ref[:, cols])
        p = jax.nn.gelu(conv) * gate
        pending = p.astype(BF16)
        carry_ref[:, :, cols] = a.reshape(nseg, seg, fc)[:, seg - 2:seg, :]
    y = o_ref[...] + _dot(pending, wdn_ref[d_ff - fc:d_ff, :])
    nst_ref[...] = carry_ref[...]
    if final_norm:
        y = y * _rms_scale(y) * gf_ref[...]
    o_ref[...] = y


def conv_ffn(x, g, w_up, conv_w, conv_b, w_down, conv_state, seq_len, nseg, seg, g_final=None,
             fc=256):
    t, d = x.shape
    d_ff = w_down[0].shape[1]
    b = conv_state.shape[0]
    tiles_per_stream = max(seq_len // (nseg * seg), 1)
    n_outer = b // nseg
    tm = nseg * seg
    final_norm = g_final is not None
    in_specs = [pl.BlockSpec((tm, d), lambda o, i: (o * tiles_per_stream + i, 0)),
                _const_spec((1, d)),
                _weight_spec(w_up),
                _const_spec((3, d_ff)),
                _const_spec((1, d_ff)),
                _weight_spec(w_down),
                pl.BlockSpec((nseg, 2, d_ff), lambda o, i: (o, 0, 0))]
    args = [x, g.reshape(1, d), w_up[0], conv_w, conv_b.reshape(1, d_ff), w_down[0], conv_state]
    if final_norm:
        in_specs.append(_const_spec((1, d)))
        args.append(g_final.reshape(1, d))
    return pl.pallas_call(
        functools.partial(_ffn_kernel, nseg=nseg, seg=seg, fc=fc, final_norm=final_norm),
        grid=(n_outer, tiles_per_stream),
        in_specs=in_specs,
        out_specs=[pl.BlockSpec((tm, d), lambda o, i: (o * tiles_per_stream + i, 0)),
                   pl.BlockSpec((nseg, 2, d_ff), lambda o, i: (o, 0, 0))],
        out_shape=[jax.ShapeDtypeStruct((t, d), F32),
                   jax.ShapeDtypeStruct((b, 2, d_ff), F32)],
        scratch_shapes=[pltpu.VMEM((nseg, 2, d_ff), F32)],
        compiler_params=_cparams("parallel", "arbitrary"),
        name="conv_ffn",
    )(*args)


def _rotate(x, cos, sin):
    half = x.shape[1] // 2
    x1, x2 = x[:, :half], x[:, half:]
    return jnp.concatenate([x1 * cos - x2 * sin, x2 * cos + x1 * sin], axis=1)


def _ret_head(h, q, k, v, g, cos, sin, s_ref, chunk):
    log_g = RET_LOG_G[h]
    li = lax.broadcasted_iota(jnp.int32, (chunk, chunk), 0)
    mi = lax.broadcasted_iota(jnp.int32, (chunk, chunk), 1)
    diff = (li - mi).astype(F32)
    pos = lax.broadcasted_iota(jnp.int32, (chunk, 1), 0).astype(F32)
    qr = _rotate(q, cos, sin).astype(BF16)
    kr = _rotate(k, cos, sin) * (RET_DK ** -0.5)
    decay = jnp.where(diff >= 0, jnp.exp(jnp.maximum(diff, 0.0) * log_g), 0.0)
    scores = _dot_nt(qr, kr.astype(BF16)) * decay
    intra = _dot(scores.astype(BF16), v)
    s_old = s_ref[0, h]
    cross = _dot(qr, s_old.astype(BF16)) * jnp.exp((pos + 1.0) * log_g)
    o = intra + cross
    k_tail = (kr * jnp.exp((chunk - 1.0 - pos) * log_g)).astype(BF16)
    s_ref[0, h] = math.exp(chunk * log_g) * s_old + lax.dot_general(
        k_tail, v, (((0,), (0,)), ((), ())), preferred_element_type=F32)
    mu = jnp.mean(o, axis=-1, keepdims=True)
    oc = o - mu
    var = jnp.mean(oc * oc, axis=-1, keepdims=True)
    return jax.nn.silu(g) * (oc * lax.rsqrt(var + GN_EPS))


def _ret_kernel(p_ref, cos_ref, sin_ref, s_in_ref, y_ref, s_out_ref, *, chunk):
    @pl.when(pl.program_id(1) == 0)
    def _():
        s_out_ref[...] = s_in_ref[...]

    cos = cos_ref[...]
    sin = sin_ref[...]
    for h in range(RET_HEADS):
        q = p_ref[:, h * RET_DK:(h + 1) * RET_DK].astype(F32)
        k = p_ref[:, RET_QK + h * RET_DK:RET_QK + (h + 1) * RET_DK].astype(F32)
        v = p_ref[:, 2 * RET_QK + h * RET_DV:2 * RET_QK + (h + 1) * RET_DV]
        g = p_ref[:, 2 * RET_QK + RET_V + h * RET_DV:
                  2 * RET_QK + RET_V + (h + 1) * RET_DV].astype(F32)
        y = _ret_head(h, q, k, v, g, cos, sin, s_out_ref, chunk)
        y_ref[:, h * RET_DV:(h + 1) * RET_DV] = y.astype(y_ref.dtype)


def _ret_fused_kernel(x_ref, g_ref, w_ref, wo_ref, cos_ref, sin_ref, s_in_ref, o_ref, s_out_ref,
                      *, chunk):
    @pl.when(pl.program_id(1) == 0)
    def _():
        s_out_ref[...] = s_in_ref[...]

    cos = cos_ref[...]
    sin = sin_ref[...]
    x = x_ref[...]
    hn = (x * _rms_scale(x) * g_ref[...]).astype(BF16)
    o_ref[...] = x

    def proj(h):
        c = [h * RET_DK, RET_QK + h * RET_DK, 2 * RET_QK + h * RET_DV, 2 * RET_QK + RET_V + h * RET_DV]
        return (_dot(hn, w_ref[:, c[0]:c[0] + RET_DK]), _dot(hn, w_ref[:, c[1]:c[1] + RET_DK]),
                _dot(hn, w_ref[:, c[2]:c[2] + RET_DV]).astype(BF16),
                _dot(hn, w_ref[:, c[3]:c[3] + RET_DV]))

    group = RET_HEAD_GROUP
    width = group * RET_DV
    nxt = [proj(h) for h in range(group)]
    pending = None
    for h0 in range(0, RET_HEADS, group):
        cur = nxt
        if h0 + group < RET_HEADS:
            nxt = [proj(h) for h in range(h0 + group, h0 + 2 * group)]
        if pending is not None:
            o_ref[...] += _dot(pending, wo_ref[(h0 - group) * RET_DV:h0 * RET_DV, :])
        ys = [_ret_head(h0 + i, *cur[i], cos, sin, s_out_ref, chunk).astype(BF16)
              for i in range(group)]
        pending = ys[0] if group == 1 else jnp.concatenate(ys, axis=1)
    o_ref[...] += _dot(pending, wo_ref[RET_V - width:, :])


def _ret_specs(seq_len, chunk):
    nchunks = seq_len // chunk
    half = RET_DK // 2
    row_spec = lambda n: pl.BlockSpec((chunk, n), lambda o, i: (o * nchunks + i, 0))
    rope_spec = pl.BlockSpec((chunk, half), lambda o, i: (i, 0))
    state_spec = pl.BlockSpec((1, RET_HEADS, RET_DK, RET_DV), lambda o, i: (o, 0, 0, 0))
    return nchunks, row_spec, rope_spec, state_spec


def ret_core(p, cos, sin, state, seq_len, chunk):
    t, n = p.shape
    nchunks, row_spec, rope_spec, state_spec = _ret_specs(seq_len, chunk)
    return pl.pallas_call(
        functools.partial(_ret_kernel, chunk=chunk),
        grid=(state.shape[0], nchunks),
        in_specs=[row_spec(n), rope_spec, rope_spec, state_spec],
        out_specs=[row_spec(RET_V), state_spec],
        out_shape=[jax.ShapeDtypeStruct((t, RET_V), BF16),
                   jax.ShapeDtypeStruct(state.shape, F32)],
        compiler_params=_cparams("parallel", "arbitrary"),
        name="ret_core",
    )(p, cos, sin, state)


def ret_mixer(x, g, w_qkvg, w_o, cos, sin, state, seq_len, chunk):
    t, d = x.shape
    nchunks, row_spec, rope_spec, state_spec = _ret_specs(seq_len, chunk)
    return pl.pallas_call(
        functools.partial(_ret_fused_kernel, chunk=chunk),
        grid=(state.shape[0], nchunks),
        in_specs=[row_spec(d), _const_spec((1, d)), _weight_spec(w_qkvg),
                  _weight_spec(w_o), rope_spec, rope_spec, state_spec],
        out_specs=[row_spec(d), state_spec],
        out_shape=[jax.ShapeDtypeStruct((t, d), F32), jax.ShapeDtypeStruct(state.shape, F32)],
        compiler_params=_cparams("parallel", "arbitrary"),
        name="ret_mixer",
    )(x, g.reshape(1, d), w_qkvg[0], w_o[0], cos, sin, state)


def _rope_tables(pos0, length):
    half = RET_DK // 2
    freqs = ROPE_BASE ** (-np.arange(half, dtype=np.float64) / half)
    ang = (pos0 + np.arange(length, dtype=np.float64))[:, None] * freqs[None, :]
    return jnp.asarray(np.cos(ang), F32), jnp.asarray(np.sin(ang), F32)


def _cmul(ar, ai, br, bi):
    return ar * br - ai * bi, ar * bi + ai * br


def _s5_prep_kernel(lam_re_ref, lam_im_ref, ldt_ref, b_re_ref, b_im_ref, c_re_ref, c_im_ref,
                    w_re_ref, w_im_ref, bt_ref, cct_ref, a16_ref):
    dt = jnp.exp(ldt_ref[...])
    lam_re, lam_im = lam_re_ref[...], lam_im_ref[...]
    mag = jnp.exp(lam_re * dt)
    ar, ai = mag * jnp.cos(lam_im * dt), mag * jnp.sin(lam_im * dt)
    den = lam_re * lam_re + lam_im * lam_im
    nr, ni = ar - 1.0, ai
    coef_re = (nr * lam_re + ni * lam_im) / den
    coef_im = (ni * lam_re - nr * lam_im) / den
    bb_re, bb_im = _cmul(coef_re, coef_im, b_re_ref[...], b_im_ref[...])
    c_re, c_im = c_re_ref[...], c_im_ref[...]
    lane = lax.broadcasted_iota(jnp.int32, (1, LANES), 1)
    low = lane < SSM_STATE
    rg = lax.broadcasted_iota(jnp.int32, (LANES, 1), 0) // SSM_GROUP
    pair = rg // 2
    own_half = (lane // SSM_STATE) == (rg % 2)
    same_group = rg == lane // SSM_GROUP
    c_hi, c_lo = _split_bf16(jnp.where(low, c_re, -c_im))
    tk = []
    pr, pi = jnp.ones_like(ar), jnp.zeros_like(ar)
    for j in range(SSM_FOLD):
        zr, zi = _cmul(pr, pi, bb_re, bb_im)
        w_re_ref[SSM_FOLD - 1 - j] = jnp.where(own_half, zr, 0.0)
        w_im_ref[SSM_FOLD - 1 - j] = jnp.where(own_half, zi, 0.0)
        b_hi, b_lo = _split_bf16(jnp.where(low, zr, zi))
        kk = _dot_nt(b_hi, c_hi) + _dot_nt(b_lo, c_hi) + _dot_nt(b_hi, c_lo)
        tk.append(jnp.where(same_group, kk, 0.0).astype(BF16))
        pr, pi = _cmul(pr, pi, ar, ai)
        zr, zi = _cmul(c_re, c_im, pr, pi)
        for ri, z in enumerate((zr, -zi)):
            z = jnp.where(own_half, z, 0.0)
            for t in range(GROUPS_PER_TILE // 2):
                c0 = ri * STATE_PER_TILE + t * LANES
                cct_ref[0, j * LANES:(j + 1) * LANES, c0:c0 + LANES] = (
                    jnp.where(pair == t, z, 0.0).astype(BF16))
    a16_ref[0] = pr
    a16_ref[1] = pi
    zero = jnp.zeros((LANES, LANES), BF16)
    for d2 in range(SSM_FOLD // 2):
        top = jnp.concatenate([tk[2 * d2], tk[2 * d2 + 1]], axis=1)
        bot = jnp.concatenate([tk[2 * d2 - 1] if d2 else zero, tk[2 * d2]], axis=1)
        bt_ref[0, d2] = jnp.concatenate([top, bot], axis=0)


def s5_prepare(a_re, a_im, log_dt, b_re, b_im, c_re, c_im):
    g, p, c = SSM_GROUPS, SSM_STATE, SSM_GROUP
    fold, nt = SSM_FOLD, N_LANE_TILES
    rows = g * c
    twice = lambda x: jnp.tile(x, (1, 2))
    lam = lambda x: twice(jnp.repeat(x, c, axis=0))
    in_spec = pl.BlockSpec((LANES, LANES), lambda t: (t, 0))
    w_spec = pl.BlockSpec((fold, LANES, LANES), lambda t: (0, t, 0))
    w_shape = jax.ShapeDtypeStruct((fold, rows, LANES), F32)
    w_re, w_im, bt, cct, a16 = pl.pallas_call(
        _s5_prep_kernel,
        grid=(nt,),
        in_specs=[in_spec, in_spec, pl.BlockSpec((LANES, 1), lambda t: (t, 0))] + [in_spec] * 4,
        out_specs=[w_spec, w_spec,
                   pl.BlockSpec((1, fold // 2, 2 * LANES, 2 * LANES), lambda t: (t, 0, 0, 0)),
                   pl.BlockSpec((1, fold * LANES, 2 * STATE_PER_TILE), lambda t: (t, 0, 0)),
                   pl.BlockSpec((2, LANES, LANES), lambda t: (0, t, 0))],
        out_shape=[w_shape, w_shape,
                   jax.ShapeDtypeStruct((nt, fold // 2, 2 * LANES, 2 * LANES), BF16),
                   jax.ShapeDtypeStruct((nt, fold * LANES, 2 * STATE_PER_TILE), BF16),
                   jax.ShapeDtypeStruct((2, rows, LANES), F32)],
        compiler_params=_cparams("parallel"),
        name="s5_prep",
    )(lam(a_re), lam(a_im), jnp.repeat(log_dt, c).reshape(rows, 1),
      twice(b_re.transpose(0, 2, 1).reshape(rows, p)), twice(b_im.transpose(0, 2, 1).reshape(rows, p)),
      twice(c_re.reshape(rows, p)), twice(c_im.reshape(rows, p)))
    a16 = a16[:, ::c, :p].reshape(2, 1, N_STATE)
    return bt, cct, w_re, w_im, a16


FOLD_BLOCK = SSM_FOLD * SSM_FOLD


def _fold_perm():
    idx = np.arange(FOLD_BLOCK)
    perm = np.zeros((FOLD_BLOCK, FOLD_BLOCK), np.float32)
    perm[(idx % SSM_FOLD) * SSM_FOLD + idx // SSM_FOLD, idx] = 1.0
    return jnp.asarray(perm, BF16)


def _s5_fold_kernel(x_ref, g_ref, perm_ref, u_ref):
    for blk in range(x_ref.shape[0] // FOLD_BLOCK):
        x = x_ref[blk * FOLD_BLOCK:(blk + 1) * FOLD_BLOCK, :]
        h = (x * _rms_scale(x) * g_ref[...]).astype(BF16)
        f = _dot(perm_ref[...], h).astype(BF16)
        for m in range(SSM_FOLD):
            for k in range(N_LANE_TILES):
                u_ref[m, k, blk * SSM_FOLD:(blk + 1) * SSM_FOLD, :] = (
                    f[m * SSM_FOLD:(m + 1) * SSM_FOLD, k * LANES:(k + 1) * LANES])


def s5_fold(x, g, perm, tokens):
    t, d = x.shape
    rows = t // SSM_FOLD
    return pl.pallas_call(
        _s5_fold_kernel,
        grid=(t // tokens,),
        in_specs=[pl.BlockSpec((tokens, d), lambda i: (i, 0)), _const_spec((1, d)),
                  _const_spec((FOLD_BLOCK, FOLD_BLOCK))],
        out_specs=pl.BlockSpec((SSM_FOLD, N_LANE_TILES, tokens // SSM_FOLD, LANES),
                               lambda i: (0, 0, i, 0)),
        out_shape=jax.ShapeDtypeStruct((SSM_FOLD, N_LANE_TILES, rows, LANES), BF16),
        compiler_params=_cparams("parallel"),
        name="s5_fold",
    )(x, g.reshape(1, d), perm)


def _s5_state_in_kernel(u_ref, w_re_ref, w_im_ref, vre_ref, vim_ref, wd_ref):
    @pl.when(pl.program_id(1) == 0)
    def _():
        pair = lax.broadcasted_iota(jnp.int32, (LANES, 1), 0) // SSM_GROUP // 2
        for m in range(SSM_FOLD):
            for ri, ref in enumerate((w_re_ref, w_im_ref)):
                w = ref[m]
                for t in range(GROUPS_PER_TILE // 2):
                    c0 = ri * STATE_PER_TILE + t * LANES
                    wd_ref[m * LANES:(m + 1) * LANES, c0:c0 + LANES] = (
                        jnp.where(pair == t, w, 0.0).astype(BF16))

    lhs = jnp.concatenate([u_ref[m, 0] for m in range(SSM_FOLD)], axis=1)
    v = _dot(lhs, wd_ref[...])
    vre_ref[...] = v[:, :STATE_PER_TILE]
    vim_ref[...] = v[:, STATE_PER_TILE:]


def _s5_tile_specs(rt):
    u_spec = pl.BlockSpec((SSM_FOLD, 1, rt, LANES), lambda t, i: (0, t, i, 0))
    s_spec = pl.BlockSpec((rt, STATE_PER_TILE), lambda t, i: (i, t))
    return u_spec, s_spec


def s5_state_in(u, w_re, w_im, rt):
    rows = u.shape[2]
    u_spec, s_spec = _s5_tile_specs(rt)
    out = jax.ShapeDtypeStruct((rows, N_STATE), F32)
    w_spec = pl.BlockSpec((SSM_FOLD, LANES, LANES), lambda t, i: (0, t, 0))
    return pl.pallas_call(
        _s5_state_in_kernel,
        grid=(N_LANE_TILES, rows // rt),
        in_specs=[u_spec, w_spec, w_spec],
        out_specs=[s_spec, s_spec],
        out_shape=[out, out],
        scratch_shapes=[pltpu.VMEM((SSM_FOLD * LANES, 2 * STATE_PER_TILE), BF16)],
        compiler_params=_cparams("parallel", "arbitrary"),
        name="s5_state_in",
    )(u, w_re, w_im)


def _s5_scan_pairs_kernel(vre_ref, vim_ref, a_ref, s0re_ref, s0im_ref,
                          sre_ref, sim_ref, fre_ref, fim_ref):
    ar, ai = a_ref[0], a_ref[1]
    s0r, s0i = s0re_ref[...], s0im_ref[...]
    vr, vi = vre_ref[...], vim_ref[...]
    first = lax.broadcasted_iota(jnp.int32, (s0r.shape[0], 1), 0) % 2 == 0
    tr, ti = ar * s0r - ai * s0i + vr, ar * s0i + ai * s0r + vi
    sr = jnp.where(first, s0r, pltpu.roll(tr, 1, 0))
    si = jnp.where(first, s0i, pltpu.roll(ti, 1, 0))
    sre_ref[...] = sr
    sim_ref[...] = si
    fre_ref[...] = ar * sr - ai * si + vr
    fim_ref[...] = ar * si + ai * sr + vi


def _s5_scan_stream_kernel(vre_ref, vim_ref, a_ref, s0re_ref, s0im_ref,
                           sre_ref, sim_ref, fre_ref, fim_ref, *, steps):
    q = lax.broadcasted_iota(jnp.int32, (SUBLANES, 1), 0)
    a1 = (a_ref[0], a_ref[1])
    a2 = _cmul(*a1, *a1)
    a4 = _cmul(*a2, *a2)
    a8 = _cmul(*a4, *a4)
    strides = ((1, a1), (2, a2), (4, a4))
    width = a1[0].shape[1]
    pw = (jnp.ones((SUBLANES, width), F32), jnp.zeros((SUBLANES, width), F32))
    for k, ak in strides:
        nxt = _cmul(*pw, *ak)
        bit = (q & k) != 0
        pw = (jnp.where(bit, nxt[0], pw[0]), jnp.where(bit, nxt[1], pw[1]))

    def shifted(x, k):
        return jnp.where(q >= k, pltpu.roll(x, k, 0), 0.0)

    def body(t, carry):
        cr, ci = carry
        rows = pl.ds(pl.multiple_of(t * SUBLANES, SUBLANES), SUBLANES)
        pr, pi = vre_ref[rows, :], vim_ref[rows, :]
        for k, ak in strides:
            dr, di = _cmul(shifted(pr, k), shifted(pi, k), *ak)
            pr, pi = pr + dr, pi + di
        xr, xi = _cmul(*pw, cr, ci)
        sre_ref[rows, :] = xr + shifted(pr, 1)
        sim_ref[rows, :] = xi + shifted(pi, 1)
        nr, ni = _cmul(*a8, cr, ci)
        return nr + pr[SUBLANES - 1:], ni + pi[SUBLANES - 1:]

    cr, ci = lax.fori_loop(0, steps // SUBLANES, body, (s0re_ref[0], s0im_ref[0]))
    fre_ref[0] = cr
    fim_ref[0] = ci


def s5_scan(v_re, v_im, a16, s0_re, s0_im, width):
    rows, n = v_re.shape
    streams = s0_re.shape[0]
    steps = rows // streams
    a_spec = pl.BlockSpec((2, 1, width), lambda *i: (0, 0, i[-1]))
    vout = jax.ShapeDtypeStruct((rows, n), F32)
    long_streams = steps % SUBLANES == 0
    if long_streams:
        kern = functools.partial(_s5_scan_stream_kernel, steps=steps)
        grid = (streams, n // width)
        vspec = pl.BlockSpec((steps, width), lambda b, c: (b, c))
        sspec = pl.BlockSpec((1, 1, width), lambda b, c: (b, 0, c))
        s0_re, s0_im = s0_re.reshape(streams, 1, n), s0_im.reshape(streams, 1, n)
        sout = jax.ShapeDtypeStruct((streams, 1, n), F32)
    else:
        assert steps == 2 and rows % SUBLANES == 0
        kern = _s5_scan_pairs_kernel
        grid = (n // width,)
        vspec = pl.BlockSpec((rows, width), lambda c: (0, c))
        sspec = vspec
        s0_re, s0_im = jnp.repeat(s0_re, steps, axis=0), jnp.repeat(s0_im, steps, axis=0)
        sout = vout
    s_re, s_im, f_re, f_im = pl.pallas_call(
        kern,
        grid=grid,
        in_specs=[vspec, vspec, a_spec, sspec, sspec],
        out_specs=[vspec, vspec, sspec, sspec],
        out_shape=[vout, vout, sout, sout],
        compiler_params=_cparams(*(["parallel"] * len(grid))),
        name="s5_scan",
    )(v_re, v_im, a16, s0_re, s0_im)
    if long_streams:
        return s_re, s_im, f_re.reshape(streams, n), f_im.reshape(streams, n)
    return s_re, s_im, f_re[steps - 1::steps], f_im[steps - 1::steps]


def _s5_toeplitz_kernel(u_ref, sre_ref, sim_ref, bt_ref, cct_ref, y_ref):
    npair = SSM_FOLD // 2
    u2 = [jnp.concatenate([u_ref[2 * mm, 0], u_ref[2 * mm + 1, 0]], axis=1) for mm in range(npair)]
    s_prev = jnp.concatenate([sre_ref[...], sim_ref[...]], axis=1).astype(BF16)
    for ll in range(npair):
        acc = _dot_nt(s_prev, cct_ref[0, ll * 2 * LANES:(ll + 1) * 2 * LANES, :])
        for mm in range(ll + 1):
            acc = acc + _dot(u2[mm], bt_ref[0, ll - mm])
        y_ref[2 * ll, 0] = acc[:, :LANES]
        y_ref[2 * ll + 1, 0] = acc[:, LANES:]


def s5_toeplitz(u, s_re, s_im, bt, cct, rt):
    rows = u.shape[2]
    u_spec, s_spec = _s5_tile_specs(rt)
    return pl.pallas_call(
        _s5_toeplitz_kernel,
        grid=(N_LANE_TILES, rows // rt),
        in_specs=[u_spec, s_spec, s_spec,
                  pl.BlockSpec((1, SSM_FOLD // 2, 2 * LANES, 2 * LANES), lambda t, i: (t, 0, 0, 0)),
                  pl.BlockSpec((1, SSM_FOLD * LANES, 2 * STATE_PER_TILE), lambda t, i: (t, 0, 0))],
        out_specs=u_spec,
        out_shape=jax.ShapeDtypeStruct(u.shape, F32),
        compiler_params=_cparams("parallel", "parallel"),
        name="s5_toeplitz",
    )(u, s_re, s_im, bt, cct)


def _s5_glu_kernel(x_ref, u_ref, y_ref, d_ref, wglu_ref, bglu_ref, perm_ref, o_ref):
    nblk = x_ref.shape[0] // FOLD_BLOCK
    gl = []
    for blk in range(nblk):
        rows = slice(blk * SSM_FOLD, (blk + 1) * SSM_FOLD)
        for m in range(SSM_FOLD):
            ym = jnp.concatenate([y_ref[m, k, rows, :] for k in range(N_LANE_TILES)], axis=1)
            hm = jnp.concatenate([u_ref[m, k, rows, :] for k in range(N_LANE_TILES)], axis=1)
            gl.append(jax.nn.gelu(ym + d_ref[...] * hm.astype(F32)))
    gl = jnp.concatenate(gl, axis=0)
    out = gl * jax.nn.sigmoid(_dot(gl.astype(BF16), wglu_ref[...]) + bglu_ref[...])
    for blk in range(nblk):
        tok = slice(blk * FOLD_BLOCK, (blk + 1) * FOLD_BLOCK)
        hi, lo = _split_bf16(out[tok])
        o_ref[tok, :] = x_ref[tok, :] + (_dot(perm_ref[...], hi) + _dot(perm_ref[...], lo))


def s5_glu(x, u, y, d, w_glu, b_glu, perm, tokens):
    t, dm = x.shape
    fold_spec = pl.BlockSpec((SSM_FOLD, N_LANE_TILES, tokens // SSM_FOLD, LANES),
                             lambda i: (0, 0, i, 0))
    x_spec = pl.BlockSpec((tokens, dm), lambda i: (i, 0))
    return pl.pallas_call(
        _s5_glu_kernel,
        grid=(t // tokens,),
        in_specs=[x_spec, fold_spec, fold_spec, _const_spec((1, dm)), _weight_spec(w_glu),
                  _const_spec((1, dm)), _const_spec((FOLD_BLOCK, FOLD_BLOCK))],
        out_specs=x_spec,
        out_shape=jax.ShapeDtypeStruct((t, dm), F32),
        compiler_params=_cparams("parallel"),
        name="s5_glu",
    )(x, u, y, d.reshape(1, dm), w_glu[0], b_glu.reshape(1, dm), perm)


def s5_mixer(x, st_re, st_im, g, ops, d, w_glu, b_glu):
    bt, cct, w_re, w_im, a16 = ops
    b, l, dm = x.shape
    rows = b * l // SSM_FOLD
    x2 = x.reshape(b * l, dm)
    perm = _fold_perm()
    tokens = min(S5_TOKEN_TILE, b * l)
    rt = min(S5_ROW_TILE, rows)
    u = s5_fold(x2, g, perm, tokens)
    v_re, v_im = s5_state_in(u, w_re, w_im, rt)
    s_re, s_im, f_re, f_im = s5_scan(v_re, v_im, a16, st_re.reshape(b, N_STATE),
                                     st_im.reshape(b, N_STATE), S5_SCAN_WIDTH)
    y = s5_toeplitz(u, s_re, s_im, bt, cct, rt)
    out = s5_glu(x2, u, y, d, w_glu, b_glu, perm, tokens)
    shape = (b, SSM_GROUPS, SSM_STATE)
    return out.reshape(b, l, dm), f_re.reshape(shape), f_im.reshape(shape)


def _stream_tiling(seq_len, n_streams, tile_rows):
    if seq_len >= tile_rows:
        return 1, tile_rows
    return min(n_streams, 256 // seq_len), seq_len


def kernel(x_prompt, x_sample, mem_prompt, state_ssm_re, state_ssm_im, state_ret, cache_mem_k, cache_mem_v, cache_conv, norm_mix, norm_mem_q, norm_mem_kv, norm_ffn, norm_final, ssm_a_re, ssm_a_im, ssm_log_dt, ssm_b_re, ssm_b_im, ssm_c_re, ssm_c_im, ssm_d, ssm_w_glu, ssm_b_glu, ret_w_qkvg, ret_w_o, mem_w_q, mem_w_kv, mem_w_o, ffn_w_up, ffn_conv_w, ffn_conv_b, ffn_w_down):
    bp, lp, dm = x_prompt.shape
    bs, ls, _ = x_sample.shape
    depth = norm_mix.shape[0]
    n_mem = mem_prompt.shape[1]
    d_ff = ffn_w_down.shape[1]
    streams = [dict(x=x, b=x.shape[0], l=x.shape[1],
                    attn_tiling=_stream_tiling(x.shape[1], x.shape[0], ATTN_TILE_ROWS),
                    ffn_tiling=_stream_tiling(x.shape[1], x.shape[0], FFN_TILE_ROWS))
               for x in (x_prompt, x_sample)]
    cache_k, cache_v = _kv_rows(cache_mem_k), _kv_rows(cache_mem_v)
    ssm_w_glu, ret_w_qkvg, ret_w_o, mem_w_q, mem_w_kv, mem_w_o, ffn_w_up, ffn_w_down = (
        w.astype(BF16) for w in (ssm_w_glu, ret_w_qkvg, ret_w_o, mem_w_q, mem_w_kv, mem_w_o,
                                 ffn_w_up, ffn_w_down))
    outs = dict(ssm_re=([], []), ssm_im=([], []), ret=([], []), conv=([], []), mk=[], mv=[])

    for i in range(depth):
        j = i // 2
        if i % 2 == 0:
            ops = s5_prepare(ssm_a_re[j], ssm_a_im[j], ssm_log_dt[j], ssm_b_re[j], ssm_b_im[j],
                             ssm_c_re[j], ssm_c_im[j])
            w_glu = (ssm_w_glu, j)
            zeros = jnp.zeros((bp, SSM_GROUPS, SSM_STATE), F32)
            states = [(zeros, zeros), (state_ssm_re[j], state_ssm_im[j])]
            for si, (s, (st_re, st_im)) in enumerate(zip(streams, states)):
                s["x"], f_re, f_im = s5_mixer(s["x"], st_re, st_im, norm_mix[i], ops, ssm_d[j],
                                              w_glu, ssm_b_glu[j])
                outs["ssm_re"][si].append(f_re)
                outs["ssm_im"][si].append(f_im)
        else:
            w_qkvg = (ret_w_qkvg, j)
            w_o = (ret_w_o, j)
            zeros = jnp.zeros((bp, RET_HEADS, RET_DK, RET_DV), F32)
            chunk = RET_CHUNK_ROWS
            for si, (s, st, pos0) in enumerate(zip(streams, [zeros, state_ret[j]], [0, PAST_LEN])):
                x2 = s["x"].reshape(s["b"] * s["l"], dm)
                cos, sin = _rope_tables(pos0, s["l"])
                if s["l"] >= chunk:
                    y, st_new = ret_mixer(x2, norm_mix[i], w_qkvg, w_o, cos, sin, st, s["l"], chunk)
                else:
                    p = rms_matmul(x2, norm_mix[i], w_qkvg, BF16)
                    y, st_new = ret_core(p, cos, sin, st, s["l"], s["l"])
                    y = matmul_res(y, w_o, x2)
                s["x"] = y.reshape(s["x"].shape)
                outs["ret"][si].append(st_new)

        mk, mv = mem_kv(mem_prompt, norm_mem_kv[i], (mem_w_kv, i))
        outs["mk"].append(_kv_unrows(mk, MEM_HEADS))
        outs["mv"].append(_kv_unrows(mv, MEM_HEADS))
        w_q = (mem_w_q, i)
        w_o = (mem_w_o, i)
        for s, (k_, v_, layer) in zip(streams, [(mk[None], mv[None], 0), (cache_k, cache_v, i)]):
            x2 = s["x"].reshape(s["b"] * s["l"], dm)
            y = mem_attn(x2, norm_mem_q[i], w_q, k_, v_, layer, w_o, s["l"], *s["attn_tiling"])
            s["x"] = y.reshape(s["x"].shape)

        w_up = (ffn_w_up, i)
        w_dn = (ffn_w_down, i)
        g_final = norm_final if i == depth - 1 else None
        conv_states = [jnp.zeros((bp, 2, d_ff), F32), cache_conv[i]]
        for si, (s, cst) in enumerate(zip(streams, conv_states)):
            x2 = s["x"].reshape(s["b"] * s["l"], dm)
            y, cnew = conv_ffn(x2, norm_ffn[i], w_up, ffn_conv_w[i], ffn_conv_b[i], w_dn, cst,
                               s["l"], *s["ffn_tiling"], g_final=g_final)
            s["x"] = y.reshape(s["x"].shape)
            outs["conv"][si].append(cnew)

    return (streams[0]["x"], streams[1]["x"],
            jnp.stack(outs["ssm_re"][0]), jnp.stack(outs["ssm_im"][0]),
            jnp.stack(outs["ssm_re"][1]), jnp.stack(outs["ssm_im"][1]),
            jnp.stack(outs["ret"][0]), jnp.stack(outs["ret"][1]),
            jnp.stack(outs["mk"]), jnp.stack(outs["mv"]),
            jnp.stack(outs["conv"][0]), jnp.stack(outs["conv"][1]))
```

```python
import functools
import math

import numpy as np
import jax
import jax.numpy as jnp
from jax import lax
from jax.experimental import pallas as pl
from jax.experimental.pallas import tpu as pltpu

F32 = jnp.float32
BF16 = jnp.bfloat16

D_MODEL = 1024
PAST_LEN = 1024
EPS = 1e-6
GN_EPS = 1e-5
ROPE_BASE = 10000.0
LANES = 128
SUBLANES = 8
SSM_GROUP = 16
SSM_GROUPS = D_MODEL // SSM_GROUP
SSM_STATE = 64
SSM_FOLD = 16
GROUPS_PER_TILE = LANES // SSM_GROUP
N_LANE_TILES = D_MODEL // LANES
STATE_PER_TILE = GROUPS_PER_TILE * SSM_STATE
N_STATE = SSM_GROUPS * SSM_STATE
RET_HEADS = 4
RET_DK = D_MODEL // RET_HEADS
RET_DV = 2 * D_MODEL // RET_HEADS
RET_QK = RET_HEADS * RET_DK
RET_V = RET_HEADS * RET_DV
RET_HEAD_GROUP = 2
RET_LOG_G = tuple(math.log(1.0 - 2.0 ** (-5.0 - h)) for h in range(RET_HEADS))
MEM_HEADS = 4
MEM_HD = D_MODEL // MEM_HEADS
VMEM_LIMIT = 56 * 1024 * 1024
ATTN_TILE_ROWS = 512
FFN_TILE_ROWS = 256
RET_CHUNK_ROWS = 256
S5_TOKEN_TILE = 1024
S5_ROW_TILE = 512
S5_SCAN_WIDTH = 1024


def _cparams(*sem):
    return pltpu.CompilerParams(dimension_semantics=sem, vmem_limit_bytes=VMEM_LIMIT)


def _const_spec(shape):
    nd = len(shape)
    return pl.BlockSpec(shape, lambda *_: (0,) * nd, pipeline_mode=pl.Buffered(1))


def _weight_spec(w):
    stack, layer = w
    return pl.BlockSpec((None,) + stack.shape[1:], lambda *_: (layer, 0, 0),
                        pipeline_mode=pl.Buffered(1))


def _rms_scale(x):
    return lax.rsqrt(jnp.mean(x * x, axis=-1, keepdims=True) + EPS)


def _split_bf16(x):
    hi = x.astype(BF16)
    lo = (x - hi.astype(F32)).astype(BF16)
    return hi, lo


def _dot(a, b):
    return jnp.dot(a, b, preferred_element_type=F32)


def _dot_nt(a, b):
    return lax.dot_general(a, b, (((1,), (1,)), ((), ())), preferred_element_type=F32)


def _rms_matmul_kernel(x_ref, g_ref, w_ref, o_ref, *, tn):
    x = x_ref[...]
    h = (x * _rms_scale(x) * g_ref[...]).astype(BF16)
    for n0 in range(0, w_ref.shape[1], tn):
        o_ref[:, n0:n0 + tn] = _dot(h, w_ref[:, n0:n0 + tn]).astype(o_ref.dtype)


def rms_matmul(x, g, w, out_dtype, tm=512, tn=512):
    t, d = x.shape
    n = w[0].shape[2]
    tm = min(tm, t)
    return pl.pallas_call(
        functools.partial(_rms_matmul_kernel, tn=tn),
        grid=(t // tm,),
        in_specs=[pl.BlockSpec((tm, d), lambda i: (i, 0)),
                  _const_spec((1, d)),
                  _weight_spec(w)],
        out_specs=pl.BlockSpec((tm, n), lambda i: (i, 0)),
        out_shape=jax.ShapeDtypeStruct((t, n), out_dtype),
        compiler_params=_cparams("parallel"),
        name="rms_matmul",
    )(x, g.reshape(1, d), w[0])


def _matmul_res_kernel(y_ref, w_ref, x_ref, o_ref):
    o_ref[...] = x_ref[...] + _dot(y_ref[...], w_ref[...])


def matmul_res(y, w, x, tm=512):
    t, k = y.shape
    d = w[0].shape[2]
    tm = min(tm, t)
    return pl.pallas_call(
        _matmul_res_kernel,
        grid=(t // tm,),
        in_specs=[pl.BlockSpec((tm, k), lambda i: (i, 0)),
                  _weight_spec(w),
                  pl.BlockSpec((tm, d), lambda i: (i, 0))],
        out_specs=pl.BlockSpec((tm, d), lambda i: (i, 0)),
        out_shape=jax.ShapeDtypeStruct((t, d), F32),
        compiler_params=_cparams("parallel"),
        name="matmul_res",
    )(y, w[0], x)


MEM_LANE_TILES = MEM_HD // LANES


def _kv_rows(x):
    *lead, n, h, hd = x.shape
    x = x.reshape(*lead, n, h, MEM_LANE_TILES, LANES)
    return jnp.swapaxes(x, -3, -2).reshape(*lead, n * MEM_LANE_TILES * h, LANES)


def _kv_unrows(x, heads):
    *lead, rows, _ = x.shape
    n = rows // (MEM_LANE_TILES * heads)
    x = x.reshape(*lead, n, MEM_LANE_TILES, heads, LANES)
    return jnp.swapaxes(x, -3, -2).reshape(*lead, n, heads, MEM_LANE_TILES * LANES)


def _mem_kv_kernel(x_ref, g_ref, w_ref, k_ref, v_ref):
    x = x_ref[...]
    h = (x * _rms_scale(x) * g_ref[...]).astype(BF16)
    n_mem, d = x.shape
    stride = MEM_LANE_TILES * MEM_HEADS
    for which, ref in enumerate((k_ref, v_ref)):
        for hd in range(MEM_HEADS):
            c0 = which * d + hd * MEM_HD
            res = _dot(h, w_ref[:, c0:c0 + MEM_HD])
            for j in range(MEM_LANE_TILES):
                ref[pl.ds(j * MEM_HEADS + hd, n_mem, stride=stride), :] = (
                    res[:, j * LANES:(j + 1) * LANES])


def mem_kv(mem, g, w_kv):
    b, n_mem, d = mem.shape
    rows = n_mem * MEM_LANE_TILES * MEM_HEADS
    out_spec = pl.BlockSpec((None, rows, LANES), lambda i: (i, 0, 0))
    out = jax.ShapeDtypeStruct((b, rows, LANES), F32)
    return pl.pallas_call(
        _mem_kv_kernel,
        grid=(b,),
        in_specs=[pl.BlockSpec((None, n_mem, d), lambda i: (i, 0, 0)), _const_spec((1, d)),
                  _weight_spec(w_kv)],
        out_specs=[out_spec, out_spec],
        out_shape=[out, out],
        compiler_params=_cparams("parallel"),
        name="mem_kv",
    )(mem, g.reshape(1, d), w_kv[0])


def _head_rows(kv_ref, s, hd):
    stride = MEM_LANE_TILES * MEM_HEADS
    n_mem = kv_ref.shape[2] // stride
    return jnp.concatenate(
        [kv_ref[0, s, pl.ds(j * MEM_HEADS + hd, n_mem, stride=stride), :]
         for j in range(MEM_LANE_TILES)], axis=1).astype(BF16)


def _mem_attn_kernel(x_ref, g_ref, wq_ref, k_ref, v_ref, wo_ref, o_ref, *, nseg, seg, head_group):
    scale = MEM_HD ** -0.5
    x = x_ref[...]
    h = (x * _rms_scale(x) * g_ref[...]).astype(BF16)
    o_ref[...] = x

    width = head_group * MEM_HD

    def q_proj(grp):
        return _dot(h, wq_ref[:, grp * width:(grp + 1) * width]).astype(BF16)

    nxt = q_proj(0)
    pending = None
    for grp in range(MEM_HEADS // head_group):
        q = nxt
        if (grp + 1) * head_group < MEM_HEADS:
            nxt = q_proj(grp + 1)
        if pending is not None:
            o_ref[...] += _dot(pending, wo_ref[(grp - 1) * width:grp * width, :])
        heads = []
        for hg in range(head_group):
            hd = grp * head_group + hg
            outs = []
            for s in range(nseg):
                kh = _head_rows(k_ref, s, hd)
                vh = _head_rows(v_ref, s, hd)
                sc = _dot_nt(q[s * seg:(s + 1) * seg, hg * MEM_HD:(hg + 1) * MEM_HD], kh) * scale
                p = jnp.exp(sc - jnp.max(sc, axis=-1, keepdims=True))
                denom = jnp.sum(p, axis=-1, keepdims=True)
                outs.append((_dot(p.astype(BF16), vh) / denom).astype(BF16))
            heads.append(outs[0] if nseg == 1 else jnp.concatenate(outs, axis=0))
        pending = heads[0] if head_group == 1 else jnp.concatenate(heads, axis=1)
    o_ref[...] += _dot(pending, wo_ref[MEM_HEADS * MEM_HD - width:, :])


def mem_attn(x, g, w_q, mk, mv, layer, w_o, seq_len, nseg, seg):
    t, d = x.shape
    b = mk.shape[1]
    tiles_per_stream = max(seq_len // (nseg * seg), 1)
    n_outer = b // nseg
    tm = nseg * seg
    x_spec = pl.BlockSpec((tm, d), lambda o, i: (o * tiles_per_stream + i, 0))
    kv_spec = pl.BlockSpec((1, nseg) + mk.shape[2:], lambda o, i: (layer, o, 0, 0))
    return pl.pallas_call(
        functools.partial(_mem_attn_kernel, nseg=nseg, seg=seg, head_group=2 if nseg == 1 else 1),
        grid=(n_outer, tiles_per_stream),
        in_specs=[x_spec, _const_spec((1, d)), _weight_spec(w_q), kv_spec, kv_spec,
                  _weight_spec(w_o)],
        out_specs=x_spec,
        out_shape=jax.ShapeDtypeStruct((t, d), F32),
        compiler_params=_cparams("parallel", "parallel"),
        name="mem_attn",
    )(x, g.reshape(1, d), w_q[0], mk, mv, w_o[0])


def _ffn_kernel(x_ref, g_ref, wup_ref, cw_ref, cb_ref, wdn_ref, st_ref, *rest,
                nseg, seg, fc, final_norm):
    if final_norm:
        gf_ref, o_ref, nst_ref, carry_ref = rest
    else:
        o_ref, nst_ref, carry_ref = rest
    d_ff = wdn_ref.shape[0]
    tm = nseg * seg

    @pl.when(pl.program_id(1) == 0)
    def _():
        carry_ref[...] = st_ref[...]

    x = x_ref[...]
    h = (x * _rms_scale(x) * g_ref[...]).astype(BF16)
    row = lax.broadcasted_iota(jnp.int32, (tm, 1), 0) % seg
    o_ref[...] = x

    def up_proj(f0):
        return (_dot(h, wup_ref[:, f0:f0 + fc]), _dot(h, wup_ref[:, d_ff + f0:d_ff + f0 + fc]))

    nxt = up_proj(0)
    pending = None
    for f0 in range(0, d_ff, fc):
        cols = slice(f0, f0 + fc)
        a, gate = nxt
        if f0 + fc < d_ff:
            nxt = up_proj(f0 + fc)
        if pending is not None:
            o_ref[...] += _dot(pending, wdn_ref[f0 - fc:f0, :])
        p0 = jnp.broadcast_to(carry_ref[:, 0:1, cols], (nseg, seg, fc)).reshape(tm, fc)
        p1 = jnp.broadcast_to(carry_ref[:, 1:2, cols], (nseg, seg, fc)).reshape(tm, fc)
        a1 = jnp.where(row == 0, p1, pltpu.roll(a, 1, 0))
        a2 = jnp.where(row == 0, p0, jnp.where(row == 1, p1, pltpu.roll(a, 2, 0)))
        conv = (cw_ref[0:1, cols] * a2 + cw_ref[1:2, cols] * a1 + cw_ref[2:3, cols] * a
                + cb_ref[:, cols])
        p = jax.nn.gelu(conv) * gate
        pending = p.astype(BF16)
        carry_ref[:, :, cols] = a.reshape(nseg, seg, fc)[:, seg - 2:seg, :]
    y = o_ref[...] + _dot(pending, wdn_ref[d_ff - fc:d_ff, :])
    nst_ref[...] = carry_ref[...]
    if final_norm:
        y = y * _rms_scale(y) * gf_ref[...]
    o_ref[...] = y


def conv_ffn(x, g, w_up, conv_w, conv_b, w_down, conv_state, seq_len, nseg, seg, g_final=None,
             fc=256):
    t, d = x.shape
    d_ff = w_down[0].shape[1]
    b = conv_state.shape[0]
    tiles_per_stream = max(seq_len // (nseg * seg), 1)
    n_outer = b // nseg
    tm = nseg * seg
    final_norm = g_final is not None
    in_specs = [pl.BlockSpec((tm, d), lambda o, i: (o * tiles_per_stream + i, 0)),
                _const_spec((1, d)),
                _weight_spec(w_up),
                _const_spec((3, d_ff)),
                _const_spec((1, d_ff)),
                _weight_spec(w_down),
                pl.BlockSpec((nseg, 2, d_ff), lambda o, i: (o, 0, 0))]
    args = [x, g.reshape(1, d), w_up[0], conv_w, conv_b.reshape(1, d_ff), w_down[0], conv_state]
    if final_norm:
        in_specs.append(_const_spec((1, d)))
        args.append(g_final.reshape(1, d))
    return pl.pallas_call(
        functools.partial(_ffn_kernel, nseg=nseg, seg=seg, fc=fc, final_norm=final_norm),
        grid=(n_outer, tiles_per_stream),
        in_specs=in_specs,
        out_specs=[pl.BlockSpec((tm, d), lambda o, i: (o * tiles_per_stream + i, 0)),
                   pl.BlockSpec((nseg, 2, d_ff), lambda o, i: (o, 0, 0))],
        out_shape=[jax.ShapeDtypeStruct((t, d), F32),
                   jax.ShapeDtypeStruct((b, 2, d_ff), F32)],
        scratch_shapes=[pltpu.VMEM((nseg, 2, d_ff), F32)],
        compiler_params=_cparams("parallel", "arbitrary"),
        name="conv_ffn",
    )(*args)


def _rotate(x, cos, sin):
    half = x.shape[1] // 2
    x1, x2 = x[:, :half], x[:, half:]
    return jnp.concatenate([x1 * cos - x2 * sin, x2 * cos + x1 * sin], axis=1)


def _ret_head(h, q, k, v, g, cos, sin, s_ref, chunk):
    log_g = RET_LOG_G[h]
    li = lax.broadcasted_iota(jnp.int32, (chunk, chunk), 0)
    mi = lax.broadcasted_iota(jnp.int32, (chunk, chunk), 1)
    diff = (li - mi).astype(F32)
    pos = lax.broadcasted_iota(jnp.int32, (chunk, 1), 0).astype(F32)
    qr = _rotate(q, cos, sin).astype(BF16)
    kr = _rotate(k, cos, sin) * (RET_DK ** -0.5)
    decay = jnp.where(diff >= 0, jnp.exp(jnp.maximum(diff, 0.0) * log_g), 0.0)
    scores = _dot_nt(qr, kr.astype(BF16)) * decay
    intra = _dot(scores.astype(BF16), v)
    s_old = s_ref[0, h]
    cross = _dot(qr, s_old.astype(BF16)) * jnp.exp((pos + 1.0) * log_g)
    o = intra + cross
    k_tail = (kr * jnp.exp((chunk - 1.0 - pos) * log_g)).astype(BF16)
    s_ref[0, h] = math.exp(chunk * log_g) * s_old + lax.dot_general(
        k_tail, v, (((0,), (0,)), ((), ())), preferred_element_type=F32)
    mu = jnp.mean(o, axis=-1, keepdims=True)
    oc = o - mu
    var = jnp.mean(oc * oc, axis=-1, keepdims=True)
    return jax.nn.silu(g) * (oc * lax.rsqrt(var + GN_EPS))


def _ret_kernel(p_ref, cos_ref, sin_ref, s_in_ref, y_ref, s_out_ref, *, chunk):
    @pl.when(pl.program_id(1) == 0)
    def _():
        s_out_ref[...] = s_in_ref[...]

    cos = cos_ref[...]
    sin = sin_ref[...]
    for h in range(RET_HEADS):
        q = p_ref[:, h * RET_DK:(h + 1) * RET_DK].astype(F32)
        k = p_ref[:, RET_QK + h * RET_DK:RET_QK + (h + 1) * RET_DK].astype(F32)
        v = p_ref[:, 2 * RET_QK + h * RET_DV:2 * RET_QK + (h + 1) * RET_DV]
        g = p_ref[:, 2 * RET_QK + RET_V + h * RET_DV:
                  2 * RET_QK + RET_V + (h + 1) * RET_DV].astype(F32)
        y = _ret_head(h, q, k, v, g, cos, sin, s_out_ref, chunk)
        y_ref[:, h * RET_DV:(h + 1) * RET_DV] = y.astype(y_ref.dtype)


def _ret_fused_kernel(x_ref, g_ref, w_ref, wo_ref, cos_ref, sin_ref, s_in_ref, o_ref, s_out_ref,
                      *, chunk):
    @pl.when(pl.program_id(1) == 0)
    def _():
        s_out_ref[...] = s_in_ref[...]

    cos = cos_ref[...]
    sin = sin_ref[...]
    x = x_ref[...]
    hn = (x * _rms_scale(x) * g_ref[...]).astype(BF16)
    o_ref[...] = x

    def proj(h):
        c = [h * RET_DK, RET_QK + h * RET_DK, 2 * RET_QK + h * RET_DV, 2 * RET_QK + RET_V + h * RET_DV]
        return (_dot(hn, w_ref[:, c[0]:c[0] + RET_DK]), _dot(hn, w_ref[:, c[1]:c[1] + RET_DK]),
                _dot(hn, w_ref[:, c[2]:c[2] + RET_DV]).astype(BF16),
                _dot(hn, w_ref[:, c[3]:c[3] + RET_DV]))

    group = RET_HEAD_GROUP
    width = group * RET_DV
    nxt = [proj(h) for h in range(group)]
    pending = None
    for h0 in range(0, RET_HEADS, group):
        cur = nxt
        if h0 + group < RET_HEADS:
            nxt = [proj(h) for h in range(h0 + group, h0 + 2 * group)]
        if pending is not None:
            o_ref[...] += _dot(pending, wo_ref[(h0 - group) * RET_DV:h0 * RET_DV, :])
        ys = [_ret_head(h0 + i, *cur[i], cos, sin, s_out_ref, chunk).astype(BF16)
              for i in range(group)]
        pending = ys[0] if group == 1 else jnp.concatenate(ys, axis=1)
    o_ref[...] += _dot(pending, wo_ref[RET_V - width:, :])


def _ret_specs(seq_len, chunk):
    nchunks = seq_len // chunk
    half = RET_DK // 2
    row_spec = lambda n: pl.BlockSpec((chunk, n), lambda o, i: (o * nchunks + i, 0))
    rope_spec = pl.BlockSpec((chunk, half), lambda o, i: (i, 0))
    state_spec = pl.BlockSpec((1, RET_HEADS, RET_DK, RET_DV), lambda o, i: (o, 0, 0, 0))
    return nchunks, row_spec, rope_spec, state_spec


def ret_core(p, cos, sin, state, seq_len, chunk):
    t, n = p.shape
    nchunks, row_spec, rope_spec, state_spec = _ret_specs(seq_len, chunk)
    return pl.pallas_call(
        functools.partial(_ret_kernel, chunk=chunk),
        grid=(state.shape[0], nchunks),
        in_specs=[row_spec(n), rope_spec, rope_spec, state_spec],
        out_specs=[row_spec(RET_V), state_spec],
        out_shape=[jax.ShapeDtypeStruct((t, RET_V), BF16),
                   jax.ShapeDtypeStruct(state.shape, F32)],
        compiler_params=_cparams("parallel", "arbitrary"),
        name="ret_core",
    )(p, cos, sin, state)


def ret_mixer(x, g, w_qkvg, w_o, cos, sin, state, seq_len, chunk):
    t, d = x.shape
    nchunks, row_spec, rope_spec, state_spec = _ret_specs(seq_len, chunk)
    return pl.pallas_call(
        functools.partial(_ret_fused_kernel, chunk=chunk),
        grid=(state.shape[0], nchunks),
        in_specs=[row_spec(d), _const_spec((1, d)), _weight_spec(w_qkvg),
                  _weight_spec(w_o), rope_spec, rope_spec, state_spec],
        out_specs=[row_spec(d), state_spec],
        out_shape=[jax.ShapeDtypeStruct((t, d), F32), jax.ShapeDtypeStruct(state.shape, F32)],
        compiler_params=_cparams("parallel", "arbitrary"),
        name="ret_mixer",
    )(x, g.reshape(1, d), w_qkvg[0], w_o[0], cos, sin, state)


def _rope_tables(pos0, length):
    half = RET_DK // 2
    freqs = ROPE_BASE ** (-np.arange(half, dtype=np.float64) / half)
    ang = (pos0 + np.arange(length, dtype=np.float64))[:, None] * freqs[None, :]
    return jnp.asarray(np.cos(ang), F32), jnp.asarray(np.sin(ang), F32)


def _cmul(ar, ai, br, bi):
    return ar * br - ai * bi, ar * bi + ai * br


def _s5_prep_kernel(lam_re_ref, lam_im_ref, ldt_ref, b_re_ref, b_im_ref, c_re_ref, c_im_ref,
                    w_re_ref, w_im_ref, bt_ref, cct_ref, a16_ref):
    dt = jnp.exp(ldt_ref[...])
    lam_re, lam_im = lam_re_ref[...], lam_im_ref[...]
    mag = jnp.exp(lam_re * dt)
    ar, ai = mag * jnp.cos(lam_im * dt), mag * jnp.sin(lam_im * dt)
    den = lam_re * lam_re + lam_im * lam_im
    nr, ni = ar - 1.0, ai
    coef_re = (nr * lam_re + ni * lam_im) / den
    coef_im = (ni * lam_re - nr * lam_im) / den
    bb_re, bb_im = _cmul(coef_re, coef_im, b_re_ref[...], b_im_ref[...])
    c_re, c_im = c_re_ref[...], c_im_ref[...]
    lane = lax.broadcasted_iota(jnp.int32, (1, LANES), 1)
    low = lane < SSM_STATE
    rg = lax.broadcasted_iota(jnp.int32, (LANES, 1), 0) // SSM_GROUP
    pair = rg // 2
    own_half = (lane // SSM_STATE) == (rg % 2)
    same_group = rg == lane // SSM_GROUP
    c_hi, c_lo = _split_bf16(jnp.where(low, c_re, -c_im))
    tk = []
    pr, pi = jnp.ones_like(ar), jnp.zeros_like(ar)
    for j in range(SSM_FOLD):
        zr, zi = _cmul(pr, pi, bb_re, bb_im)
        w_re_ref[SSM_FOLD - 1 - j] = jnp.where(own_half, zr, 0.0)
        w_im_ref[SSM_FOLD - 1 - j] = jnp.where(own_half, zi, 0.0)
        b_hi, b_lo = _split_bf16(jnp.where(low, zr, zi))
        kk = _dot_nt(b_hi, c_hi) + _dot_nt(b_lo, c_hi) + _dot_nt(b_hi, c_lo)
        tk.append(jnp.where(same_group, kk, 0.0).astype(BF16))
        pr, pi = _cmul(pr, pi, ar, ai)
        zr, zi = _cmul(c_re, c_im, pr, pi)
        for ri, z in enumerate((zr, -zi)):
            z = jnp.where(own_half, z, 0.0)
            for t in range(GROUPS_PER_TILE // 2):
                c0 = ri * STATE_PER_TILE + t * LANES
                cct_ref[0, j * LANES:(j + 1) * LANES, c0:c0 + LANES] = (
                    jnp.where(pair == t, z, 0.0).astype(BF16))
    a16_ref[0] = pr
    a16_ref[1] = pi
    zero = jnp.zeros((LANES, LANES), BF16)
    for d2 in range(SSM_FOLD // 2):
        top = jnp.concatenate([tk[2 * d2], tk[2 * d2 + 1]], axis=1)
        bot = jnp.concatenate([tk[2 * d2 - 1] if d2 else zero, tk[2 * d2]], axis=1)
        bt_ref[0, d2] = jnp.concatenate([top, bot], axis=0)


def s5_prepare(a_re, a_im, log_dt, b_re, b_im, c_re, c_im):
    g, p, c = SSM_GROUPS, SSM_STATE, SSM_GROUP
    fold, nt = SSM_FOLD, N_LANE_TILES
    rows = g * c
    twice = lambda x: jnp.tile(x, (1, 2))
    lam = lambda x: twice(jnp.repeat(x, c, axis=0))
    in_spec = pl.BlockSpec((LANES, LANES), lambda t: (t, 0))
    w_spec = pl.BlockSpec((fold, LANES, LANES), lambda t: (0, t, 0))
    w_shape = jax.ShapeDtypeStruct((fold, rows, LANES), F32)
    w_re, w_im, bt, cct, a16 = pl.pallas_call(
        _s5_prep_kernel,
        grid=(nt,),
        in_specs=[in_spec, in_spec, pl.BlockSpec((LANES, 1), lambda t: (t, 0))] + [in_spec] * 4,
        out_specs=[w_spec, w_spec,
                   pl.BlockSpec((1, fold // 2, 2 * LANES, 2 * LANES), lambda t: (t, 0, 0, 0)),
                   pl.BlockSpec((1, fold * LANES, 2 * STATE_PER_TILE), lambda t: (t, 0, 0)),
                   pl.BlockSpec((2, LANES, LANES), lambda t: (0, t, 0))],
        out_shape=[w_shape, w_shape,
                   jax.ShapeDtypeStruct((nt, fold // 2, 2 * LANES, 2 * LANES), BF16),
                   jax.ShapeDtypeStruct((nt, fold * LANES, 2 * STATE_PER_TILE), BF16),
                   jax.ShapeDtypeStruct((2, rows, LANES), F32)],
        compiler_params=_cparams("parallel"),
        name="s5_prep",
    )(lam(a_re), lam(a_im), jnp.repeat(log_dt, c).reshape(rows, 1),
      twice(b_re.transpose(0, 2, 1).reshape(rows, p)), twice(b_im.transpose(0, 2, 1).reshape(rows, p)),
      twice(c_re.reshape(rows, p)), twice(c_im.reshape(rows, p)))
    a16 = a16[:, ::c, :p].reshape(2, 1, N_STATE)
    return bt, cct, w_re, w_im, a16


FOLD_BLOCK = SSM_FOLD * SSM_FOLD


def _fold_perm():
    idx = np.arange(FOLD_BLOCK)
    perm = np.zeros((FOLD_BLOCK, FOLD_BLOCK), np.float32)
    perm[(idx % SSM_FOLD) * SSM_FOLD + idx // SSM_FOLD, idx] = 1.0
    return jnp.asarray(perm, BF16)


def _s5_fold_kernel(x_ref, g_ref, perm_ref, u_ref):
    for blk in range(x_ref.shape[0] // FOLD_BLOCK):
        x = x_ref[blk * FOLD_BLOCK:(blk + 1) * FOLD_BLOCK, :]
        h = (x * _rms_scale(x) * g_ref[...]).astype(BF16)
        f = _dot(perm_ref[...], h).astype(BF16)
        for m in range(SSM_FOLD):
            for k in range(N_LANE_TILES):
                u_ref[m, k, blk * SSM_FOLD:(blk + 1) * SSM_FOLD, :] = (
                    f[m * SSM_FOLD:(m + 1) * SSM_FOLD, k * LANES:(k + 1) * LANES])


def s5_fold(x, g, perm, tokens):
    t, d = x.shape
    rows = t // SSM_FOLD
    return pl.pallas_call(
        _s5_fold_kernel,
        grid=(t // tokens,),
        in_specs=[pl.BlockSpec((tokens, d), lambda i: (i, 0)), _const_spec((1, d)),
                  _const_spec((FOLD_BLOCK, FOLD_BLOCK))],
        out_specs=pl.BlockSpec((SSM_FOLD, N_LANE_TILES, tokens // SSM_FOLD, LANES),
                               lambda i: (0, 0, i, 0)),
        out_shape=jax.ShapeDtypeStruct((SSM_FOLD, N_LANE_TILES, rows, LANES), BF16),
        compiler_params=_cparams("parallel"),
        name="s5_fold",
    )(x, g.reshape(1, d), perm)


def _s5_state_in_kernel(u_ref, w_re_ref, w_im_ref, vre_ref, vim_ref, wd_ref):
    @pl.when(pl.program_id(1) == 0)
    def _():
        pair = lax.broadcasted_iota(jnp.int32, (LANES, 1), 0) // SSM_GROUP // 2
        for m in range(SSM_FOLD):
            for ri, ref in enumerate((w_re_ref, w_im_ref)):
                w = ref[m]
                for t in range(GROUPS_PER_TILE // 2):
                    c0 = ri * STATE_PER_TILE + t * LANES
                    wd_ref[m * LANES:(m + 1) * LANES, c0:c0 + LANES] = (
                        jnp.where(pair == t, w, 0.0).astype(BF16))

    lhs = jnp.concatenate([u_ref[m, 0] for m in range(SSM_FOLD)], axis=1)
    v = _dot(lhs, wd_ref[...])
    vre_ref[...] = v[:, :STATE_PER_TILE]
    vim_ref[...] = v[:, STATE_PER_TILE:]


def _s5_tile_specs(rt):
    u_spec = pl.BlockSpec((SSM_FOLD, 1, rt, LANES), lambda t, i: (0, t, i, 0))
    s_spec = pl.BlockSpec((rt, STATE_PER_TILE), lambda t, i: (i, t))
    return u_spec, s_spec


def s5_state_in(u, w_re, w_im, rt):
    rows = u.shape[2]
    u_spec, s_spec = _s5_tile_specs(rt)
    out = jax.ShapeDtypeStruct((rows, N_STATE), F32)
    w_spec = pl.BlockSpec((SSM_FOLD, LANES, LANES), lambda t, i: (0, t, 0))
    return pl.pallas_call(
        _s5_state_in_kernel,
        grid=(N_LANE_TILES, rows // rt),
        in_specs=[u_spec, w_spec, w_spec],
        out_specs=[s_spec, s_spec],
        out_shape=[out, out],
        scratch_shapes=[pltpu.VMEM((SSM_FOLD * LANES, 2 * STATE_PER_TILE), BF16)],
        compiler_params=_cparams("parallel", "arbitrary"),
        name="s5_state_in",
    )(u, w_re, w_im)


def _s5_scan_pairs_kernel(vre_ref, vim_ref, a_ref, s0re_ref, s0im_ref,
                          sre_ref, sim_ref, fre_ref, fim_ref):
    ar, ai = a_ref[0], a_ref[1]
    s0r, s0i = s0re_ref[...], s0im_ref[...]
    vr, vi = vre_ref[...], vim_ref[...]
    first = lax.broadcasted_iota(jnp.int32, (s0r.shape[0], 1), 0) % 2 == 0
    tr, ti = ar * s0r - ai * s0i + vr, ar * s0i + ai * s0r + vi
    sr = jnp.where(first, s0r, pltpu.roll(tr, 1, 0))
    si = jnp.where(first, s0i, pltpu.roll(ti, 1, 0))
    sre_ref[...] = sr
    sim_ref[...] = si
    fre_ref[...] = ar * sr - ai * si + vr
    fim_ref[...] = ar * si + ai * sr + vi


def _s5_scan_stream_kernel(vre_ref, vim_ref, a_ref, s0re_ref, s0im_ref,
                           sre_ref, sim_ref, fre_ref, fim_ref, *, steps):
    q = lax.broadcasted_iota(jnp.int32, (SUBLANES, 1), 0)
    a1 = (a_ref[0], a_ref[1])
    a2 = _cmul(*a1, *a1)
    a4 = _cmul(*a2, *a2)
    a8 = _cmul(*a4, *a4)
    strides = ((1, a1), (2, a2), (4, a4))
    width = a1[0].shape[1]
    pw = (jnp.ones((SUBLANES, width), F32), jnp.zeros((SUBLANES, width), F32))
    for k, ak in strides:
        nxt = _cmul(*pw, *ak)
        bit = (q & k) != 0
        pw = (jnp.where(bit, nxt[0], pw[0]), jnp.where(bit, nxt[1], pw[1]))

    def shifted(x, k):
        return jnp.where(q >= k, pltpu.roll(x, k, 0), 0.0)

    def body(t, carry):
        cr, ci = carry
        rows = pl.ds(pl.multiple_of(t * SUBLANES, SUBLANES), SUBLANES)
        pr, pi = vre_ref[rows, :], vim_ref[rows, :]
        for k, ak in strides:
            dr, di = _cmul(shifted(pr, k), shifted(pi, k), *ak)
            pr, pi = pr + dr, pi + di
        xr, xi = _cmul(*pw, cr, ci)
        sre_ref[rows, :] = xr + shifted(pr, 1)
        sim_ref[rows, :] = xi + shifted(pi, 1)
        nr, ni = _cmul(*a8, cr, ci)
        return nr + pr[SUBLANES - 1:], ni + pi[SUBLANES - 1:]

    cr, ci = lax.fori_loop(0, steps // SUBLANES, body, (s0re_ref[0], s0im_ref[0]))
    fre_ref[0] = cr
    fim_ref[0] = ci


def s5_scan(v_re, v_im, a16, s0_re, s0_im, width):
    rows, n = v_re.shape
    streams = s0_re.shape[0]
    steps = rows // streams
    a_spec = pl.BlockSpec((2, 1, width), lambda *i: (0, 0, i[-1]))
    vout = jax.ShapeDtypeStruct((rows, n), F32)
    long_streams = steps % SUBLANES == 0
    if long_streams:
        kern = functools.partial(_s5_scan_stream_kernel, steps=steps)
        grid = (streams, n // width)
        vspec = pl.BlockSpec((steps, width), lambda b, c: (b, c))
        sspec = pl.BlockSpec((1, 1, width), lambda b, c: (b, 0, c))
        s0_re, s0_im = s0_re.reshape(streams, 1, n), s0_im.reshape(streams, 1, n)
        sout = jax.ShapeDtypeStruct((streams, 1, n), F32)
    else:
        assert steps == 2 and rows % SUBLANES == 0
        kern = _s5_scan_pairs_kernel
        grid = (n // width,)
        vspec = pl.BlockSpec((rows, width), lambda c: (0, c))
        sspec = vspec
        s0_re, s0_im = jnp.repeat(s0_re, steps, axis=0), jnp.repeat(s0_im, steps, axis=0)
        sout = vout
    s_re, s_im, f_re, f_im = pl.pallas_call(
        kern,
        grid=grid,
        in_specs=[vspec, vspec, a_spec, sspec, sspec],
        out_specs=[vspec, vspec, sspec, sspec],
        out_shape=[vout, vout, sout, sout],
        compiler_params=_cparams(*(["parallel"] * len(grid))),
        name="s5_scan",
    )(v_re, v_im, a16, s0_re, s0_im)
    if long_streams:
        return s_re, s_im, f_re.reshape(streams, n), f_im.reshape(streams, n)
    return s_re, s_im, f_re[steps - 1::steps], f_im[steps - 1::steps]


def _s5_toeplitz_kernel(u_ref, sre_ref, sim_ref, bt_ref, cct_ref, y_ref):
    npair = SSM_FOLD // 2
    u2 = [jnp.concatenate([u_ref[2 * mm, 0], u_ref[2 * mm + 1, 0]], axis=1) for mm in range(npair)]
    s_prev = jnp.concatenate([sre_ref[...], sim_ref[...]], axis=1).astype(BF16)
    for ll in range(npair):
        acc = _dot_nt(s_prev, cct_ref[0, ll * 2 * LANES:(ll + 1) * 2 * LANES, :])
        for mm in range(ll + 1):
            acc = acc + _dot(u2[mm], bt_ref[0, ll - mm])
        y_ref[2 * ll, 0] = acc[:, :LANES]
        y_ref[2 * ll + 1, 0] = acc[:, LANES:]


def s5_toeplitz(u, s_re, s_im, bt, cct, rt):
    rows = u.shape[2]
    u_spec, s_spec = _s5_tile_specs(rt)
    return pl.pallas_call(
        _s5_toeplitz_kernel,
        grid=(N_LANE_TILES, rows // rt),
        in_specs=[u_spec, s_spec, s_spec,
                  pl.BlockSpec((1, SSM_FOLD // 2, 2 * LANES, 2 * LANES), lambda t, i: (t, 0, 0, 0)),
                  pl.BlockSpec((1, SSM_FOLD * LANES, 2 * STATE_PER_TILE), lambda t, i: (t, 0, 0))],
        out_specs=u_spec,
        out_shape=jax.ShapeDtypeStruct(u.shape, F32),
        compiler_params=_cparams("parallel", "parallel"),
        name="s5_toeplitz",
    )(u, s_re, s_im, bt, cct)


def _s5_glu_kernel(x_ref, u_ref, y_ref, d_ref, wglu_ref, bglu_ref, perm_ref, o_ref):
    nblk = x_ref.shape[0] // FOLD_BLOCK

    def gelu_block(blk):
        rows = slice(blk * SSM_FOLD, (blk + 1) * SSM_FOLD)
        gl = []
        for m in range(SSM_FOLD):
            ym = jnp.concatenate([y_ref[m, k, rows, :] for k in range(N_LANE_TILES)], axis=1)
            hm = jnp.concatenate([u_ref[m, k, rows, :] for k in range(N_LANE_TILES)], axis=1)
            gl.append(jax.nn.gelu(ym + d_ref[...] * hm.astype(F32)))
        gl = jnp.concatenate(gl, axis=0)
        return gl, _dot(gl.astype(BF16), wglu_ref[...])

    def finish(blk, gl, z):
        tok = slice(blk * FOLD_BLOCK, (blk + 1) * FOLD_BLOCK)
        hi, lo = _split_bf16(gl * jax.nn.sigmoid(z + bglu_ref[...]))
        o_ref[tok, :] = x_ref[tok, :] + (_dot(perm_ref[...], hi) + _dot(perm_ref[...], lo))

    prev = gelu_block(0)
    for blk in range(1, nblk):
        cur = gelu_block(blk)
        finish(blk - 1, *prev)
        prev = cur
    finish(nblk - 1, *prev)


def s5_glu(x, u, y, d, w_glu, b_glu, perm, tokens):
    t, dm = x.shape
    fold_spec = pl.BlockSpec((SSM_FOLD, N_LANE_TILES, tokens // SSM_FOLD, LANES),
                             lambda i: (0, 0, i, 0))
    x_spec = pl.BlockSpec((tokens, dm), lambda i: (i, 0))
    return pl.pallas_call(
        _s5_glu_kernel,
        grid=(t // tokens,),
        in_specs=[x_spec, fold_spec, fold_spec, _const_spec((1, dm)), _weight_spec(w_glu),
                  _const_spec((1, dm)), _const_spec((FOLD_BLOCK, FOLD_BLOCK))],
        out_specs=x_spec,
        out_shape=jax.ShapeDtypeStruct((t, dm), F32),
        compiler_params=_cparams("parallel"),
        name="s5_glu",
    )(x, u, y, d.reshape(1, dm), w_glu[0], b_glu.reshape(1, dm), perm)


def s5_mixer(x, st_re, st_im, g, ops, d, w_glu, b_glu):
    bt, cct, w_re, w_im, a16 = ops
    b, l, dm = x.shape
    rows = b * l // SSM_FOLD
    x2 = x.reshape(b * l, dm)
    perm = _fold_perm()
    tokens = min(S5_TOKEN_TILE, b * l)
    rt = min(S5_ROW_TILE, rows)
    u = s5_fold(x2, g, perm, tokens)
    v_re, v_im = s5_state_in(u, w_re, w_im, rt)
    s_re, s_im, f_re, f_im = s5_scan(v_re, v_im, a16, st_re.reshape(b, N_STATE),
                                     st_im.reshape(b, N_STATE), S5_SCAN_WIDTH)
    y = s5_toeplitz(u, s_re, s_im, bt, cct, rt)
    out = s5_glu(x2, u, y, d, w_glu, b_glu, perm, tokens)
    shape = (b, SSM_GROUPS, SSM_STATE)
    return out.reshape(b, l, dm), f_re.reshape(shape), f_im.reshape(shape)


def _stream_tiling(seq_len, n_streams, tile_rows):
    if seq_len >= tile_rows:
        return 1, tile_rows
    return min(n_streams, 256 // seq_len), seq_len


def kernel(x_prompt, x_sample, mem_prompt, state_ssm_re, state_ssm_im, state_ret, cache_mem_k, cache_mem_v, cache_conv, norm_mix, norm_mem_q, norm_mem_kv, norm_ffn, norm_final, ssm_a_re, ssm_a_im, ssm_log_dt, ssm_b_re, ssm_b_im, ssm_c_re, ssm_c_im, ssm_d, ssm_w_glu, ssm_b_glu, ret_w_qkvg, ret_w_o, mem_w_q, mem_w_kv, mem_w_o, ffn_w_up, ffn_conv_w, ffn_conv_b, ffn_w_down):
    bp, lp, dm = x_prompt.shape
    bs, ls, _ = x_sample.shape
    depth = norm_mix.shape[0]
    n_mem = mem_prompt.shape[1]
    d_ff = ffn_w_down.shape[1]
    streams = [dict(x=x, b=x.shape[0], l=x.shape[1],
                    attn_tiling=_stream_tiling(x.shape[1], x.shape[0], ATTN_TILE_ROWS),
                    ffn_tiling=_stream_tiling(x.shape[1], x.shape[0], FFN_TILE_ROWS))
               for x in (x_prompt, x_sample)]
    cache_k, cache_v = _kv_rows(cache_mem_k), _kv_rows(cache_mem_v)
    ssm_w_glu, ret_w_qkvg, ret_w_o, mem_w_q, mem_w_kv, mem_w_o, ffn_w_up, ffn_w_down = (
        w.astype(BF16) for w in (ssm_w_glu, ret_w_qkvg, ret_w_o, mem_w_q, mem_w_kv, mem_w_o,
                                 ffn_w_up, ffn_w_down))
    outs = dict(ssm_re=([], []), ssm_im=([], []), ret=([], []), conv=([], []), mk=[], mv=[])

    for i in range(depth):
        j = i // 2
        if i % 2 == 0:
            ops = s5_prepare(ssm_a_re[j], ssm_a_im[j], ssm_log_dt[j], ssm_b_re[j], ssm_b_im[j],
                             ssm_c_re[j], ssm_c_im[j])
            w_glu = (ssm_w_glu, j)
            zeros = jnp.zeros((bp, SSM_GROUPS, SSM_STATE), F32)
            states = [(zeros, zeros), (state_ssm_re[j], state_ssm_im[j])]
            for si, (s, (st_re, st_im)) in enumerate(zip(streams, states)):
                s["x"], f_re, f_im = s5_mixer(s["x"], st_re, st_im, norm_mix[i], ops, ssm_d[j],
                                              w_glu, ssm_b_glu[j])
                outs["ssm_re"][si].append(f_re)
                outs["ssm_im"][si].append(f_im)
        else:
            w_qkvg = (ret_w_qkvg, j)
            w_o = (ret_w_o, j)
            zeros = jnp.zeros((bp, RET_HEADS, RET_DK, RET_DV), F32)
            chunk = RET_CHUNK_ROWS
            for si, (s, st, pos0) in enumerate(zip(streams, [zeros, state_ret[j]], [0, PAST_LEN])):
                x2 = s["x"].reshape(s["b"] * s["l"], dm)
                cos, sin = _rope_tables(pos0, s["l"])
                if s["l"] >= chunk:
                    y, st_new = ret_mixer(x2, norm_mix[i], w_qkvg, w_o, cos, sin, st, s["l"], chunk)
                else:
                    p = rms_matmul(x2, norm_mix[i], w_qkvg, BF16)
                    y, st_new = ret_core(p, cos, sin, st, s["l"], s["l"])
                    y = matmul_res(y, w_o, x2)
                s["x"] = y.reshape(s["x"].shape)
                outs["ret"][si].append(st_new)

        mk, mv = mem_kv(mem_prompt, norm_mem_kv[i], (mem_w_kv, i))
        outs["mk"].append(_kv_unrows(mk, MEM_HEADS))
        outs["mv"].append(_kv_unrows(mv, MEM_HEADS))
        w_q = (mem_w_q, i)
        w_o = (mem_w_o, i)
        for s, (k_, v_, layer) in zip(streams, [(mk[None], mv[None], 0), (cache_k, cache_v, i)]):
            x2 = s["x"].reshape(s["b"] * s["l"], dm)
            y = mem_attn(x2, norm_mem_q[i], w_q, k_, v_, layer, w_o, s["l"], *s["attn_tiling"])
            s["x"] = y.reshape(s["x"].shape)

        w_up = (ffn_w_up, i)
        w_dn = (ffn_w_down, i)
        g_final = norm_final if i == depth - 1 else None
        conv_states = [jnp.zeros((bp, 2, d_ff), F32), cache_conv[i]]
        for si, (s, cst) in enumerate(zip(streams, conv_states)):
            x2 = s["x"].reshape(s["b"] * s["l"], dm)
            y, cnew = conv_ffn(x2, norm_ffn[i], w_up, ffn_conv_w[i], ffn_conv_b[i], w_dn, cst,
                               s["l"], *s["ffn_tiling"], g_final=g_final)
            s["x"] = y.reshape(s["x"].shape)
            outs["conv"][si].append(cnew)

    return (streams[0]["x"], streams[1]["x"],
            jnp.stack(outs["ssm_re"][0]), jnp.stack(outs["ssm_im"][0]),
            jnp.stack(outs["ssm_re"][1]), jnp.stack(outs["ssm_im"][1]),
            jnp.stack(outs["ret"][0]), jnp.stack(outs["ret"][1]),
            jnp.stack(outs["mk"]), jnp.stack(outs["mv"]),
            jnp.stack(outs["conv"][0]), jnp.stack(outs["conv"][1]))
```

```python
import functools
import math

import numpy as np
import jax
import jax.numpy as jnp
from jax import lax
from jax.experimental import pallas as pl
from jax.experimental.pallas import tpu as pltpu

F32 = jnp.float32
BF16 = jnp.bfloat16

D_MODEL = 1024
PAST_LEN = 1024
EPS = 1e-6
GN_EPS = 1e-5
ROPE_BASE = 10000.0
LANES = 128
SUBLANES = 8
SSM_GROUP = 16
SSM_GROUPS = D_MODEL // SSM_GROUP
SSM_STATE = 64
SSM_FOLD = 16
GROUPS_PER_TILE = LANES // SSM_GROUP
N_LANE_TILES = D_MODEL // LANES
STATE_PER_TILE = GROUPS_PER_TILE * SSM_STATE
N_STATE = SSM_GROUPS * SSM_STATE
RET_HEADS = 4
RET_DK = D_MODEL // RET_HEADS
RET_DV = 2 * D_MODEL // RET_HEADS
RET_QK = RET_HEADS * RET_DK
RET_V = RET_HEADS * RET_DV
RET_HEAD_GROUP = 2
RET_LOG_G = tuple(math.log(1.0 - 2.0 ** (-5.0 - h)) for h in range(RET_HEADS))
MEM_HEADS = 4
MEM_HD = D_MODEL // MEM_HEADS
VMEM_LIMIT = 56 * 1024 * 1024
ATTN_TILE_ROWS = 512
FFN_TILE_ROWS = 256
RET_CHUNK_ROWS = 256
S5_TOKEN_TILE = 1024
S5_ROW_TILE = 512
S5_SCAN_WIDTH = 1024


def _cparams(*sem):
    return pltpu.CompilerParams(dimension_semantics=sem, vmem_limit_bytes=VMEM_LIMIT)


def _const_spec(shape):
    nd = len(shape)
    return pl.BlockSpec(shape, lambda *_: (0,) * nd, pipeline_mode=pl.Buffered(1))


def _weight_spec(w):
    stack, layer = w
    return pl.BlockSpec((None,) + stack.shape[1:], lambda *_: (layer, 0, 0),
                        pipeline_mode=pl.Buffered(1))


def _rms_scale(x):
    return lax.rsqrt(jnp.mean(x * x, axis=-1, keepdims=True) + EPS)


def _prenorm(x, g_ref):
    return (x * g_ref[...]).astype(BF16), _rms_scale(x)


def _split_bf16(x):
    hi = x.astype(BF16)
    lo = (x - hi.astype(F32)).astype(BF16)
    return hi, lo


def _dot(a, b):
    return jnp.dot(a, b, preferred_element_type=F32)


def _dot_nt(a, b):
    return lax.dot_general(a, b, (((1,), (1,)), ((), ())), preferred_element_type=F32)


def _rms_matmul_kernel(x_ref, g_ref, w_ref, o_ref, *, tn):
    x = x_ref[...]
    h = (x * _rms_scale(x) * g_ref[...]).astype(BF16)
    for n0 in range(0, w_ref.shape[1], tn):
        o_ref[:, n0:n0 + tn] = _dot(h, w_ref[:, n0:n0 + tn]).astype(o_ref.dtype)


def rms_matmul(x, g, w, out_dtype, tm=512, tn=512):
    t, d = x.shape
    n = w[0].shape[2]
    tm = min(tm, t)
    return pl.pallas_call(
        functools.partial(_rms_matmul_kernel, tn=tn),
        grid=(t // tm,),
        in_specs=[pl.BlockSpec((tm, d), lambda i: (i, 0)),
                  _const_spec((1, d)),
                  _weight_spec(w)],
        out_specs=pl.BlockSpec((tm, n), lambda i: (i, 0)),
        out_shape=jax.ShapeDtypeStruct((t, n), out_dtype),
        compiler_params=_cparams("parallel"),
        name="rms_matmul",
    )(x, g.reshape(1, d), w[0])


def _matmul_res_kernel(y_ref, w_ref, x_ref, o_ref):
    o_ref[...] = x_ref[...] + _dot(y_ref[...], w_ref[...])


def matmul_res(y, w, x, tm=512):
    t, k = y.shape
    d = w[0].shape[2]
    tm = min(tm, t)
    return pl.pallas_call(
        _matmul_res_kernel,
        grid=(t // tm,),
        in_specs=[pl.BlockSpec((tm, k), lambda i: (i, 0)),
                  _weight_spec(w),
                  pl.BlockSpec((tm, d), lambda i: (i, 0))],
        out_specs=pl.BlockSpec((tm, d), lambda i: (i, 0)),
        out_shape=jax.ShapeDtypeStruct((t, d), F32),
        compiler_params=_cparams("parallel"),
        name="matmul_res",
    )(y, w[0], x)


MEM_LANE_TILES = MEM_HD // LANES


def _kv_rows(x):
    *lead, n, h, hd = x.shape
    x = x.reshape(*lead, n, h, MEM_LANE_TILES, LANES)
    return jnp.swapaxes(x, -3, -2).reshape(*lead, n * MEM_LANE_TILES * h, LANES)


def _kv_unrows(x, heads):
    *lead, rows, _ = x.shape
    n = rows // (MEM_LANE_TILES * heads)
    x = x.reshape(*lead, n, MEM_LANE_TILES, heads, LANES)
    return jnp.swapaxes(x, -3, -2).reshape(*lead, n, heads, MEM_LANE_TILES * LANES)


def _mem_kv_kernel(x_ref, g_ref, w_ref, k_ref, v_ref):
    x = x_ref[...]
    h = (x * _rms_scale(x) * g_ref[...]).astype(BF16)
    n_mem, d = x.shape
    stride = MEM_LANE_TILES * MEM_HEADS
    for which, ref in enumerate((k_ref, v_ref)):
        for hd in range(MEM_HEADS):
            c0 = which * d + hd * MEM_HD
            res = _dot(h, w_ref[:, c0:c0 + MEM_HD])
            for j in range(MEM_LANE_TILES):
                ref[pl.ds(j * MEM_HEADS + hd, n_mem, stride=stride), :] = (
                    res[:, j * LANES:(j + 1) * LANES])


def mem_kv(mem, g, w_kv):
    b, n_mem, d = mem.shape
    depth = w_kv.shape[0]
    rows = n_mem * MEM_LANE_TILES * MEM_HEADS
    out_spec = pl.BlockSpec((None, None, rows, LANES), lambda l, i: (l, i, 0, 0))
    out = jax.ShapeDtypeStruct((depth, b, rows, LANES), F32)
    return pl.pallas_call(
        _mem_kv_kernel,
        grid=(depth, b),
        in_specs=[pl.BlockSpec((None, n_mem, d), lambda l, i: (i, 0, 0)),
                  pl.BlockSpec((None, 1, d), lambda l, i: (l, 0, 0)),
                  pl.BlockSpec((None,) + w_kv.shape[1:], lambda l, i: (l, 0, 0))],
        out_specs=[out_spec, out_spec],
        out_shape=[out, out],
        compiler_params=_cparams("parallel", "parallel"),
        name="mem_kv",
    )(mem, g.reshape(depth, 1, d), w_kv)


def _head_rows(kv_ref, s, hd):
    stride = MEM_LANE_TILES * MEM_HEADS
    n_mem = kv_ref.shape[2] // stride
    return jnp.concatenate(
        [kv_ref[0, s, pl.ds(j * MEM_HEADS + hd, n_mem, stride=stride), :]
         for j in range(MEM_LANE_TILES)], axis=1).astype(BF16)


def _mem_attn_kernel(x_ref, g_ref, wq_ref, k_ref, v_ref, wo_ref, o_ref, *, nseg, seg, head_group):
    scale = MEM_HD ** -0.5
    x = x_ref[...]
    h, row_scale = _prenorm(x, g_ref)
    o_ref[...] = x

    width = head_group * MEM_HD

    def q_proj(grp):
        return (_dot(h, wq_ref[:, grp * width:(grp + 1) * width]) * row_scale).astype(BF16)

    nxt = q_proj(0)
    pending = None
    for grp in range(MEM_HEADS // head_group):
        q = nxt
        if (grp + 1) * head_group < MEM_HEADS:
            nxt = q_proj(grp + 1)
        if pending is not None:
            o_ref[...] += _dot(pending, wo_ref[(grp - 1) * width:grp * width, :])
        heads = []
        for hg in range(head_group):
            hd = grp * head_group + hg
            outs = []
            for s in range(nseg):
                kh = _head_rows(k_ref, s, hd)
                vh = _head_rows(v_ref, s, hd)
                sc = _dot_nt(q[s * seg:(s + 1) * seg, hg * MEM_HD:(hg + 1) * MEM_HD], kh) * scale
                p = jnp.exp(sc - jnp.max(sc, axis=-1, keepdims=True))
                denom = jnp.sum(p, axis=-1, keepdims=True)
                outs.append((_dot(p.astype(BF16), vh) / denom).astype(BF16))
            heads.append(outs[0] if nseg == 1 else jnp.concatenate(outs, axis=0))
        pending = heads[0] if head_group == 1 else jnp.concatenate(heads, axis=1)
    o_ref[...] += _dot(pending, wo_ref[MEM_HEADS * MEM_HD - width:, :])


def mem_attn(x, g, w_q, mk, mv, layer, w_o, seq_len, nseg, seg):
    t, d = x.shape
    b = mk.shape[1]
    tiles_per_stream = max(seq_len // (nseg * seg), 1)
    n_outer = b // nseg
    tm = nseg * seg
    x_spec = pl.BlockSpec((tm, d), lambda o, i: (o * tiles_per_stream + i, 0))
    kv_spec = pl.BlockSpec((1, nseg) + mk.shape[2:], lambda o, i: (layer, o, 0, 0))
    return pl.pallas_call(
        functools.partial(_mem_attn_kernel, nseg=nseg, seg=seg, head_group=2 if nseg == 1 else 1),
        grid=(n_outer, tiles_per_stream),
        in_specs=[x_spec, _const_spec((1, d)), _weight_spec(w_q), kv_spec, kv_spec,
                  _weight_spec(w_o)],
        out_specs=x_spec,
        out_shape=jax.ShapeDtypeStruct((t, d), F32),
        compiler_params=_cparams("parallel", "parallel"),
        name="mem_attn",
    )(x, g.reshape(1, d), w_q[0], mk, mv, w_o[0])


def _ffn_kernel(x_ref, g_ref, wup_ref, cw_ref, cb_ref, wdn_ref, st_ref, *rest,
                nseg, seg, fc, final_norm):
    if final_norm:
        gf_ref, o_ref, nst_ref, carry_ref = rest
    else:
        o_ref, nst_ref, carry_ref = rest
    d_ff = wdn_ref.shape[0]
    tm = nseg * seg

    @pl.when(pl.program_id(1) == 0)
    def _():
        carry_ref[...] = st_ref[...]

    x = x_ref[...]
    h, scale = _prenorm(x, g_ref)
    row = lax.broadcasted_iota(jnp.int32, (tm, 1), 0) % seg
    o_ref[...] = x

    def up_proj(f0):
        return (_dot(h, wup_ref[:, f0:f0 + fc]) * scale,
                _dot(h, wup_ref[:, d_ff + f0:d_ff + f0 + fc]) * scale)

    nxt = up_proj(0)
    pending = None
    for f0 in range(0, d_ff, fc):
        cols = slice(f0, f0 + fc)
        a, gate = nxt
        if f0 + fc < d_ff:
            nxt = up_proj(f0 + fc)
        if pending is not None:
            o_ref[...] += _dot(pending, wdn_ref[f0 - fc:f0, :])
        p0 = jnp.broadcast_to(carry_ref[:, 0:1, cols], (nseg, seg, fc)).reshape(tm, fc)
        p1 = jnp.broadcast_to(carry_ref[:, 1:2, cols], (nseg, seg, fc)).reshape(tm, fc)
        a1 = jnp.where(row == 0, p1, pltpu.roll(a, 1, 0))
        a2 = jnp.where(row == 0, p0, jnp.where(row == 1, p1, pltpu.roll(a, 2, 0)))
        conv = (cw_ref[0:1, cols] * a2 + cw_ref[1:2, cols] * a1 + cw_ref[2:3, cols] * a
                + cb_ref[:, cols])
        p = jax.nn.gelu(conv) * gate
        pending = p.astype(BF16)
        carry_ref[:, :, cols] = a.reshape(nseg, seg, fc)[:, seg - 2:seg, :]
    y = o_ref[...] + _dot(pending, wdn_ref[d_ff - fc:d_ff, :])
    nst_ref[...] = carry_ref[...]
    if final_norm:
        y = y * _rms_scale(y) * gf_ref[...]
    o_ref[...] = y


def conv_ffn(x, g, w_up, conv_w, conv_b, w_down, conv_state, seq_len, nseg, seg, g_final=None,
             fc=256):
    t, d = x.shape
    d_ff = w_down[0].shape[1]
    b = conv_state.shape[0]
    tiles_per_stream = max(seq_len // (nseg * seg), 1)
    n_outer = b // nseg
    tm = nseg * seg
    final_norm = g_final is not None
    in_specs = [pl.BlockSpec((tm, d), lambda o, i: (o * tiles_per_stream + i, 0)),
                _const_spec((1, d)),
                _weight_spec(w_up),
                _const_spec((3, d_ff)),
                _const_spec((1, d_ff)),
                _weight_spec(w_down),
                pl.BlockSpec((nseg, 2, d_ff), lambda o, i: (o, 0, 0))]
    args = [x, g.reshape(1, d), w_up[0], conv_w, conv_b.reshape(1, d_ff), w_down[0], conv_state]
    if final_norm:
        in_specs.append(_const_spec((1, d)))
        args.append(g_final.reshape(1, d))
    return pl.pallas_call(
        functools.partial(_ffn_kernel, nseg=nseg, seg=seg, fc=fc, final_norm=final_norm),
        grid=(n_outer, tiles_per_stream),
        in_specs=in_specs,
        out_specs=[pl.BlockSpec((tm, d), lambda o, i: (o * tiles_per_stream + i, 0)),
                   pl.BlockSpec((nseg, 2, d_ff), lambda o, i: (o, 0, 0))],
        out_shape=[jax.ShapeDtypeStruct((t, d), F32),
                   jax.ShapeDtypeStruct((b, 2, d_ff), F32)],
        scratch_shapes=[pltpu.VMEM((nseg, 2, d_ff), F32)],
        compiler_params=_cparams("parallel", "arbitrary"),
        name="conv_ffn",
    )(*args)


def _rotate(x, cos, sin):
    half = x.shape[1] // 2
    x1, x2 = x[:, :half], x[:, half:]
    return jnp.concatenate([x1 * cos - x2 * sin, x2 * cos + x1 * sin], axis=1)


def _ret_head(h, q, k, v, g, cos, sin, s_ref, chunk):
    log_g = RET_LOG_G[h]
    li = lax.broadcasted_iota(jnp.int32, (chunk, chunk), 0)
    mi = lax.broadcasted_iota(jnp.int32, (chunk, chunk), 1)
    diff = (li - mi).astype(F32)
    pos = lax.broadcasted_iota(jnp.int32, (chunk, 1), 0).astype(F32)
    qr = _rotate(q, cos, sin).astype(BF16)
    kr = _rotate(k, cos, sin) * (RET_DK ** -0.5)
    decay = jnp.where(diff >= 0, jnp.exp(jnp.maximum(diff, 0.0) * log_g), 0.0)
    scores = _dot_nt(qr, kr.astype(BF16)) * decay
    intra = _dot(scores.astype(BF16), v)
    s_old = s_ref[0, h]
    cross = _dot(qr, s_old.astype(BF16)) * jnp.exp((pos + 1.0) * log_g)
    o = intra + cross
    k_tail = (kr * jnp.exp((chunk - 1.0 - pos) * log_g)).astype(BF16)
    s_ref[0, h] = math.exp(chunk * log_g) * s_old + lax.dot_general(
        k_tail, v, (((0,), (0,)), ((), ())), preferred_element_type=F32)
    mu = jnp.mean(o, axis=-1, keepdims=True)
    oc = o - mu
    var = jnp.mean(oc * oc, axis=-1, keepdims=True)
    return jax.nn.silu(g) * (oc * lax.rsqrt(var + GN_EPS))


def _ret_kernel(p_ref, cos_ref, sin_ref, s_in_ref, y_ref, s_out_ref, *, chunk):
    @pl.when(pl.program_id(1) == 0)
    def _():
        s_out_ref[...] = s_in_ref[...]

    cos = cos_ref[...]
    sin = sin_ref[...]
    for h in range(RET_HEADS):
        q = p_ref[:, h * RET_DK:(h + 1) * RET_DK].astype(F32)
        k = p_ref[:, RET_QK + h * RET_DK:RET_QK + (h + 1) * RET_DK].astype(F32)
        v = p_ref[:, 2 * RET_QK + h * RET_DV:2 * RET_QK + (h + 1) * RET_DV]
        g = p_ref[:, 2 * RET_QK + RET_V + h * RET_DV:
                  2 * RET_QK + RET_V + (h + 1) * RET_DV].astype(F32)
        y = _ret_head(h, q, k, v, g, cos, sin, s_out_ref, chunk)
        y_ref[:, h * RET_DV:(h + 1) * RET_DV] = y.astype(y_ref.dtype)


def _ret_fused_kernel(x_ref, g_ref, w_ref, wo_ref, cos_ref, sin_ref, s_in_ref, o_ref, s_out_ref,
                      *, chunk):
    @pl.when(pl.program_id(1) == 0)
    def _():
        s_out_ref[...] = s_in_ref[...]

    cos = cos_ref[...]
    sin = sin_ref[...]
    x = x_ref[...]
    hn, row_scale = _prenorm(x, g_ref)
    o_ref[...] = x

    def proj(h):
        c = [h * RET_DK, RET_QK + h * RET_DK, 2 * RET_QK + h * RET_DV, 2 * RET_QK + RET_V + h * RET_DV]
        return (_dot(hn, w_ref[:, c[0]:c[0] + RET_DK]) * row_scale,
                _dot(hn, w_ref[:, c[1]:c[1] + RET_DK]) * row_scale,
                (_dot(hn, w_ref[:, c[2]:c[2] + RET_DV]) * row_scale).astype(BF16),
                _dot(hn, w_ref[:, c[3]:c[3] + RET_DV]) * row_scale)

    group = RET_HEAD_GROUP
    width = group * RET_DV
    nxt = [proj(h) for h in range(group)]
    pending = None
    for h0 in range(0, RET_HEADS, group):
        cur = nxt
        if h0 + group < RET_HEADS:
            nxt = [proj(h) for h in range(h0 + group, h0 + 2 * group)]
        if pending is not None:
            o_ref[...] += _dot(pending, wo_ref[(h0 - group) * RET_DV:h0 * RET_DV, :])
        ys = [_ret_head(h0 + i, *cur[i], cos, sin, s_out_ref, chunk).astype(BF16)
              for i in range(group)]
        pending = ys[0] if group == 1 else jnp.concatenate(ys, axis=1)
    o_ref[...] += _dot(pending, wo_ref[RET_V - width:, :])


def _ret_specs(seq_len, chunk):
    nchunks = seq_len // chunk
    half = RET_DK // 2
    row_spec = lambda n: pl.BlockSpec((chunk, n), lambda o, i: (o * nchunks + i, 0))
    rope_spec = pl.BlockSpec((chunk, half), lambda o, i: (i, 0))
    state_spec = pl.BlockSpec((1, RET_HEADS, RET_DK, RET_DV), lambda o, i: (o, 0, 0, 0))
    return nchunks, row_spec, rope_spec, state_spec


def ret_core(p, cos, sin, state, seq_len, chunk):
    t, n = p.shape
    nchunks, row_spec, rope_spec, state_spec = _ret_specs(seq_len, chunk)
    return pl.pallas_call(
        functools.partial(_ret_kernel, chunk=chunk),
        grid=(state.shape[0], nchunks),
        in_specs=[row_spec(n), rope_spec, rope_spec, state_spec],
        out_specs=[row_spec(RET_V), state_spec],
        out_shape=[jax.ShapeDtypeStruct((t, RET_V), BF16),
                   jax.ShapeDtypeStruct(state.shape, F32)],
        compiler_params=_cparams("parallel", "arbitrary"),
        name="ret_core",
    )(p, cos, sin, state)


def ret_mixer(x, g, w_qkvg, w_o, cos, sin, state, seq_len, chunk):
    t, d = x.shape
    nchunks, row_spec, rope_spec, state_spec = _ret_specs(seq_len, chunk)
    return pl.pallas_call(
        functools.partial(_ret_fused_kernel, chunk=chunk),
        grid=(state.shape[0], nchunks),
        in_specs=[row_spec(d), _const_spec((1, d)), _weight_spec(w_qkvg),
                  _weight_spec(w_o), rope_spec, rope_spec, state_spec],
        out_specs=[row_spec(d), state_spec],
        out_shape=[jax.ShapeDtypeStruct((t, d), F32), jax.ShapeDtypeStruct(state.shape, F32)],
        compiler_params=_cparams("parallel", "arbitrary"),
        name="ret_mixer",
    )(x, g.reshape(1, d), w_qkvg[0], w_o[0], cos, sin, state)


def _rope_tables(pos0, length):
    half = RET_DK // 2
    freqs = ROPE_BASE ** (-np.arange(half, dtype=np.float64) / half)
    ang = (pos0 + np.arange(length, dtype=np.float64))[:, None] * freqs[None, :]
    return jnp.asarray(np.cos(ang), F32), jnp.asarray(np.sin(ang), F32)


def _cmul(ar, ai, br, bi):
    return ar * br - ai * bi, ar * bi + ai * br


def _s5_prep_kernel(lam_re_ref, lam_im_ref, ldt_ref, b_re_ref, b_im_ref, c_re_ref, c_im_ref,
                    w_re_ref, w_im_ref, bt_ref, cct_ref, a16_ref):
    dt = jnp.exp(ldt_ref[...])
    lam_re, lam_im = lam_re_ref[...], lam_im_ref[...]
    mag = jnp.exp(lam_re * dt)
    ar, ai = mag * jnp.cos(lam_im * dt), mag * jnp.sin(lam_im * dt)
    den = lam_re * lam_re + lam_im * lam_im
    nr, ni = ar - 1.0, ai
    coef_re = (nr * lam_re + ni * lam_im) / den
    coef_im = (ni * lam_re - nr * lam_im) / den
    bb_re, bb_im = _cmul(coef_re, coef_im, b_re_ref[...], b_im_ref[...])
    c_re, c_im = c_re_ref[...], c_im_ref[...]
    lane = lax.broadcasted_iota(jnp.int32, (1, LANES), 1)
    low = lane < SSM_STATE
    rg = lax.broadcasted_iota(jnp.int32, (LANES, 1), 0) // SSM_GROUP
    pair = rg // 2
    own_half = (lane // SSM_STATE) == (rg % 2)
    same_group = rg == lane // SSM_GROUP
    c_hi, c_lo = _split_bf16(jnp.where(low, c_re, -c_im))
    tk = []
    pr, pi = jnp.ones_like(ar), jnp.zeros_like(ar)
    for j in range(SSM_FOLD):
        zr, zi = _cmul(pr, pi, bb_re, bb_im)
        w_re_ref[SSM_FOLD - 1 - j] = jnp.where(own_half, zr, 0.0)
        w_im_ref[SSM_FOLD - 1 - j] = jnp.where(own_half, zi, 0.0)
        b_hi, b_lo = _split_bf16(jnp.where(low, zr, zi))
        kk = _dot_nt(b_hi, c_hi) + _dot_nt(b_lo, c_hi) + _dot_nt(b_hi, c_lo)
        tk.append(jnp.where(same_group, kk, 0.0).astype(BF16))
        pr, pi = _cmul(pr, pi, ar, ai)
        zr, zi = _cmul(c_re, c_im, pr, pi)
        for ri, z in enumerate((zr, -zi)):
            z = jnp.where(own_half, z, 0.0)
            for t in range(GROUPS_PER_TILE // 2):
                c0 = ri * STATE_PER_TILE + t * LANES
                cct_ref[0, j * LANES:(j + 1) * LANES, c0:c0 + LANES] = (
                    jnp.where(pair == t, z, 0.0).astype(BF16))
    a16_ref[0] = pr
    a16_ref[1] = pi
    zero = jnp.zeros((LANES, LANES), BF16)
    for d2 in range(SSM_FOLD // 2):
        top = jnp.concatenate([tk[2 * d2], tk[2 * d2 + 1]], axis=1)
        bot = jnp.concatenate([tk[2 * d2 - 1] if d2 else zero, tk[2 * d2]], axis=1)
        bt_ref[0, d2] = jnp.concatenate([top, bot], axis=0)


def s5_prepare(a_re, a_im, log_dt, b_re, b_im, c_re, c_im):
    g, p, c = SSM_GROUPS, SSM_STATE, SSM_GROUP
    fold, nt = SSM_FOLD, N_LANE_TILES
    rows = g * c
    twice = lambda x: jnp.tile(x, (1, 2))
    lam = lambda x: twice(jnp.repeat(x, c, axis=0))
    in_spec = pl.BlockSpec((LANES, LANES), lambda t: (t, 0))
    w_spec = pl.BlockSpec((fold, LANES, LANES), lambda t: (0, t, 0))
    w_shape = jax.ShapeDtypeStruct((fold, rows, LANES), F32)
    w_re, w_im, bt, cct, a16 = pl.pallas_call(
        _s5_prep_kernel,
        grid=(nt,),
        in_specs=[in_spec, in_spec, pl.BlockSpec((LANES, 1), lambda t: (t, 0))] + [in_spec] * 4,
        out_specs=[w_spec, w_spec,
                   pl.BlockSpec((1, fold // 2, 2 * LANES, 2 * LANES), lambda t: (t, 0, 0, 0)),
                   pl.BlockSpec((1, fold * LANES, 2 * STATE_PER_TILE), lambda t: (t, 0, 0)),
                   pl.BlockSpec((2, LANES, LANES), lambda t: (0, t, 0))],
        out_shape=[w_shape, w_shape,
                   jax.ShapeDtypeStruct((nt, fold // 2, 2 * LANES, 2 * LANES), BF16),
                   jax.ShapeDtypeStruct((nt, fold * LANES, 2 * STATE_PER_TILE), BF16),
                   jax.ShapeDtypeStruct((2, rows, LANES), F32)],
        compiler_params=_cparams("parallel"),
        name="s5_prep",
    )(lam(a_re), lam(a_im), jnp.repeat(log_dt, c).reshape(rows, 1),
      twice(b_re.transpose(0, 2, 1).reshape(rows, p)), twice(b_im.transpose(0, 2, 1).reshape(rows, p)),
      twice(c_re.reshape(rows, p)), twice(c_im.reshape(rows, p)))
    a16 = a16[:, ::c, :p].reshape(2, 1, N_STATE)
    return bt, cct, w_re, w_im, a16


FOLD_BLOCK = SSM_FOLD * SSM_FOLD


def _fold_perm():
    idx = np.arange(FOLD_BLOCK)
    perm = np.zeros((FOLD_BLOCK, FOLD_BLOCK), np.float32)
    perm[(idx % SSM_FOLD) * SSM_FOLD + idx // SSM_FOLD, idx] = 1.0
    return jnp.asarray(perm, BF16)


def _s5_fold_kernel(x_ref, g_ref, perm_ref, u_ref):
    for blk in range(x_ref.shape[0] // FOLD_BLOCK):
        x = x_ref[blk * FOLD_BLOCK:(blk + 1) * FOLD_BLOCK, :]
        h = (x * _rms_scale(x) * g_ref[...]).astype(BF16)
        f = _dot(perm_ref[...], h).astype(BF16)
        for m in range(SSM_FOLD):
            for k in range(N_LANE_TILES):
                u_ref[m, k, blk * SSM_FOLD:(blk + 1) * SSM_FOLD, :] = (
                    f[m * SSM_FOLD:(m + 1) * SSM_FOLD, k * LANES:(k + 1) * LANES])


def s5_fold(x, g, perm, tokens):
    t, d = x.shape
    rows = t // SSM_FOLD
    return pl.pallas_call(
        _s5_fold_kernel,
        grid=(t // tokens,),
        in_specs=[pl.BlockSpec((tokens, d), lambda i: (i, 0)), _const_spec((1, d)),
                  _const_spec((FOLD_BLOCK, FOLD_BLOCK))],
        out_specs=pl.BlockSpec((SSM_FOLD, N_LANE_TILES, tokens // SSM_FOLD, LANES),
                               lambda i: (0, 0, i, 0)),
        out_shape=jax.ShapeDtypeStruct((SSM_FOLD, N_LANE_TILES, rows, LANES), BF16),
        compiler_params=_cparams("parallel"),
        name="s5_fold",
    )(x, g.reshape(1, d), perm)


def _s5_state_in_kernel(u_ref, w_re_ref, w_im_ref, vre_ref, vim_ref, wd_ref):
    @pl.when(pl.program_id(1) == 0)
    def _():
        pair = lax.broadcasted_iota(jnp.int32, (LANES, 1), 0) // SSM_GROUP // 2
        for m in range(SSM_FOLD):
            for ri, ref in enumerate((w_re_ref, w_im_ref)):
                w = ref[m]
                for t in range(GROUPS_PER_TILE // 2):
                    c0 = ri * STATE_PER_TILE + t * LANES
                    wd_ref[m * LANES:(m + 1) * LANES, c0:c0 + LANES] = (
                        jnp.where(pair == t, w, 0.0).astype(BF16))

    lhs = jnp.concatenate([u_ref[m, 0] for m in range(SSM_FOLD)], axis=1)
    v = _dot(lhs, wd_ref[...])
    vre_ref[...] = v[:, :STATE_PER_TILE]
    vim_ref[...] = v[:, STATE_PER_TILE:]


def _s5_tile_specs(rt):
    u_spec = pl.BlockSpec((SSM_FOLD, 1, rt, LANES), lambda t, i: (0, t, i, 0))
    s_spec = pl.BlockSpec((rt, STATE_PER_TILE), lambda t, i: (i, t))
    return u_spec, s_spec


def s5_state_in(u, w_re, w_im, rt):
    rows = u.shape[2]
    u_spec, s_spec = _s5_tile_specs(rt)
    out = jax.ShapeDtypeStruct((rows, N_STATE), F32)
    w_spec = pl.BlockSpec((SSM_FOLD, LANES, LANES), lambda t, i: (0, t, 0))
    return pl.pallas_call(
        _s5_state_in_kernel,
        grid=(N_LANE_TILES, rows // rt),
        in_specs=[u_spec, w_spec, w_spec],
        out_specs=[s_spec, s_spec],
        out_shape=[out, out],
        scratch_shapes=[pltpu.VMEM((SSM_FOLD * LANES, 2 * STATE_PER_TILE), BF16)],
        compiler_params=_cparams("parallel", "arbitrary"),
        name="s5_state_in",
    )(u, w_re, w_im)


def _s5_scan_pairs_kernel(vre_ref, vim_ref, a_ref, s0re_ref, s0im_ref,
                          sre_ref, sim_ref, fre_ref, fim_ref):
    ar, ai = a_ref[0], a_ref[1]
    s0r, s0i = s0re_ref[...], s0im_ref[...]
    vr, vi = vre_ref[...], vim_ref[...]
    first = lax.broadcasted_iota(jnp.int32, (s0r.shape[0], 1), 0) % 2 == 0
    tr, ti = ar * s0r - ai * s0i + vr, ar * s0i + ai * s0r + vi
    sr = jnp.where(first, s0r, pltpu.roll(tr, 1, 0))
    si = jnp.where(first, s0i, pltpu.roll(ti, 1, 0))
    sre_ref[...] = sr
    sim_ref[...] = si
    fre_ref[...] = ar * sr - ai * si + vr
    fim_ref[...] = ar * si + ai * sr + vi


def _s5_scan_stream_kernel(vre_ref, vim_ref, a_ref, s0re_ref, s0im_ref,
                           sre_ref, sim_ref, fre_ref, fim_ref, *, steps):
    q = lax.broadcasted_iota(jnp.int32, (SUBLANES, 1), 0)
    a1 = (a_ref[0], a_ref[1])
    a2 = _cmul(*a1, *a1)
    a4 = _cmul(*a2, *a2)
    a8 = _cmul(*a4, *a4)
    strides = ((1, a1), (2, a2), (4, a4))
    width = a1[0].shape[1]
    pw = (jnp.ones((SUBLANES, width), F32), jnp.zeros((SUBLANES, width), F32))
    for k, ak in strides:
        nxt = _cmul(*pw, *ak)
        bit = (q & k) != 0
        pw = (jnp.where(bit, nxt[0], pw[0]), jnp.where(bit, nxt[1], pw[1]))

    def shifted(x, k):
        return jnp.where(q >= k, pltpu.roll(x, k, 0), 0.0)

    def body(t, carry):
        cr, ci = carry
        rows = pl.ds(pl.multiple_of(t * SUBLANES, SUBLANES), SUBLANES)
        pr, pi = vre_ref[rows, :], vim_ref[rows, :]
        for k, ak in strides:
            dr, di = _cmul(shifted(pr, k), shifted(pi, k), *ak)
            pr, pi = pr + dr, pi + di
        xr, xi = _cmul(*pw, cr, ci)
        sre_ref[rows, :] = xr + shifted(pr, 1)
        sim_ref[rows, :] = xi + shifted(pi, 1)
        nr, ni = _cmul(*a8, cr, ci)
        return nr + pr[SUBLANES - 1:], ni + pi[SUBLANES - 1:]

    cr, ci = lax.fori_loop(0, steps // SUBLANES, body, (s0re_ref[0], s0im_ref[0]))
    fre_ref[0] = cr
    fim_ref[0] = ci


def s5_scan(v_re, v_im, a16, s0_re, s0_im, width):
    rows, n = v_re.shape
    streams = s0_re.shape[0]
    steps = rows // streams
    a_spec = pl.BlockSpec((2, 1, width), lambda *i: (0, 0, i[-1]))
    vout = jax.ShapeDtypeStruct((rows, n), F32)
    long_streams = steps % SUBLANES == 0
    if long_streams:
        kern = functools.partial(_s5_scan_stream_kernel, steps=steps)
        grid = (streams, n // width)
        vspec = pl.BlockSpec((steps, width), lambda b, c: (b, c))
        sspec = pl.BlockSpec((1, 1, width), lambda b, c: (b, 0, c))
        s0_re, s0_im = s0_re.reshape(streams, 1, n), s0_im.reshape(streams, 1, n)
        sout = jax.ShapeDtypeStruct((streams, 1, n), F32)
    else:
        assert steps == 2 and rows % SUBLANES == 0
        kern = _s5_scan_pairs_kernel
        grid = (n // width,)
        vspec = pl.BlockSpec((rows, width), lambda c: (0, c))
        sspec = vspec
        s0_re, s0_im = jnp.repeat(s0_re, steps, axis=0), jnp.repeat(s0_im, steps, axis=0)
        sout = vout
    s_re, s_im, f_re, f_im = pl.pallas_call(
        kern,
        grid=grid,
        in_specs=[vspec, vspec, a_spec, sspec, sspec],
        out_specs=[vspec, vspec, sspec, sspec],
        out_shape=[vout, vout, sout, sout],
        compiler_params=_cparams(*(["parallel"] * len(grid))),
        name="s5_scan",
    )(v_re, v_im, a16, s0_re, s0_im)
    if long_streams:
        return s_re, s_im, f_re.reshape(streams, n), f_im.reshape(streams, n)
    return s_re, s_im, f_re[steps - 1::steps], f_im[steps - 1::steps]


def _s5_toeplitz_kernel(u_ref, sre_ref, sim_ref, bt_ref, cct_ref, y_ref):
    npair = SSM_FOLD // 2
    u2 = [jnp.concatenate([u_ref[2 * mm, 0], u_ref[2 * mm + 1, 0]], axis=1) for mm in range(npair)]
    s_prev = jnp.concatenate([sre_ref[...], sim_ref[...]], axis=1).astype(BF16)
    for ll in range(npair):
        acc = _dot_nt(s_prev, cct_ref[0, ll * 2 * LANES:(ll + 1) * 2 * LANES, :])
        for mm in range(ll + 1):
            acc = acc + _dot(u2[mm], bt_ref[0, ll - mm])
        y_ref[2 * ll, 0] = acc[:, :LANES]
        y_ref[2 * ll + 1, 0] = acc[:, LANES:]


def s5_toeplitz(u, s_re, s_im, bt, cct, rt):
    rows = u.shape[2]
    u_spec, s_spec = _s5_tile_specs(rt)
    return pl.pallas_call(
        _s5_toeplitz_kernel,
        grid=(N_LANE_TILES, rows // rt),
        in_specs=[u_spec, s_spec, s_spec,
                  pl.BlockSpec((1, SSM_FOLD // 2, 2 * LANES, 2 * LANES), lambda t, i: (t, 0, 0, 0)),
                  pl.BlockSpec((1, SSM_FOLD * LANES, 2 * STATE_PER_TILE), lambda t, i: (t, 0, 0))],
        out_specs=u_spec,
        out_shape=jax.ShapeDtypeStruct(u.shape, F32),
        compiler_params=_cparams("parallel", "parallel"),
        name="s5_toeplitz",
    )(u, s_re, s_im, bt, cct)


def _s5_glu_kernel(x_ref, u_ref, y_ref, d_ref, wglu_ref, bglu_ref, perm_ref, o_ref):
    nblk = x_ref.shape[0] // FOLD_BLOCK

    def gelu_block(blk):
        rows = slice(blk * SSM_FOLD, (blk + 1) * SSM_FOLD)
        gl = []
        for m in range(SSM_FOLD):
            ym = jnp.concatenate([y_ref[m, k, rows, :] for k in range(N_LANE_TILES)], axis=1)
            hm = jnp.concatenate([u_ref[m, k, rows, :] for k in range(N_LANE_TILES)], axis=1)
            gl.append(jax.nn.gelu(ym + d_ref[...] * hm.astype(F32)))
        gl = jnp.concatenate(gl, axis=0)
        return gl, _dot(gl.astype(BF16), wglu_ref[...])

    def finish(blk, gl, z):
        tok = slice(blk * FOLD_BLOCK, (blk + 1) * FOLD_BLOCK)
        hi, lo = _split_bf16(gl * jax.nn.sigmoid(z + bglu_ref[...]))
        o_ref[tok, :] = x_ref[tok, :] + (_dot(perm_ref[...], hi) + _dot(perm_ref[...], lo))

    prev = gelu_block(0)
    for blk in range(1, nblk):
        cur = gelu_block(blk)
        finish(blk - 1, *prev)
        prev = cur
    finish(nblk - 1, *prev)


def s5_glu(x, u, y, d, w_glu, b_glu, perm, tokens):
    t, dm = x.shape
    fold_spec = pl.BlockSpec((SSM_FOLD, N_LANE_TILES, tokens // SSM_FOLD, LANES),
                             lambda i: (0, 0, i, 0))
    x_spec = pl.BlockSpec((tokens, dm), lambda i: (i, 0))
    return pl.pallas_call(
        _s5_glu_kernel,
        grid=(t // tokens,),
        in_specs=[x_spec, fold_spec, fold_spec, _const_spec((1, dm)), _weight_spec(w_glu),
                  _const_spec((1, dm)), _const_spec((FOLD_BLOCK, FOLD_BLOCK))],
        out_specs=x_spec,
        out_shape=jax.ShapeDtypeStruct((t, dm), F32),
        compiler_params=_cparams("parallel"),
        name="s5_glu",
    )(x, u, y, d.reshape(1, dm), w_glu[0], b_glu.reshape(1, dm), perm)


def s5_mixer(x, st_re, st_im, g, ops, d, w_glu, b_glu):
    bt, cct, w_re, w_im, a16 = ops
    b, l, dm = x.shape
    rows = b * l // SSM_FOLD
    x2 = x.reshape(b * l, dm)
    perm = _fold_perm()
    tokens = min(S5_TOKEN_TILE, b * l)
    rt = min(S5_ROW_TILE, rows)
    u = s5_fold(x2, g, perm, tokens)
    v_re, v_im = s5_state_in(u, w_re, w_im, rt)
    s_re, s_im, f_re, f_im = s5_scan(v_re, v_im, a16, st_re.reshape(b, N_STATE),
                                     st_im.reshape(b, N_STATE), S5_SCAN_WIDTH)
    y = s5_toeplitz(u, s_re, s_im, bt, cct, rt)
    out = s5_glu(x2, u, y, d, w_glu, b_glu, perm, tokens)
    shape = (b, SSM_GROUPS, SSM_STATE)
    return out.reshape(b, l, dm), f_re.reshape(shape), f_im.reshape(shape)


def _stream_tiling(seq_len, n_streams, tile_rows):
    if seq_len >= tile_rows:
        return 1, tile_rows
    return min(n_streams, 256 // seq_len), seq_len


def kernel(x_prompt, x_sample, mem_prompt, state_ssm_re, state_ssm_im, state_ret, cache_mem_k, cache_mem_v, cache_conv, norm_mix, norm_mem_q, norm_mem_kv, norm_ffn, norm_final, ssm_a_re, ssm_a_im, ssm_log_dt, ssm_b_re, ssm_b_im, ssm_c_re, ssm_c_im, ssm_d, ssm_w_glu, ssm_b_glu, ret_w_qkvg, ret_w_o, mem_w_q, mem_w_kv, mem_w_o, ffn_w_up, ffn_conv_w, ffn_conv_b, ffn_w_down):
    bp, lp, dm = x_prompt.shape
    bs, ls, _ = x_sample.shape
    depth = norm_mix.shape[0]
    n_mem = mem_prompt.shape[1]
    d_ff = ffn_w_down.shape[1]
    streams = [dict(x=x, b=x.shape[0], l=x.shape[1],
                    attn_tiling=_stream_tiling(x.shape[1], x.shape[0], ATTN_TILE_ROWS),
                    ffn_tiling=_stream_tiling(x.shape[1], x.shape[0], FFN_TILE_ROWS))
               for x in (x_prompt, x_sample)]
    cache_k, cache_v = _kv_rows(cache_mem_k), _kv_rows(cache_mem_v)
    ssm_w_glu, ret_w_qkvg, ret_w_o, mem_w_q, mem_w_kv, mem_w_o, ffn_w_up, ffn_w_down = (
        w.astype(BF16) for w in (ssm_w_glu, ret_w_qkvg, ret_w_o, mem_w_q, mem_w_kv, mem_w_o,
                                 ffn_w_up, ffn_w_down))
    mem_k, mem_v = mem_kv(mem_prompt, norm_mem_kv, mem_w_kv)
    outs = dict(ssm_re=([], []), ssm_im=([], []), ret=([], []), conv=([], []))

    for i in range(depth):
        j = i // 2
        if i % 2 == 0:
            ops = s5_prepare(ssm_a_re[j], ssm_a_im[j], ssm_log_dt[j], ssm_b_re[j], ssm_b_im[j],
                             ssm_c_re[j], ssm_c_im[j])
            w_glu = (ssm_w_glu, j)
            zeros = jnp.zeros((bp, SSM_GROUPS, SSM_STATE), F32)
            states = [(zeros, zeros), (state_ssm_re[j], state_ssm_im[j])]
            for si, (s, (st_re, st_im)) in enumerate(zip(streams, states)):
                s["x"], f_re, f_im = s5_mixer(s["x"], st_re, st_im, norm_mix[i], ops, ssm_d[j],
                                              w_glu, ssm_b_glu[j])
                outs["ssm_re"][si].append(f_re)
                outs["ssm_im"][si].append(f_im)
        else:
            w_qkvg = (ret_w_qkvg, j)
            w_o = (ret_w_o, j)
            zeros = jnp.zeros((bp, RET_HEADS, RET_DK, RET_DV), F32)
            chunk = RET_CHUNK_ROWS
            for si, (s, st, pos0) in enumerate(zip(streams, [zeros, state_ret[j]], [0, PAST_LEN])):
                x2 = s["x"].reshape(s["b"] * s["l"], dm)
                cos, sin = _rope_tables(pos0, s["l"])
                if s["l"] >= chunk:
                    y, st_new = ret_mixer(x2, norm_mix[i], w_qkvg, w_o, cos, sin, st, s["l"], chunk)
                else:
                    p = rms_matmul(x2, norm_mix[i], w_qkvg, BF16)
                    y, st_new = ret_core(p, cos, sin, st, s["l"], s["l"])
                    y = matmul_res(y, w_o, x2)
                s["x"] = y.reshape(s["x"].shape)
                outs["ret"][si].append(st_new)

        w_q = (mem_w_q, i)
        w_o = (mem_w_o, i)
        for s, (k_, v_) in zip(streams, [(mem_k, mem_v), (cache_k, cache_v)]):
            x2 = s["x"].reshape(s["b"] * s["l"], dm)
            y = mem_attn(x2, norm_mem_q[i], w_q, k_, v_, i, w_o, s["l"], *s["attn_tiling"])
            s["x"] = y.reshape(s["x"].shape)

        w_up = (ffn_w_up, i)
        w_dn = (ffn_w_down, i)
        g_final = norm_final if i == depth - 1 else None
        conv_states = [jnp.zeros((bp, 2, d_ff), F32), cache_conv[i]]
        for si, (s, cst) in enumerate(zip(streams, conv_states)):
            x2 = s["x"].reshape(s["b"] * s["l"], dm)
            y, cnew = conv_ffn(x2, norm_ffn[i], w_up, ffn_conv_w[i], ffn_conv_b[i], w_dn, cst,
                               s["l"], *s["ffn_tiling"], g_final=g_final)
            s["x"] = y.reshape(s["x"].shape)
            outs["conv"][si].append(cnew)

    return (streams[0]["x"], streams[1]["x"],
            jnp.stack(outs["ssm_re"][0]), jnp.stack(outs["ssm_im"][0]),
            jnp.stack(outs["ssm_re"][1]), jnp.stack(outs["ssm_im"][1]),
            jnp.stack(outs["ret"][0]), jnp.stack(outs["ret"][1]),
            _kv_unrows(mem_k, MEM_HEADS), _kv_unrows(mem_v, MEM_HEADS),
            jnp.stack(outs["conv"][0]), jnp.stack(outs["conv"][1]))
```

```python
import functools
import math

import numpy as np
import jax
import jax.numpy as jnp
from jax import lax
from jax.experimental import pallas as pl
from jax.experimental.pallas import tpu as pltpu

F32 = jnp.float32
BF16 = jnp.bfloat16

D_MODEL = 1024
PAST_LEN = 1024
EPS = 1e-6
GN_EPS = 1e-5
ROPE_BASE = 10000.0
LANES = 128
SUBLANES = 8
SSM_GROUP = 16
SSM_GROUPS = D_MODEL // SSM_GROUP
SSM_STATE = 64
SSM_FOLD = 16
GROUPS_PER_TILE = LANES // SSM_GROUP
N_LANE_TILES = D_MODEL // LANES
STATE_PER_TILE = GROUPS_PER_TILE * SSM_STATE
N_STATE = SSM_GROUPS * SSM_STATE
RET_HEADS = 4
RET_DK = D_MODEL // RET_HEADS
RET_DV = 2 * D_MODEL // RET_HEADS
RET_QK = RET_HEADS * RET_DK
RET_V = RET_HEADS * RET_DV
RET_HEAD_GROUP = 2
RET_LOG_G = tuple(math.log(1.0 - 2.0 ** (-5.0 - h)) for h in range(RET_HEADS))
MEM_HEADS = 4
MEM_HD = D_MODEL // MEM_HEADS
VMEM_LIMIT = 56 * 1024 * 1024
ATTN_TILE_ROWS = 512
FFN_TILE_ROWS = 256
RET_CHUNK_ROWS = 256
S5_TOKEN_TILE = 1024
S5_ROW_TILE = 512
S5_SCAN_WIDTH = 1024


def _cparams(*sem):
    return pltpu.CompilerParams(dimension_semantics=sem, vmem_limit_bytes=VMEM_LIMIT)


def _const_spec(shape):
    nd = len(shape)
    return pl.BlockSpec(shape, lambda *_: (0,) * nd, pipeline_mode=pl.Buffered(1))


def _weight_spec(w):
    stack, layer = w
    return pl.BlockSpec((None,) + stack.shape[1:], lambda *_: (layer, 0, 0),
                        pipeline_mode=pl.Buffered(1))


def _rms_scale(x):
    return lax.rsqrt(jnp.mean(x * x, axis=-1, keepdims=True) + EPS)


def _prenorm(x, g_ref):
    return (x * g_ref[...]).astype(BF16), _rms_scale(x)


def _split_bf16(x):
    hi = x.astype(BF16)
    lo = (x - hi.astype(F32)).astype(BF16)
    return hi, lo


def _dot(a, b):
    return jnp.dot(a, b, preferred_element_type=F32)


def _dot_nt(a, b):
    return lax.dot_general(a, b, (((1,), (1,)), ((), ())), preferred_element_type=F32)


def _rms_matmul_kernel(x_ref, g_ref, w_ref, o_ref, *, tn):
    x = x_ref[...]
    h = (x * _rms_scale(x) * g_ref[...]).astype(BF16)
    for n0 in range(0, w_ref.shape[1], tn):
        o_ref[:, n0:n0 + tn] = _dot(h, w_ref[:, n0:n0 + tn]).astype(o_ref.dtype)


def rms_matmul(x, g, w, out_dtype, tm=512, tn=512):
    t, d = x.shape
    n = w[0].shape[2]
    tm = min(tm, t)
    return pl.pallas_call(
        functools.partial(_rms_matmul_kernel, tn=tn),
        grid=(t // tm,),
        in_specs=[pl.BlockSpec((tm, d), lambda i: (i, 0)),
                  _const_spec((1, d)),
                  _weight_spec(w)],
        out_specs=pl.BlockSpec((tm, n), lambda i: (i, 0)),
        out_shape=jax.ShapeDtypeStruct((t, n), out_dtype),
        compiler_params=_cparams("parallel"),
        name="rms_matmul",
    )(x, g.reshape(1, d), w[0])


def _matmul_res_kernel(y_ref, w_ref, x_ref, o_ref):
    o_ref[...] = x_ref[...] + _dot(y_ref[...], w_ref[...])


def matmul_res(y, w, x, tm=512):
    t, k = y.shape
    d = w[0].shape[2]
    tm = min(tm, t)
    return pl.pallas_call(
        _matmul_res_kernel,
        grid=(t // tm,),
        in_specs=[pl.BlockSpec((tm, k), lambda i: (i, 0)),
                  _weight_spec(w),
                  pl.BlockSpec((tm, d), lambda i: (i, 0))],
        out_specs=pl.BlockSpec((tm, d), lambda i: (i, 0)),
        out_shape=jax.ShapeDtypeStruct((t, d), F32),
        compiler_params=_cparams("parallel"),
        name="matmul_res",
    )(y, w[0], x)


MEM_LANE_TILES = MEM_HD // LANES


def _kv_rows(x):
    *lead, n, h, hd = x.shape
    x = x.reshape(*lead, n, h, MEM_LANE_TILES, LANES)
    return jnp.swapaxes(x, -3, -2).reshape(*lead, n * MEM_LANE_TILES * h, LANES)


def _kv_unrows(x, heads):
    *lead, rows, _ = x.shape
    n = rows // (MEM_LANE_TILES * heads)
    x = x.reshape(*lead, n, MEM_LANE_TILES, heads, LANES)
    return jnp.swapaxes(x, -3, -2).reshape(*lead, n, heads, MEM_LANE_TILES * LANES)


def _mem_kv_kernel(x_ref, g_ref, w_ref, k_ref, v_ref):
    x = x_ref[...]
    h = (x * _rms_scale(x) * g_ref[...]).astype(BF16)
    n_mem, d = x.shape
    stride = MEM_LANE_TILES * MEM_HEADS
    for which, ref in enumerate((k_ref, v_ref)):
        for hd in range(MEM_HEADS):
            c0 = which * d + hd * MEM_HD
            res = _dot(h, w_ref[:, c0:c0 + MEM_HD])
            for j in range(MEM_LANE_TILES):
                ref[pl.ds(j * MEM_HEADS + hd, n_mem, stride=stride), :] = (
                    res[:, j * LANES:(j + 1) * LANES])


def mem_kv(mem, g, w_kv):
    b, n_mem, d = mem.shape
    depth = w_kv.shape[0]
    rows = n_mem * MEM_LANE_TILES * MEM_HEADS
    out_spec = pl.BlockSpec((None, None, rows, LANES), lambda l, i: (l, i, 0, 0))
    out = jax.ShapeDtypeStruct((depth, b, rows, LANES), F32)
    return pl.pallas_call(
        _mem_kv_kernel,
        grid=(depth, b),
        in_specs=[pl.BlockSpec((None, n_mem, d), lambda l, i: (i, 0, 0)),
                  pl.BlockSpec((None, 1, d), lambda l, i: (l, 0, 0)),
                  pl.BlockSpec((None,) + w_kv.shape[1:], lambda l, i: (l, 0, 0))],
        out_specs=[out_spec, out_spec],
        out_shape=[out, out],
        compiler_params=_cparams("parallel", "parallel"),
        name="mem_kv",
    )(mem, g.reshape(depth, 1, d), w_kv)


def _head_rows(kv_ref, s, hd):
    stride = MEM_LANE_TILES * MEM_HEADS
    n_mem = kv_ref.shape[2] // stride
    return jnp.concatenate(
        [kv_ref[0, s, pl.ds(j * MEM_HEADS + hd, n_mem, stride=stride), :]
         for j in range(MEM_LANE_TILES)], axis=1).astype(BF16)


def _mem_attn_kernel(x_ref, g_ref, wq_ref, k_ref, v_ref, wo_ref, o_ref, *, nseg, seg, head_group):
    scale = MEM_HD ** -0.5
    x = x_ref[...]
    h, row_scale = _prenorm(x, g_ref)
    o_ref[...] = x

    width = head_group * MEM_HD

    def q_proj(grp):
        return (_dot(h, wq_ref[:, grp * width:(grp + 1) * width]) * row_scale).astype(BF16)

    nxt = q_proj(0)
    pending = None
    for grp in range(MEM_HEADS // head_group):
        q = nxt
        pairs = [(grp * head_group + hg, hg, s) for hg in range(head_group) for s in range(nseg)]
        scores = [_dot_nt(q[s * seg:(s + 1) * seg, hg * MEM_HD:(hg + 1) * MEM_HD],
                          _head_rows(k_ref, s, hd)) for hd, hg, s in pairs]
        if (grp + 1) * head_group < MEM_HEADS:
            nxt = q_proj(grp + 1)
        if pending is not None:
            o_ref[...] += _dot(pending, wo_ref[(grp - 1) * width:grp * width, :])
        outs = []
        for (hd, hg, s), sc in zip(pairs, scores):
            sc = sc * scale
            p = jnp.exp(sc - jnp.max(sc, axis=-1, keepdims=True))
            denom = jnp.sum(p, axis=-1, keepdims=True)
            outs.append((_dot(p.astype(BF16), _head_rows(v_ref, s, hd)) / denom).astype(BF16))
        heads = [outs[hg * nseg] if nseg == 1 else jnp.concatenate(outs[hg * nseg:(hg + 1) * nseg], axis=0)
                 for hg in range(head_group)]
        pending = heads[0] if head_group == 1 else jnp.concatenate(heads, axis=1)
    o_ref[...] += _dot(pending, wo_ref[MEM_HEADS * MEM_HD - width:, :])


def mem_attn(x, g, w_q, mk, mv, layer, w_o, seq_len, nseg, seg):
    t, d = x.shape
    b = mk.shape[1]
    tiles_per_stream = max(seq_len // (nseg * seg), 1)
    n_outer = b // nseg
    tm = nseg * seg
    x_spec = pl.BlockSpec((tm, d), lambda o, i: (o * tiles_per_stream + i, 0))
    kv_spec = pl.BlockSpec((1, nseg) + mk.shape[2:], lambda o, i: (layer, o, 0, 0))
    return pl.pallas_call(
        functools.partial(_mem_attn_kernel, nseg=nseg, seg=seg, head_group=2 if nseg == 1 else 1),
        grid=(n_outer, tiles_per_stream),
        in_specs=[x_spec, _const_spec((1, d)), _weight_spec(w_q), kv_spec, kv_spec,
                  _weight_spec(w_o)],
        out_specs=x_spec,
        out_shape=jax.ShapeDtypeStruct((t, d), F32),
        compiler_params=_cparams("parallel", "parallel"),
        name="mem_attn",
    )(x, g.reshape(1, d), w_q[0], mk, mv, w_o[0])


def _ffn_kernel(x_ref, g_ref, wup_ref, cw_ref, cb_ref, wdn_ref, st_ref, *rest,
                nseg, seg, fc, final_norm):
    if final_norm:
        gf_ref, o_ref, nst_ref, carry_ref = rest
    else:
        o_ref, nst_ref, carry_ref = rest
    d_ff = wdn_ref.shape[0]
    tm = nseg * seg

    @pl.when(pl.program_id(1) == 0)
    def _():
        carry_ref[...] = st_ref[...]

    x = x_ref[...]
    h, scale = _prenorm(x, g_ref)
    row = lax.broadcasted_iota(jnp.int32, (tm, 1), 0) % seg
    o_ref[...] = x

    def up_proj(f0):
        return (_dot(h, wup_ref[:, f0:f0 + fc]) * scale,
                _dot(h, wup_ref[:, d_ff + f0:d_ff + f0 + fc]) * scale)

    nxt = up_proj(0)
    pending = None
    for f0 in range(0, d_ff, fc):
        cols = slice(f0, f0 + fc)
        a, gate = nxt
        if f0 + fc < d_ff:
            nxt = up_proj(f0 + fc)
        if pending is not None:
            o_ref[...] += _dot(pending, wdn_ref[f0 - fc:f0, :])
        p0 = jnp.broadcast_to(carry_ref[:, 0:1, cols], (nseg, seg, fc)).reshape(tm, fc)
        p1 = jnp.broadcast_to(carry_ref[:, 1:2, cols], (nseg, seg, fc)).reshape(tm, fc)
        a1 = jnp.where(row == 0, p1, pltpu.roll(a, 1, 0))
        a2 = jnp.where(row == 0, p0, jnp.where(row == 1, p1, pltpu.roll(a, 2, 0)))
        conv = (cw_ref[0:1, cols] * a2 + cw_ref[1:2, cols] * a1 + cw_ref[2:3, cols] * a
                + cb_ref[:, cols])
        p = jax.nn.gelu(conv) * gate
        pending = p.astype(BF16)
        carry_ref[:, :, cols] = a.reshape(nseg, seg, fc)[:, seg - 2:seg, :]
    y = o_ref[...] + _dot(pending, wdn_ref[d_ff - fc:d_ff, :])
    nst_ref[...] = carry_ref[...]
    if final_norm:
        y = y * _rms_scale(y) * gf_ref[...]
    o_ref[...] = y


def conv_ffn(x, g, w_up, conv_w, conv_b, w_down, conv_state, seq_len, nseg, seg, g_final=None,
             fc=256):
    t, d = x.shape
    d_ff = w_down[0].shape[1]
    b = conv_state.shape[0]
    tiles_per_stream = max(seq_len // (nseg * seg), 1)
    n_outer = b // nseg
    tm = nseg * seg
    final_norm = g_final is not None
    in_specs = [pl.BlockSpec((tm, d), lambda o, i: (o * tiles_per_stream + i, 0)),
                _const_spec((1, d)),
                _weight_spec(w_up),
                _const_spec((3, d_ff)),
                _const_spec((1, d_ff)),
                _weight_spec(w_down),
                pl.BlockSpec((nseg, 2, d_ff), lambda o, i: (o, 0, 0))]
    args = [x, g.reshape(1, d), w_up[0], conv_w, conv_b.reshape(1, d_ff), w_down[0], conv_state]
    if final_norm:
        in_specs.append(_const_spec((1, d)))
        args.append(g_final.reshape(1, d))
    return pl.pallas_call(
        functools.partial(_ffn_kernel, nseg=nseg, seg=seg, fc=fc, final_norm=final_norm),
        grid=(n_outer, tiles_per_stream),
        in_specs=in_specs,
        out_specs=[pl.BlockSpec((tm, d), lambda o, i: (o * tiles_per_stream + i, 0)),
                   pl.BlockSpec((nseg, 2, d_ff), lambda o, i: (o, 0, 0))],
        out_shape=[jax.ShapeDtypeStruct((t, d), F32),
                   jax.ShapeDtypeStruct((b, 2, d_ff), F32)],
        scratch_shapes=[pltpu.VMEM((nseg, 2, d_ff), F32)],
        compiler_params=_cparams("parallel", "arbitrary"),
        name="conv_ffn",
    )(*args)


def _rotate(x, cos, sin):
    half = x.shape[1] // 2
    x1, x2 = x[:, :half], x[:, half:]
    return jnp.concatenate([x1 * cos - x2 * sin, x2 * cos + x1 * sin], axis=1)


def _ret_head(h, q, k, v, g, cos, sin, s_ref, chunk, fillers=()):
    fillers = list(fillers) + [lambda: None] * (3 - len(fillers))
    log_g = RET_LOG_G[h]
    li = lax.broadcasted_iota(jnp.int32, (chunk, chunk), 0)
    mi = lax.broadcasted_iota(jnp.int32, (chunk, chunk), 1)
    diff = (li - mi).astype(F32)
    pos = lax.broadcasted_iota(jnp.int32, (chunk, 1), 0).astype(F32)
    fillers[0]()
    qr = _rotate(q, cos, sin).astype(BF16)
    kr = _rotate(k, cos, sin) * (RET_DK ** -0.5)
    decay = jnp.where(diff >= 0, jnp.exp(jnp.maximum(diff, 0.0) * log_g), 0.0)
    scores = _dot_nt(qr, kr.astype(BF16))
    fillers[1]()
    scores = scores * decay
    intra = _dot(scores.astype(BF16), v)
    s_old = s_ref[0, h]
    cross = _dot(qr, s_old.astype(BF16)) * jnp.exp((pos + 1.0) * log_g)
    o = intra + cross
    k_tail = (kr * jnp.exp((chunk - 1.0 - pos) * log_g)).astype(BF16)
    s_ref[0, h] = math.exp(chunk * log_g) * s_old + lax.dot_general(
        k_tail, v, (((0,), (0,)), ((), ())), preferred_element_type=F32)
    fillers[2]()
    mu = jnp.mean(o, axis=-1, keepdims=True)
    oc = o - mu
    var = jnp.mean(oc * oc, axis=-1, keepdims=True)
    return jax.nn.silu(g) * (oc * lax.rsqrt(var + GN_EPS))


def _ret_kernel(p_ref, cos_ref, sin_ref, s_in_ref, y_ref, s_out_ref, *, chunk):
    @pl.when(pl.program_id(1) == 0)
    def _():
        s_out_ref[...] = s_in_ref[...]

    cos = cos_ref[...]
    sin = sin_ref[...]
    for h in range(RET_HEADS):
        q = p_ref[:, h * RET_DK:(h + 1) * RET_DK].astype(F32)
        k = p_ref[:, RET_QK + h * RET_DK:RET_QK + (h + 1) * RET_DK].astype(F32)
        v = p_ref[:, 2 * RET_QK + h * RET_DV:2 * RET_QK + (h + 1) * RET_DV]
        g = p_ref[:, 2 * RET_QK + RET_V + h * RET_DV:
                  2 * RET_QK + RET_V + (h + 1) * RET_DV].astype(F32)
        y = _ret_head(h, q, k, v, g, cos, sin, s_out_ref, chunk)
        y_ref[:, h * RET_DV:(h + 1) * RET_DV] = y.astype(y_ref.dtype)


def _ret_fused_kernel(x_ref, g_ref, w_ref, wo_ref, cos_ref, sin_ref, s_in_ref, o_ref, s_out_ref,
                      *, chunk):
    @pl.when(pl.program_id(1) == 0)
    def _():
        s_out_ref[...] = s_in_ref[...]

    cos = cos_ref[...]
    sin = sin_ref[...]
    x = x_ref[...]
    hn, row_scale = _prenorm(x, g_ref)
    o_ref[...] = x

    offsets = (0, RET_QK, 2 * RET_QK, 2 * RET_QK + RET_V)
    widths = (RET_DK, RET_DK, RET_DV, RET_DV)

    def proj(h, part):
        c0 = offsets[part] + h * widths[part]
        y = _dot(hn, w_ref[:, c0:c0 + widths[part]]) * row_scale
        return y.astype(BF16) if part == 2 else y

    def out_proj(h, y):
        o_ref[...] += _dot(y, wo_ref[h * RET_DV:(h + 1) * RET_DV, :])

    cur = [proj(0, part) for part in range(4)]
    ys = []
    for h in range(RET_HEADS):
        nxt = [None] * 4

        def fill_qk(h=h, nxt=nxt):
            nxt[0], nxt[1] = proj(h + 1, 0), proj(h + 1, 1)

        def fill_v(h=h, nxt=nxt):
            nxt[2] = proj(h + 1, 2)

        def fill_g(h=h, nxt=nxt):
            nxt[3] = proj(h + 1, 3)

        if h + 1 < RET_HEADS:
            fillers = (fill_qk, fill_v, fill_g)
        else:
            fillers = [functools.partial(out_proj, i, y) for i, y in enumerate(ys)]
        ys.append(_ret_head(h, *cur, cos, sin, s_out_ref, chunk, fillers).astype(BF16))
        cur = nxt
    out_proj(RET_HEADS - 1, ys[-1])


def _ret_specs(seq_len, chunk):
    nchunks = seq_len // chunk
    half = RET_DK // 2
    row_spec = lambda n: pl.BlockSpec((chunk, n), lambda o, i: (o * nchunks + i, 0))
    rope_spec = pl.BlockSpec((chunk, half), lambda o, i: (i, 0))
    state_spec = pl.BlockSpec((1, RET_HEADS, RET_DK, RET_DV), lambda o, i: (o, 0, 0, 0))
    return nchunks, row_spec, rope_spec, state_spec


def ret_core(p, cos, sin, state, seq_len, chunk):
    t, n = p.shape
    nchunks, row_spec, rope_spec, state_spec = _ret_specs(seq_len, chunk)
    return pl.pallas_call(
        functools.partial(_ret_kernel, chunk=chunk),
        grid=(state.shape[0], nchunks),
        in_specs=[row_spec(n), rope_spec, rope_spec, state_spec],
        out_specs=[row_spec(RET_V), state_spec],
        out_shape=[jax.ShapeDtypeStruct((t, RET_V), BF16),
                   jax.ShapeDtypeStruct(state.shape, F32)],
        compiler_params=_cparams("parallel", "arbitrary"),
        name="ret_core",
    )(p, cos, sin, state)


def ret_mixer(x, g, w_qkvg, w_o, cos, sin, state, seq_len, chunk):
    t, d = x.shape
    nchunks, row_spec, rope_spec, state_spec = _ret_specs(seq_len, chunk)
    return pl.pallas_call(
        functools.partial(_ret_fused_kernel, chunk=chunk),
        grid=(state.shape[0], nchunks),
        in_specs=[row_spec(d), _const_spec((1, d)), _weight_spec(w_qkvg),
                  _weight_spec(w_o), rope_spec, rope_spec, state_spec],
        out_specs=[row_spec(d), state_spec],
        out_shape=[jax.ShapeDtypeStruct((t, d), F32), jax.ShapeDtypeStruct(state.shape, F32)],
        compiler_params=_cparams("parallel", "arbitrary"),
        name="ret_mixer",
    )(x, g.reshape(1, d), w_qkvg[0], w_o[0], cos, sin, state)


def _rope_tables(pos0, length):
    half = RET_DK // 2
    freqs = ROPE_BASE ** (-np.arange(half, dtype=np.float64) / half)
    ang = (pos0 + np.arange(length, dtype=np.float64))[:, None] * freqs[None, :]
    return jnp.asarray(np.cos(ang), F32), jnp.asarray(np.sin(ang), F32)


def _cmul(ar, ai, br, bi):
    return ar * br - ai * bi, ar * bi + ai * br


def _s5_prep_kernel(lam_re_ref, lam_im_ref, ldt_ref, b_re_ref, b_im_ref, c_re_ref, c_im_ref,
                    w_re_ref, w_im_ref, bt_ref, cct_ref, a16_ref):
    dt = jnp.exp(ldt_ref[...])
    lam_re, lam_im = lam_re_ref[...], lam_im_ref[...]
    mag = jnp.exp(lam_re * dt)
    ar, ai = mag * jnp.cos(lam_im * dt), mag * jnp.sin(lam_im * dt)
    den = lam_re * lam_re + lam_im * lam_im
    nr, ni = ar - 1.0, ai
    coef_re = (nr * lam_re + ni * lam_im) / den
    coef_im = (ni * lam_re - nr * lam_im) / den
    bb_re, bb_im = _cmul(coef_re, coef_im, b_re_ref[...], b_im_ref[...])
    c_re, c_im = c_re_ref[...], c_im_ref[...]
    lane = lax.broadcasted_iota(jnp.int32, (1, LANES), 1)
    low = lane < SSM_STATE
    rg = lax.broadcasted_iota(jnp.int32, (LANES, 1), 0) // SSM_GROUP
    pair = rg // 2
    own_half = (lane // SSM_STATE) == (rg % 2)
    same_group = rg == lane // SSM_GROUP
    c_hi, c_lo = _split_bf16(jnp.where(low, c_re, -c_im))
    tk = []
    pr, pi = jnp.ones_like(ar), jnp.zeros_like(ar)
    for j in range(SSM_FOLD):
        zr, zi = _cmul(pr, pi, bb_re, bb_im)
        w_re_ref[SSM_FOLD - 1 - j] = jnp.where(own_half, zr, 0.0)
        w_im_ref[SSM_FOLD - 1 - j] = jnp.where(own_half, zi, 0.0)
        b_hi, b_lo = _split_bf16(jnp.where(low, zr, zi))
        kk = _dot_nt(b_hi, c_hi) + _dot_nt(b_lo, c_hi) + _dot_nt(b_hi, c_lo)
        tk.append(jnp.where(same_group, kk, 0.0).astype(BF16))
        pr, pi = _cmul(pr, pi, ar, ai)
        zr, zi = _cmul(c_re, c_im, pr, pi)
        for ri, z in enumerate((zr, -zi)):
            z = jnp.where(own_half, z, 0.0)
            for t in range(GROUPS_PER_TILE // 2):
                c0 = ri * STATE_PER_TILE + t * LANES
                cct_ref[0, j * LANES:(j + 1) * LANES, c0:c0 + LANES] = (
                    jnp.where(pair == t, z, 0.0).astype(BF16))
    a16_ref[0] = pr
    a16_ref[1] = pi
    zero = jnp.zeros((LANES, LANES), BF16)
    for d2 in range(SSM_FOLD // 2):
        top = jnp.concatenate([tk[2 * d2], tk[2 * d2 + 1]], axis=1)
        bot = jnp.concatenate([tk[2 * d2 - 1] if d2 else zero, tk[2 * d2]], axis=1)
        bt_ref[0, d2] = jnp.concatenate([top, bot], axis=0)


def s5_prepare(a_re, a_im, log_dt, b_re, b_im, c_re, c_im):
    g, p, c = SSM_GROUPS, SSM_STATE, SSM_GROUP
    fold, nt = SSM_FOLD, N_LANE_TILES
    rows = g * c
    twice = lambda x: jnp.tile(x, (1, 2))
    lam = lambda x: twice(jnp.repeat(x, c, axis=0))
    in_spec = pl.BlockSpec((LANES, LANES), lambda t: (t, 0))
    w_spec = pl.BlockSpec((fold, LANES, LANES), lambda t: (0, t, 0))
    w_shape = jax.ShapeDtypeStruct((fold, rows, LANES), F32)
    w_re, w_im, bt, cct, a16 = pl.pallas_call(
        _s5_prep_kernel,
        grid=(nt,),
        in_specs=[in_spec, in_spec, pl.BlockSpec((LANES, 1), lambda t: (t, 0))] + [in_spec] * 4,
        out_specs=[w_spec, w_spec,
                   pl.BlockSpec((1, fold // 2, 2 * LANES, 2 * LANES), lambda t: (t, 0, 0, 0)),
                   pl.BlockSpec((1, fold * LANES, 2 * STATE_PER_TILE), lambda t: (t, 0, 0)),
                   pl.BlockSpec((2, LANES, LANES), lambda t: (0, t, 0))],
        out_shape=[w_shape, w_shape,
                   jax.ShapeDtypeStruct((nt, fold // 2, 2 * LANES, 2 * LANES), BF16),
                   jax.ShapeDtypeStruct((nt, fold * LANES, 2 * STATE_PER_TILE), BF16),
                   jax.ShapeDtypeStruct((2, rows, LANES), F32)],
        compiler_params=_cparams("parallel"),
        name="s5_prep",
    )(lam(a_re), lam(a_im), jnp.repeat(log_dt, c).reshape(rows, 1),
      twice(b_re.transpose(0, 2, 1).reshape(rows, p)), twice(b_im.transpose(0, 2, 1).reshape(rows, p)),
      twice(c_re.reshape(rows, p)), twice(c_im.reshape(rows, p)))
    a16 = a16[:, ::c, :p].reshape(2, 1, N_STATE)
    return bt, cct, w_re, w_im, a16


FOLD_BLOCK = SSM_FOLD * SSM_FOLD


def _fold_perm():
    idx = np.arange(FOLD_BLOCK)
    perm = np.zeros((FOLD_BLOCK, FOLD_BLOCK), np.float32)
    perm[(idx % SSM_FOLD) * SSM_FOLD + idx // SSM_FOLD, idx] = 1.0
    return jnp.asarray(perm, BF16)


def _s5_fold_kernel(x_ref, g_ref, perm_ref, u_ref):
    for blk in range(x_ref.shape[0] // FOLD_BLOCK):
        x = x_ref[blk * FOLD_BLOCK:(blk + 1) * FOLD_BLOCK, :]
        h = (x * _rms_scale(x) * g_ref[...]).astype(BF16)
        f = _dot(perm_ref[...], h).astype(BF16)
        for m in range(SSM_FOLD):
            for k in range(N_LANE_TILES):
                u_ref[m, k, blk * SSM_FOLD:(blk + 1) * SSM_FOLD, :] = (
                    f[m * SSM_FOLD:(m + 1) * SSM_FOLD, k * LANES:(k + 1) * LANES])


def s5_fold(x, g, perm, tokens):
    t, d = x.shape
    rows = t // SSM_FOLD
    return pl.pallas_call(
        _s5_fold_kernel,
        grid=(t // tokens,),
        in_specs=[pl.BlockSpec((tokens, d), lambda i: (i, 0)), _const_spec((1, d)),
                  _const_spec((FOLD_BLOCK, FOLD_BLOCK))],
        out_specs=pl.BlockSpec((SSM_FOLD, N_LANE_TILES, tokens // SSM_FOLD, LANES),
                               lambda i: (0, 0, i, 0)),
        out_shape=jax.ShapeDtypeStruct((SSM_FOLD, N_LANE_TILES, rows, LANES), BF16),
        compiler_params=_cparams("parallel"),
        name="s5_fold",
    )(x, g.reshape(1, d), perm)


def _s5_state_in_kernel(u_ref, w_re_ref, w_im_ref, vre_ref, vim_ref, wd_ref):
    @pl.when(pl.program_id(1) == 0)
    def _():
        pair = lax.broadcasted_iota(jnp.int32, (LANES, 1), 0) // SSM_GROUP // 2
        for m in range(SSM_FOLD):
            for ri, ref in enumerate((w_re_ref, w_im_ref)):
                w = ref[m]
                for t in range(GROUPS_PER_TILE // 2):
                    c0 = ri * STATE_PER_TILE + t * LANES
                    wd_ref[m * LANES:(m + 1) * LANES, c0:c0 + LANES] = (
                        jnp.where(pair == t, w, 0.0).astype(BF16))

    lhs = jnp.concatenate([u_ref[m, 0] for m in range(SSM_FOLD)], axis=1)
    v = _dot(lhs, wd_ref[...])
    vre_ref[...] = v[:, :STATE_PER_TILE]
    vim_ref[...] = v[:, STATE_PER_TILE:]


def _s5_tile_specs(rt):
    u_spec = pl.BlockSpec((SSM_FOLD, 1, rt, LANES), lambda t, i: (0, t, i, 0))
    s_spec = pl.BlockSpec((rt, STATE_PER_TILE), lambda t, i: (i, t))
    return u_spec, s_spec


def s5_state_in(u, w_re, w_im, rt):
    rows = u.shape[2]
    u_spec, s_spec = _s5_tile_specs(rt)
    out = jax.ShapeDtypeStruct((rows, N_STATE), F32)
    w_spec = pl.BlockSpec((SSM_FOLD, LANES, LANES), lambda t, i: (0, t, 0))
    return pl.pallas_call(
        _s5_state_in_kernel,
        grid=(N_LANE_TILES, rows // rt),
        in_specs=[u_spec, w_spec, w_spec],
        out_specs=[s_spec, s_spec],
        out_shape=[out, out],
        scratch_shapes=[pltpu.VMEM((SSM_FOLD * LANES, 2 * STATE_PER_TILE), BF16)],
        compiler_params=_cparams("parallel", "arbitrary"),
        name="s5_state_in",
    )(u, w_re, w_im)


def _s5_scan_pairs_kernel(vre_ref, vim_ref, a_ref, s0re_ref, s0im_ref,
                          sre_ref, sim_ref, fre_ref, fim_ref):
    ar, ai = a_ref[0], a_ref[1]
    s0r, s0i = s0re_ref[...], s0im_ref[...]
    vr, vi = vre_ref[...], vim_ref[...]
    first = lax.broadcasted_iota(jnp.int32, (s0r.shape[0], 1), 0) % 2 == 0
    tr, ti = ar * s0r - ai * s0i + vr, ar * s0i + ai * s0r + vi
    sr = jnp.where(first, s0r, pltpu.roll(tr, 1, 0))
    si = jnp.where(first, s0i, pltpu.roll(ti, 1, 0))
    sre_ref[...] = sr
    sim_ref[...] = si
    fre_ref[...] = ar * sr - ai * si + vr
    fim_ref[...] = ar * si + ai * sr + vi


def _s5_scan_stream_kernel(vre_ref, vim_ref, a_ref, s0re_ref, s0im_ref,
                           sre_ref, sim_ref, fre_ref, fim_ref, *, steps):
    q = lax.broadcasted_iota(jnp.int32, (SUBLANES, 1), 0)
    a1 = (a_ref[0], a_ref[1])
    a2 = _cmul(*a1, *a1)
    a4 = _cmul(*a2, *a2)
    a8 = _cmul(*a4, *a4)
    strides = ((1, a1), (2, a2), (4, a4))
    width = a1[0].shape[1]
    pw = (jnp.ones((SUBLANES, width), F32), jnp.zeros((SUBLANES, width), F32))
    for k, ak in strides:
        nxt = _cmul(*pw, *ak)
        bit = (q & k) != 0
        pw = (jnp.where(bit, nxt[0], pw[0]), jnp.where(bit, nxt[1], pw[1]))

    def shifted(x, k):
        return jnp.where(q >= k, pltpu.roll(x, k, 0), 0.0)

    def body(t, carry):
        cr, ci = carry
        rows = pl.ds(pl.multiple_of(t * SUBLANES, SUBLANES), SUBLANES)
        pr, pi = vre_ref[rows, :], vim_ref[rows, :]
        for k, ak in strides:
            dr, di = _cmul(shifted(pr, k), shifted(pi, k), *ak)
            pr, pi = pr + dr, pi + di
        xr, xi = _cmul(*pw, cr, ci)
        sre_ref[rows, :] = xr + shifted(pr, 1)
        sim_ref[rows, :] = xi + shifted(pi, 1)
        nr, ni = _cmul(*a8, cr, ci)
        return nr + pr[SUBLANES - 1:], ni + pi[SUBLANES - 1:]

    cr, ci = lax.fori_loop(0, steps // SUBLANES, body, (s0re_ref[0], s0im_ref[0]))
    fre_ref[0] = cr
    fim_ref[0] = ci


def s5_scan(v_re, v_im, a16, s0_re, s0_im, width):
    rows, n = v_re.shape
    streams = s0_re.shape[0]
    steps = rows // streams
    a_spec = pl.BlockSpec((2, 1, width), lambda *i: (0, 0, i[-1]))
    vout = jax.ShapeDtypeStruct((rows, n), F32)
    long_streams = steps % SUBLANES == 0
    if long_streams:
        kern = functools.partial(_s5_scan_stream_kernel, steps=steps)
        grid = (streams, n // width)
        vspec = pl.BlockSpec((steps, width), lambda b, c: (b, c))
        sspec = pl.BlockSpec((1, 1, width), lambda b, c: (b, 0, c))
        s0_re, s0_im = s0_re.reshape(streams, 1, n), s0_im.reshape(streams, 1, n)
        sout = jax.ShapeDtypeStruct((streams, 1, n), F32)
    else:
        assert steps == 2 and rows % SUBLANES == 0
        kern = _s5_scan_pairs_kernel
        grid = (n // width,)
        vspec = pl.BlockSpec((rows, width), lambda c: (0, c))
        sspec = vspec
        s0_re, s0_im = jnp.repeat(s0_re, steps, axis=0), jnp.repeat(s0_im, steps, axis=0)
        sout = vout
    s_re, s_im, f_re, f_im = pl.pallas_call(
        kern,
        grid=grid,
        in_specs=[vspec, vspec, a_spec, sspec, sspec],
        out_specs=[vspec, vspec, sspec, sspec],
        out_shape=[vout, vout, sout, sout],
        compiler_params=_cparams(*(["parallel"] * len(grid))),
        name="s5_scan",
    )(v_re, v_im, a16, s0_re, s0_im)
    if long_streams:
        return s_re, s_im, f_re.reshape(streams, n), f_im.reshape(streams, n)
    return s_re, s_im, f_re[steps - 1::steps], f_im[steps - 1::steps]


def _s5_toeplitz_kernel(u_ref, sre_ref, sim_ref, bt_ref, cct_ref, y_ref):
    npair = SSM_FOLD // 2
    u2 = [jnp.concatenate([u_ref[2 * mm, 0], u_ref[2 * mm + 1, 0]], axis=1) for mm in range(npair)]
    s_prev = jnp.concatenate([sre_ref[...], sim_ref[...]], axis=1).astype(BF16)
    for ll in range(npair):
        acc = _dot_nt(s_prev, cct_ref[0, ll * 2 * LANES:(ll + 1) * 2 * LANES, :])
        for mm in range(ll + 1):
            acc = acc + _dot(u2[mm], bt_ref[0, ll - mm])
        y_ref[2 * ll, 0] = acc[:, :LANES]
        y_ref[2 * ll + 1, 0] = acc[:, LANES:]


def s5_toeplitz(u, s_re, s_im, bt, cct, rt):
    rows = u.shape[2]
    u_spec, s_spec = _s5_tile_specs(rt)
    return pl.pallas_call(
        _s5_toeplitz_kernel,
        grid=(N_LANE_TILES, rows // rt),
        in_specs=[u_spec, s_spec, s_spec,
                  pl.BlockSpec((1, SSM_FOLD // 2, 2 * LANES, 2 * LANES), lambda t, i: (t, 0, 0, 0)),
                  pl.BlockSpec((1, SSM_FOLD * LANES, 2 * STATE_PER_TILE), lambda t, i: (t, 0, 0))],
        out_specs=u_spec,
        out_shape=jax.ShapeDtypeStruct(u.shape, F32),
        compiler_params=_cparams("parallel", "parallel"),
        name="s5_toeplitz",
    )(u, s_re, s_im, bt, cct)


def _s5_glu_kernel(x_ref, u_ref, y_ref, d_ref, wglu_ref, bglu_ref, perm_ref, o_ref):
    nblk = x_ref.shape[0] // FOLD_BLOCK

    def gelu_block(blk):
        rows = slice(blk * SSM_FOLD, (blk + 1) * SSM_FOLD)
        gl = []
        for m in range(SSM_FOLD):
            ym = jnp.concatenate([y_ref[m, k, rows, :] for k in range(N_LANE_TILES)], axis=1)
            hm = jnp.concatenate([u_ref[m, k, rows, :] for k in range(N_LANE_TILES)], axis=1)
            gl.append(jax.nn.gelu(ym + d_ref[...] * hm.astype(F32)))
        gl = jnp.concatenate(gl, axis=0)
        return gl, _dot(gl.astype(BF16), wglu_ref[...])

    def finish(blk, gl, z):
        tok = slice(blk * FOLD_BLOCK, (blk + 1) * FOLD_BLOCK)
        hi, lo = _split_bf16(gl * jax.nn.sigmoid(z + bglu_ref[...]))
        o_ref[tok, :] = x_ref[tok, :] + (_dot(perm_ref[...], hi) + _dot(perm_ref[...], lo))

    prev = gelu_block(0)
    for blk in range(1, nblk):
        cur = gelu_block(blk)
        finish(blk - 1, *prev)
        prev = cur
    finish(nblk - 1, *prev)


def s5_glu(x, u, y, d, w_glu, b_glu, perm, tokens):
    t, dm = x.shape
    fold_spec = pl.BlockSpec((SSM_FOLD, N_LANE_TILES, tokens // SSM_FOLD, LANES),
                             lambda i: (0, 0, i, 0))
    x_spec = pl.BlockSpec((tokens, dm), lambda i: (i, 0))
    return pl.pallas_call(
        _s5_glu_kernel,
        grid=(t // tokens,),
        in_specs=[x_spec, fold_spec, fold_spec, _const_spec((1, dm)), _weight_spec(w_glu),
                  _const_spec((1, dm)), _const_spec((FOLD_BLOCK, FOLD_BLOCK))],
        out_specs=x_spec,
        out_shape=jax.ShapeDtypeStruct((t, dm), F32),
        compiler_params=_cparams("parallel"),
        name="s5_glu",
    )(x, u, y, d.reshape(1, dm), w_glu[0], b_glu.reshape(1, dm), perm)


def s5_mixer(x, st_re, st_im, g, ops, d, w_glu, b_glu):
    bt, cct, w_re, w_im, a16 = ops
    b, l, dm = x.shape
    rows = b * l // SSM_FOLD
    x2 = x.reshape(b * l, dm)
    perm = _fold_perm()
    tokens = min(S5_TOKEN_TILE, b * l)
    rt = min(S5_ROW_TILE, rows)
    u = s5_fold(x2, g, perm, tokens)
    v_re, v_im = s5_state_in(u, w_re, w_im, rt)
    s_re, s_im, f_re, f_im = s5_scan(v_re, v_im, a16, st_re.reshape(b, N_STATE),
                                     st_im.reshape(b, N_STATE), S5_SCAN_WIDTH)
    y = s5_toeplitz(u, s_re, s_im, bt, cct, rt)
    out = s5_glu(x2, u, y, d, w_glu, b_glu, perm, tokens)
    shape = (b, SSM_GROUPS, SSM_STATE)
    return out.reshape(b, l, dm), f_re.reshape(shape), f_im.reshape(shape)


def _stream_tiling(seq_len, n_streams, tile_rows):
    if seq_len >= tile_rows:
        return 1, tile_rows
    return min(n_streams, 256 // seq_len), seq_len


def kernel(x_prompt, x_sample, mem_prompt, state_ssm_re, state_ssm_im, state_ret, cache_mem_k, cache_mem_v, cache_conv, norm_mix, norm_mem_q, norm_mem_kv, norm_ffn, norm_final, ssm_a_re, ssm_a_im, ssm_log_dt, ssm_b_re, ssm_b_im, ssm_c_re, ssm_c_im, ssm_d, ssm_w_glu, ssm_b_glu, ret_w_qkvg, ret_w_o, mem_w_q, mem_w_kv, mem_w_o, ffn_w_up, ffn_conv_w, ffn_conv_b, ffn_w_down):
    bp, lp, dm = x_prompt.shape
    bs, ls, _ = x_sample.shape
    depth = norm_mix.shape[0]
    n_mem = mem_prompt.shape[1]
    d_ff = ffn_w_down.shape[1]
    streams = [dict(x=x, b=x.shape[0], l=x.shape[1],
                    attn_tiling=_stream_tiling(x.shape[1], x.shape[0], ATTN_TILE_ROWS),
                    ffn_tiling=_stream_tiling(x.shape[1], x.shape[0], FFN_TILE_ROWS))
               for x in (x_prompt, x_sample)]
    cache_k, cache_v = _kv_rows(cache_mem_k), _kv_rows(cache_mem_v)
    ssm_w_glu, ret_w_qkvg, ret_w_o, mem_w_q, mem_w_kv, mem_w_o, ffn_w_up, ffn_w_down = (
        w.astype(BF16) for w in (ssm_w_glu, ret_w_qkvg, ret_w_o, mem_w_q, mem_w_kv, mem_w_o,
                                 ffn_w_up, ffn_w_down))
    mem_k, mem_v = mem_kv(mem_prompt, norm_mem_kv, mem_w_kv)
    outs = dict(ssm_re=([], []), ssm_im=([], []), ret=([], []), conv=([], []))

    for i in range(depth):
        j = i // 2
        if i % 2 == 0:
            ops = s5_prepare(ssm_a_re[j], ssm_a_im[j], ssm_log_dt[j], ssm_b_re[j], ssm_b_im[j],
                             ssm_c_re[j], ssm_c_im[j])
            w_glu = (ssm_w_glu, j)
            zeros = jnp.zeros((bp, SSM_GROUPS, SSM_STATE), F32)
            states = [(zeros, zeros), (state_ssm_re[j], state_ssm_im[j])]
            for si, (s, (st_re, st_im)) in enumerate(zip(streams, states)):
                s["x"], f_re, f_im = s5_mixer(s["x"], st_re, st_im, norm_mix[i], ops, ssm_d[j],
                                              w_glu, ssm_b_glu[j])
                outs["ssm_re"][si].append(f_re)
                outs["ssm_im"][si].append(f_im)
        else:
            w_qkvg = (ret_w_qkvg, j)
            w_o = (ret_w_o, j)
            zeros = jnp.zeros((bp, RET_HEADS, RET_DK, RET_DV), F32)
            chunk = RET_CHUNK_ROWS
            for si, (s, st, pos0) in enumerate(zip(streams, [zeros, state_ret[j]], [0, PAST_LEN])):
                x2 = s["x"].reshape(s["b"] * s["l"], dm)
                cos, sin = _rope_tables(pos0, s["l"])
                if s["l"] >= chunk:
                    y, st_new = ret_mixer(x2, norm_mix[i], w_qkvg, w_o, cos, sin, st, s["l"], chunk)
                else:
                    p = rms_matmul(x2, norm_mix[i], w_qkvg, BF16)
                    y, st_new = ret_core(p, cos, sin, st, s["l"], s["l"])
                    y = matmul_res(y, w_o, x2)
                s["x"] = y.reshape(s["x"].shape)
                outs["ret"][si].append(st_new)

        w_q = (mem_w_q, i)
        w_o = (mem_w_o, i)
        for s, (k_, v_) in zip(streams, [(mem_k, mem_v), (cache_k, cache_v)]):
            x2 = s["x"].reshape(s["b"] * s["l"], dm)
            y = mem_attn(x2, norm_mem_q[i], w_q, k_, v_, i, w_o, s["l"], *s["attn_tiling"])
            s["x"] = y.reshape(s["x"].shape)

        w_up = (ffn_w_up, i)
        w_dn = (ffn_w_down, i)
        g_final = norm_final if i == depth - 1 else None
        conv_states = [jnp.zeros((bp, 2, d_ff), F32), cache_conv[i]]
        for si, (s, cst) in enumerate(zip(streams, conv_states)):
            x2 = s["x"].reshape(s["b"] * s["l"], dm)
            y, cnew = conv_ffn(x2, norm_ffn[i], w_up, ffn_conv_w[i], ffn_conv_b[i], w_dn, cst,
                               s["l"], *s["ffn_tiling"], g_final=g_final)
            s["x"] = y.reshape(s["x"].shape)
            outs["conv"][si].append(cnew)

    return (streams[0]["x"], streams[1]["x"],
            jnp.stack(outs["ssm_re"][0]), jnp.stack(outs["ssm_im"][0]),
            jnp.stack(outs["ssm_re"][1]), jnp.stack(outs["ssm_im"][1]),
            jnp.stack(outs["ret"][0]), jnp.stack(outs["ret"][1]),
            _kv_unrows(mem_k, MEM_HEADS), _kv_unrows(mem_v, MEM_HEADS),
            jnp.stack(outs["conv"][0]), jnp.stack(outs["conv"][1]))
```

```python
import functools
import math

import numpy as np
import jax
import jax.numpy as jnp
from jax import lax
from jax.experimental import pallas as pl
from jax.experimental.pallas import tpu as pltpu

F32 = jnp.float32
BF16 = jnp.bfloat16

D_MODEL = 1024
PAST_LEN = 1024
EPS = 1e-6
GN_EPS = 1e-5
ROPE_BASE = 10000.0
LANES = 128
SUBLANES = 8
SSM_GROUP = 16
SSM_GROUPS = D_MODEL // SSM_GROUP
SSM_STATE = 64
SSM_FOLD = 16
GROUPS_PER_TILE = LANES // SSM_GROUP
N_LANE_TILES = D_MODEL // LANES
STATE_PER_TILE = GROUPS_PER_TILE * SSM_STATE
N_STATE = SSM_GROUPS * SSM_STATE
RET_HEADS = 4
RET_DK = D_MODEL // RET_HEADS
RET_DV = 2 * D_MODEL // RET_HEADS
RET_QK = RET_HEADS * RET_DK
RET_V = RET_HEADS * RET_DV
RET_LOG_G = tuple(math.log(1.0 - 2.0 ** (-5.0 - h)) for h in range(RET_HEADS))
MEM_HEADS = 4
MEM_HD = D_MODEL // MEM_HEADS
VMEM_LIMIT = 56 * 1024 * 1024
ATTN_TILE_ROWS = 1024
FFN_TILE_ROWS = 256
RET_CHUNK_ROWS = 256
S5_TOKEN_TILE = 1024
S5_ROW_TILE = 512
S5_SCAN_WIDTH = 1024


def _cparams(*sem):
    return pltpu.CompilerParams(dimension_semantics=sem, vmem_limit_bytes=VMEM_LIMIT)


def _const_spec(shape):
    nd = len(shape)
    return pl.BlockSpec(shape, lambda *_: (0,) * nd, pipeline_mode=pl.Buffered(1))


def _weight_spec(w):
    stack, layer = w
    return pl.BlockSpec((None,) + stack.shape[1:], lambda *_: (layer, 0, 0),
                        pipeline_mode=pl.Buffered(1))


def _rms_scale(x):
    return lax.rsqrt(jnp.mean(x * x, axis=-1, keepdims=True) + EPS)


def _prenorm(x, g_ref):
    return (x * g_ref[...]).astype(BF16), _rms_scale(x)


def _split_bf16(x):
    hi = x.astype(BF16)
    lo = (x - hi.astype(F32)).astype(BF16)
    return hi, lo


def _dot(a, b):
    return jnp.dot(a, b, preferred_element_type=F32)


def _dot_nt(a, b):
    return lax.dot_general(a, b, (((1,), (1,)), ((), ())), preferred_element_type=F32)


def _rms_matmul_kernel(x_ref, g_ref, w_ref, o_ref, *, tn):
    x = x_ref[...]
    h = (x * _rms_scale(x) * g_ref[...]).astype(BF16)
    for n0 in range(0, w_ref.shape[1], tn):
        o_ref[:, n0:n0 + tn] = _dot(h, w_ref[:, n0:n0 + tn]).astype(o_ref.dtype)


def rms_matmul(x, g, w, out_dtype, tm=512, tn=512):
    t, d = x.shape
    n = w[0].shape[2]
    tm = min(tm, t)
    return pl.pallas_call(
        functools.partial(_rms_matmul_kernel, tn=tn),
        grid=(t // tm,),
        in_specs=[pl.BlockSpec((tm, d), lambda i: (i, 0)),
                  _const_spec((1, d)),
                  _weight_spec(w)],
        out_specs=pl.BlockSpec((tm, n), lambda i: (i, 0)),
        out_shape=jax.ShapeDtypeStruct((t, n), out_dtype),
        compiler_params=_cparams("parallel"),
        name="rms_matmul",
    )(x, g.reshape(1, d), w[0])


def _matmul_res_kernel(y_ref, w_ref, x_ref, o_ref):
    o_ref[...] = x_ref[...] + _dot(y_ref[...], w_ref[...])


def matmul_res(y, w, x, tm=512):
    t, k = y.shape
    d = w[0].shape[2]
    tm = min(tm, t)
    return pl.pallas_call(
        _matmul_res_kernel,
        grid=(t // tm,),
        in_specs=[pl.BlockSpec((tm, k), lambda i: (i, 0)),
                  _weight_spec(w),
                  pl.BlockSpec((tm, d), lambda i: (i, 0))],
        out_specs=pl.BlockSpec((tm, d), lambda i: (i, 0)),
        out_shape=jax.ShapeDtypeStruct((t, d), F32),
        compiler_params=_cparams("parallel"),
        name="matmul_res",
    )(y, w[0], x)


MEM_LANE_TILES = MEM_HD // LANES


def _kv_rows(x):
    *lead, n, h, hd = x.shape
    x = x.reshape(*lead, n, h, MEM_LANE_TILES, LANES)
    return jnp.swapaxes(x, -3, -2).reshape(*lead, n * MEM_LANE_TILES * h, LANES)


def _kv_unrows(x, heads):
    *lead, rows, _ = x.shape
    n = rows // (MEM_LANE_TILES * heads)
    x = x.reshape(*lead, n, MEM_LANE_TILES, heads, LANES)
    return jnp.swapaxes(x, -3, -2).reshape(*lead, n, heads, MEM_LANE_TILES * LANES)


def _mem_kv_kernel(x_ref, g_ref, w_ref, k_ref, v_ref):
    x = x_ref[...]
    h = (x * _rms_scale(x) * g_ref[...]).astype(BF16)
    n_mem, d = x.shape
    stride = MEM_LANE_TILES * MEM_HEADS
    for which, ref in enumerate((k_ref, v_ref)):
        for hd in range(MEM_HEADS):
            c0 = which * d + hd * MEM_HD
            res = _dot(h, w_ref[:, c0:c0 + MEM_HD])
            for j in range(MEM_LANE_TILES):
                ref[pl.ds(j * MEM_HEADS + hd, n_mem, stride=stride), :] = (
                    res[:, j * LANES:(j + 1) * LANES])


def mem_kv(mem, g, w_kv):
    b, n_mem, d = mem.shape
    depth = w_kv.shape[0]
    rows = n_mem * MEM_LANE_TILES * MEM_HEADS
    out_spec = pl.BlockSpec((None, None, rows, LANES), lambda l, i: (l, i, 0, 0))
    out = jax.ShapeDtypeStruct((depth, b, rows, LANES), F32)
    return pl.pallas_call(
        _mem_kv_kernel,
        grid=(depth, b),
        in_specs=[pl.BlockSpec((None, n_mem, d), lambda l, i: (i, 0, 0)),
                  pl.BlockSpec((None, 1, d), lambda l, i: (l, 0, 0)),
                  pl.BlockSpec((None,) + w_kv.shape[1:], lambda l, i: (l, 0, 0))],
        out_specs=[out_spec, out_spec],
        out_shape=[out, out],
        compiler_params=_cparams("parallel", "parallel"),
        name="mem_kv",
    )(mem, g.reshape(depth, 1, d), w_kv)


def _head_rows(kv_ref, s, hd):
    stride = MEM_LANE_TILES * MEM_HEADS
    n_mem = kv_ref.shape[2] // stride
    return jnp.concatenate(
        [kv_ref[0, s, pl.ds(j * MEM_HEADS + hd, n_mem, stride=stride), :]
         for j in range(MEM_LANE_TILES)], axis=1).astype(BF16)


def _mem_attn_kernel(x_ref, g_ref, wq_ref, k_ref, v_ref, wo_ref, o_ref, *, nseg, seg, head_group):
    scale = MEM_HD ** -0.5
    x = x_ref[...]
    h, row_scale = _prenorm(x, g_ref)
    o_ref[...] = x

    width = head_group * MEM_HD

    def q_proj(grp):
        return (_dot(h, wq_ref[:, grp * width:(grp + 1) * width]) * row_scale).astype(BF16)

    nxt = q_proj(0)
    pending = None
    for grp in range(MEM_HEADS // head_group):
        q = nxt
        pairs = [(grp * head_group + hg, hg, s) for hg in range(head_group) for s in range(nseg)]
        scores = [_dot_nt(q[s * seg:(s + 1) * seg, hg * MEM_HD:(hg + 1) * MEM_HD],
                          _head_rows(k_ref, s, hd)) for hd, hg, s in pairs]
        if (grp + 1) * head_group < MEM_HEADS:
            nxt = q_proj(grp + 1)
        if pending is not None:
            o_ref[...] += _dot(pending, wo_ref[(grp - 1) * width:grp * width, :])
        outs = []
        for (hd, hg, s), sc in zip(pairs, scores):
            sc = sc * scale
            p = jnp.exp(sc - jnp.max(sc, axis=-1, keepdims=True))
            denom = jnp.sum(p, axis=-1, keepdims=True)
            outs.append((_dot(p.astype(BF16), _head_rows(v_ref, s, hd)) / denom).astype(BF16))
        heads = [outs[hg * nseg] if nseg == 1 else jnp.concatenate(outs[hg * nseg:(hg + 1) * nseg], axis=0)
                 for hg in range(head_group)]
        pending = heads[0] if head_group == 1 else jnp.concatenate(heads, axis=1)
    o_ref[...] += _dot(pending, wo_ref[MEM_HEADS * MEM_HD - width:, :])


def mem_attn(x, g, w_q, mk, mv, layer, w_o, seq_len, nseg, seg):
    t, d = x.shape
    b = mk.shape[1]
    tiles_per_stream = max(seq_len // (nseg * seg), 1)
    n_outer = b // nseg
    tm = nseg * seg
    x_spec = pl.BlockSpec((tm, d), lambda o, i: (o * tiles_per_stream + i, 0))
    kv_spec = pl.BlockSpec((1, nseg) + mk.shape[2:], lambda o, i: (layer, o, 0, 0))
    return pl.pallas_call(
        functools.partial(_mem_attn_kernel, nseg=nseg, seg=seg, head_group=2 if nseg == 1 else 1),
        grid=(n_outer, tiles_per_stream),
        in_specs=[x_spec, _const_spec((1, d)), _weight_spec(w_q), kv_spec, kv_spec,
                  _weight_spec(w_o)],
        out_specs=x_spec,
        out_shape=jax.ShapeDtypeStruct((t, d), F32),
        compiler_params=_cparams("parallel", "parallel"),
        name="mem_attn",
    )(x, g.reshape(1, d), w_q[0], mk, mv, w_o[0])


def _ffn_kernel(x_ref, g_ref, wup_ref, cw_ref, cb_ref, wdn_ref, st_ref, *rest,
                nseg, seg, fc, final_norm):
    if final_norm:
        gf_ref, o_ref, nst_ref, carry_ref = rest
    else:
        o_ref, nst_ref, carry_ref = rest
    d_ff = wdn_ref.shape[0]
    tm = nseg * seg

    @pl.when(pl.program_id(1) == 0)
    def _():
        carry_ref[...] = st_ref[...]

    x = x_ref[...]
    h, scale = _prenorm(x, g_ref)
    row = lax.broadcasted_iota(jnp.int32, (tm, 1), 0) % seg
    o_ref[...] = x

    def up_proj(f0):
        return (_dot(h, wup_ref[:, f0:f0 + fc]) * scale,
                _dot(h, wup_ref[:, d_ff + f0:d_ff + f0 + fc]) * scale)

    nxt = up_proj(0)
    pending = None
    for f0 in range(0, d_ff, fc):
        cols = slice(f0, f0 + fc)
        a, gate = nxt
        if f0 + fc < d_ff:
            nxt = up_proj(f0 + fc)
        if pending is not None:
            o_ref[...] += _dot(pending, wdn_ref[f0 - fc:f0, :])
        p0 = jnp.broadcast_to(carry_ref[:, 0:1, cols], (nseg, seg, fc)).reshape(tm, fc)
        p1 = jnp.broadcast_to(carry_ref[:, 1:2, cols], (nseg, seg, fc)).reshape(tm, fc)
        a1 = jnp.where(row == 0, p1, pltpu.roll(a, 1, 0))
        a2 = jnp.where(row == 0, p0, jnp.where(row == 1, p1, pltpu.roll(a, 2, 0)))
        conv = (cw_ref[0:1, cols] * a2 + cw_ref[1:2, cols] * a1 + cw_ref[2:3, cols] * a
                + cb_ref[:, cols])
        p = jax.nn.gelu(conv) * gate
        pending = p.astype(BF16)
        carry_ref[:, :, cols] = a.reshape(nseg, seg, fc)[:, seg - 2:seg, :]
    y = o_ref[...] + _dot(pending, wdn_ref[d_ff - fc:d_ff, :])
    nst_ref[...] = carry_ref[...]
    if final_norm:
        y = y * _rms_scale(y) * gf_ref[...]
    o_ref[...] = y


def conv_ffn(x, g, w_up, conv_w, conv_b, w_down, conv_state, seq_len, nseg, seg, g_final=None,
             fc=256):
    t, d = x.shape
    d_ff = w_down[0].shape[1]
    b = conv_state.shape[0]
    tiles_per_stream = max(seq_len // (nseg * seg), 1)
    n_outer = b // nseg
    tm = nseg * seg
    final_norm = g_final is not None
    in_specs = [pl.BlockSpec((tm, d), lambda o, i: (o * tiles_per_stream + i, 0)),
                _const_spec((1, d)),
                _weight_spec(w_up),
                _const_spec((3, d_ff)),
                _const_spec((1, d_ff)),
                _weight_spec(w_down),
                pl.BlockSpec((nseg, 2, d_ff), lambda o, i: (o, 0, 0))]
    args = [x, g.reshape(1, d), w_up[0], conv_w, conv_b.reshape(1, d_ff), w_down[0], conv_state]
    if final_norm:
        in_specs.append(_const_spec((1, d)))
        args.append(g_final.reshape(1, d))
    return pl.pallas_call(
        functools.partial(_ffn_kernel, nseg=nseg, seg=seg, fc=fc, final_norm=final_norm),
        grid=(n_outer, tiles_per_stream),
        in_specs=in_specs,
        out_specs=[pl.BlockSpec((tm, d), lambda o, i: (o * tiles_per_stream + i, 0)),
                   pl.BlockSpec((nseg, 2, d_ff), lambda o, i: (o, 0, 0))],
        out_shape=[jax.ShapeDtypeStruct((t, d), F32),
                   jax.ShapeDtypeStruct((b, 2, d_ff), F32)],
        scratch_shapes=[pltpu.VMEM((nseg, 2, d_ff), F32)],
        compiler_params=_cparams("parallel", "arbitrary"),
        name="conv_ffn",
    )(*args)


def _rotate(x, cos, sin):
    half = x.shape[1] // 2
    x1, x2 = x[:, :half], x[:, half:]
    return jnp.concatenate([x1 * cos - x2 * sin, x2 * cos + x1 * sin], axis=1)


def _ret_head(h, q, k, v, g, cos, sin, s_ref, chunk, fillers=()):
    fillers = list(fillers) + [lambda: None] * (3 - len(fillers))
    log_g = RET_LOG_G[h]
    li = lax.broadcasted_iota(jnp.int32, (chunk, chunk), 0)
    mi = lax.broadcasted_iota(jnp.int32, (chunk, chunk), 1)
    diff = (li - mi).astype(F32)
    pos = lax.broadcasted_iota(jnp.int32, (chunk, 1), 0).astype(F32)
    fillers[0]()
    qr = _rotate(q, cos, sin).astype(BF16)
    kr = _rotate(k, cos, sin) * (RET_DK ** -0.5)
    decay = jnp.where(diff >= 0, jnp.exp(jnp.maximum(diff, 0.0) * log_g), 0.0)
    scores = _dot_nt(qr, kr.astype(BF16))
    fillers[1]()
    scores = scores * decay
    intra = _dot(scores.astype(BF16), v)
    s_old = s_ref[0, h]
    cross = _dot(qr, s_old.astype(BF16)) * jnp.exp((pos + 1.0) * log_g)
    o = intra + cross
    k_tail = (kr * jnp.exp((chunk - 1.0 - pos) * log_g)).astype(BF16)
    s_ref[0, h] = math.exp(chunk * log_g) * s_old + lax.dot_general(
        k_tail, v, (((0,), (0,)), ((), ())), preferred_element_type=F32)
    fillers[2]()
    mu = jnp.mean(o, axis=-1, keepdims=True)
    oc = o - mu
    var = jnp.mean(oc * oc, axis=-1, keepdims=True)
    return jax.nn.silu(g) * (oc * lax.rsqrt(var + GN_EPS))


def _ret_kernel(p_ref, cos_ref, sin_ref, s_in_ref, y_ref, s_out_ref, *, chunk):
    @pl.when(pl.program_id(1) == 0)
    def _():
        s_out_ref[...] = s_in_ref[...]

    cos = cos_ref[...]
    sin = sin_ref[...]
    for h in range(RET_HEADS):
        q = p_ref[:, h * RET_DK:(h + 1) * RET_DK].astype(F32)
        k = p_ref[:, RET_QK + h * RET_DK:RET_QK + (h + 1) * RET_DK].astype(F32)
        v = p_ref[:, 2 * RET_QK + h * RET_DV:2 * RET_QK + (h + 1) * RET_DV]
        g = p_ref[:, 2 * RET_QK + RET_V + h * RET_DV:
                  2 * RET_QK + RET_V + (h + 1) * RET_DV].astype(F32)
        y = _ret_head(h, q, k, v, g, cos, sin, s_out_ref, chunk)
        y_ref[:, h * RET_DV:(h + 1) * RET_DV] = y.astype(y_ref.dtype)


def _ret_fused_kernel(x_ref, g_ref, w_ref, wo_ref, cos_ref, sin_ref, s_in_ref, o_ref, s_out_ref,
                      *, chunk):
    @pl.when(pl.program_id(1) == 0)
    def _():
        s_out_ref[...] = s_in_ref[...]

    cos = cos_ref[...]
    sin = sin_ref[...]
    x = x_ref[...]
    hn, row_scale = _prenorm(x, g_ref)
    o_ref[...] = x

    offsets = (0, RET_QK, 2 * RET_QK, 2 * RET_QK + RET_V)
    widths = (RET_DK, RET_DK, RET_DV, RET_DV)

    def proj(h, part):
        c0 = offsets[part] + h * widths[part]
        y = _dot(hn, w_ref[:, c0:c0 + widths[part]]) * row_scale
        return y.astype(BF16) if part == 2 else y

    def out_proj(h, y):
        o_ref[...] += _dot(y, wo_ref[h * RET_DV:(h + 1) * RET_DV, :])

    cur = [proj(0, part) for part in range(4)]
    ys = []
    for h in range(RET_HEADS):
        nxt = [None] * 4

        def fill_qk(h=h, nxt=nxt):
            nxt[0], nxt[1] = proj(h + 1, 0), proj(h + 1, 1)

        def fill_v(h=h, nxt=nxt):
            nxt[2] = proj(h + 1, 2)

        def fill_g(h=h, nxt=nxt):
            nxt[3] = proj(h + 1, 3)

        if h + 1 < RET_HEADS:
            fillers = (fill_qk, fill_v, fill_g)
        else:
            fillers = [functools.partial(out_proj, i, y) for i, y in enumerate(ys)]
        ys.append(_ret_head(h, *cur, cos, sin, s_out_ref, chunk, fillers).astype(BF16))
        cur = nxt
    out_proj(RET_HEADS - 1, ys[-1])


def _ret_specs(seq_len, chunk):
    nchunks = seq_len // chunk
    half = RET_DK // 2
    row_spec = lambda n: pl.BlockSpec((chunk, n), lambda o, i: (o * nchunks + i, 0))
    rope_spec = pl.BlockSpec((chunk, half), lambda o, i: (i, 0))
    state_spec = pl.BlockSpec((1, RET_HEADS, RET_DK, RET_DV), lambda o, i: (o, 0, 0, 0))
    return nchunks, row_spec, rope_spec, state_spec


def ret_core(p, cos, sin, state, seq_len, chunk):
    t, n = p.shape
    nchunks, row_spec, rope_spec, state_spec = _ret_specs(seq_len, chunk)
    return pl.pallas_call(
        functools.partial(_ret_kernel, chunk=chunk),
        grid=(state.shape[0], nchunks),
        in_specs=[row_spec(n), rope_spec, rope_spec, state_spec],
        out_specs=[row_spec(RET_V), state_spec],
        out_shape=[jax.ShapeDtypeStruct((t, RET_V), BF16),
                   jax.ShapeDtypeStruct(state.shape, F32)],
        compiler_params=_cparams("parallel", "arbitrary"),
        name="ret_core",
    )(p, cos, sin, state)


def ret_mixer(x, g, w_qkvg, w_o, cos, sin, state, seq_len, chunk):
    t, d = x.shape
    nchunks, row_spec, rope_spec, state_spec = _ret_specs(seq_len, chunk)
    return pl.pallas_call(
        functools.partial(_ret_fused_kernel, chunk=chunk),
        grid=(state.shape[0], nchunks),
        in_specs=[row_spec(d), _const_spec((1, d)), _weight_spec(w_qkvg),
                  _weight_spec(w_o), rope_spec, rope_spec, state_spec],
        out_specs=[row_spec(d), state_spec],
        out_shape=[jax.ShapeDtypeStruct((t, d), F32), jax.ShapeDtypeStruct(state.shape, F32)],
        compiler_params=_cparams("parallel", "arbitrary"),
        name="ret_mixer",
    )(x, g.reshape(1, d), w_qkvg[0], w_o[0], cos, sin, state)


def _rope_tables(pos0, length):
    half = RET_DK // 2
    freqs = ROPE_BASE ** (-np.arange(half, dtype=np.float64) / half)
    ang = (pos0 + np.arange(length, dtype=np.float64))[:, None] * freqs[None, :]
    return jnp.asarray(np.cos(ang), F32), jnp.asarray(np.sin(ang), F32)


def _cmul(ar, ai, br, bi):
    return ar * br - ai * bi, ar * bi + ai * br


def _s5_prep_kernel(lam_re_ref, lam_im_ref, ldt_ref, b_re_ref, b_im_ref, c_re_ref, c_im_ref,
                    w_re_ref, w_im_ref, bt_ref, cct_ref, a16_ref):
    dt = jnp.exp(ldt_ref[...])
    lam_re, lam_im = lam_re_ref[...], lam_im_ref[...]
    mag = jnp.exp(lam_re * dt)
    ar, ai = mag * jnp.cos(lam_im * dt), mag * jnp.sin(lam_im * dt)
    den = lam_re * lam_re + lam_im * lam_im
    nr, ni = ar - 1.0, ai
    coef_re = (nr * lam_re + ni * lam_im) / den
    coef_im = (ni * lam_re - nr * lam_im) / den
    bb_re, bb_im = _cmul(coef_re, coef_im, b_re_ref[...], b_im_ref[...])
    c_re, c_im = c_re_ref[...], c_im_ref[...]
    lane = lax.broadcasted_iota(jnp.int32, (1, LANES), 1)
    low = lane < SSM_STATE
    rg = lax.broadcasted_iota(jnp.int32, (LANES, 1), 0) // SSM_GROUP
    pair = rg // 2
    own_half = (lane // SSM_STATE) == (rg % 2)
    same_group = rg == lane // SSM_GROUP
    c_hi, c_lo = _split_bf16(jnp.where(low, c_re, -c_im))
    tk = []
    pr, pi = jnp.ones_like(ar), jnp.zeros_like(ar)
    for j in range(SSM_FOLD):
        zr, zi = _cmul(pr, pi, bb_re, bb_im)
        w_re_ref[SSM_FOLD - 1 - j] = jnp.where(own_half, zr, 0.0)
        w_im_ref[SSM_FOLD - 1 - j] = jnp.where(own_half, zi, 0.0)
        b_hi, b_lo = _split_bf16(jnp.where(low, zr, zi))
        kk = _dot_nt(b_hi, c_hi) + _dot_nt(b_lo, c_hi) + _dot_nt(b_hi, c_lo)
        tk.append(jnp.where(same_group, kk, 0.0).astype(BF16))
        pr, pi = _cmul(pr, pi, ar, ai)
        zr, zi = _cmul(c_re, c_im, pr, pi)
        for ri, z in enumerate((zr, -zi)):
            z = jnp.where(own_half, z, 0.0)
            for t in range(GROUPS_PER_TILE // 2):
                c0 = ri * STATE_PER_TILE + t * LANES
                cct_ref[0, j * LANES:(j + 1) * LANES, c0:c0 + LANES] = (
                    jnp.where(pair == t, z, 0.0).astype(BF16))
    a16_ref[0] = pr
    a16_ref[1] = pi
    zero = jnp.zeros((LANES, LANES), BF16)
    for d2 in range(SSM_FOLD // 2):
        top = jnp.concatenate([tk[2 * d2], tk[2 * d2 + 1]], axis=1)
        bot = jnp.concatenate([tk[2 * d2 - 1] if d2 else zero, tk[2 * d2]], axis=1)
        bt_ref[0, d2] = jnp.concatenate([top, bot], axis=0)


def s5_prepare(a_re, a_im, log_dt, b_re, b_im, c_re, c_im):
    g, p, c = SSM_GROUPS, SSM_STATE, SSM_GROUP
    fold, nt = SSM_FOLD, N_LANE_TILES
    rows = g * c
    twice = lambda x: jnp.tile(x, (1, 2))
    lam = lambda x: twice(jnp.repeat(x, c, axis=0))
    in_spec = pl.BlockSpec((LANES, LANES), lambda t: (t, 0))
    w_spec = pl.BlockSpec((fold, LANES, LANES), lambda t: (0, t, 0))
    w_shape = jax.ShapeDtypeStruct((fold, rows, LANES), F32)
    w_re, w_im, bt, cct, a16 = pl.pallas_call(
        _s5_prep_kernel,
        grid=(nt,),
        in_specs=[in_spec, in_spec, pl.BlockSpec((LANES, 1), lambda t: (t, 0))] + [in_spec] * 4,
        out_specs=[w_spec, w_spec,
                   pl.BlockSpec((1, fold // 2, 2 * LANES, 2 * LANES), lambda t: (t, 0, 0, 0)),
                   pl.BlockSpec((1, fold * LANES, 2 * STATE_PER_TILE), lambda t: (t, 0, 0)),
                   pl.BlockSpec((2, LANES, LANES), lambda t: (0, t, 0))],
        out_shape=[w_shape, w_shape,
                   jax.ShapeDtypeStruct((nt, fold // 2, 2 * LANES, 2 * LANES), BF16),
                   jax.ShapeDtypeStruct((nt, fold * LANES, 2 * STATE_PER_TILE), BF16),
                   jax.ShapeDtypeStruct((2, rows, LANES), F32)],
        compiler_params=_cparams("parallel"),
        name="s5_prep",
    )(lam(a_re), lam(a_im), jnp.repeat(log_dt, c).reshape(rows, 1),
      twice(b_re.transpose(0, 2, 1).reshape(rows, p)), twice(b_im.transpose(0, 2, 1).reshape(rows, p)),
      twice(c_re.reshape(rows, p)), twice(c_im.reshape(rows, p)))
    a16 = a16[:, ::c, :p].reshape(2, 1, N_STATE)
    return bt, cct, w_re, w_im, a16


FOLD_BLOCK = SSM_FOLD * SSM_FOLD


def _fold_perm():
    idx = np.arange(FOLD_BLOCK)
    perm = np.zeros((FOLD_BLOCK, FOLD_BLOCK), np.float32)
    perm[(idx % SSM_FOLD) * SSM_FOLD + idx // SSM_FOLD, idx] = 1.0
    return jnp.asarray(perm, BF16)


def _s5_fold_kernel(x_ref, g_ref, perm_ref, u_ref):
    for blk in range(x_ref.shape[0] // FOLD_BLOCK):
        x = x_ref[blk * FOLD_BLOCK:(blk + 1) * FOLD_BLOCK, :]
        h = (x * _rms_scale(x) * g_ref[...]).astype(BF16)
        f = _dot(perm_ref[...], h).astype(BF16)
        for m in range(SSM_FOLD):
            for k in range(N_LANE_TILES):
                u_ref[m, k, blk * SSM_FOLD:(blk + 1) * SSM_FOLD, :] = (
                    f[m * SSM_FOLD:(m + 1) * SSM_FOLD, k * LANES:(k + 1) * LANES])


def s5_fold(x, g, perm, tokens):
    t, d = x.shape
    rows = t // SSM_FOLD
    return pl.pallas_call(
        _s5_fold_kernel,
        grid=(t // tokens,),
        in_specs=[pl.BlockSpec((tokens, d), lambda i: (i, 0)), _const_spec((1, d)),
                  _const_spec((FOLD_BLOCK, FOLD_BLOCK))],
        out_specs=pl.BlockSpec((SSM_FOLD, N_LANE_TILES, tokens // SSM_FOLD, LANES),
                               lambda i: (0, 0, i, 0)),
        out_shape=jax.ShapeDtypeStruct((SSM_FOLD, N_LANE_TILES, rows, LANES), BF16),
        compiler_params=_cparams("parallel"),
        name="s5_fold",
    )(x, g.reshape(1, d), perm)


def _s5_state_in_kernel(u_ref, w_re_ref, w_im_ref, vre_ref, vim_ref, wd_ref):
    @pl.when(pl.program_id(1) == 0)
    def _():
        pair = lax.broadcasted_iota(jnp.int32, (LANES, 1), 0) // SSM_GROUP // 2
        for m in range(SSM_FOLD):
            for ri, ref in enumerate((w_re_ref, w_im_ref)):
                w = ref[m]
                for t in range(GROUPS_PER_TILE // 2):
                    c0 = ri * STATE_PER_TILE + t * LANES
                    wd_ref[m * LANES:(m + 1) * LANES, c0:c0 + LANES] = (
                        jnp.where(pair == t, w, 0.0).astype(BF16))

    lhs = jnp.concatenate([u_ref[m, 0] for m in range(SSM_FOLD)], axis=1)
    v = _dot(lhs, wd_ref[...])
    vre_ref[...] = v[:, :STATE_PER_TILE]
    vim_ref[...] = v[:, STATE_PER_TILE:]


def _s5_tile_specs(rt):
    u_spec = pl.BlockSpec((SSM_FOLD, 1, rt, LANES), lambda t, i: (0, t, i, 0))
    s_spec = pl.BlockSpec((rt, STATE_PER_TILE), lambda t, i: (i, t))
    return u_spec, s_spec


def s5_state_in(u, w_re, w_im, rt):
    rows = u.shape[2]
    u_spec, s_spec = _s5_tile_specs(rt)
    out = jax.ShapeDtypeStruct((rows, N_STATE), F32)
    w_spec = pl.BlockSpec((SSM_FOLD, LANES, LANES), lambda t, i: (0, t, 0))
    return pl.pallas_call(
        _s5_state_in_kernel,
        grid=(N_LANE_TILES, rows // rt),
        in_specs=[u_spec, w_spec, w_spec],
        out_specs=[s_spec, s_spec],
        out_shape=[out, out],
        scratch_shapes=[pltpu.VMEM((SSM_FOLD * LANES, 2 * STATE_PER_TILE), BF16)],
        compiler_params=_cparams("parallel", "arbitrary"),
        name="s5_state_in",
    )(u, w_re, w_im)


def _s5_scan_pairs_kernel(vre_ref, vim_ref, a_ref, s0re_ref, s0im_ref,
                          sre_ref, sim_ref, fre_ref, fim_ref):
    ar, ai = a_ref[0], a_ref[1]
    s0r, s0i = s0re_ref[...], s0im_ref[...]
    vr, vi = vre_ref[...], vim_ref[...]
    first = lax.broadcasted_iota(jnp.int32, (s0r.shape[0], 1), 0) % 2 == 0
    tr, ti = ar * s0r - ai * s0i + vr, ar * s0i + ai * s0r + vi
    sr = jnp.where(first, s0r, pltpu.roll(tr, 1, 0))
    si = jnp.where(first, s0i, pltpu.roll(ti, 1, 0))
    sre_ref[...] = sr
    sim_ref[...] = si
    fre_ref[...] = ar * sr - ai * si + vr
    fim_ref[...] = ar * si + ai * sr + vi


def _s5_scan_stream_kernel(vre_ref, vim_ref, a_ref, s0re_ref, s0im_ref,
                           sre_ref, sim_ref, fre_ref, fim_ref, *, steps):
    q = lax.broadcasted_iota(jnp.int32, (SUBLANES, 1), 0)
    a1 = (a_ref[0], a_ref[1])
    a2 = _cmul(*a1, *a1)
    a4 = _cmul(*a2, *a2)
    a8 = _cmul(*a4, *a4)
    strides = ((1, a1), (2, a2), (4, a4))
    width = a1[0].shape[1]
    pw = (jnp.ones((SUBLANES, width), F32), jnp.zeros((SUBLANES, width), F32))
    masked = []
    for k, ak in strides:
        nxt = _cmul(*pw, *ak)
        bit = (q & k) != 0
        pw = (jnp.where(bit, nxt[0], pw[0]), jnp.where(bit, nxt[1], pw[1]))
        masked.append((k, jnp.where(q >= k, ak[0], 0.0), jnp.where(q >= k, ak[1], 0.0)))

    def shifted(x, k):
        return jnp.where(q >= k, pltpu.roll(x, k, 0), 0.0)

    def body(t, carry):
        cr, ci = carry
        rows = pl.ds(pl.multiple_of(t * SUBLANES, SUBLANES), SUBLANES)
        pr, pi = vre_ref[rows, :], vim_ref[rows, :]
        for k, mr, mi in masked:
            dr, di = _cmul(pltpu.roll(pr, k, 0), pltpu.roll(pi, k, 0), mr, mi)
            pr, pi = pr + dr, pi + di
        xr, xi = _cmul(*pw, cr, ci)
        sre_ref[rows, :] = xr + shifted(pr, 1)
        sim_ref[rows, :] = xi + shifted(pi, 1)
        nr, ni = _cmul(*a8, cr, ci)
        return nr + pr[SUBLANES - 1:], ni + pi[SUBLANES - 1:]

    cr, ci = lax.fori_loop(0, steps // SUBLANES, body, (s0re_ref[0], s0im_ref[0]))
    fre_ref[0] = cr
    fim_ref[0] = ci


def s5_scan(v_re, v_im, a16, s0_re, s0_im, width):
    rows, n = v_re.shape
    streams = s0_re.shape[0]
    steps = rows // streams
    a_spec = pl.BlockSpec((2, 1, width), lambda *i: (0, 0, i[-1]))
    vout = jax.ShapeDtypeStruct((rows, n), F32)
    long_streams = steps % SUBLANES == 0
    if long_streams:
        kern = functools.partial(_s5_scan_stream_kernel, steps=steps)
        grid = (streams, n // width)
        vspec = pl.BlockSpec((steps, width), lambda b, c: (b, c))
        sspec = pl.BlockSpec((1, 1, width), lambda b, c: (b, 0, c))
        s0_re, s0_im = s0_re.reshape(streams, 1, n), s0_im.reshape(streams, 1, n)
        sout = jax.ShapeDtypeStruct((streams, 1, n), F32)
    else:
        assert steps == 2 and rows % SUBLANES == 0
        kern = _s5_scan_pairs_kernel
        grid = (n // width,)
        vspec = pl.BlockSpec((rows, width), lambda c: (0, c))
        sspec = vspec
        s0_re, s0_im = jnp.repeat(s0_re, steps, axis=0), jnp.repeat(s0_im, steps, axis=0)
        sout = vout
    s_re, s_im, f_re, f_im = pl.pallas_call(
        kern,
        grid=grid,
        in_specs=[vspec, vspec, a_spec, sspec, sspec],
        out_specs=[vspec, vspec, sspec, sspec],
        out_shape=[vout, vout, sout, sout],
        compiler_params=_cparams(*(["parallel"] * len(grid))),
        name="s5_scan",
    )(v_re, v_im, a16, s0_re, s0_im)
    if long_streams:
        return s_re, s_im, f_re.reshape(streams, n), f_im.reshape(streams, n)
    return s_re, s_im, f_re[steps - 1::steps], f_im[steps - 1::steps]


def _s5_toeplitz_kernel(u_ref, sre_ref, sim_ref, bt_ref, cct_ref, y_ref):
    npair = SSM_FOLD // 2
    u2 = [jnp.concatenate([u_ref[2 * mm, 0], u_ref[2 * mm + 1, 0]], axis=1) for mm in range(npair)]
    s_prev = jnp.concatenate([sre_ref[...], sim_ref[...]], axis=1).astype(BF16)
    for ll in range(npair):
        acc = _dot_nt(s_prev, cct_ref[0, ll * 2 * LANES:(ll + 1) * 2 * LANES, :])
        for mm in range(ll + 1):
            acc = acc + _dot(u2[mm], bt_ref[0, ll - mm])
        y_ref[2 * ll, 0] = acc[:, :LANES]
        y_ref[2 * ll + 1, 0] = acc[:, LANES:]


def s5_toeplitz(u, s_re, s_im, bt, cct, rt):
    rows = u.shape[2]
    u_spec, s_spec = _s5_tile_specs(rt)
    return pl.pallas_call(
        _s5_toeplitz_kernel,
        grid=(N_LANE_TILES, rows // rt),
        in_specs=[u_spec, s_spec, s_spec,
                  pl.BlockSpec((1, SSM_FOLD // 2, 2 * LANES, 2 * LANES), lambda t, i: (t, 0, 0, 0)),
                  pl.BlockSpec((1, SSM_FOLD * LANES, 2 * STATE_PER_TILE), lambda t, i: (t, 0, 0))],
        out_specs=u_spec,
        out_shape=jax.ShapeDtypeStruct(u.shape, F32),
        compiler_params=_cparams("parallel", "parallel"),
        name="s5_toeplitz",
    )(u, s_re, s_im, bt, cct)


def _s5_glu_kernel(x_ref, u_ref, y_ref, d_ref, wglu_ref, bglu_ref, perm_ref, o_ref):
    nblk = x_ref.shape[0] // FOLD_BLOCK

    def gelu_block(blk):
        rows = slice(blk * SSM_FOLD, (blk + 1) * SSM_FOLD)
        gl = []
        for m in range(SSM_FOLD):
            ym = jnp.concatenate([y_ref[m, k, rows, :] for k in range(N_LANE_TILES)], axis=1)
            hm = jnp.concatenate([u_ref[m, k, rows, :] for k in range(N_LANE_TILES)], axis=1)
            gl.append(jax.nn.gelu(ym + d_ref[...] * hm.astype(F32)))
        gl = jnp.concatenate(gl, axis=0)
        return gl, _dot(gl.astype(BF16), wglu_ref[...])

    def finish(blk, gl, z):
        tok = slice(blk * FOLD_BLOCK, (blk + 1) * FOLD_BLOCK)
        hi, lo = _split_bf16(gl * jax.nn.sigmoid(z + bglu_ref[...]))
        o_ref[tok, :] = x_ref[tok, :] + (_dot(perm_ref[...], hi) + _dot(perm_ref[...], lo))

    prev = gelu_block(0)
    for blk in range(1, nblk):
        cur = gelu_block(blk)
        finish(blk - 1, *prev)
        prev = cur
    finish(nblk - 1, *prev)


def s5_glu(x, u, y, d, w_glu, b_glu, perm, tokens):
    t, dm = x.shape
    fold_spec = pl.BlockSpec((SSM_FOLD, N_LANE_TILES, tokens // SSM_FOLD, LANES),
                             lambda i: (0, 0, i, 0))
    x_spec = pl.BlockSpec((tokens, dm), lambda i: (i, 0))
    return pl.pallas_call(
        _s5_glu_kernel,
        grid=(t // tokens,),
        in_specs=[x_spec, fold_spec, fold_spec, _const_spec((1, dm)), _weight_spec(w_glu),
                  _const_spec((1, dm)), _const_spec((FOLD_BLOCK, FOLD_BLOCK))],
        out_specs=x_spec,
        out_shape=jax.ShapeDtypeStruct((t, dm), F32),
        compiler_params=_cparams("parallel"),
        name="s5_glu",
    )(x, u, y, d.reshape(1, dm), w_glu[0], b_glu.reshape(1, dm), perm)


def s5_mixer(x, st_re, st_im, g, ops, d, w_glu, b_glu):
    bt, cct, w_re, w_im, a16 = ops
    b, l, dm = x.shape
    rows = b * l // SSM_FOLD
    x2 = x.reshape(b * l, dm)
    perm = _fold_perm()
    tokens = min(S5_TOKEN_TILE, b * l)
    rt = min(S5_ROW_TILE, rows)
    u = s5_fold(x2, g, perm, tokens)
    v_re, v_im = s5_state_in(u, w_re, w_im, rt)
    s_re, s_im, f_re, f_im = s5_scan(v_re, v_im, a16, st_re.reshape(b, N_STATE),
                                     st_im.reshape(b, N_STATE), S5_SCAN_WIDTH)
    y = s5_toeplitz(u, s_re, s_im, bt, cct, rt)
    out = s5_glu(x2, u, y, d, w_glu, b_glu, perm, tokens)
    shape = (b, SSM_GROUPS, SSM_STATE)
    return out.reshape(b, l, dm), f_re.reshape(shape), f_im.reshape(shape)


def _stream_tiling(seq_len, n_streams, tile_rows):
    if seq_len >= tile_rows:
        return 1, tile_rows
    return min(n_streams, 256 // seq_len), seq_len


def kernel(x_prompt, x_sample, mem_prompt, state_ssm_re, state_ssm_im, state_ret, cache_mem_k, cache_mem_v, cache_conv, norm_mix, norm_mem_q, norm_mem_kv, norm_ffn, norm_final, ssm_a_re, ssm_a_im, ssm_log_dt, ssm_b_re, ssm_b_im, ssm_c_re, ssm_c_im, ssm_d, ssm_w_glu, ssm_b_glu, ret_w_qkvg, ret_w_o, mem_w_q, mem_w_kv, mem_w_o, ffn_w_up, ffn_conv_w, ffn_conv_b, ffn_w_down):
    bp, _, dm = x_prompt.shape
    depth = norm_mix.shape[0]
    d_ff = ffn_w_down.shape[1]
    streams = [dict(x=x, b=x.shape[0], l=x.shape[1],
                    attn_tiling=_stream_tiling(x.shape[1], x.shape[0], ATTN_TILE_ROWS),
                    ffn_tiling=_stream_tiling(x.shape[1], x.shape[0], FFN_TILE_ROWS))
               for x in (x_prompt, x_sample)]
    cache_k, cache_v = _kv_rows(cache_mem_k), _kv_rows(cache_mem_v)
    ssm_w_glu, ret_w_qkvg, ret_w_o, mem_w_q, mem_w_kv, mem_w_o, ffn_w_up, ffn_w_down = (
        w.astype(BF16) for w in (ssm_w_glu, ret_w_qkvg, ret_w_o, mem_w_q, mem_w_kv, mem_w_o,
                                 ffn_w_up, ffn_w_down))
    mem_k, mem_v = mem_kv(mem_prompt, norm_mem_kv, mem_w_kv)
    outs = dict(ssm_re=([], []), ssm_im=([], []), ret=([], []), conv=([], []))

    for i in range(depth):
        j = i // 2
        if i % 2 == 0:
            ops = s5_prepare(ssm_a_re[j], ssm_a_im[j], ssm_log_dt[j], ssm_b_re[j], ssm_b_im[j],
                             ssm_c_re[j], ssm_c_im[j])
            w_glu = (ssm_w_glu, j)
            zeros = jnp.zeros((bp, SSM_GROUPS, SSM_STATE), F32)
            states = [(zeros, zeros), (state_ssm_re[j], state_ssm_im[j])]
            for si, (s, (st_re, st_im)) in enumerate(zip(streams, states)):
                s["x"], f_re, f_im = s5_mixer(s["x"], st_re, st_im, norm_mix[i], ops, ssm_d[j],
                                              w_glu, ssm_b_glu[j])
                outs["ssm_re"][si].append(f_re)
                outs["ssm_im"][si].append(f_im)
        else:
            w_qkvg = (ret_w_qkvg, j)
            w_o = (ret_w_o, j)
            zeros = jnp.zeros((bp, RET_HEADS, RET_DK, RET_DV), F32)
            chunk = RET_CHUNK_ROWS
            for si, (s, st, pos0) in enumerate(zip(streams, [zeros, state_ret[j]], [0, PAST_LEN])):
                x2 = s["x"].reshape(s["b"] * s["l"], dm)
                cos, sin = _rope_tables(pos0, s["l"])
                if s["l"] >= chunk:
                    y, st_new = ret_mixer(x2, norm_mix[i], w_qkvg, w_o, cos, sin, st, s["l"], chunk)
                else:
                    p = rms_matmul(x2, norm_mix[i], w_qkvg, BF16)
                    y, st_new = ret_core(p, cos, sin, st, s["l"], s["l"])
                    y = matmul_res(y, w_o, x2)
                s["x"] = y.reshape(s["x"].shape)
                outs["ret"][si].append(st_new)

        w_q = (mem_w_q, i)
        w_o = (mem_w_o, i)
        for s, (k_, v_) in zip(streams, [(mem_k, mem_v), (cache_k, cache_v)]):
            x2 = s["x"].reshape(s["b"] * s["l"], dm)
            y = mem_attn(x2, norm_mem_q[i], w_q, k_, v_, i, w_o, s["l"], *s["attn_tiling"])
            s["x"] = y.reshape(s["x"].shape)

        w_up = (ffn_w_up, i)
        w_dn = (ffn_w_down, i)
        g_final = norm_final if i == depth - 1 else None
        conv_states = [jnp.zeros((bp, 2, d_ff), F32), cache_conv[i]]
        for si, (s, cst) in enumerate(zip(streams, conv_states)):
            x2 = s["x"].reshape(s["b"] * s["l"], dm)
            y, cnew = conv_ffn(x2, norm_ffn[i], w_up, ffn_conv_w[i], ffn_conv_b[i], w_dn, cst,
                               s["l"], *s["ffn_tiling"], g_final=g_final)
            s["x"] = y.reshape(s["x"].shape)
            outs["conv"][si].append(cnew)

    return (streams[0]["x"], streams[1]["x"],
            jnp.stack(outs["ssm_re"][0]), jnp.stack(outs["ssm_im"][0]),
            jnp.stack(outs["ssm_re"][1]), jnp.stack(outs["ssm_im"][1]),
            jnp.stack(outs["ret"][0]), jnp.stack(outs["ret"][1]),
            _kv_unrows(mem_k, MEM_HEADS), _kv_unrows(mem_v, MEM_HEADS),
            jnp.stack(outs["conv"][0]), jnp.stack(outs["conv"][1]))
```

```python
import functools
import math

import numpy as np
import jax
import jax.numpy as jnp
from jax import lax
from jax.experimental import pallas as pl
from jax.experimental.pallas import tpu as pltpu

F32 = jnp.float32
BF16 = jnp.bfloat16

D_MODEL = 1024
PAST_LEN = 1024
EPS = 1e-6
GN_EPS = 1e-5
ROPE_BASE = 10000.0
LANES = 128
SUBLANES = 8
SSM_GROUP = 16
SSM_GROUPS = D_MODEL // SSM_GROUP
SSM_STATE = 64
SSM_FOLD = 16
GROUPS_PER_TILE = LANES // SSM_GROUP
N_LANE_TILES = D_MODEL // LANES
STATE_PER_TILE = GROUPS_PER_TILE * SSM_STATE
N_STATE = SSM_GROUPS * SSM_STATE
RET_HEADS = 4
RET_DK = D_MODEL // RET_HEADS
RET_DV = 2 * D_MODEL // RET_HEADS
RET_QK = RET_HEADS * RET_DK
RET_V = RET_HEADS * RET_DV
RET_LOG_G = tuple(math.log(1.0 - 2.0 ** (-5.0 - h)) for h in range(RET_HEADS))
MEM_HEADS = 4
MEM_HD = D_MODEL // MEM_HEADS
VMEM_LIMIT = 56 * 1024 * 1024
ATTN_TILE_ROWS = 1024
FFN_TILE_ROWS = 256
SHORT_STREAM_TILE_ROWS = 256
RET_CHUNK_ROWS = 256
RET_STREAMS_PER_STEP = 4
S5_TOKEN_TILE = 1024
S5_ROW_TILE = 512
S5_SCAN_WIDTH = 1024


def _cparams(*sem):
    return pltpu.CompilerParams(dimension_semantics=sem, vmem_limit_bytes=VMEM_LIMIT)


def _const_spec(shape):
    nd = len(shape)
    return pl.BlockSpec(shape, lambda *_: (0,) * nd, pipeline_mode=pl.Buffered(1))


def _weight_spec(w):
    stack, layer = w
    return pl.BlockSpec((None,) + stack.shape[1:], lambda *_: (layer, 0, 0),
                        pipeline_mode=pl.Buffered(1))


def _rms_scale(x):
    return lax.rsqrt(jnp.mean(x * x, axis=-1, keepdims=True) + EPS)


def _prenorm(x, g_ref):
    return (x * g_ref[...]).astype(BF16), _rms_scale(x)


def _split_bf16(x):
    hi = x.astype(BF16)
    lo = (x - hi.astype(F32)).astype(BF16)
    return hi, lo


def _dot(a, b):
    return jnp.dot(a, b, preferred_element_type=F32)


def _dot_nt(a, b):
    return lax.dot_general(a, b, (((1,), (1,)), ((), ())), preferred_element_type=F32)


def _rms_matmul_kernel(x_ref, g_ref, w_ref, o_ref, *, tn):
    x = x_ref[...]
    h = (x * _rms_scale(x) * g_ref[...]).astype(BF16)
    for n0 in range(0, w_ref.shape[1], tn):
        o_ref[:, n0:n0 + tn] = _dot(h, w_ref[:, n0:n0 + tn]).astype(o_ref.dtype)


def rms_matmul(x, g, w, out_dtype, tm=512, tn=512):
    t, d = x.shape
    n = w[0].shape[2]
    tm = min(tm, t)
    return pl.pallas_call(
        functools.partial(_rms_matmul_kernel, tn=tn),
        grid=(t // tm,),
        in_specs=[pl.BlockSpec((tm, d), lambda i: (i, 0)),
                  _const_spec((1, d)),
                  _weight_spec(w)],
        out_specs=pl.BlockSpec((tm, n), lambda i: (i, 0)),
        out_shape=jax.ShapeDtypeStruct((t, n), out_dtype),
        compiler_params=_cparams("parallel"),
        name="rms_matmul",
    )(x, g.reshape(1, d), w[0])


def _matmul_res_kernel(y_ref, w_ref, x_ref, o_ref):
    o_ref[...] = x_ref[...] + _dot(y_ref[...], w_ref[...])


def matmul_res(y, w, x, tm=512):
    t, k = y.shape
    d = w[0].shape[2]
    tm = min(tm, t)
    return pl.pallas_call(
        _matmul_res_kernel,
        grid=(t // tm,),
        in_specs=[pl.BlockSpec((tm, k), lambda i: (i, 0)),
                  _weight_spec(w),
                  pl.BlockSpec((tm, d), lambda i: (i, 0))],
        out_specs=pl.BlockSpec((tm, d), lambda i: (i, 0)),
        out_shape=jax.ShapeDtypeStruct((t, d), F32),
        compiler_params=_cparams("parallel"),
        name="matmul_res",
    )(y, w[0], x)


MEM_LANE_TILES = MEM_HD // LANES


def _kv_rows(x):
    *lead, n, h, hd = x.shape
    x = x.reshape(*lead, n, h, MEM_LANE_TILES, LANES)
    return jnp.swapaxes(x, -3, -2).reshape(*lead, n * MEM_LANE_TILES * h, LANES)


def _kv_unrows(x, heads):
    *lead, rows, _ = x.shape
    n = rows // (MEM_LANE_TILES * heads)
    x = x.reshape(*lead, n, MEM_LANE_TILES, heads, LANES)
    return jnp.swapaxes(x, -3, -2).reshape(*lead, n, heads, MEM_LANE_TILES * LANES)


def _mem_kv_kernel(x_ref, g_ref, w_ref, k_ref, v_ref):
    x = x_ref[...]
    h = (x * _rms_scale(x) * g_ref[...]).astype(BF16)
    n_mem, d = x.shape
    stride = MEM_LANE_TILES * MEM_HEADS
    for which, ref in enumerate((k_ref, v_ref)):
        for hd in range(MEM_HEADS):
            c0 = which * d + hd * MEM_HD
            res = _dot(h, w_ref[:, c0:c0 + MEM_HD])
            for j in range(MEM_LANE_TILES):
                ref[pl.ds(j * MEM_HEADS + hd, n_mem, stride=stride), :] = (
                    res[:, j * LANES:(j + 1) * LANES])


def mem_kv(mem, g, w_kv):
    b, n_mem, d = mem.shape
    depth = w_kv.shape[0]
    rows = n_mem * MEM_LANE_TILES * MEM_HEADS
    out_spec = pl.BlockSpec((None, None, rows, LANES), lambda l, i: (l, i, 0, 0))
    out = jax.ShapeDtypeStruct((depth, b, rows, LANES), F32)
    return pl.pallas_call(
        _mem_kv_kernel,
        grid=(depth, b),
        in_specs=[pl.BlockSpec((None, n_mem, d), lambda l, i: (i, 0, 0)),
                  pl.BlockSpec((None, 1, d), lambda l, i: (l, 0, 0)),
                  pl.BlockSpec((None,) + w_kv.shape[1:], lambda l, i: (l, 0, 0))],
        out_specs=[out_spec, out_spec],
        out_shape=[out, out],
        compiler_params=_cparams("parallel", "parallel"),
        name="mem_kv",
    )(mem, g.reshape(depth, 1, d), w_kv)


def _head_rows(kv_ref, s, hd):
    stride = MEM_LANE_TILES * MEM_HEADS
    n_mem = kv_ref.shape[2] // stride
    return jnp.concatenate(
        [kv_ref[0, s, pl.ds(j * MEM_HEADS + hd, n_mem, stride=stride), :]
         for j in range(MEM_LANE_TILES)], axis=1).astype(BF16)


def _mem_attn_kernel(x_ref, g_ref, wq_ref, k_ref, v_ref, wo_ref, o_ref, *, nseg, seg, head_group):
    scale = MEM_HD ** -0.5
    x = x_ref[...]
    h, row_scale = _prenorm(x, g_ref)
    o_ref[...] = x

    width = head_group * MEM_HD

    def q_proj(grp):
        return (_dot(h, wq_ref[:, grp * width:(grp + 1) * width]) * row_scale).astype(BF16)

    nxt = q_proj(0)
    pending = None
    for grp in range(MEM_HEADS // head_group):
        q = nxt
        pairs = [(grp * head_group + hg, hg, s) for hg in range(head_group) for s in range(nseg)]
        scores = [_dot_nt(q[s * seg:(s + 1) * seg, hg * MEM_HD:(hg + 1) * MEM_HD],
                          _head_rows(k_ref, s, hd)) for hd, hg, s in pairs]
        if (grp + 1) * head_group < MEM_HEADS:
            nxt = q_proj(grp + 1)
        if pending is not None:
            o_ref[...] += _dot(pending, wo_ref[(grp - 1) * width:grp * width, :])
        outs = []
        for (hd, hg, s), sc in zip(pairs, scores):
            sc = sc * scale
            p = jnp.exp(sc - jnp.max(sc, axis=-1, keepdims=True))
            denom = jnp.sum(p, axis=-1, keepdims=True)
            outs.append((_dot(p.astype(BF16), _head_rows(v_ref, s, hd)) / denom).astype(BF16))
        heads = [outs[hg * nseg] if nseg == 1 else jnp.concatenate(outs[hg * nseg:(hg + 1) * nseg], axis=0)
                 for hg in range(head_group)]
        pending = heads[0] if head_group == 1 else jnp.concatenate(heads, axis=1)
    o_ref[...] += _dot(pending, wo_ref[MEM_HEADS * MEM_HD - width:, :])


def mem_attn(x, g, w_q, mk, mv, layer, w_o, seq_len, nseg, seg):
    t, d = x.shape
    b = mk.shape[1]
    tiles_per_stream = max(seq_len // (nseg * seg), 1)
    n_outer = b // nseg
    tm = nseg * seg
    x_spec = pl.BlockSpec((tm, d), lambda o, i: (o * tiles_per_stream + i, 0))
    kv_spec = pl.BlockSpec((1, nseg) + mk.shape[2:], lambda o, i: (layer, o, 0, 0))
    return pl.pallas_call(
        functools.partial(_mem_attn_kernel, nseg=nseg, seg=seg, head_group=2 if nseg == 1 else 1),
        grid=(n_outer, tiles_per_stream),
        in_specs=[x_spec, _const_spec((1, d)), _weight_spec(w_q), kv_spec, kv_spec,
                  _weight_spec(w_o)],
        out_specs=x_spec,
        out_shape=jax.ShapeDtypeStruct((t, d), F32),
        compiler_params=_cparams("parallel", "parallel"),
        name="mem_attn",
    )(x, g.reshape(1, d), w_q[0], mk, mv, w_o[0])


def _ffn_kernel(x_ref, g_ref, wup_ref, cw_ref, cb_ref, wdn_ref, st_ref, *rest,
                nseg, seg, fc, final_norm):
    if final_norm:
        gf_ref, o_ref, nst_ref, carry_ref = rest
    else:
        o_ref, nst_ref, carry_ref = rest
    d_ff = wdn_ref.shape[0]
    tm = nseg * seg

    @pl.when(pl.program_id(1) == 0)
    def _():
        carry_ref[...] = st_ref[...]

    x = x_ref[...]
    h, scale = _prenorm(x, g_ref)
    row = lax.broadcasted_iota(jnp.int32, (tm, 1), 0) % seg
    o_ref[...] = x

    def up_proj(f0):
        return (_dot(h, wup_ref[:, f0:f0 + fc]) * scale,
                _dot(h, wup_ref[:, d_ff + f0:d_ff + f0 + fc]) * scale)

    nxt = up_proj(0)
    pending = None
    for f0 in range(0, d_ff, fc):
        cols = slice(f0, f0 + fc)
        a, gate = nxt
        if f0 + fc < d_ff:
            nxt = up_proj(f0 + fc)
        if pending is not None:
            o_ref[...] += _dot(pending, wdn_ref[f0 - fc:f0, :])
        p0 = jnp.broadcast_to(carry_ref[:, 0:1, cols], (nseg, seg, fc)).reshape(tm, fc)
        p1 = jnp.broadcast_to(carry_ref[:, 1:2, cols], (nseg, seg, fc)).reshape(tm, fc)
        a1 = jnp.where(row == 0, p1, pltpu.roll(a, 1, 0))
        a2 = jnp.where(row == 0, p0, jnp.where(row == 1, p1, pltpu.roll(a, 2, 0)))
        conv = (cw_ref[0:1, cols] * a2 + cw_ref[1:2, cols] * a1 + cw_ref[2:3, cols] * a
                + cb_ref[:, cols])
        p = jax.nn.gelu(conv) * gate
        pending = p.astype(BF16)
        carry_ref[:, :, cols] = a.reshape(nseg, seg, fc)[:, seg - 2:seg, :]
    y = o_ref[...] + _dot(pending, wdn_ref[d_ff - fc:d_ff, :])
    nst_ref[...] = carry_ref[...]
    if final_norm:
        y = y * _rms_scale(y) * gf_ref[...]
    o_ref[...] = y


def conv_ffn(x, g, w_up, conv_w, conv_b, w_down, conv_state, seq_len, nseg, seg, g_final=None,
             fc=256):
    t, d = x.shape
    d_ff = w_down[0].shape[1]
    b = conv_state.shape[0]
    tiles_per_stream = max(seq_len // (nseg * seg), 1)
    n_outer = b // nseg
    tm = nseg * seg
    final_norm = g_final is not None
    in_specs = [pl.BlockSpec((tm, d), lambda o, i: (o * tiles_per_stream + i, 0)),
                _const_spec((1, d)),
                _weight_spec(w_up),
                _const_spec((3, d_ff)),
                _const_spec((1, d_ff)),
                _weight_spec(w_down),
                pl.BlockSpec((nseg, 2, d_ff), lambda o, i: (o, 0, 0))]
    args = [x, g.reshape(1, d), w_up[0], conv_w, conv_b.reshape(1, d_ff), w_down[0], conv_state]
    if final_norm:
        in_specs.append(_const_spec((1, d)))
        args.append(g_final.reshape(1, d))
    return pl.pallas_call(
        functools.partial(_ffn_kernel, nseg=nseg, seg=seg, fc=fc, final_norm=final_norm),
        grid=(n_outer, tiles_per_stream),
        in_specs=in_specs,
        out_specs=[pl.BlockSpec((tm, d), lambda o, i: (o * tiles_per_stream + i, 0)),
                   pl.BlockSpec((nseg, 2, d_ff), lambda o, i: (o, 0, 0))],
        out_shape=[jax.ShapeDtypeStruct((t, d), F32),
                   jax.ShapeDtypeStruct((b, 2, d_ff), F32)],
        scratch_shapes=[pltpu.VMEM((nseg, 2, d_ff), F32)],
        compiler_params=_cparams("parallel", "arbitrary"),
        name="conv_ffn",
    )(*args)


def _rotate(x, cos, sin):
    half = x.shape[1] // 2
    x1, x2 = x[:, :half], x[:, half:]
    return jnp.concatenate([x1 * cos - x2 * sin, x2 * cos + x1 * sin], axis=1)


def _ret_head(h, q, k, v, g, cos, sin, s_ref, chunk, fillers=(), stream=0):
    fillers = list(fillers) + [lambda: None] * (3 - len(fillers))
    log_g = RET_LOG_G[h]
    li = lax.broadcasted_iota(jnp.int32, (chunk, chunk), 0)
    mi = lax.broadcasted_iota(jnp.int32, (chunk, chunk), 1)
    diff = (li - mi).astype(F32)
    pos = lax.broadcasted_iota(jnp.int32, (chunk, 1), 0).astype(F32)
    fillers[0]()
    qr = _rotate(q, cos, sin).astype(BF16)
    kr = _rotate(k, cos, sin) * (RET_DK ** -0.5)
    decay = jnp.where(diff >= 0, jnp.exp(jnp.maximum(diff, 0.0) * log_g), 0.0)
    scores = _dot_nt(qr, kr.astype(BF16))
    fillers[1]()
    scores = scores * decay
    intra = _dot(scores.astype(BF16), v)
    s_old = s_ref[stream, h]
    cross = _dot(qr, s_old.astype(BF16)) * jnp.exp((pos + 1.0) * log_g)
    o = intra + cross
    k_tail = (kr * jnp.exp((chunk - 1.0 - pos) * log_g)).astype(BF16)
    s_ref[stream, h] = math.exp(chunk * log_g) * s_old + lax.dot_general(
        k_tail, v, (((0,), (0,)), ((), ())), preferred_element_type=F32)
    fillers[2]()
    mu = jnp.mean(o, axis=-1, keepdims=True)
    oc = o - mu
    var = jnp.mean(oc * oc, axis=-1, keepdims=True)
    return jax.nn.silu(g) * (oc * lax.rsqrt(var + GN_EPS))


def _ret_kernel(p_ref, cos_ref, sin_ref, s_in_ref, y_ref, s_out_ref, *, chunk):
    s_out_ref[...] = s_in_ref[...]
    cos = cos_ref[...]
    sin = sin_ref[...]
    for s in range(s_in_ref.shape[0]):
        rows = slice(s * chunk, (s + 1) * chunk)
        for h in range(RET_HEADS):
            q = p_ref[rows, h * RET_DK:(h + 1) * RET_DK].astype(F32)
            k = p_ref[rows, RET_QK + h * RET_DK:RET_QK + (h + 1) * RET_DK].astype(F32)
            v = p_ref[rows, 2 * RET_QK + h * RET_DV:2 * RET_QK + (h + 1) * RET_DV]
            g = p_ref[rows, 2 * RET_QK + RET_V + h * RET_DV:
                      2 * RET_QK + RET_V + (h + 1) * RET_DV].astype(F32)
            y = _ret_head(h, q, k, v, g, cos, sin, s_out_ref, chunk, stream=s)
            y_ref[rows, h * RET_DV:(h + 1) * RET_DV] = y.astype(y_ref.dtype)


def _ret_fused_kernel(x_ref, g_ref, w_ref, wo_ref, cos_ref, sin_ref, s_in_ref, o_ref, s_out_ref,
                      *, chunk):
    @pl.when(pl.program_id(1) == 0)
    def _():
        s_out_ref[...] = s_in_ref[...]

    cos = cos_ref[...]
    sin = sin_ref[...]
    x = x_ref[...]
    hn, row_scale = _prenorm(x, g_ref)
    o_ref[...] = x

    offsets = (0, RET_QK, 2 * RET_QK, 2 * RET_QK + RET_V)
    widths = (RET_DK, RET_DK, RET_DV, RET_DV)

    def proj(h, part):
        c0 = offsets[part] + h * widths[part]
        y = _dot(hn, w_ref[:, c0:c0 + widths[part]]) * row_scale
        return y.astype(BF16) if part == 2 else y

    def out_proj(h, y):
        o_ref[...] += _dot(y, wo_ref[h * RET_DV:(h + 1) * RET_DV, :])

    cur = [proj(0, part) for part in range(4)]
    ys = []
    for h in range(RET_HEADS):
        nxt = [None] * 4

        def fill_qk(h=h, nxt=nxt):
            nxt[0], nxt[1] = proj(h + 1, 0), proj(h + 1, 1)

        def fill_v(h=h, nxt=nxt):
            nxt[2] = proj(h + 1, 2)

        def fill_g(h=h, nxt=nxt):
            nxt[3] = proj(h + 1, 3)

        if h + 1 < RET_HEADS:
            fillers = (fill_qk, fill_v, fill_g)
        else:
            fillers = [functools.partial(out_proj, i, y) for i, y in enumerate(ys)]
        ys.append(_ret_head(h, *cur, cos, sin, s_out_ref, chunk, fillers).astype(BF16))
        cur = nxt
    out_proj(RET_HEADS - 1, ys[-1])


def _ret_specs(seq_len, chunk):
    nchunks = seq_len // chunk
    half = RET_DK // 2
    row_spec = lambda n: pl.BlockSpec((chunk, n), lambda o, i: (o * nchunks + i, 0))
    rope_spec = pl.BlockSpec((chunk, half), lambda o, i: (i, 0))
    state_spec = pl.BlockSpec((1, RET_HEADS, RET_DK, RET_DV), lambda o, i: (o, 0, 0, 0))
    return nchunks, row_spec, rope_spec, state_spec


def ret_core(p, cos, sin, state, chunk, streams_per_step):
    t, n = p.shape
    b = state.shape[0]
    nb = streams_per_step
    rows_spec = lambda width: pl.BlockSpec((nb * chunk, width), lambda i: (i, 0))
    rope_spec = _const_spec(cos.shape)
    state_spec = pl.BlockSpec((nb,) + state.shape[1:], lambda i: (i, 0, 0, 0))
    return pl.pallas_call(
        functools.partial(_ret_kernel, chunk=chunk),
        grid=(b // nb,),
        in_specs=[rows_spec(n), rope_spec, rope_spec, state_spec],
        out_specs=[rows_spec(RET_V), state_spec],
        out_shape=[jax.ShapeDtypeStruct((t, RET_V), BF16),
                   jax.ShapeDtypeStruct(state.shape, F32)],
        compiler_params=_cparams("parallel"),
        name="ret_core",
    )(p, cos, sin, state)


def ret_mixer(x, g, w_qkvg, w_o, cos, sin, state, seq_len, chunk):
    t, d = x.shape
    nchunks, row_spec, rope_spec, state_spec = _ret_specs(seq_len, chunk)
    return pl.pallas_call(
        functools.partial(_ret_fused_kernel, chunk=chunk),
        grid=(state.shape[0], nchunks),
        in_specs=[row_spec(d), _const_spec((1, d)), _weight_spec(w_qkvg),
                  _weight_spec(w_o), rope_spec, rope_spec, state_spec],
        out_specs=[row_spec(d), state_spec],
        out_shape=[jax.ShapeDtypeStruct((t, d), F32), jax.ShapeDtypeStruct(state.shape, F32)],
        compiler_params=_cparams("parallel", "arbitrary"),
        name="ret_mixer",
    )(x, g.reshape(1, d), w_qkvg[0], w_o[0], cos, sin, state)


def _rope_tables(pos0, length):
    half = RET_DK // 2
    freqs = ROPE_BASE ** (-np.arange(half, dtype=np.float64) / half)
    ang = (pos0 + np.arange(length, dtype=np.float64))[:, None] * freqs[None, :]
    return jnp.asarray(np.cos(ang), F32), jnp.asarray(np.sin(ang), F32)


def _cmul(ar, ai, br, bi):
    return ar * br - ai * bi, ar * bi + ai * br


def _s5_prep_kernel(lam_re_ref, lam_im_ref, ldt_ref, b_re_ref, b_im_ref, c_re_ref, c_im_ref,
                    w_re_ref, w_im_ref, bt_ref, cct_ref, a16_ref):
    dt = jnp.exp(ldt_ref[...])
    lam_re, lam_im = lam_re_ref[...], lam_im_ref[...]
    mag = jnp.exp(lam_re * dt)
    ar, ai = mag * jnp.cos(lam_im * dt), mag * jnp.sin(lam_im * dt)
    den = lam_re * lam_re + lam_im * lam_im
    nr, ni = ar - 1.0, ai
    coef_re = (nr * lam_re + ni * lam_im) / den
    coef_im = (ni * lam_re - nr * lam_im) / den
    bb_re, bb_im = _cmul(coef_re, coef_im, b_re_ref[...], b_im_ref[...])
    c_re, c_im = c_re_ref[...], c_im_ref[...]
    lane = lax.broadcasted_iota(jnp.int32, (1, LANES), 1)
    low = lane < SSM_STATE
    rg = lax.broadcasted_iota(jnp.int32, (LANES, 1), 0) // SSM_GROUP
    pair = rg // 2
    own_half = (lane // SSM_STATE) == (rg % 2)
    same_group = rg == lane // SSM_GROUP
    c_hi, c_lo = _split_bf16(jnp.where(low, c_re, -c_im))
    tk = []
    pr, pi = jnp.ones_like(ar), jnp.zeros_like(ar)
    for j in range(SSM_FOLD):
        zr, zi = _cmul(pr, pi, bb_re, bb_im)
        w_re_ref[SSM_FOLD - 1 - j] = jnp.where(own_half, zr, 0.0)
        w_im_ref[SSM_FOLD - 1 - j] = jnp.where(own_half, zi, 0.0)
        b_hi, b_lo = _split_bf16(jnp.where(low, zr, zi))
        kk = _dot_nt(b_hi, c_hi) + _dot_nt(b_lo, c_hi) + _dot_nt(b_hi, c_lo)
        tk.append(jnp.where(same_group, kk, 0.0).astype(BF16))
        pr, pi = _cmul(pr, pi, ar, ai)
        zr, zi = _cmul(c_re, c_im, pr, pi)
        for ri, z in enumerate((zr, -zi)):
            z = jnp.where(own_half, z, 0.0)
            for t in range(GROUPS_PER_TILE // 2):
                c0 = ri * STATE_PER_TILE + t * LANES
                cct_ref[0, j * LANES:(j + 1) * LANES, c0:c0 + LANES] = (
                    jnp.where(pair == t, z, 0.0).astype(BF16))
    a16_ref[0] = pr
    a16_ref[1] = pi
    zero = jnp.zeros((LANES, LANES), BF16)
    for d2 in range(SSM_FOLD // 2):
        top = jnp.concatenate([tk[2 * d2], tk[2 * d2 + 1]], axis=1)
        bot = jnp.concatenate([tk[2 * d2 - 1] if d2 else zero, tk[2 * d2]], axis=1)
        bt_ref[0, d2] = jnp.concatenate([top, bot], axis=0)


def s5_prepare(a_re, a_im, log_dt, b_re, b_im, c_re, c_im):
    g, p, c = SSM_GROUPS, SSM_STATE, SSM_GROUP
    fold, nt = SSM_FOLD, N_LANE_TILES
    rows = g * c
    twice = lambda x: jnp.tile(x, (1, 2))
    lam = lambda x: twice(jnp.repeat(x, c, axis=0))
    in_spec = pl.BlockSpec((LANES, LANES), lambda t: (t, 0))
    w_spec = pl.BlockSpec((fold, LANES, LANES), lambda t: (0, t, 0))
    w_shape = jax.ShapeDtypeStruct((fold, rows, LANES), F32)
    w_re, w_im, bt, cct, a16 = pl.pallas_call(
        _s5_prep_kernel,
        grid=(nt,),
        in_specs=[in_spec, in_spec, pl.BlockSpec((LANES, 1), lambda t: (t, 0))] + [in_spec] * 4,
        out_specs=[w_spec, w_spec,
                   pl.BlockSpec((1, fold // 2, 2 * LANES, 2 * LANES), lambda t: (t, 0, 0, 0)),
                   pl.BlockSpec((1, fold * LANES, 2 * STATE_PER_TILE), lambda t: (t, 0, 0)),
                   pl.BlockSpec((2, LANES, LANES), lambda t: (0, t, 0))],
        out_shape=[w_shape, w_shape,
                   jax.ShapeDtypeStruct((nt, fold // 2, 2 * LANES, 2 * LANES), BF16),
                   jax.ShapeDtypeStruct((nt, fold * LANES, 2 * STATE_PER_TILE), BF16),
                   jax.ShapeDtypeStruct((2, rows, LANES), F32)],
        compiler_params=_cparams("parallel"),
        name="s5_prep",
    )(lam(a_re), lam(a_im), jnp.repeat(log_dt, c).reshape(rows, 1),
      twice(b_re.transpose(0, 2, 1).reshape(rows, p)), twice(b_im.transpose(0, 2, 1).reshape(rows, p)),
      twice(c_re.reshape(rows, p)), twice(c_im.reshape(rows, p)))
    a16 = a16[:, ::c, :p].reshape(2, 1, N_STATE)
    return bt, cct, w_re, w_im, a16


FOLD_BLOCK = SSM_FOLD * SSM_FOLD


def _fold_perm():
    idx = np.arange(FOLD_BLOCK)
    perm = np.zeros((FOLD_BLOCK, FOLD_BLOCK), np.float32)
    perm[(idx % SSM_FOLD) * SSM_FOLD + idx // SSM_FOLD, idx] = 1.0
    return jnp.asarray(perm, BF16)


def _s5_fold_kernel(x_ref, g_ref, perm_ref, u_ref):
    for blk in range(x_ref.shape[0] // FOLD_BLOCK):
        x = x_ref[blk * FOLD_BLOCK:(blk + 1) * FOLD_BLOCK, :]
        h = (x * _rms_scale(x) * g_ref[...]).astype(BF16)
        f = _dot(perm_ref[...], h).astype(BF16)
        for m in range(SSM_FOLD):
            for k in range(N_LANE_TILES):
                u_ref[m, k, blk * SSM_FOLD:(blk + 1) * SSM_FOLD, :] = (
                    f[m * SSM_FOLD:(m + 1) * SSM_FOLD, k * LANES:(k + 1) * LANES])


def s5_fold(x, g, perm, tokens):
    t, d = x.shape
    rows = t // SSM_FOLD
    return pl.pallas_call(
        _s5_fold_kernel,
        grid=(t // tokens,),
        in_specs=[pl.BlockSpec((tokens, d), lambda i: (i, 0)), _const_spec((1, d)),
                  _const_spec((FOLD_BLOCK, FOLD_BLOCK))],
        out_specs=pl.BlockSpec((SSM_FOLD, N_LANE_TILES, tokens // SSM_FOLD, LANES),
                               lambda i: (0, 0, i, 0)),
        out_shape=jax.ShapeDtypeStruct((SSM_FOLD, N_LANE_TILES, rows, LANES), BF16),
        compiler_params=_cparams("parallel"),
        name="s5_fold",
    )(x, g.reshape(1, d), perm)


def _s5_state_in_kernel(u_ref, w_re_ref, w_im_ref, vre_ref, vim_ref, wd_ref):
    @pl.when(pl.program_id(1) == 0)
    def _():
        pair = lax.broadcasted_iota(jnp.int32, (LANES, 1), 0) // SSM_GROUP // 2
        for m in range(SSM_FOLD):
            for ri, ref in enumerate((w_re_ref, w_im_ref)):
                w = ref[m]
                for t in range(GROUPS_PER_TILE // 2):
                    c0 = ri * STATE_PER_TILE + t * LANES
                    wd_ref[m * LANES:(m + 1) * LANES, c0:c0 + LANES] = (
                        jnp.where(pair == t, w, 0.0).astype(BF16))

    lhs = jnp.concatenate([u_ref[m, 0] for m in range(SSM_FOLD)], axis=1)
    v = _dot(lhs, wd_ref[...])
    vre_ref[...] = v[:, :STATE_PER_TILE]
    vim_ref[...] = v[:, STATE_PER_TILE:]


def _s5_tile_specs(rt):
    u_spec = pl.BlockSpec((SSM_FOLD, 1, rt, LANES), lambda t, i: (0, t, i, 0))
    s_spec = pl.BlockSpec((rt, STATE_PER_TILE), lambda t, i: (i, t))
    return u_spec, s_spec


def s5_state_in(u, w_re, w_im, rt):
    rows = u.shape[2]
    u_spec, s_spec = _s5_tile_specs(rt)
    out = jax.ShapeDtypeStruct((rows, N_STATE), F32)
    w_spec = pl.BlockSpec((SSM_FOLD, LANES, LANES), lambda t, i: (0, t, 0))
    return pl.pallas_call(
        _s5_state_in_kernel,
        grid=(N_LANE_TILES, rows // rt),
        in_specs=[u_spec, w_spec, w_spec],
        out_specs=[s_spec, s_spec],
        out_shape=[out, out],
        scratch_shapes=[pltpu.VMEM((SSM_FOLD * LANES, 2 * STATE_PER_TILE), BF16)],
        compiler_params=_cparams("parallel", "arbitrary"),
        name="s5_state_in",
    )(u, w_re, w_im)


def _s5_scan_pairs_kernel(vre_ref, vim_ref, a_ref, s0re_ref, s0im_ref,
                          sre_ref, sim_ref, fre_ref, fim_ref):
    ar, ai = a_ref[0], a_ref[1]
    s0r, s0i = s0re_ref[...], s0im_ref[...]
    vr, vi = vre_ref[...], vim_ref[...]
    first = lax.broadcasted_iota(jnp.int32, (s0r.shape[0], 1), 0) % 2 == 0
    tr, ti = ar * s0r - ai * s0i + vr, ar * s0i + ai * s0r + vi
    sr = jnp.where(first, s0r, pltpu.roll(tr, 1, 0))
    si = jnp.where(first, s0i, pltpu.roll(ti, 1, 0))
    sre_ref[...] = sr
    sim_ref[...] = si
    fre_ref[...] = ar * sr - ai * si + vr
    fim_ref[...] = ar * si + ai * sr + vi


def _s5_scan_stream_kernel(vre_ref, vim_ref, a_ref, s0re_ref, s0im_ref,
                           sre_ref, sim_ref, fre_ref, fim_ref, *, steps):
    q = lax.broadcasted_iota(jnp.int32, (SUBLANES, 1), 0)
    a1 = (a_ref[0], a_ref[1])
    a2 = _cmul(*a1, *a1)
    a4 = _cmul(*a2, *a2)
    a8 = _cmul(*a4, *a4)
    strides = ((1, a1), (2, a2), (4, a4))
    width = a1[0].shape[1]
    pw = (jnp.ones((SUBLANES, width), F32), jnp.zeros((SUBLANES, width), F32))
    masked = []
    for k, ak in strides:
        nxt = _cmul(*pw, *ak)
        bit = (q & k) != 0
        pw = (jnp.where(bit, nxt[0], pw[0]), jnp.where(bit, nxt[1], pw[1]))
        masked.append((k, jnp.where(q >= k, ak[0], 0.0), jnp.where(q >= k, ak[1], 0.0)))

    def shifted(x, k):
        return jnp.where(q >= k, pltpu.roll(x, k, 0), 0.0)

    def body(t, carry):
        cr, ci = carry
        rows = pl.ds(pl.multiple_of(t * SUBLANES, SUBLANES), SUBLANES)
        pr, pi = vre_ref[rows, :], vim_ref[rows, :]
        for k, mr, mi in masked:
            dr, di = _cmul(pltpu.roll(pr, k, 0), pltpu.roll(pi, k, 0), mr, mi)
            pr, pi = pr + dr, pi + di
        xr, xi = _cmul(*pw, cr, ci)
        sre_ref[rows, :] = xr + shifted(pr, 1)
        sim_ref[rows, :] = xi + shifted(pi, 1)
        nr, ni = _cmul(*a8, cr, ci)
        return nr + pr[SUBLANES - 1:], ni + pi[SUBLANES - 1:]

    cr, ci = lax.fori_loop(0, steps // SUBLANES, body, (s0re_ref[0], s0im_ref[0]))
    fre_ref[0] = cr
    fim_ref[0] = ci


def s5_scan(v_re, v_im, a16, s0_re, s0_im, width):
    rows, n = v_re.shape
    streams = s0_re.shape[0]
    steps = rows // streams
    a_spec = pl.BlockSpec((2, 1, width), lambda *i: (0, 0, i[-1]))
    vout = jax.ShapeDtypeStruct((rows, n), F32)
    long_streams = steps % SUBLANES == 0
    if long_streams:
        kern = functools.partial(_s5_scan_stream_kernel, steps=steps)
        grid = (streams, n // width)
        vspec = pl.BlockSpec((steps, width), lambda b, c: (b, c))
        sspec = pl.BlockSpec((1, 1, width), lambda b, c: (b, 0, c))
        s0_re, s0_im = s0_re.reshape(streams, 1, n), s0_im.reshape(streams, 1, n)
        sout = jax.ShapeDtypeStruct((streams, 1, n), F32)
    else:
        assert steps == 2 and rows % SUBLANES == 0
        kern = _s5_scan_pairs_kernel
        grid = (n // width,)
        vspec = pl.BlockSpec((rows, width), lambda c: (0, c))
        sspec = vspec
        s0_re, s0_im = jnp.repeat(s0_re, steps, axis=0), jnp.repeat(s0_im, steps, axis=0)
        sout = vout
    s_re, s_im, f_re, f_im = pl.pallas_call(
        kern,
        grid=grid,
        in_specs=[vspec, vspec, a_spec, sspec, sspec],
        out_specs=[vspec, vspec, sspec, sspec],
        out_shape=[vout, vout, sout, sout],
        compiler_params=_cparams(*(["parallel"] * len(grid))),
        name="s5_scan",
    )(v_re, v_im, a16, s0_re, s0_im)
    if long_streams:
        return s_re, s_im, f_re.reshape(streams, n), f_im.reshape(streams, n)
    return s_re, s_im, f_re[steps - 1::steps], f_im[steps - 1::steps]


def _s5_toeplitz_kernel(u_ref, sre_ref, sim_ref, bt_ref, cct_ref, y_ref):
    npair = SSM_FOLD // 2
    u2 = [jnp.concatenate([u_ref[2 * mm, 0], u_ref[2 * mm + 1, 0]], axis=1) for mm in range(npair)]
    s_prev = jnp.concatenate([sre_ref[...], sim_ref[...]], axis=1).astype(BF16)
    for ll in range(npair):
        acc = _dot_nt(s_prev, cct_ref[0, ll * 2 * LANES:(ll + 1) * 2 * LANES, :])
        for mm in range(ll + 1):
            acc = acc + _dot(u2[mm], bt_ref[0, ll - mm])
        y_ref[2 * ll, 0] = acc[:, :LANES]
        y_ref[2 * ll + 1, 0] = acc[:, LANES:]


def s5_toeplitz(u, s_re, s_im, bt, cct, rt):
    rows = u.shape[2]
    u_spec, s_spec = _s5_tile_specs(rt)
    return pl.pallas_call(
        _s5_toeplitz_kernel,
        grid=(N_LANE_TILES, rows // rt),
        in_specs=[u_spec, s_spec, s_spec,
                  pl.BlockSpec((1, SSM_FOLD // 2, 2 * LANES, 2 * LANES), lambda t, i: (t, 0, 0, 0)),
                  pl.BlockSpec((1, SSM_FOLD * LANES, 2 * STATE_PER_TILE), lambda t, i: (t, 0, 0))],
        out_specs=u_spec,
        out_shape=jax.ShapeDtypeStruct(u.shape, F32),
        compiler_params=_cparams("parallel", "parallel"),
        name="s5_toeplitz",
    )(u, s_re, s_im, bt, cct)


def _s5_glu_kernel(x_ref, u_ref, y_ref, d_ref, wglu_ref, bglu_ref, perm_ref, o_ref):
    nblk = x_ref.shape[0] // FOLD_BLOCK

    def gelu_block(blk):
        rows = slice(blk * SSM_FOLD, (blk + 1) * SSM_FOLD)
        gl = []
        for m in range(SSM_FOLD):
            ym = jnp.concatenate([y_ref[m, k, rows, :] for k in range(N_LANE_TILES)], axis=1)
            hm = jnp.concatenate([u_ref[m, k, rows, :] for k in range(N_LANE_TILES)], axis=1)
            gl.append(jax.nn.gelu(ym + d_ref[...] * hm.astype(F32)))
        gl = jnp.concatenate(gl, axis=0)
        return gl, _dot(gl.astype(BF16), wglu_ref[...])

    def finish(blk, gl, z):
        tok = slice(blk * FOLD_BLOCK, (blk + 1) * FOLD_BLOCK)
        hi, lo = _split_bf16(gl * jax.nn.sigmoid(z + bglu_ref[...]))
        o_ref[tok, :] = x_ref[tok, :] + (_dot(perm_ref[...], hi) + _dot(perm_ref[...], lo))

    prev = gelu_block(0)
    for blk in range(1, nblk):
        cur = gelu_block(blk)
        finish(blk - 1, *prev)
        prev = cur
    finish(nblk - 1, *prev)


def s5_glu(x, u, y, d, w_glu, b_glu, perm, tokens):
    t, dm = x.shape
    fold_spec = pl.BlockSpec((SSM_FOLD, N_LANE_TILES, tokens // SSM_FOLD, LANES),
                             lambda i: (0, 0, i, 0))
    x_spec = pl.BlockSpec((tokens, dm), lambda i: (i, 0))
    return pl.pallas_call(
        _s5_glu_kernel,
        grid=(t // tokens,),
        in_specs=[x_spec, fold_spec, fold_spec, _const_spec((1, dm)), _weight_spec(w_glu),
                  _const_spec((1, dm)), _const_spec((FOLD_BLOCK, FOLD_BLOCK))],
        out_specs=x_spec,
        out_shape=jax.ShapeDtypeStruct((t, dm), F32),
        compiler_params=_cparams("parallel"),
        name="s5_glu",
    )(x, u, y, d.reshape(1, dm), w_glu[0], b_glu.reshape(1, dm), perm)


def s5_mixer(x, st_re, st_im, g, ops, d, w_glu, b_glu):
    bt, cct, w_re, w_im, a16 = ops
    b, l, dm = x.shape
    rows = b * l // SSM_FOLD
    x2 = x.reshape(b * l, dm)
    perm = _fold_perm()
    tokens = min(S5_TOKEN_TILE, b * l)
    rt = min(S5_ROW_TILE, rows)
    u = s5_fold(x2, g, perm, tokens)
    v_re, v_im = s5_state_in(u, w_re, w_im, rt)
    s_re, s_im, f_re, f_im = s5_scan(v_re, v_im, a16, st_re.reshape(b, N_STATE),
                                     st_im.reshape(b, N_STATE), S5_SCAN_WIDTH)
    y = s5_toeplitz(u, s_re, s_im, bt, cct, rt)
    out = s5_glu(x2, u, y, d, w_glu, b_glu, perm, tokens)
    shape = (b, SSM_GROUPS, SSM_STATE)
    return out.reshape(b, l, dm), f_re.reshape(shape), f_im.reshape(shape)


def _stream_tiling(seq_len, n_streams, tile_rows):
    if seq_len >= tile_rows:
        return 1, tile_rows
    return min(n_streams, max(SHORT_STREAM_TILE_ROWS // seq_len, 1)), seq_len


def kernel(x_prompt, x_sample, mem_prompt, state_ssm_re, state_ssm_im, state_ret, cache_mem_k, cache_mem_v, cache_conv, norm_mix, norm_mem_q, norm_mem_kv, norm_ffn, norm_final, ssm_a_re, ssm_a_im, ssm_log_dt, ssm_b_re, ssm_b_im, ssm_c_re, ssm_c_im, ssm_d, ssm_w_glu, ssm_b_glu, ret_w_qkvg, ret_w_o, mem_w_q, mem_w_kv, mem_w_o, ffn_w_up, ffn_conv_w, ffn_conv_b, ffn_w_down):
    bp, _, dm = x_prompt.shape
    depth = norm_mix.shape[0]
    d_ff = ffn_w_down.shape[1]
    streams = [dict(x=x, b=x.shape[0], l=x.shape[1],
                    attn_tiling=_stream_tiling(x.shape[1], x.shape[0], ATTN_TILE_ROWS),
                    ffn_tiling=_stream_tiling(x.shape[1], x.shape[0], FFN_TILE_ROWS))
               for x in (x_prompt, x_sample)]
    cache_k, cache_v = _kv_rows(cache_mem_k), _kv_rows(cache_mem_v)
    ssm_w_glu, ret_w_qkvg, ret_w_o, mem_w_q, mem_w_kv, mem_w_o, ffn_w_up, ffn_w_down = (
        w.astype(BF16) for w in (ssm_w_glu, ret_w_qkvg, ret_w_o, mem_w_q, mem_w_kv, mem_w_o,
                                 ffn_w_up, ffn_w_down))
    mem_k, mem_v = mem_kv(mem_prompt, norm_mem_kv, mem_w_kv)
    outs = dict(ssm_re=([], []), ssm_im=([], []), ret=([], []), conv=([], []))

    for i in range(depth):
        j = i // 2
        if i % 2 == 0:
            ops = s5_prepare(ssm_a_re[j], ssm_a_im[j], ssm_log_dt[j], ssm_b_re[j], ssm_b_im[j],
                             ssm_c_re[j], ssm_c_im[j])
            w_glu = (ssm_w_glu, j)
            zeros = jnp.zeros((bp, SSM_GROUPS, SSM_STATE), F32)
            states = [(zeros, zeros), (state_ssm_re[j], state_ssm_im[j])]
            for si, (s, (st_re, st_im)) in enumerate(zip(streams, states)):
                s["x"], f_re, f_im = s5_mixer(s["x"], st_re, st_im, norm_mix[i], ops, ssm_d[j],
                                              w_glu, ssm_b_glu[j])
                outs["ssm_re"][si].append(f_re)
                outs["ssm_im"][si].append(f_im)
        else:
            w_qkvg = (ret_w_qkvg, j)
            w_o = (ret_w_o, j)
            zeros = jnp.zeros((bp, RET_HEADS, RET_DK, RET_DV), F32)
            chunk = RET_CHUNK_ROWS
            for si, (s, st, pos0) in enumerate(zip(streams, [zeros, state_ret[j]], [0, PAST_LEN])):
                x2 = s["x"].reshape(s["b"] * s["l"], dm)
                cos, sin = _rope_tables(pos0, s["l"])
                if s["l"] >= chunk:
                    y, st_new = ret_mixer(x2, norm_mix[i], w_qkvg, w_o, cos, sin, st, s["l"], chunk)
                else:
                    p = rms_matmul(x2, norm_mix[i], w_qkvg, BF16)
                    y, st_new = ret_core(p, cos, sin, st, s["l"], min(RET_STREAMS_PER_STEP, s["b"]))
                    y = matmul_res(y, w_o, x2)
                s["x"] = y.reshape(s["x"].shape)
                outs["ret"][si].append(st_new)

        w_q = (mem_w_q, i)
        w_o = (mem_w_o, i)
        for s, (k_, v_) in zip(streams, [(mem_k, mem_v), (cache_k, cache_v)]):
            x2 = s["x"].reshape(s["b"] * s["l"], dm)
            y = mem_attn(x2, norm_mem_q[i], w_q, k_, v_, i, w_o, s["l"], *s["attn_tiling"])
            s["x"] = y.reshape(s["x"].shape)

        w_up = (ffn_w_up, i)
        w_dn = (ffn_w_down, i)
        g_final = norm_final if i == depth - 1 else None
        conv_states = [jnp.zeros((bp, 2, d_ff), F32), cache_conv[i]]
        for si, (s, cst) in enumerate(zip(streams, conv_states)):
            x2 = s["x"].reshape(s["b"] * s["l"], dm)
            y, cnew = conv_ffn(x2, norm_ffn[i], w_up, ffn_conv_w[i], ffn_conv_b[i], w_dn, cst,
                               s["l"], *s["ffn_tiling"], g_final=g_final)
            s["x"] = y.reshape(s["x"].shape)
            outs["conv"][si].append(cnew)

    return (streams[0]["x"], streams[1]["x"],
            jnp.stack(outs["ssm_re"][0]), jnp.stack(outs["ssm_im"][0]),
            jnp.stack(outs["ssm_re"][1]), jnp.stack(outs["ssm_im"][1]),
            jnp.stack(outs["ret"][0]), jnp.stack(outs["ret"][1]),
            _kv_unrows(mem_k, MEM_HEADS), _kv_unrows(mem_v, MEM_HEADS),
            jnp.stack(outs["conv"][0]), jnp.stack(outs["conv"][1]))
```

```python
import functools
import math

import numpy as np
import jax
import jax.numpy as jnp
from jax import lax
from jax.experimental import pallas as pl
from jax.experimental.pallas import tpu as pltpu

F32 = jnp.float32
BF16 = jnp.bfloat16

D_MODEL = 1024
PAST_LEN = 1024
EPS = 1e-6
GN_EPS = 1e-5
ROPE_BASE = 10000.0
LANES = 128
SUBLANES = 8
SSM_GROUP = 16
SSM_GROUPS = D_MODEL // SSM_GROUP
SSM_STATE = 64
SSM_FOLD = 16
GROUPS_PER_TILE = LANES // SSM_GROUP
N_LANE_TILES = D_MODEL // LANES
STATE_PER_TILE = GROUPS_PER_TILE * SSM_STATE
N_STATE = SSM_GROUPS * SSM_STATE
RET_HEADS = 4
RET_DK = D_MODEL // RET_HEADS
RET_DV = 2 * D_MODEL // RET_HEADS
RET_QK = RET_HEADS * RET_DK
RET_V = RET_HEADS * RET_DV
RET_LOG_G = tuple(math.log(1.0 - 2.0 ** (-5.0 - h)) for h in range(RET_HEADS))
MEM_HEADS = 4
MEM_HD = D_MODEL // MEM_HEADS
VMEM_LIMIT = 56 * 1024 * 1024
ATTN_TILE_ROWS = 1024
FFN_TILE_ROWS = 256
SHORT_STREAM_TILE_ROWS = 256
RET_CHUNK_ROWS = 256
RET_STREAMS_PER_STEP = 4
S5_TOKEN_TILE = 1024
S5_ROW_TILE = 1024
S5_SCAN_WIDTH = 1024


def _cparams(*sem):
    return pltpu.CompilerParams(dimension_semantics=sem, vmem_limit_bytes=VMEM_LIMIT)


def _const_spec(shape):
    nd = len(shape)
    return pl.BlockSpec(shape, lambda *_: (0,) * nd, pipeline_mode=pl.Buffered(1))


def _weight_spec(w):
    stack, layer = w
    return pl.BlockSpec((None,) + stack.shape[1:], lambda *_: (layer, 0, 0),
                        pipeline_mode=pl.Buffered(1))


def _rms_scale(x):
    return lax.rsqrt(jnp.mean(x * x, axis=-1, keepdims=True) + EPS)


def _prenorm(x, g_ref):
    return (x * g_ref[...]).astype(BF16), _rms_scale(x)


def _split_bf16(x):
    hi = x.astype(BF16)
    lo = (x - hi.astype(F32)).astype(BF16)
    return hi, lo


def _dot(a, b):
    return jnp.dot(a, b, preferred_element_type=F32)


def _dot_nt(a, b):
    return lax.dot_general(a, b, (((1,), (1,)), ((), ())), preferred_element_type=F32)


def _rms_matmul_kernel(x_ref, g_ref, w_ref, o_ref, *, tn):
    x = x_ref[...]
    h = (x * _rms_scale(x) * g_ref[...]).astype(BF16)
    for n0 in range(0, w_ref.shape[1], tn):
        o_ref[:, n0:n0 + tn] = _dot(h, w_ref[:, n0:n0 + tn]).astype(o_ref.dtype)


def rms_matmul(x, g, w, out_dtype, tm=512, tn=512):
    t, d = x.shape
    n = w[0].shape[2]
    tm = min(tm, t)
    return pl.pallas_call(
        functools.partial(_rms_matmul_kernel, tn=tn),
        grid=(t // tm,),
        in_specs=[pl.BlockSpec((tm, d), lambda i: (i, 0)),
                  _const_spec((1, d)),
                  _weight_spec(w)],
        out_specs=pl.BlockSpec((tm, n), lambda i: (i, 0)),
        out_shape=jax.ShapeDtypeStruct((t, n), out_dtype),
        compiler_params=_cparams("parallel"),
        name="rms_matmul",
    )(x, g.reshape(1, d), w[0])


def _matmul_res_kernel(y_ref, w_ref, x_ref, o_ref):
    o_ref[...] = x_ref[...] + _dot(y_ref[...], w_ref[...])


def matmul_res(y, w, x, tm=512):
    t, k = y.shape
    d = w[0].shape[2]
    tm = min(tm, t)
    return pl.pallas_call(
        _matmul_res_kernel,
        grid=(t // tm,),
        in_specs=[pl.BlockSpec((tm, k), lambda i: (i, 0)),
                  _weight_spec(w),
                  pl.BlockSpec((tm, d), lambda i: (i, 0))],
        out_specs=pl.BlockSpec((tm, d), lambda i: (i, 0)),
        out_shape=jax.ShapeDtypeStruct((t, d), F32),
        compiler_params=_cparams("parallel"),
        name="matmul_res",
    )(y, w[0], x)


MEM_LANE_TILES = MEM_HD // LANES


def _kv_rows(x):
    *lead, n, h, hd = x.shape
    x = x.reshape(*lead, n, h, MEM_LANE_TILES, LANES)
    return jnp.swapaxes(x, -3, -2).reshape(*lead, n * MEM_LANE_TILES * h, LANES)


def _kv_unrows(x, heads):
    *lead, rows, _ = x.shape
    n = rows // (MEM_LANE_TILES * heads)
    x = x.reshape(*lead, n, MEM_LANE_TILES, heads, LANES)
    return jnp.swapaxes(x, -3, -2).reshape(*lead, n, heads, MEM_LANE_TILES * LANES)


def _mem_kv_kernel(x_ref, g_ref, w_ref, k_ref, v_ref):
    x = x_ref[...]
    h = (x * _rms_scale(x) * g_ref[...]).astype(BF16)
    n_mem, d = x.shape
    stride = MEM_LANE_TILES * MEM_HEADS
    for which, ref in enumerate((k_ref, v_ref)):
        for hd in range(MEM_HEADS):
            c0 = which * d + hd * MEM_HD
            res = _dot(h, w_ref[:, c0:c0 + MEM_HD])
            for j in range(MEM_LANE_TILES):
                ref[pl.ds(j * MEM_HEADS + hd, n_mem, stride=stride), :] = (
                    res[:, j * LANES:(j + 1) * LANES])


def mem_kv(mem, g, w_kv):
    b, n_mem, d = mem.shape
    depth = w_kv.shape[0]
    rows = n_mem * MEM_LANE_TILES * MEM_HEADS
    out_spec = pl.BlockSpec((None, None, rows, LANES), lambda l, i: (l, i, 0, 0))
    out = jax.ShapeDtypeStruct((depth, b, rows, LANES), F32)
    return pl.pallas_call(
        _mem_kv_kernel,
        grid=(depth, b),
        in_specs=[pl.BlockSpec((None, n_mem, d), lambda l, i: (i, 0, 0)),
                  pl.BlockSpec((None, 1, d), lambda l, i: (l, 0, 0)),
                  pl.BlockSpec((None,) + w_kv.shape[1:], lambda l, i: (l, 0, 0))],
        out_specs=[out_spec, out_spec],
        out_shape=[out, out],
        compiler_params=_cparams("parallel", "parallel"),
        name="mem_kv",
    )(mem, g.reshape(depth, 1, d), w_kv)


def _head_rows(kv_ref, s, hd):
    stride = MEM_LANE_TILES * MEM_HEADS
    n_mem = kv_ref.shape[2] // stride
    return jnp.concatenate(
        [kv_ref[0, s, pl.ds(j * MEM_HEADS + hd, n_mem, stride=stride), :]
         for j in range(MEM_LANE_TILES)], axis=1).astype(BF16)


def _mem_attn_kernel(x_ref, g_ref, wq_ref, k_ref, v_ref, wo_ref, o_ref, *, nseg, seg, head_group):
    scale = MEM_HD ** -0.5
    x = x_ref[...]
    h, row_scale = _prenorm(x, g_ref)
    o_ref[...] = x

    width = head_group * MEM_HD

    def q_proj(grp):
        return (_dot(h, wq_ref[:, grp * width:(grp + 1) * width]) * row_scale).astype(BF16)

    nxt = q_proj(0)
    pending = None
    for grp in range(MEM_HEADS // head_group):
        q = nxt
        pairs = [(grp * head_group + hg, hg, s) for hg in range(head_group) for s in range(nseg)]
        scores = [_dot_nt(q[s * seg:(s + 1) * seg, hg * MEM_HD:(hg + 1) * MEM_HD],
                          _head_rows(k_ref, s, hd)) for hd, hg, s in pairs]
        if (grp + 1) * head_group < MEM_HEADS:
            nxt = q_proj(grp + 1)
        if pending is not None:
            o_ref[...] += _dot(pending, wo_ref[(grp - 1) * width:grp * width, :])
        outs = []
        for (hd, hg, s), sc in zip(pairs, scores):
            sc = sc * scale
            p = jnp.exp(sc - jnp.max(sc, axis=-1, keepdims=True))
            denom = jnp.sum(p, axis=-1, keepdims=True)
            outs.append((_dot(p.astype(BF16), _head_rows(v_ref, s, hd)) / denom).astype(BF16))
        heads = [outs[hg * nseg] if nseg == 1 else jnp.concatenate(outs[hg * nseg:(hg + 1) * nseg], axis=0)
                 for hg in range(head_group)]
        pending = heads[0] if head_group == 1 else jnp.concatenate(heads, axis=1)
    o_ref[...] += _dot(pending, wo_ref[MEM_HEADS * MEM_HD - width:, :])


def mem_attn(x, g, w_q, mk, mv, layer, w_o, seq_len, nseg, seg):
    t, d = x.shape
    b = mk.shape[1]
    tiles_per_stream = max(seq_len // (nseg * seg), 1)
    n_outer = b // nseg
    tm = nseg * seg
    x_spec = pl.BlockSpec((tm, d), lambda o, i: (o * tiles_per_stream + i, 0))
    kv_spec = pl.BlockSpec((1, nseg) + mk.shape[2:], lambda o, i: (layer, o, 0, 0))
    return pl.pallas_call(
        functools.partial(_mem_attn_kernel, nseg=nseg, seg=seg, head_group=2 if nseg == 1 else 1),
        grid=(n_outer, tiles_per_stream),
        in_specs=[x_spec, _const_spec((1, d)), _weight_spec(w_q), kv_spec, kv_spec,
                  _weight_spec(w_o)],
        out_specs=x_spec,
        out_shape=jax.ShapeDtypeStruct((t, d), F32),
        compiler_params=_cparams("parallel", "parallel"),
        name="mem_attn",
    )(x, g.reshape(1, d), w_q[0], mk, mv, w_o[0])


def _ffn_kernel(x_ref, g_ref, wup_ref, cw_ref, cb_ref, wdn_ref, st_ref, *rest,
                nseg, seg, fc, final_norm):
    if final_norm:
        gf_ref, o_ref, nst_ref, carry_ref = rest
    else:
        o_ref, nst_ref, carry_ref = rest
    d_ff = wdn_ref.shape[0]
    tm = nseg * seg

    @pl.when(pl.program_id(1) == 0)
    def _():
        carry_ref[...] = st_ref[...]

    x = x_ref[...]
    h, scale = _prenorm(x, g_ref)
    row = lax.broadcasted_iota(jnp.int32, (tm, 1), 0) % seg
    o_ref[...] = x

    def up_proj(f0):
        return (_dot(h, wup_ref[:, f0:f0 + fc]) * scale,
                _dot(h, wup_ref[:, d_ff + f0:d_ff + f0 + fc]) * scale)

    nxt = up_proj(0)
    pending = None
    for f0 in range(0, d_ff, fc):
        cols = slice(f0, f0 + fc)
        a, gate = nxt
        if f0 + fc < d_ff:
            nxt = up_proj(f0 + fc)
        if pending is not None:
            o_ref[...] += _dot(pending, wdn_ref[f0 - fc:f0, :])
        p0 = jnp.broadcast_to(carry_ref[:, 0:1, cols], (nseg, seg, fc)).reshape(tm, fc)
        p1 = jnp.broadcast_to(carry_ref[:, 1:2, cols], (nseg, seg, fc)).reshape(tm, fc)
        a1 = jnp.where(row == 0, p1, pltpu.roll(a, 1, 0))
        a2 = jnp.where(row == 0, p0, jnp.where(row == 1, p1, pltpu.roll(a, 2, 0)))
        conv = (cw_ref[0:1, cols] * a2 + cw_ref[1:2, cols] * a1 + cw_ref[2:3, cols] * a
                + cb_ref[:, cols])
        p = jax.nn.gelu(conv) * gate
        pending = p.astype(BF16)
        carry_ref[:, :, cols] = a.reshape(nseg, seg, fc)[:, seg - 2:seg, :]
    y = o_ref[...] + _dot(pending, wdn_ref[d_ff - fc:d_ff, :])
    nst_ref[...] = carry_ref[...]
    if final_norm:
        y = y * _rms_scale(y) * gf_ref[...]
    o_ref[...] = y


def conv_ffn(x, g, w_up, conv_w, conv_b, w_down, conv_state, seq_len, nseg, seg, g_final=None,
             fc=256):
    t, d = x.shape
    d_ff = w_down[0].shape[1]
    b = conv_state.shape[0]
    tiles_per_stream = max(seq_len // (nseg * seg), 1)
    n_outer = b // nseg
    tm = nseg * seg
    final_norm = g_final is not None
    in_specs = [pl.BlockSpec((tm, d), lambda o, i: (o * tiles_per_stream + i, 0)),
                _const_spec((1, d)),
                _weight_spec(w_up),
                _const_spec((3, d_ff)),
                _const_spec((1, d_ff)),
                _weight_spec(w_down),
                pl.BlockSpec((nseg, 2, d_ff), lambda o, i: (o, 0, 0))]
    args = [x, g.reshape(1, d), w_up[0], conv_w, conv_b.reshape(1, d_ff), w_down[0], conv_state]
    if final_norm:
        in_specs.append(_const_spec((1, d)))
        args.append(g_final.reshape(1, d))
    return pl.pallas_call(
        functools.partial(_ffn_kernel, nseg=nseg, seg=seg, fc=fc, final_norm=final_norm),
        grid=(n_outer, tiles_per_stream),
        in_specs=in_specs,
        out_specs=[pl.BlockSpec((tm, d), lambda o, i: (o * tiles_per_stream + i, 0)),
                   pl.BlockSpec((nseg, 2, d_ff), lambda o, i: (o, 0, 0))],
        out_shape=[jax.ShapeDtypeStruct((t, d), F32),
                   jax.ShapeDtypeStruct((b, 2, d_ff), F32)],
        scratch_shapes=[pltpu.VMEM((nseg, 2, d_ff), F32)],
        compiler_params=_cparams("parallel", "arbitrary"),
        name="conv_ffn",
    )(*args)


def _rotate(x, cos, sin):
    half = x.shape[1] // 2
    x1, x2 = x[:, :half], x[:, half:]
    return jnp.concatenate([x1 * cos - x2 * sin, x2 * cos + x1 * sin], axis=1)


def _ret_head(h, q, k, v, g, cos, sin, s_ref, chunk, fillers=(), stream=0):
    fillers = list(fillers) + [lambda: None] * (3 - len(fillers))
    log_g = RET_LOG_G[h]
    li = lax.broadcasted_iota(jnp.int32, (chunk, chunk), 0)
    mi = lax.broadcasted_iota(jnp.int32, (chunk, chunk), 1)
    diff = (li - mi).astype(F32)
    pos = lax.broadcasted_iota(jnp.int32, (chunk, 1), 0).astype(F32)
    fillers[0]()
    qr = _rotate(q, cos, sin).astype(BF16)
    kr = _rotate(k, cos, sin) * (RET_DK ** -0.5)
    decay = jnp.where(diff >= 0, jnp.exp(jnp.maximum(diff, 0.0) * log_g), 0.0)
    scores = _dot_nt(qr, kr.astype(BF16))
    fillers[1]()
    scores = scores * decay
    intra = _dot(scores.astype(BF16), v)
    s_old = s_ref[stream, h]
    cross = _dot(qr, s_old.astype(BF16)) * jnp.exp((pos + 1.0) * log_g)
    o = intra + cross
    k_tail = (kr * jnp.exp((chunk - 1.0 - pos) * log_g)).astype(BF16)
    s_ref[stream, h] = math.exp(chunk * log_g) * s_old + lax.dot_general(
        k_tail, v, (((0,), (0,)), ((), ())), preferred_element_type=F32)
    fillers[2]()
    mu = jnp.mean(o, axis=-1, keepdims=True)
    oc = o - mu
    var = jnp.mean(oc * oc, axis=-1, keepdims=True)
    return jax.nn.silu(g) * (oc * lax.rsqrt(var + GN_EPS))


def _ret_kernel(p_ref, cos_ref, sin_ref, s_in_ref, y_ref, s_out_ref, *, chunk):
    s_out_ref[...] = s_in_ref[...]
    cos = cos_ref[...]
    sin = sin_ref[...]
    for s in range(s_in_ref.shape[0]):
        rows = slice(s * chunk, (s + 1) * chunk)
        for h in range(RET_HEADS):
            q = p_ref[rows, h * RET_DK:(h + 1) * RET_DK].astype(F32)
            k = p_ref[rows, RET_QK + h * RET_DK:RET_QK + (h + 1) * RET_DK].astype(F32)
            v = p_ref[rows, 2 * RET_QK + h * RET_DV:2 * RET_QK + (h + 1) * RET_DV]
            g = p_ref[rows, 2 * RET_QK + RET_V + h * RET_DV:
                      2 * RET_QK + RET_V + (h + 1) * RET_DV].astype(F32)
            y = _ret_head(h, q, k, v, g, cos, sin, s_out_ref, chunk, stream=s)
            y_ref[rows, h * RET_DV:(h + 1) * RET_DV] = y.astype(y_ref.dtype)


def _ret_fused_kernel(x_ref, g_ref, w_ref, wo_ref, cos_ref, sin_ref, s_in_ref, o_ref, s_out_ref,
                      *, chunk):
    @pl.when(pl.program_id(1) == 0)
    def _():
        s_out_ref[...] = s_in_ref[...]

    cos = cos_ref[...]
    sin = sin_ref[...]
    x = x_ref[...]
    hn, row_scale = _prenorm(x, g_ref)
    o_ref[...] = x

    offsets = (0, RET_QK, 2 * RET_QK, 2 * RET_QK + RET_V)
    widths = (RET_DK, RET_DK, RET_DV, RET_DV)

    def proj(h, part):
        c0 = offsets[part] + h * widths[part]
        y = _dot(hn, w_ref[:, c0:c0 + widths[part]]) * row_scale
        return y.astype(BF16) if part == 2 else y

    def out_proj(h, y):
        o_ref[...] += _dot(y, wo_ref[h * RET_DV:(h + 1) * RET_DV, :])

    cur = [proj(0, part) for part in range(4)]
    ys = []
    for h in range(RET_HEADS):
        nxt = [None] * 4

        def fill_qk(h=h, nxt=nxt):
            nxt[0], nxt[1] = proj(h + 1, 0), proj(h + 1, 1)

        def fill_v(h=h, nxt=nxt):
            nxt[2] = proj(h + 1, 2)

        def fill_g(h=h, nxt=nxt):
            nxt[3] = proj(h + 1, 3)

        if h + 1 < RET_HEADS:
            fillers = (fill_qk, fill_v, fill_g)
        else:
            fillers = [functools.partial(out_proj, i, y) for i, y in enumerate(ys)]
        ys.append(_ret_head(h, *cur, cos, sin, s_out_ref, chunk, fillers).astype(BF16))
        cur = nxt
    out_proj(RET_HEADS - 1, ys[-1])


def _ret_specs(seq_len, chunk):
    nchunks = seq_len // chunk
    half = RET_DK // 2
    row_spec = lambda n: pl.BlockSpec((chunk, n), lambda o, i: (o * nchunks + i, 0))
    rope_spec = pl.BlockSpec((chunk, half), lambda o, i: (i, 0))
    state_spec = pl.BlockSpec((1, RET_HEADS, RET_DK, RET_DV), lambda o, i: (o, 0, 0, 0))
    return nchunks, row_spec, rope_spec, state_spec


def ret_core(p, cos, sin, state, chunk, streams_per_step):
    t, n = p.shape
    b = state.shape[0]
    nb = streams_per_step
    rows_spec = lambda width: pl.BlockSpec((nb * chunk, width), lambda i: (i, 0))
    rope_spec = _const_spec(cos.shape)
    state_spec = pl.BlockSpec((nb,) + state.shape[1:], lambda i: (i, 0, 0, 0))
    return pl.pallas_call(
        functools.partial(_ret_kernel, chunk=chunk),
        grid=(b // nb,),
        in_specs=[rows_spec(n), rope_spec, rope_spec, state_spec],
        out_specs=[rows_spec(RET_V), state_spec],
        out_shape=[jax.ShapeDtypeStruct((t, RET_V), BF16),
                   jax.ShapeDtypeStruct(state.shape, F32)],
        compiler_params=_cparams("parallel"),
        name="ret_core",
    )(p, cos, sin, state)


def ret_mixer(x, g, w_qkvg, w_o, cos, sin, state, seq_len, chunk):
    t, d = x.shape
    nchunks, row_spec, rope_spec, state_spec = _ret_specs(seq_len, chunk)
    return pl.pallas_call(
        functools.partial(_ret_fused_kernel, chunk=chunk),
        grid=(state.shape[0], nchunks),
        in_specs=[row_spec(d), _const_spec((1, d)), _weight_spec(w_qkvg),
                  _weight_spec(w_o), rope_spec, rope_spec, state_spec],
        out_specs=[row_spec(d), state_spec],
        out_shape=[jax.ShapeDtypeStruct((t, d), F32), jax.ShapeDtypeStruct(state.shape, F32)],
        compiler_params=_cparams("parallel", "arbitrary"),
        name="ret_mixer",
    )(x, g.reshape(1, d), w_qkvg[0], w_o[0], cos, sin, state)


def _rope_tables(pos0, length):
    half = RET_DK // 2
    freqs = ROPE_BASE ** (-np.arange(half, dtype=np.float64) / half)
    ang = (pos0 + np.arange(length, dtype=np.float64))[:, None] * freqs[None, :]
    return jnp.asarray(np.cos(ang), F32), jnp.asarray(np.sin(ang), F32)


def _cmul(ar, ai, br, bi):
    return ar * br - ai * bi, ar * bi + ai * br


def _s5_prep_kernel(lam_re_ref, lam_im_ref, ldt_ref, b_re_ref, b_im_ref, c_re_ref, c_im_ref,
                    w_re_ref, w_im_ref, bt_ref, cct_ref, a16_ref):
    dt = jnp.exp(ldt_ref[...])
    lam_re, lam_im = lam_re_ref[...], lam_im_ref[...]
    mag = jnp.exp(lam_re * dt)
    ar, ai = mag * jnp.cos(lam_im * dt), mag * jnp.sin(lam_im * dt)
    den = lam_re * lam_re + lam_im * lam_im
    nr, ni = ar - 1.0, ai
    coef_re = (nr * lam_re + ni * lam_im) / den
    coef_im = (ni * lam_re - nr * lam_im) / den
    bb_re, bb_im = _cmul(coef_re, coef_im, b_re_ref[...], b_im_ref[...])
    c_re, c_im = c_re_ref[...], c_im_ref[...]
    lane = lax.broadcasted_iota(jnp.int32, (1, LANES), 1)
    low = lane < SSM_STATE
    rg = lax.broadcasted_iota(jnp.int32, (LANES, 1), 0) // SSM_GROUP
    pair = rg // 2
    own_half = (lane // SSM_STATE) == (rg % 2)
    same_group = rg == lane // SSM_GROUP
    c_hi, c_lo = _split_bf16(jnp.where(low, c_re, -c_im))
    tk = []
    pr, pi = jnp.ones_like(ar), jnp.zeros_like(ar)
    for j in range(SSM_FOLD):
        zr, zi = _cmul(pr, pi, bb_re, bb_im)
        w_re_ref[SSM_FOLD - 1 - j] = jnp.where(own_half, zr, 0.0)
        w_im_ref[SSM_FOLD - 1 - j] = jnp.where(own_half, zi, 0.0)
        b_hi, b_lo = _split_bf16(jnp.where(low, zr, zi))
        kk = _dot_nt(b_hi, c_hi) + _dot_nt(b_lo, c_hi) + _dot_nt(b_hi, c_lo)
        tk.append(jnp.where(same_group, kk, 0.0).astype(BF16))
        pr, pi = _cmul(pr, pi, ar, ai)
        zr, zi = _cmul(c_re, c_im, pr, pi)
        for ri, z in enumerate((zr, -zi)):
            z = jnp.where(own_half, z, 0.0)
            for t in range(GROUPS_PER_TILE // 2):
                c0 = ri * STATE_PER_TILE + t * LANES
                cct_ref[0, j * LANES:(j + 1) * LANES, c0:c0 + LANES] = (
                    jnp.where(pair == t, z, 0.0).astype(BF16))
    a16_ref[0] = pr
    a16_ref[1] = pi
    zero = jnp.zeros((LANES, LANES), BF16)
    for d2 in range(SSM_FOLD // 2):
        top = jnp.concatenate([tk[2 * d2], tk[2 * d2 + 1]], axis=1)
        bot = jnp.concatenate([tk[2 * d2 - 1] if d2 else zero, tk[2 * d2]], axis=1)
        bt_ref[0, d2] = jnp.concatenate([top, bot], axis=0)


def s5_prepare(a_re, a_im, log_dt, b_re, b_im, c_re, c_im):
    g, p, c = SSM_GROUPS, SSM_STATE, SSM_GROUP
    fold, nt = SSM_FOLD, N_LANE_TILES
    rows = g * c
    twice = lambda x: jnp.tile(x, (1, 2))
    lam = lambda x: twice(jnp.repeat(x, c, axis=0))
    in_spec = pl.BlockSpec((LANES, LANES), lambda t: (t, 0))
    w_spec = pl.BlockSpec((fold, LANES, LANES), lambda t: (0, t, 0))
    w_shape = jax.ShapeDtypeStruct((fold, rows, LANES), F32)
    w_re, w_im, bt, cct, a16 = pl.pallas_call(
        _s5_prep_kernel,
        grid=(nt,),
        in_specs=[in_spec, in_spec, pl.BlockSpec((LANES, 1), lambda t: (t, 0))] + [in_spec] * 4,
        out_specs=[w_spec, w_spec,
                   pl.BlockSpec((1, fold // 2, 2 * LANES, 2 * LANES), lambda t: (t, 0, 0, 0)),
                   pl.BlockSpec((1, fold * LANES, 2 * STATE_PER_TILE), lambda t: (t, 0, 0)),
                   pl.BlockSpec((2, LANES, LANES), lambda t: (0, t, 0))],
        out_shape=[w_shape, w_shape,
                   jax.ShapeDtypeStruct((nt, fold // 2, 2 * LANES, 2 * LANES), BF16),
                   jax.ShapeDtypeStruct((nt, fold * LANES, 2 * STATE_PER_TILE), BF16),
                   jax.ShapeDtypeStruct((2, rows, LANES), F32)],
        compiler_params=_cparams("parallel"),
        name="s5_prep",
    )(lam(a_re), lam(a_im), jnp.repeat(log_dt, c).reshape(rows, 1),
      twice(b_re.transpose(0, 2, 1).reshape(rows, p)), twice(b_im.transpose(0, 2, 1).reshape(rows, p)),
      twice(c_re.reshape(rows, p)), twice(c_im.reshape(rows, p)))
    a16 = a16[:, ::c, :p].reshape(2, 1, N_STATE)
    return bt, cct, w_re, w_im, a16


FOLD_BLOCK = SSM_FOLD * SSM_FOLD


def _fold_perm():
    idx = np.arange(FOLD_BLOCK)
    perm = np.zeros((FOLD_BLOCK, FOLD_BLOCK), np.float32)
    perm[(idx % SSM_FOLD) * SSM_FOLD + idx // SSM_FOLD, idx] = 1.0
    return jnp.asarray(perm, BF16)


def _s5_fold_kernel(x_ref, g_ref, perm_ref, u_ref):
    for blk in range(x_ref.shape[0] // FOLD_BLOCK):
        x = x_ref[blk * FOLD_BLOCK:(blk + 1) * FOLD_BLOCK, :]
        h = (x * _rms_scale(x) * g_ref[...]).astype(BF16)
        f = _dot(perm_ref[...], h).astype(BF16)
        for m in range(SSM_FOLD):
            for k in range(N_LANE_TILES):
                u_ref[m, k, blk * SSM_FOLD:(blk + 1) * SSM_FOLD, :] = (
                    f[m * SSM_FOLD:(m + 1) * SSM_FOLD, k * LANES:(k + 1) * LANES])


def s5_fold(x, g, perm, tokens):
    t, d = x.shape
    rows = t // SSM_FOLD
    return pl.pallas_call(
        _s5_fold_kernel,
        grid=(t // tokens,),
        in_specs=[pl.BlockSpec((tokens, d), lambda i: (i, 0)), _const_spec((1, d)),
                  _const_spec((FOLD_BLOCK, FOLD_BLOCK))],
        out_specs=pl.BlockSpec((SSM_FOLD, N_LANE_TILES, tokens // SSM_FOLD, LANES),
                               lambda i: (0, 0, i, 0)),
        out_shape=jax.ShapeDtypeStruct((SSM_FOLD, N_LANE_TILES, rows, LANES), BF16),
        compiler_params=_cparams("parallel"),
        name="s5_fold",
    )(x, g.reshape(1, d), perm)


def _s5_state_in_kernel(u_ref, w_re_ref, w_im_ref, vre_ref, vim_ref, wd_ref):
    @pl.when(pl.program_id(1) == 0)
    def _():
        pair = lax.broadcasted_iota(jnp.int32, (LANES, 1), 0) // SSM_GROUP // 2
        for m in range(SSM_FOLD):
            for ri, ref in enumerate((w_re_ref, w_im_ref)):
                w = ref[m]
                for t in range(GROUPS_PER_TILE // 2):
                    c0 = ri * STATE_PER_TILE + t * LANES
                    wd_ref[m * LANES:(m + 1) * LANES, c0:c0 + LANES] = (
                        jnp.where(pair == t, w, 0.0).astype(BF16))

    lhs = jnp.concatenate([u_ref[m, 0] for m in range(SSM_FOLD)], axis=1)
    v = _dot(lhs, wd_ref[...])
    vre_ref[...] = v[:, :STATE_PER_TILE]
    vim_ref[...] = v[:, STATE_PER_TILE:]


def _s5_tile_specs(rt):
    u_spec = pl.BlockSpec((SSM_FOLD, 1, rt, LANES), lambda t, i: (0, t, i, 0))
    s_spec = pl.BlockSpec((rt, STATE_PER_TILE), lambda t, i: (i, t))
    return u_spec, s_spec


def s5_state_in(u, w_re, w_im, rt):
    rows = u.shape[2]
    u_spec, s_spec = _s5_tile_specs(rt)
    out = jax.ShapeDtypeStruct((rows, N_STATE), F32)
    w_spec = pl.BlockSpec((SSM_FOLD, LANES, LANES), lambda t, i: (0, t, 0))
    return pl.pallas_call(
        _s5_state_in_kernel,
        grid=(N_LANE_TILES, rows // rt),
        in_specs=[u_spec, w_spec, w_spec],
        out_specs=[s_spec, s_spec],
        out_shape=[out, out],
        scratch_shapes=[pltpu.VMEM((SSM_FOLD * LANES, 2 * STATE_PER_TILE), BF16)],
        compiler_params=_cparams("parallel", "arbitrary"),
        name="s5_state_in",
    )(u, w_re, w_im)


def _s5_scan_pairs_kernel(vre_ref, vim_ref, a_ref, s0re_ref, s0im_ref,
                          sre_ref, sim_ref, fre_ref, fim_ref):
    ar, ai = a_ref[0], a_ref[1]
    s0r, s0i = s0re_ref[...], s0im_ref[...]
    vr, vi = vre_ref[...], vim_ref[...]
    first = lax.broadcasted_iota(jnp.int32, (s0r.shape[0], 1), 0) % 2 == 0
    tr, ti = ar * s0r - ai * s0i + vr, ar * s0i + ai * s0r + vi
    sr = jnp.where(first, s0r, pltpu.roll(tr, 1, 0))
    si = jnp.where(first, s0i, pltpu.roll(ti, 1, 0))
    sre_ref[...] = sr
    sim_ref[...] = si
    fre_ref[...] = ar * sr - ai * si + vr
    fim_ref[...] = ar * si + ai * sr + vi


def _s5_scan_stream_kernel(vre_ref, vim_ref, a_ref, s0re_ref, s0im_ref,
                           sre_ref, sim_ref, fre_ref, fim_ref, *, steps):
    q = lax.broadcasted_iota(jnp.int32, (SUBLANES, 1), 0)
    a1 = (a_ref[0], a_ref[1])
    a2 = _cmul(*a1, *a1)
    a4 = _cmul(*a2, *a2)
    a8 = _cmul(*a4, *a4)
    strides = ((1, a1), (2, a2), (4, a4))
    width = a1[0].shape[1]
    pw = (jnp.ones((SUBLANES, width), F32), jnp.zeros((SUBLANES, width), F32))
    masked = []
    for k, ak in strides:
        nxt = _cmul(*pw, *ak)
        bit = (q & k) != 0
        pw = (jnp.where(bit, nxt[0], pw[0]), jnp.where(bit, nxt[1], pw[1]))
        masked.append((k, jnp.where(q >= k, ak[0], 0.0), jnp.where(q >= k, ak[1], 0.0)))

    def shifted(x, k):
        return jnp.where(q >= k, pltpu.roll(x, k, 0), 0.0)

    def body(t, carry):
        cr, ci = carry
        rows = pl.ds(pl.multiple_of(t * SUBLANES, SUBLANES), SUBLANES)
        pr, pi = vre_ref[rows, :], vim_ref[rows, :]
        for k, mr, mi in masked:
            dr, di = _cmul(pltpu.roll(pr, k, 0), pltpu.roll(pi, k, 0), mr, mi)
            pr, pi = pr + dr, pi + di
        xr, xi = _cmul(*pw, cr, ci)
        sre_ref[rows, :] = xr + shifted(pr, 1)
        sim_ref[rows, :] = xi + shifted(pi, 1)
        nr, ni = _cmul(*a8, cr, ci)
        return nr + pr[SUBLANES - 1:], ni + pi[SUBLANES - 1:]

    cr, ci = lax.fori_loop(0, steps // SUBLANES, body, (s0re_ref[0], s0im_ref[0]))
    fre_ref[0] = cr
    fim_ref[0] = ci


def s5_scan(v_re, v_im, a16, s0_re, s0_im, width):
    rows, n = v_re.shape
    streams = s0_re.shape[0]
    steps = rows // streams
    a_spec = pl.BlockSpec((2, 1, width), lambda *i: (0, 0, i[-1]))
    vout = jax.ShapeDtypeStruct((rows, n), F32)
    long_streams = steps % SUBLANES == 0
    if long_streams:
        kern = functools.partial(_s5_scan_stream_kernel, steps=steps)
        grid = (streams, n // width)
        vspec = pl.BlockSpec((steps, width), lambda b, c: (b, c))
        sspec = pl.BlockSpec((1, 1, width), lambda b, c: (b, 0, c))
        s0_re, s0_im = s0_re.reshape(streams, 1, n), s0_im.reshape(streams, 1, n)
        sout = jax.ShapeDtypeStruct((streams, 1, n), F32)
    else:
        assert steps == 2 and rows % SUBLANES == 0
        kern = _s5_scan_pairs_kernel
        grid = (n // width,)
        vspec = pl.BlockSpec((rows, width), lambda c: (0, c))
        sspec = vspec
        s0_re, s0_im = jnp.repeat(s0_re, steps, axis=0), jnp.repeat(s0_im, steps, axis=0)
        sout = vout
    s_re, s_im, f_re, f_im = pl.pallas_call(
        kern,
        grid=grid,
        in_specs=[vspec, vspec, a_spec, sspec, sspec],
        out_specs=[vspec, vspec, sspec, sspec],
        out_shape=[vout, vout, sout, sout],
        compiler_params=_cparams(*(["parallel"] * len(grid))),
        name="s5_scan",
    )(v_re, v_im, a16, s0_re, s0_im)
    if long_streams:
        return s_re, s_im, f_re.reshape(streams, n), f_im.reshape(streams, n)
    return s_re, s_im, f_re[steps - 1::steps], f_im[steps - 1::steps]


def _s5_toeplitz_kernel(u_ref, sre_ref, sim_ref, bt_ref, cct_ref, y_ref):
    npair = SSM_FOLD // 2
    u2 = [jnp.concatenate([u_ref[2 * mm, 0], u_ref[2 * mm + 1, 0]], axis=1) for mm in range(npair)]
    s_prev = jnp.concatenate([sre_ref[...], sim_ref[...]], axis=1).astype(BF16)
    for ll in range(npair):
        acc = _dot_nt(s_prev, cct_ref[0, ll * 2 * LANES:(ll + 1) * 2 * LANES, :])
        for mm in range(ll + 1):
            acc = acc + _dot(u2[mm], bt_ref[0, ll - mm])
        y_ref[2 * ll, 0] = acc[:, :LANES]
        y_ref[2 * ll + 1, 0] = acc[:, LANES:]


def s5_toeplitz(u, s_re, s_im, bt, cct, rt):
    rows = u.shape[2]
    u_spec, s_spec = _s5_tile_specs(rt)
    return pl.pallas_call(
        _s5_toeplitz_kernel,
        grid=(N_LANE_TILES, rows // rt),
        in_specs=[u_spec, s_spec, s_spec,
                  pl.BlockSpec((1, SSM_FOLD // 2, 2 * LANES, 2 * LANES), lambda t, i: (t, 0, 0, 0)),
                  pl.BlockSpec((1, SSM_FOLD * LANES, 2 * STATE_PER_TILE), lambda t, i: (t, 0, 0))],
        out_specs=u_spec,
        out_shape=jax.ShapeDtypeStruct(u.shape, F32),
        compiler_params=_cparams("parallel", "parallel"),
        name="s5_toeplitz",
    )(u, s_re, s_im, bt, cct)


def _s5_glu_kernel(x_ref, u_ref, y_ref, d_ref, wglu_ref, bglu_ref, perm_ref, o_ref):
    nblk = x_ref.shape[0] // FOLD_BLOCK

    def gelu_block(blk):
        rows = slice(blk * SSM_FOLD, (blk + 1) * SSM_FOLD)
        gl = []
        for m in range(SSM_FOLD):
            ym = jnp.concatenate([y_ref[m, k, rows, :] for k in range(N_LANE_TILES)], axis=1)
            hm = jnp.concatenate([u_ref[m, k, rows, :] for k in range(N_LANE_TILES)], axis=1)
            gl.append(jax.nn.gelu(ym + d_ref[...] * hm.astype(F32)))
        gl = jnp.concatenate(gl, axis=0)
        return gl, _dot(gl.astype(BF16), wglu_ref[...])

    def finish(blk, gl, z):
        tok = slice(blk * FOLD_BLOCK, (blk + 1) * FOLD_BLOCK)
        hi, lo = _split_bf16(gl * jax.nn.sigmoid(z + bglu_ref[...]))
        o_ref[tok, :] = x_ref[tok, :] + (_dot(perm_ref[...], hi) + _dot(perm_ref[...], lo))

    prev = gelu_block(0)
    for blk in range(1, nblk):
        cur = gelu_block(blk)
        finish(blk - 1, *prev)
        prev = cur
    finish(nblk - 1, *prev)


def s5_glu(x, u, y, d, w_glu, b_glu, perm, tokens):
    t, dm = x.shape
    fold_spec = pl.BlockSpec((SSM_FOLD, N_LANE_TILES, tokens // SSM_FOLD, LANES),
                             lambda i: (0, 0, i, 0))
    x_spec = pl.BlockSpec((tokens, dm), lambda i: (i, 0))
    return pl.pallas_call(
        _s5_glu_kernel,
        grid=(t // tokens,),
        in_specs=[x_spec, fold_spec, fold_spec, _const_spec((1, dm)), _weight_spec(w_glu),
                  _const_spec((1, dm)), _const_spec((FOLD_BLOCK, FOLD_BLOCK))],
        out_specs=x_spec,
        out_shape=jax.ShapeDtypeStruct((t, dm), F32),
        compiler_params=_cparams("parallel"),
        name="s5_glu",
    )(x, u, y, d.reshape(1, dm), w_glu[0], b_glu.reshape(1, dm), perm)


def s5_mixer(x, st_re, st_im, g, ops, d, w_glu, b_glu):
    bt, cct, w_re, w_im, a16 = ops
    b, l, dm = x.shape
    rows = b * l // SSM_FOLD
    x2 = x.reshape(b * l, dm)
    perm = _fold_perm()
    tokens = min(S5_TOKEN_TILE, b * l)
    rt = min(S5_ROW_TILE, rows)
    u = s5_fold(x2, g, perm, tokens)
    v_re, v_im = s5_state_in(u, w_re, w_im, rt)
    s_re, s_im, f_re, f_im = s5_scan(v_re, v_im, a16, st_re.reshape(b, N_STATE),
                                     st_im.reshape(b, N_STATE), S5_SCAN_WIDTH)
    y = s5_toeplitz(u, s_re, s_im, bt, cct, rt)
    out = s5_glu(x2, u, y, d, w_glu, b_glu, perm, tokens)
    shape = (b, SSM_GROUPS, SSM_STATE)
    return out.reshape(b, l, dm), f_re.reshape(shape), f_im.reshape(shape)


def _stream_tiling(seq_len, n_streams, tile_rows):
    if seq_len >= tile_rows:
        return 1, tile_rows
    return min(n_streams, max(SHORT_STREAM_TILE_ROWS // seq_len, 1)), seq_len


def kernel(x_prompt, x_sample, mem_prompt, state_ssm_re, state_ssm_im, state_ret, cache_mem_k, cache_mem_v, cache_conv, norm_mix, norm_mem_q, norm_mem_kv, norm_ffn, norm_final, ssm_a_re, ssm_a_im, ssm_log_dt, ssm_b_re, ssm_b_im, ssm_c_re, ssm_c_im, ssm_d, ssm_w_glu, ssm_b_glu, ret_w_qkvg, ret_w_o, mem_w_q, mem_w_kv, mem_w_o, ffn_w_up, ffn_conv_w, ffn_conv_b, ffn_w_down):
    bp, _, dm = x_prompt.shape
    depth = norm_mix.shape[0]
    d_ff = ffn_w_down.shape[1]
    streams = [dict(x=x, b=x.shape[0], l=x.shape[1],
                    attn_tiling=_stream_tiling(x.shape[1], x.shape[0], ATTN_TILE_ROWS),
                    ffn_tiling=_stream_tiling(x.shape[1], x.shape[0], FFN_TILE_ROWS))
               for x in (x_prompt, x_sample)]
    cache_k, cache_v = _kv_rows(cache_mem_k), _kv_rows(cache_mem_v)
    ssm_w_glu, ret_w_qkvg, ret_w_o, mem_w_q, mem_w_kv, mem_w_o, ffn_w_up, ffn_w_down = (
        w.astype(BF16) for w in (ssm_w_glu, ret_w_qkvg, ret_w_o, mem_w_q, mem_w_kv, mem_w_o,
                                 ffn_w_up, ffn_w_down))
    mem_k, mem_v = mem_kv(mem_prompt, norm_mem_kv, mem_w_kv)
    outs = dict(ssm_re=([], []), ssm_im=([], []), ret=([], []), conv=([], []))

    for i in range(depth):
        j = i // 2
        if i % 2 == 0:
            ops = s5_prepare(ssm_a_re[j], ssm_a_im[j], ssm_log_dt[j], ssm_b_re[j], ssm_b_im[j],
                             ssm_c_re[j], ssm_c_im[j])
            w_glu = (ssm_w_glu, j)
            zeros = jnp.zeros((bp, SSM_GROUPS, SSM_STATE), F32)
            states = [(zeros, zeros), (state_ssm_re[j], state_ssm_im[j])]
            for si, (s, (st_re, st_im)) in enumerate(zip(streams, states)):
                s["x"], f_re, f_im = s5_mixer(s["x"], st_re, st_im, norm_mix[i], ops, ssm_d[j],
                                              w_glu, ssm_b_glu[j])
                outs["ssm_re"][si].append(f_re)
                outs["ssm_im"][si].append(f_im)
        else:
            w_qkvg = (ret_w_qkvg, j)
            w_o = (ret_w_o, j)
            zeros = jnp.zeros((bp, RET_HEADS, RET_DK, RET_DV), F32)
            chunk = RET_CHUNK_ROWS
            for si, (s, st, pos0) in enumerate(zip(streams, [zeros, state_ret[j]], [0, PAST_LEN])):
                x2 = s["x"].reshape(s["b"] * s["l"], dm)
                cos, sin = _rope_tables(pos0, s["l"])
                if s["l"] >= chunk:
                    y, st_new = ret_mixer(x2, norm_mix[i], w_qkvg, w_o, cos, sin, st, s["l"], chunk)
                else:
                    p = rms_matmul(x2, norm_mix[i], w_qkvg, BF16)
                    y, st_new = ret_core(p, cos, sin, st, s["l"], min(RET_STREAMS_PER_STEP, s["b"]))
                    y = matmul_res(y, w_o, x2)
                s["x"] = y.reshape(s["x"].shape)
                outs["ret"][si].append(st_new)

        w_q = (mem_w_q, i)
        w_o = (mem_w_o, i)
        for s, (k_, v_) in zip(streams, [(mem_k, mem_v), (cache_k, cache_v)]):
            x2 = s["x"].reshape(s["b"] * s["l"], dm)
            y = mem_attn(x2, norm_mem_q[i], w_q, k_, v_, i, w_o, s["l"], *s["attn_tiling"])
            s["x"] = y.reshape(s["x"].shape)

        w_up = (ffn_w_up, i)
        w_dn = (ffn_w_down, i)
        g_final = norm_final if i == depth - 1 else None
        conv_states = [jnp.zeros((bp, 2, d_ff), F32), cache_conv[i]]
        for si, (s, cst) in enumerate(zip(streams, conv_states)):
            x2 = s["x"].reshape(s["b"] * s["l"], dm)
            y, cnew = conv_ffn(x2, norm_ffn[i], w_up, ffn_conv_w[i], ffn_conv_b[i], w_dn, cst,
                               s["l"], *s["ffn_tiling"], g_final=g_final)
            s["x"] = y.reshape(s["x"].shape)
            outs["conv"][si].append(cnew)

    return (streams[0]["x"], streams[1]["x"],
            jnp.stack(outs["ssm_re"][0]), jnp.stack(outs["ssm_im"][0]),
            jnp.stack(outs["ssm_re"][1]), jnp.stack(outs["ssm_im"][1]),
            jnp.stack(outs["ret"][0]), jnp.stack(outs["ret"][1]),
            _kv_unrows(mem_k, MEM_HEADS), _kv_unrows(mem_v, MEM_HEADS),
            jnp.stack(outs["conv"][0]), jnp.stack(outs["conv"][1]))
```

```python
import functools
import math

import numpy as np
import jax
import jax.numpy as jnp
from jax import lax
from jax.experimental import pallas as pl
from jax.experimental.pallas import tpu as pltpu

F32 = jnp.float32
BF16 = jnp.bfloat16

D_MODEL = 1024
PAST_LEN = 1024
EPS = 1e-6
GN_EPS = 1e-5
ROPE_BASE = 10000.0
LANES = 128
SUBLANES = 8
SSM_GROUP = 16
SSM_GROUPS = D_MODEL // SSM_GROUP
SSM_STATE = 64
SSM_FOLD = 16
GROUPS_PER_TILE = LANES // SSM_GROUP
N_LANE_TILES = D_MODEL // LANES
STATE_PER_TILE = GROUPS_PER_TILE * SSM_STATE
N_STATE = SSM_GROUPS * SSM_STATE
RET_HEADS = 4
RET_DK = D_MODEL // RET_HEADS
RET_DV = 2 * D_MODEL // RET_HEADS
RET_QK = RET_HEADS * RET_DK
RET_V = RET_HEADS * RET_DV
RET_LOG_G = tuple(math.log(1.0 - 2.0 ** (-5.0 - h)) for h in range(RET_HEADS))
MEM_HEADS = 4
MEM_HD = D_MODEL // MEM_HEADS
VMEM_LIMIT = 56 * 1024 * 1024
ATTN_TILE_ROWS = 1024
FFN_TILE_ROWS = 256
SHORT_STREAM_TILE_ROWS = 256
RET_CHUNK_ROWS = 256
RET_STREAMS_PER_STEP = 4
S5_TOKEN_TILE = 1024
S5_ROW_TILE = 1024
S5_SCAN_WIDTH = 1024


def _cparams(*sem):
    return pltpu.CompilerParams(dimension_semantics=sem, vmem_limit_bytes=VMEM_LIMIT)


def _const_spec(shape):
    nd = len(shape)
    return pl.BlockSpec(shape, lambda *_: (0,) * nd, pipeline_mode=pl.Buffered(1))


def _weight_spec(w):
    stack, layer = w
    return pl.BlockSpec((None,) + stack.shape[1:], lambda *_: (layer, 0, 0),
                        pipeline_mode=pl.Buffered(1))


def _rms_scale(x):
    return lax.rsqrt(jnp.mean(x * x, axis=-1, keepdims=True) + EPS)


def _prenorm(x, g_ref):
    return (x * g_ref[...]).astype(BF16), _rms_scale(x)


def _split_bf16(x):
    hi = x.astype(BF16)
    lo = (x - hi.astype(F32)).astype(BF16)
    return hi, lo


def _dot(a, b):
    return jnp.dot(a, b, preferred_element_type=F32)


def _dot_nt(a, b):
    return lax.dot_general(a, b, (((1,), (1,)), ((), ())), preferred_element_type=F32)


def _rms_matmul_kernel(x_ref, g_ref, w_ref, o_ref, *, tn):
    x = x_ref[...]
    h = (x * _rms_scale(x) * g_ref[...]).astype(BF16)
    for n0 in range(0, w_ref.shape[1], tn):
        o_ref[:, n0:n0 + tn] = _dot(h, w_ref[:, n0:n0 + tn]).astype(o_ref.dtype)


def rms_matmul(x, g, w, out_dtype, tm=512, tn=512):
    t, d = x.shape
    n = w[0].shape[2]
    tm = min(tm, t)
    return pl.pallas_call(
        functools.partial(_rms_matmul_kernel, tn=tn),
        grid=(t // tm,),
        in_specs=[pl.BlockSpec((tm, d), lambda i: (i, 0)),
                  _const_spec((1, d)),
                  _weight_spec(w)],
        out_specs=pl.BlockSpec((tm, n), lambda i: (i, 0)),
        out_shape=jax.ShapeDtypeStruct((t, n), out_dtype),
        compiler_params=_cparams("parallel"),
        name="rms_matmul",
    )(x, g.reshape(1, d), w[0])


def _matmul_res_kernel(y_ref, w_ref, x_ref, o_ref):
    o_ref[...] = x_ref[...] + _dot(y_ref[...], w_ref[...])


def matmul_res(y, w, x, tm=512):
    t, k = y.shape
    d = w[0].shape[2]
    tm = min(tm, t)
    return pl.pallas_call(
        _matmul_res_kernel,
        grid=(t // tm,),
        in_specs=[pl.BlockSpec((tm, k), lambda i: (i, 0)),
                  _weight_spec(w),
                  pl.BlockSpec((tm, d), lambda i: (i, 0))],
        out_specs=pl.BlockSpec((tm, d), lambda i: (i, 0)),
        out_shape=jax.ShapeDtypeStruct((t, d), F32),
        compiler_params=_cparams("parallel"),
        name="matmul_res",
    )(y, w[0], x)


MEM_LANE_TILES = MEM_HD // LANES
MEM_HEAD_GROUP = 2


def _kv_rows(x):
    *lead, n, h, hd = x.shape
    x = x.reshape(*lead, n, h, MEM_LANE_TILES, LANES)
    return jnp.swapaxes(x, -3, -2).reshape(*lead, n * MEM_LANE_TILES * h, LANES)


def _kv_unrows(x, heads):
    *lead, rows, _ = x.shape
    n = rows // (MEM_LANE_TILES * heads)
    x = x.reshape(*lead, n, MEM_LANE_TILES, heads, LANES)
    return jnp.swapaxes(x, -3, -2).reshape(*lead, n, heads, MEM_LANE_TILES * LANES)


def _mem_kv_kernel(x_ref, g_ref, w_ref, k_ref, v_ref):
    x = x_ref[...]
    h = (x * _rms_scale(x) * g_ref[...]).astype(BF16)
    n_mem, d = x.shape
    stride = MEM_LANE_TILES * MEM_HEADS
    for which, ref in enumerate((k_ref, v_ref)):
        for hd in range(MEM_HEADS):
            c0 = which * d + hd * MEM_HD
            res = _dot(h, w_ref[:, c0:c0 + MEM_HD])
            for j in range(MEM_LANE_TILES):
                ref[pl.ds(j * MEM_HEADS + hd, n_mem, stride=stride), :] = (
                    res[:, j * LANES:(j + 1) * LANES])


def mem_kv(mem, g, w_kv):
    b, n_mem, d = mem.shape
    depth = w_kv.shape[0]
    rows = n_mem * MEM_LANE_TILES * MEM_HEADS
    out_spec = pl.BlockSpec((None, None, rows, LANES), lambda l, i: (l, i, 0, 0))
    out = jax.ShapeDtypeStruct((depth, b, rows, LANES), F32)
    return pl.pallas_call(
        _mem_kv_kernel,
        grid=(depth, b),
        in_specs=[pl.BlockSpec((None, n_mem, d), lambda l, i: (i, 0, 0)),
                  pl.BlockSpec((None, 1, d), lambda l, i: (l, 0, 0)),
                  pl.BlockSpec((None,) + w_kv.shape[1:], lambda l, i: (l, 0, 0))],
        out_specs=[out_spec, out_spec],
        out_shape=[out, out],
        compiler_params=_cparams("parallel", "parallel"),
        name="mem_kv",
    )(mem, g.reshape(depth, 1, d), w_kv)


def _head_rows(kv_ref, s, hd):
    stride = MEM_LANE_TILES * MEM_HEADS
    n_mem = kv_ref.shape[2] // stride
    return jnp.concatenate(
        [kv_ref[0, s, pl.ds(j * MEM_HEADS + hd, n_mem, stride=stride), :]
         for j in range(MEM_LANE_TILES)], axis=1).astype(BF16)


def _mem_attn_kernel(x_ref, g_ref, wq_ref, k_ref, v_ref, wo_ref, o_ref, *, nseg, seg, head_group):
    scale = MEM_HD ** -0.5
    x = x_ref[...]
    h, row_scale = _prenorm(x, g_ref)
    o_ref[...] = x

    width = head_group * MEM_HD

    def q_proj(grp):
        return (_dot(h, wq_ref[:, grp * width:(grp + 1) * width]) * row_scale).astype(BF16)

    nxt = q_proj(0)
    pending = None
    for grp in range(MEM_HEADS // head_group):
        q = nxt
        pairs = [(grp * head_group + hg, hg, s) for hg in range(head_group) for s in range(nseg)]
        scores = [_dot_nt(q[s * seg:(s + 1) * seg, hg * MEM_HD:(hg + 1) * MEM_HD],
                          _head_rows(k_ref, s, hd)) for hd, hg, s in pairs]
        if (grp + 1) * head_group < MEM_HEADS:
            nxt = q_proj(grp + 1)
        if pending is not None:
            o_ref[...] += _dot(pending, wo_ref[(grp - 1) * width:grp * width, :])
        outs = []
        for (hd, hg, s), sc in zip(pairs, scores):
            sc = sc * scale
            p = jnp.exp(sc - jnp.max(sc, axis=-1, keepdims=True))
            denom = jnp.sum(p, axis=-1, keepdims=True)
            outs.append((_dot(p.astype(BF16), _head_rows(v_ref, s, hd)) / denom).astype(BF16))
        heads = [outs[hg * nseg] if nseg == 1 else jnp.concatenate(outs[hg * nseg:(hg + 1) * nseg], axis=0)
                 for hg in range(head_group)]
        pending = heads[0] if head_group == 1 else jnp.concatenate(heads, axis=1)
    o_ref[...] += _dot(pending, wo_ref[MEM_HEADS * MEM_HD - width:, :])


def mem_attn(x, g, w_q, mk, mv, layer, w_o, seq_len, nseg, seg):
    t, d = x.shape
    b = mk.shape[1]
    tiles_per_stream = max(seq_len // (nseg * seg), 1)
    n_outer = b // nseg
    tm = nseg * seg
    x_spec = pl.BlockSpec((tm, d), lambda o, i: (o * tiles_per_stream + i, 0))
    kv_spec = pl.BlockSpec((1, nseg) + mk.shape[2:], lambda o, i: (layer, o, 0, 0))
    return pl.pallas_call(
        functools.partial(_mem_attn_kernel, nseg=nseg, seg=seg, head_group=MEM_HEAD_GROUP),
        grid=(n_outer, tiles_per_stream),
        in_specs=[x_spec, _const_spec((1, d)), _weight_spec(w_q), kv_spec, kv_spec,
                  _weight_spec(w_o)],
        out_specs=x_spec,
        out_shape=jax.ShapeDtypeStruct((t, d), F32),
        compiler_params=_cparams("parallel", "parallel"),
        name="mem_attn",
    )(x, g.reshape(1, d), w_q[0], mk, mv, w_o[0])


def _ffn_kernel(x_ref, g_ref, wup_ref, cw_ref, cb_ref, wdn_ref, st_ref, *rest,
                nseg, seg, fc, final_norm):
    if final_norm:
        gf_ref, o_ref, nst_ref, carry_ref = rest
    else:
        o_ref, nst_ref, carry_ref = rest
    d_ff = wdn_ref.shape[0]
    tm = nseg * seg

    @pl.when(pl.program_id(1) == 0)
    def _():
        carry_ref[...] = st_ref[...]

    x = x_ref[...]
    h, scale = _prenorm(x, g_ref)
    row = lax.broadcasted_iota(jnp.int32, (tm, 1), 0) % seg
    o_ref[...] = x

    def up_proj(f0):
        return (_dot(h, wup_ref[:, f0:f0 + fc]) * scale,
                _dot(h, wup_ref[:, d_ff + f0:d_ff + f0 + fc]) * scale)

    nxt = up_proj(0)
    pending = None
    for f0 in range(0, d_ff, fc):
        cols = slice(f0, f0 + fc)
        a, gate = nxt
        if f0 + fc < d_ff:
            nxt = up_proj(f0 + fc)
        if pending is not None:
            o_ref[...] += _dot(pending, wdn_ref[f0 - fc:f0, :])
        p0 = jnp.broadcast_to(carry_ref[:, 0:1, cols], (nseg, seg, fc)).reshape(tm, fc)
        p1 = jnp.broadcast_to(carry_ref[:, 1:2, cols], (nseg, seg, fc)).reshape(tm, fc)
        a1 = jnp.where(row == 0, p1, pltpu.roll(a, 1, 0))
        a2 = jnp.where(row == 0, p0, jnp.where(row == 1, p1, pltpu.roll(a, 2, 0)))
        conv = (cw_ref[0:1, cols] * a2 + cw_ref[1:2, cols] * a1 + cw_ref[2:3, cols] * a
                + cb_ref[:, cols])
        p = jax.nn.gelu(conv) * gate
        pending = p.astype(BF16)
        carry_ref[:, :, cols] = a.reshape(nseg, seg, fc)[:, seg - 2:seg, :]
    y = o_ref[...] + _dot(pending, wdn_ref[d_ff - fc:d_ff, :])
    nst_ref[...] = carry_ref[...]
    if final_norm:
        y = y * _rms_scale(y) * gf_ref[...]
    o_ref[...] = y


def conv_ffn(x, g, w_up, conv_w, conv_b, w_down, conv_state, seq_len, nseg, seg, g_final=None,
             fc=256):
    t, d = x.shape
    d_ff = w_down[0].shape[1]
    b = conv_state.shape[0]
    tiles_per_stream = max(seq_len // (nseg * seg), 1)
    n_outer = b // nseg
    tm = nseg * seg
    final_norm = g_final is not None
    in_specs = [pl.BlockSpec((tm, d), lambda o, i: (o * tiles_per_stream + i, 0)),
                _const_spec((1, d)),
                _weight_spec(w_up),
                _const_spec((3, d_ff)),
                _const_spec((1, d_ff)),
                _weight_spec(w_down),
                pl.BlockSpec((nseg, 2, d_ff), lambda o, i: (o, 0, 0))]
    args = [x, g.reshape(1, d), w_up[0], conv_w, conv_b.reshape(1, d_ff), w_down[0], conv_state]
    if final_norm:
        in_specs.append(_const_spec((1, d)))
        args.append(g_final.reshape(1, d))
    return pl.pallas_call(
        functools.partial(_ffn_kernel, nseg=nseg, seg=seg, fc=fc, final_norm=final_norm),
        grid=(n_outer, tiles_per_stream),
        in_specs=in_specs,
        out_specs=[pl.BlockSpec((tm, d), lambda o, i: (o * tiles_per_stream + i, 0)),
                   pl.BlockSpec((nseg, 2, d_ff), lambda o, i: (o, 0, 0))],
        out_shape=[jax.ShapeDtypeStruct((t, d), F32),
                   jax.ShapeDtypeStruct((b, 2, d_ff), F32)],
        scratch_shapes=[pltpu.VMEM((nseg, 2, d_ff), F32)],
        compiler_params=_cparams("parallel", "arbitrary"),
        name="conv_ffn",
    )(*args)


def _rotate(x, cos, sin):
    half = x.shape[1] // 2
    x1, x2 = x[:, :half], x[:, half:]
    return jnp.concatenate([x1 * cos - x2 * sin, x2 * cos + x1 * sin], axis=1)


def _ret_head(h, q, k, v, g, cos, sin, s_ref, chunk, fillers=(), stream=0):
    fillers = list(fillers) + [lambda: None] * (3 - len(fillers))
    log_g = RET_LOG_G[h]
    li = lax.broadcasted_iota(jnp.int32, (chunk, chunk), 0)
    mi = lax.broadcasted_iota(jnp.int32, (chunk, chunk), 1)
    diff = (li - mi).astype(F32)
    pos = lax.broadcasted_iota(jnp.int32, (chunk, 1), 0).astype(F32)
    fillers[0]()
    qr = _rotate(q, cos, sin).astype(BF16)
    kr = _rotate(k, cos, sin) * (RET_DK ** -0.5)
    decay = jnp.where(diff >= 0, jnp.exp(jnp.maximum(diff, 0.0) * log_g), 0.0)
    scores = _dot_nt(qr, kr.astype(BF16))
    fillers[1]()
    scores = scores * decay
    intra = _dot(scores.astype(BF16), v)
    s_old = s_ref[stream, h]
    cross = _dot(qr, s_old.astype(BF16)) * jnp.exp((pos + 1.0) * log_g)
    o = intra + cross
    k_tail = (kr * jnp.exp((chunk - 1.0 - pos) * log_g)).astype(BF16)
    s_ref[stream, h] = math.exp(chunk * log_g) * s_old + lax.dot_general(
        k_tail, v, (((0,), (0,)), ((), ())), preferred_element_type=F32)
    fillers[2]()
    mu = jnp.mean(o, axis=-1, keepdims=True)
    oc = o - mu
    var = jnp.mean(oc * oc, axis=-1, keepdims=True)
    return jax.nn.silu(g) * (oc * lax.rsqrt(var + GN_EPS))


def _ret_kernel(p_ref, cos_ref, sin_ref, s_in_ref, y_ref, s_out_ref, *, chunk):
    s_out_ref[...] = s_in_ref[...]
    cos = cos_ref[...]
    sin = sin_ref[...]
    for s in range(s_in_ref.shape[0]):
        rows = slice(s * chunk, (s + 1) * chunk)
        for h in range(RET_HEADS):
            q = p_ref[rows, h * RET_DK:(h + 1) * RET_DK].astype(F32)
            k = p_ref[rows, RET_QK + h * RET_DK:RET_QK + (h + 1) * RET_DK].astype(F32)
            v = p_ref[rows, 2 * RET_QK + h * RET_DV:2 * RET_QK + (h + 1) * RET_DV]
            g = p_ref[rows, 2 * RET_QK + RET_V + h * RET_DV:
                      2 * RET_QK + RET_V + (h + 1) * RET_DV].astype(F32)
            y = _ret_head(h, q, k, v, g, cos, sin, s_out_ref, chunk, stream=s)
            y_ref[rows, h * RET_DV:(h + 1) * RET_DV] = y.astype(y_ref.dtype)


def _ret_fused_kernel(x_ref, g_ref, w_ref, wo_ref, cos_ref, sin_ref, s_in_ref, o_ref, s_out_ref,
                      *, chunk):
    @pl.when(pl.program_id(1) == 0)
    def _():
        s_out_ref[...] = s_in_ref[...]

    cos = cos_ref[...]
    sin = sin_ref[...]
    x = x_ref[...]
    hn, row_scale = _prenorm(x, g_ref)
    o_ref[...] = x

    offsets = (0, RET_QK, 2 * RET_QK, 2 * RET_QK + RET_V)
    widths = (RET_DK, RET_DK, RET_DV, RET_DV)

    def proj(h, part):
        c0 = offsets[part] + h * widths[part]
        y = _dot(hn, w_ref[:, c0:c0 + widths[part]]) * row_scale
        return y.astype(BF16) if part == 2 else y

    def out_proj(h, y):
        o_ref[...] += _dot(y, wo_ref[h * RET_DV:(h + 1) * RET_DV, :])

    cur = [proj(0, part) for part in range(4)]
    ys = []
    for h in range(RET_HEADS):
        nxt = [None] * 4

        def fill_qk(h=h, nxt=nxt):
            nxt[0], nxt[1] = proj(h + 1, 0), proj(h + 1, 1)

        def fill_v(h=h, nxt=nxt):
            nxt[2] = proj(h + 1, 2)

        def fill_g(h=h, nxt=nxt):
            nxt[3] = proj(h + 1, 3)

        if h + 1 < RET_HEADS:
            fillers = (fill_qk, fill_v, fill_g)
        else:
            fillers = [functools.partial(out_proj, i, y) for i, y in enumerate(ys)]
        ys.append(_ret_head(h, *cur, cos, sin, s_out_ref, chunk, fillers).astype(BF16))
        cur = nxt
    out_proj(RET_HEADS - 1, ys[-1])


def _ret_specs(seq_len, chunk):
    nchunks = seq_len // chunk
    half = RET_DK // 2
    row_spec = lambda n: pl.BlockSpec((chunk, n), lambda o, i: (o * nchunks + i, 0))
    rope_spec = pl.BlockSpec((chunk, half), lambda o, i: (i, 0))
    state_spec = pl.BlockSpec((1, RET_HEADS, RET_DK, RET_DV), lambda o, i: (o, 0, 0, 0))
    return nchunks, row_spec, rope_spec, state_spec


def ret_core(p, cos, sin, state, chunk, streams_per_step):
    t, n = p.shape
    b = state.shape[0]
    nb = streams_per_step
    rows_spec = lambda width: pl.BlockSpec((nb * chunk, width), lambda i: (i, 0))
    rope_spec = _const_spec(cos.shape)
    state_spec = pl.BlockSpec((nb,) + state.shape[1:], lambda i: (i, 0, 0, 0))
    return pl.pallas_call(
        functools.partial(_ret_kernel, chunk=chunk),
        grid=(b // nb,),
        in_specs=[rows_spec(n), rope_spec, rope_spec, state_spec],
        out_specs=[rows_spec(RET_V), state_spec],
        out_shape=[jax.ShapeDtypeStruct((t, RET_V), BF16),
                   jax.ShapeDtypeStruct(state.shape, F32)],
        compiler_params=_cparams("parallel"),
        name="ret_core",
    )(p, cos, sin, state)


def ret_mixer(x, g, w_qkvg, w_o, cos, sin, state, seq_len, chunk):
    t, d = x.shape
    nchunks, row_spec, rope_spec, state_spec = _ret_specs(seq_len, chunk)
    return pl.pallas_call(
        functools.partial(_ret_fused_kernel, chunk=chunk),
        grid=(state.shape[0], nchunks),
        in_specs=[row_spec(d), _const_spec((1, d)), _weight_spec(w_qkvg),
                  _weight_spec(w_o), rope_spec, rope_spec, state_spec],
        out_specs=[row_spec(d), state_spec],
        out_shape=[jax.ShapeDtypeStruct((t, d), F32), jax.ShapeDtypeStruct(state.shape, F32)],
        compiler_params=_cparams("parallel", "arbitrary"),
        name="ret_mixer",
    )(x, g.reshape(1, d), w_qkvg[0], w_o[0], cos, sin, state)


def _rope_tables(pos0, length):
    half = RET_DK // 2
    freqs = ROPE_BASE ** (-np.arange(half, dtype=np.float64) / half)
    ang = (pos0 + np.arange(length, dtype=np.float64))[:, None] * freqs[None, :]
    return jnp.asarray(np.cos(ang), F32), jnp.asarray(np.sin(ang), F32)


def _cmul(ar, ai, br, bi):
    return ar * br - ai * bi, ar * bi + ai * br


def _s5_prep_kernel(lam_re_ref, lam_im_ref, ldt_ref, b_re_ref, b_im_ref, c_re_ref, c_im_ref,
                    w_re_ref, w_im_ref, bt_ref, cct_ref, a16_ref):
    dt = jnp.exp(ldt_ref[...])
    lam_re, lam_im = lam_re_ref[...], lam_im_ref[...]
    mag = jnp.exp(lam_re * dt)
    ar, ai = mag * jnp.cos(lam_im * dt), mag * jnp.sin(lam_im * dt)
    den = lam_re * lam_re + lam_im * lam_im
    nr, ni = ar - 1.0, ai
    coef_re = (nr * lam_re + ni * lam_im) / den
    coef_im = (ni * lam_re - nr * lam_im) / den
    bb_re, bb_im = _cmul(coef_re, coef_im, b_re_ref[...], b_im_ref[...])
    c_re, c_im = c_re_ref[...], c_im_ref[...]
    lane = lax.broadcasted_iota(jnp.int32, (1, LANES), 1)
    low = lane < SSM_STATE
    rg = lax.broadcasted_iota(jnp.int32, (LANES, 1), 0) // SSM_GROUP
    pair = rg // 2
    own_half = (lane // SSM_STATE) == (rg % 2)
    same_group = rg == lane // SSM_GROUP
    c_hi, c_lo = _split_bf16(jnp.where(low, c_re, -c_im))
    tk = []
    pr, pi = jnp.ones_like(ar), jnp.zeros_like(ar)
    for j in range(SSM_FOLD):
        zr, zi = _cmul(pr, pi, bb_re, bb_im)
        w_re_ref[SSM_FOLD - 1 - j] = jnp.where(own_half, zr, 0.0)
        w_im_ref[SSM_FOLD - 1 - j] = jnp.where(own_half, zi, 0.0)
        b_hi, b_lo = _split_bf16(jnp.where(low, zr, zi))
        kk = _dot_nt(b_hi, c_hi) + _dot_nt(b_lo, c_hi) + _dot_nt(b_hi, c_lo)
        tk.append(jnp.where(same_group, kk, 0.0).astype(BF16))
        pr, pi = _cmul(pr, pi, ar, ai)
        zr, zi = _cmul(c_re, c_im, pr, pi)
        for ri, z in enumerate((zr, -zi)):
            z = jnp.where(own_half, z, 0.0)
            for t in range(GROUPS_PER_TILE // 2):
                c0 = ri * STATE_PER_TILE + t * LANES
                cct_ref[0, j * LANES:(j + 1) * LANES, c0:c0 + LANES] = (
                    jnp.where(pair == t, z, 0.0).astype(BF16))
    a16_ref[0] = pr
    a16_ref[1] = pi
    zero = jnp.zeros((LANES, LANES), BF16)
    for d2 in range(SSM_FOLD // 2):
        top = jnp.concatenate([tk[2 * d2], tk[2 * d2 + 1]], axis=1)
        bot = jnp.concatenate([tk[2 * d2 - 1] if d2 else zero, tk[2 * d2]], axis=1)
        bt_ref[0, d2] = jnp.concatenate([top, bot], axis=0)


def s5_prepare(a_re, a_im, log_dt, b_re, b_im, c_re, c_im):
    g, p, c = SSM_GROUPS, SSM_STATE, SSM_GROUP
    fold, nt = SSM_FOLD, N_LANE_TILES
    rows = g * c
    twice = lambda x: jnp.tile(x, (1, 2))
    lam = lambda x: twice(jnp.repeat(x, c, axis=0))
    in_spec = pl.BlockSpec((LANES, LANES), lambda t: (t, 0))
    w_spec = pl.BlockSpec((fold, LANES, LANES), lambda t: (0, t, 0))
    w_shape = jax.ShapeDtypeStruct((fold, rows, LANES), F32)
    w_re, w_im, bt, cct, a16 = pl.pallas_call(
        _s5_prep_kernel,
        grid=(nt,),
        in_specs=[in_spec, in_spec, pl.BlockSpec((LANES, 1), lambda t: (t, 0))] + [in_spec] * 4,
        out_specs=[w_spec, w_spec,
                   pl.BlockSpec((1, fold // 2, 2 * LANES, 2 * LANES), lambda t: (t, 0, 0, 0)),
                   pl.BlockSpec((1, fold * LANES, 2 * STATE_PER_TILE), lambda t: (t, 0, 0)),
                   pl.BlockSpec((2, LANES, LANES), lambda t: (0, t, 0))],
        out_shape=[w_shape, w_shape,
                   jax.ShapeDtypeStruct((nt, fold // 2, 2 * LANES, 2 * LANES), BF16),
                   jax.ShapeDtypeStruct((nt, fold * LANES, 2 * STATE_PER_TILE), BF16),
                   jax.ShapeDtypeStruct((2, rows, LANES), F32)],
        compiler_params=_cparams("parallel"),
        name="s5_prep",
    )(lam(a_re), lam(a_im), jnp.repeat(log_dt, c).reshape(rows, 1),
      twice(b_re.transpose(0, 2, 1).reshape(rows, p)), twice(b_im.transpose(0, 2, 1).reshape(rows, p)),
      twice(c_re.reshape(rows, p)), twice(c_im.reshape(rows, p)))
    a16 = a16[:, ::c, :p].reshape(2, 1, N_STATE)
    return bt, cct, w_re, w_im, a16


FOLD_BLOCK = SSM_FOLD * SSM_FOLD


def _fold_perm():
    idx = np.arange(FOLD_BLOCK)
    perm = np.zeros((FOLD_BLOCK, FOLD_BLOCK), np.float32)
    perm[(idx % SSM_FOLD) * SSM_FOLD + idx // SSM_FOLD, idx] = 1.0
    return jnp.asarray(perm, BF16)


def _s5_fold_kernel(x_ref, g_ref, perm_ref, u_ref):
    for blk in range(x_ref.shape[0] // FOLD_BLOCK):
        x = x_ref[blk * FOLD_BLOCK:(blk + 1) * FOLD_BLOCK, :]
        h = (x * _rms_scale(x) * g_ref[...]).astype(BF16)
        f = _dot(perm_ref[...], h).astype(BF16)
        for m in range(SSM_FOLD):
            for k in range(N_LANE_TILES):
                u_ref[m, k, blk * SSM_FOLD:(blk + 1) * SSM_FOLD, :] = (
                    f[m * SSM_FOLD:(m + 1) * SSM_FOLD, k * LANES:(k + 1) * LANES])


def s5_fold(x, g, perm, tokens):
    t, d = x.shape
    rows = t // SSM_FOLD
    return pl.pallas_call(
        _s5_fold_kernel,
        grid=(t // tokens,),
        in_specs=[pl.BlockSpec((tokens, d), lambda i: (i, 0)), _const_spec((1, d)),
                  _const_spec((FOLD_BLOCK, FOLD_BLOCK))],
        out_specs=pl.BlockSpec((SSM_FOLD, N_LANE_TILES, tokens // SSM_FOLD, LANES),
                               lambda i: (0, 0, i, 0)),
        out_shape=jax.ShapeDtypeStruct((SSM_FOLD, N_LANE_TILES, rows, LANES), BF16),
        compiler_params=_cparams("parallel"),
        name="s5_fold",
    )(x, g.reshape(1, d), perm)


def _s5_state_in_kernel(u_ref, w_re_ref, w_im_ref, vre_ref, vim_ref, wd_ref):
    @pl.when(pl.program_id(1) == 0)
    def _():
        pair = lax.broadcasted_iota(jnp.int32, (LANES, 1), 0) // SSM_GROUP // 2
        for m in range(SSM_FOLD):
            for ri, ref in enumerate((w_re_ref, w_im_ref)):
                w = ref[m]
                for t in range(GROUPS_PER_TILE // 2):
                    c0 = ri * STATE_PER_TILE + t * LANES
                    wd_ref[m * LANES:(m + 1) * LANES, c0:c0 + LANES] = (
                        jnp.where(pair == t, w, 0.0).astype(BF16))

    lhs = jnp.concatenate([u_ref[m, 0] for m in range(SSM_FOLD)], axis=1)
    v = _dot(lhs, wd_ref[...])
    vre_ref[...] = v[:, :STATE_PER_TILE]
    vim_ref[...] = v[:, STATE_PER_TILE:]


def _s5_tile_specs(rt):
    u_spec = pl.BlockSpec((SSM_FOLD, 1, rt, LANES), lambda t, i: (0, t, i, 0))
    s_spec = pl.BlockSpec((rt, STATE_PER_TILE), lambda t, i: (i, t))
    return u_spec, s_spec


def s5_state_in(u, w_re, w_im, rt):
    rows = u.shape[2]
    u_spec, s_spec = _s5_tile_specs(rt)
    out = jax.ShapeDtypeStruct((rows, N_STATE), F32)
    w_spec = pl.BlockSpec((SSM_FOLD, LANES, LANES), lambda t, i: (0, t, 0))
    return pl.pallas_call(
        _s5_state_in_kernel,
        grid=(N_LANE_TILES, rows // rt),
        in_specs=[u_spec, w_spec, w_spec],
        out_specs=[s_spec, s_spec],
        out_shape=[out, out],
        scratch_shapes=[pltpu.VMEM((SSM_FOLD * LANES, 2 * STATE_PER_TILE), BF16)],
        compiler_params=_cparams("parallel", "arbitrary"),
        name="s5_state_in",
    )(u, w_re, w_im)


def _s5_scan_pairs_kernel(vre_ref, vim_ref, a_ref, s0re_ref, s0im_ref,
                          sre_ref, sim_ref, fre_ref, fim_ref):
    ar, ai = a_ref[0], a_ref[1]
    s0r, s0i = s0re_ref[...], s0im_ref[...]
    vr, vi = vre_ref[...], vim_ref[...]
    first = lax.broadcasted_iota(jnp.int32, (s0r.shape[0], 1), 0) % 2 == 0
    tr, ti = ar * s0r - ai * s0i + vr, ar * s0i + ai * s0r + vi
    sr = jnp.where(first, s0r, pltpu.roll(tr, 1, 0))
    si = jnp.where(first, s0i, pltpu.roll(ti, 1, 0))
    sre_ref[...] = sr
    sim_ref[...] = si
    fre_ref[...] = ar * sr - ai * si + vr
    fim_ref[...] = ar * si + ai * sr + vi


def _s5_scan_stream_kernel(vre_ref, vim_ref, a_ref, s0re_ref, s0im_ref,
                           sre_ref, sim_ref, fre_ref, fim_ref, *, steps):
    q = lax.broadcasted_iota(jnp.int32, (SUBLANES, 1), 0)
    a1 = (a_ref[0], a_ref[1])
    a2 = _cmul(*a1, *a1)
    a4 = _cmul(*a2, *a2)
    a8 = _cmul(*a4, *a4)
    strides = ((1, a1), (2, a2), (4, a4))
    width = a1[0].shape[1]
    pw = (jnp.ones((SUBLANES, width), F32), jnp.zeros((SUBLANES, width), F32))
    masked = []
    for k, ak in strides:
        nxt = _cmul(*pw, *ak)
        bit = (q & k) != 0
        pw = (jnp.where(bit, nxt[0], pw[0]), jnp.where(bit, nxt[1], pw[1]))
        masked.append((k, jnp.where(q >= k, ak[0], 0.0), jnp.where(q >= k, ak[1], 0.0)))

    def shifted(x, k):
        return jnp.where(q >= k, pltpu.roll(x, k, 0), 0.0)

    def body(t, carry):
        cr, ci = carry
        rows = pl.ds(pl.multiple_of(t * SUBLANES, SUBLANES), SUBLANES)
        pr, pi = vre_ref[rows, :], vim_ref[rows, :]
        for k, mr, mi in masked:
            dr, di = _cmul(pltpu.roll(pr, k, 0), pltpu.roll(pi, k, 0), mr, mi)
            pr, pi = pr + dr, pi + di
        xr, xi = _cmul(*pw, cr, ci)
        sre_ref[rows, :] = xr + shifted(pr, 1)
        sim_ref[rows, :] = xi + shifted(pi, 1)
        nr, ni = _cmul(*a8, cr, ci)
        return nr + pr[SUBLANES - 1:], ni + pi[SUBLANES - 1:]

    cr, ci = lax.fori_loop(0, steps // SUBLANES, body, (s0re_ref[0], s0im_ref[0]))
    fre_ref[0] = cr
    fim_ref[0] = ci


def s5_scan(v_re, v_im, a16, s0_re, s0_im, width):
    rows, n = v_re.shape
    streams = s0_re.shape[0]
    steps = rows // streams
    a_spec = pl.BlockSpec((2, 1, width), lambda *i: (0, 0, i[-1]))
    vout = jax.ShapeDtypeStruct((rows, n), F32)
    long_streams = steps % SUBLANES == 0
    if long_streams:
        kern = functools.partial(_s5_scan_stream_kernel, steps=steps)
        grid = (streams, n // width)
        vspec = pl.BlockSpec((steps, width), lambda b, c: (b, c))
        sspec = pl.BlockSpec((1, 1, width), lambda b, c: (b, 0, c))
        s0_re, s0_im = s0_re.reshape(streams, 1, n), s0_im.reshape(streams, 1, n)
        sout = jax.ShapeDtypeStruct((streams, 1, n), F32)
    else:
        assert steps == 2 and rows % SUBLANES == 0
        kern = _s5_scan_pairs_kernel
        grid = (n // width,)
        vspec = pl.BlockSpec((rows, width), lambda c: (0, c))
        sspec = vspec
        s0_re, s0_im = jnp.repeat(s0_re, steps, axis=0), jnp.repeat(s0_im, steps, axis=0)
        sout = vout
    s_re, s_im, f_re, f_im = pl.pallas_call(
        kern,
        grid=grid,
        in_specs=[vspec, vspec, a_spec, sspec, sspec],
        out_specs=[vspec, vspec, sspec, sspec],
        out_shape=[vout, vout, sout, sout],
        compiler_params=_cparams(*(["parallel"] * len(grid))),
        name="s5_scan",
    )(v_re, v_im, a16, s0_re, s0_im)
    if long_streams:
        return s_re, s_im, f_re.reshape(streams, n), f_im.reshape(streams, n)
    return s_re, s_im, f_re[steps - 1::steps], f_im[steps - 1::steps]


def _s5_toeplitz_kernel(u_ref, sre_ref, sim_ref, bt_ref, cct_ref, y_ref):
    npair = SSM_FOLD // 2
    u2 = [jnp.concatenate([u_ref[2 * mm, 0], u_ref[2 * mm + 1, 0]], axis=1) for mm in range(npair)]
    s_prev = jnp.concatenate([sre_ref[...], sim_ref[...]], axis=1).astype(BF16)
    for ll in range(npair):
        acc = _dot_nt(s_prev, cct_ref[0, ll * 2 * LANES:(ll + 1) * 2 * LANES, :])
        for mm in range(ll + 1):
            acc = acc + _dot(u2[mm], bt_ref[0, ll - mm])
        y_ref[2 * ll, 0] = acc[:, :LANES]
        y_ref[2 * ll + 1, 0] = acc[:, LANES:]


def s5_toeplitz(u, s_re, s_im, bt, cct, rt):
    rows = u.shape[2]
    u_spec, s_spec = _s5_tile_specs(rt)
    return pl.pallas_call(
        _s5_toeplitz_kernel,
        grid=(N_LANE_TILES, rows // rt),
        in_specs=[u_spec, s_spec, s_spec,
                  pl.BlockSpec((1, SSM_FOLD // 2, 2 * LANES, 2 * LANES), lambda t, i: (t, 0, 0, 0)),
                  pl.BlockSpec((1, SSM_FOLD * LANES, 2 * STATE_PER_TILE), lambda t, i: (t, 0, 0))],
        out_specs=u_spec,
        out_shape=jax.ShapeDtypeStruct(u.shape, F32),
        compiler_params=_cparams("parallel", "parallel"),
        name="s5_toeplitz",
    )(u, s_re, s_im, bt, cct)


def _s5_glu_kernel(x_ref, u_ref, y_ref, d_ref, wglu_ref, bglu_ref, perm_ref, o_ref):
    nblk = x_ref.shape[0] // FOLD_BLOCK

    def gelu_block(blk):
        rows = slice(blk * SSM_FOLD, (blk + 1) * SSM_FOLD)
        gl = []
        for m in range(SSM_FOLD):
            ym = jnp.concatenate([y_ref[m, k, rows, :] for k in range(N_LANE_TILES)], axis=1)
            hm = jnp.concatenate([u_ref[m, k, rows, :] for k in range(N_LANE_TILES)], axis=1)
            gl.append(jax.nn.gelu(ym + d_ref[...] * hm.astype(F32)))
        gl = jnp.concatenate(gl, axis=0)
        return gl, _dot(gl.astype(BF16), wglu_ref[...])

    def finish(blk, gl, z):
        tok = slice(blk * FOLD_BLOCK, (blk + 1) * FOLD_BLOCK)
        hi, lo = _split_bf16(gl * jax.nn.sigmoid(z + bglu_ref[...]))
        o_ref[tok, :] = x_ref[tok, :] + (_dot(perm_ref[...], hi) + _dot(perm_ref[...], lo))

    prev = gelu_block(0)
    for blk in range(1, nblk):
        cur = gelu_block(blk)
        finish(blk - 1, *prev)
        prev = cur
    finish(nblk - 1, *prev)


def s5_glu(x, u, y, d, w_glu, b_glu, perm, tokens):
    t, dm = x.shape
    fold_spec = pl.BlockSpec((SSM_FOLD, N_LANE_TILES, tokens // SSM_FOLD, LANES),
                             lambda i: (0, 0, i, 0))
    x_spec = pl.BlockSpec((tokens, dm), lambda i: (i, 0))
    return pl.pallas_call(
        _s5_glu_kernel,
        grid=(t // tokens,),
        in_specs=[x_spec, fold_spec, fold_spec, _const_spec((1, dm)), _weight_spec(w_glu),
                  _const_spec((1, dm)), _const_spec((FOLD_BLOCK, FOLD_BLOCK))],
        out_specs=x_spec,
        out_shape=jax.ShapeDtypeStruct((t, dm), F32),
        compiler_params=_cparams("parallel"),
        name="s5_glu",
    )(x, u, y, d.reshape(1, dm), w_glu[0], b_glu.reshape(1, dm), perm)


def s5_mixer(x, st_re, st_im, g, ops, d, w_glu, b_glu):
    bt, cct, w_re, w_im, a16 = ops
    b, l, dm = x.shape
    rows = b * l // SSM_FOLD
    x2 = x.reshape(b * l, dm)
    perm = _fold_perm()
    tokens = min(S5_TOKEN_TILE, b * l)
    rt = min(S5_ROW_TILE, rows)
    u = s5_fold(x2, g, perm, tokens)
    v_re, v_im = s5_state_in(u, w_re, w_im, rt)
    s_re, s_im, f_re, f_im = s5_scan(v_re, v_im, a16, st_re.reshape(b, N_STATE),
                                     st_im.reshape(b, N_STATE), S5_SCAN_WIDTH)
    y = s5_toeplitz(u, s_re, s_im, bt, cct, rt)
    out = s5_glu(x2, u, y, d, w_glu, b_glu, perm, tokens)
    shape = (b, SSM_GROUPS, SSM_STATE)
    return out.reshape(b, l, dm), f_re.reshape(shape), f_im.reshape(shape)


def _stream_tiling(seq_len, n_streams, tile_rows):
    if seq_len >= tile_rows:
        return 1, tile_rows
    return min(n_streams, max(SHORT_STREAM_TILE_ROWS // seq_len, 1)), seq_len


def kernel(x_prompt, x_sample, mem_prompt, state_ssm_re, state_ssm_im, state_ret, cache_mem_k, cache_mem_v, cache_conv, norm_mix, norm_mem_q, norm_mem_kv, norm_ffn, norm_final, ssm_a_re, ssm_a_im, ssm_log_dt, ssm_b_re, ssm_b_im, ssm_c_re, ssm_c_im, ssm_d, ssm_w_glu, ssm_b_glu, ret_w_qkvg, ret_w_o, mem_w_q, mem_w_kv, mem_w_o, ffn_w_up, ffn_conv_w, ffn_conv_b, ffn_w_down):
    bp, _, dm = x_prompt.shape
    depth = norm_mix.shape[0]
    d_ff = ffn_w_down.shape[1]
    streams = [dict(x=x, b=x.shape[0], l=x.shape[1],
                    attn_tiling=_stream_tiling(x.shape[1], x.shape[0], ATTN_TILE_ROWS),
                    ffn_tiling=_stream_tiling(x.shape[1], x.shape[0], FFN_TILE_ROWS))
               for x in (x_prompt, x_sample)]
    cache_k, cache_v = _kv_rows(cache_mem_k), _kv_rows(cache_mem_v)
    ssm_w_glu, ret_w_qkvg, ret_w_o, mem_w_q, mem_w_kv, mem_w_o, ffn_w_up, ffn_w_down = (
        w.astype(BF16) for w in (ssm_w_glu, ret_w_qkvg, ret_w_o, mem_w_q, mem_w_kv, mem_w_o,
                                 ffn_w_up, ffn_w_down))
    mem_k, mem_v = mem_kv(mem_prompt, norm_mem_kv, mem_w_kv)
    outs = dict(ssm_re=([], []), ssm_im=([], []), ret=([], []), conv=([], []))

    for i in range(depth):
        j = i // 2
        if i % 2 == 0:
            ops = s5_prepare(ssm_a_re[j], ssm_a_im[j], ssm_log_dt[j], ssm_b_re[j], ssm_b_im[j],
                             ssm_c_re[j], ssm_c_im[j])
            w_glu = (ssm_w_glu, j)
            zeros = jnp.zeros((bp, SSM_GROUPS, SSM_STATE), F32)
            states = [(zeros, zeros), (state_ssm_re[j], state_ssm_im[j])]
            for si, (s, (st_re, st_im)) in enumerate(zip(streams, states)):
                s["x"], f_re, f_im = s5_mixer(s["x"], st_re, st_im, norm_mix[i], ops, ssm_d[j],
                                              w_glu, ssm_b_glu[j])
                outs["ssm_re"][si].append(f_re)
                outs["ssm_im"][si].append(f_im)
        else:
            w_qkvg = (ret_w_qkvg, j)
            w_o = (ret_w_o, j)
            zeros = jnp.zeros((bp, RET_HEADS, RET_DK, RET_DV), F32)
            chunk = RET_CHUNK_ROWS
            for si, (s, st, pos0) in enumerate(zip(streams, [zeros, state_ret[j]], [0, PAST_LEN])):
                x2 = s["x"].reshape(s["b"] * s["l"], dm)
                cos, sin = _rope_tables(pos0, s["l"])
                if s["l"] >= chunk:
                    y, st_new = ret_mixer(x2, norm_mix[i], w_qkvg, w_o, cos, sin, st, s["l"], chunk)
                else:
                    p = rms_matmul(x2, norm_mix[i], w_qkvg, BF16)
                    y, st_new = ret_core(p, cos, sin, st, s["l"], min(RET_STREAMS_PER_STEP, s["b"]))
                    y = matmul_res(y, w_o, x2)
                s["x"] = y.reshape(s["x"].shape)
                outs["ret"][si].append(st_new)

        w_q = (mem_w_q, i)
        w_o = (mem_w_o, i)
        for s, (k_, v_) in zip(streams, [(mem_k, mem_v), (cache_k, cache_v)]):
            x2 = s["x"].reshape(s["b"] * s["l"], dm)
            y = mem_attn(x2, norm_mem_q[i], w_q, k_, v_, i, w_o, s["l"], *s["attn_tiling"])
            s["x"] = y.reshape(s["x"].shape)

        w_up = (ffn_w_up, i)
        w_dn = (ffn_w_down, i)
        g_final = norm_final if i == depth - 1 else None
        conv_states = [jnp.zeros((bp, 2, d_ff), F32), cache_conv[i]]
        for si, (s, cst) in enumerate(zip(streams, conv_states)):
            x2 = s["x"].reshape(s["b"] * s["l"], dm)
            y, cnew = conv_ffn(x2, norm_ffn[i], w_up, ffn_conv_w[i], ffn_conv_b[i], w_dn, cst,
                               s["l"], *s["ffn_tiling"], g_final=g_final)
            s["x"] = y.reshape(s["x"].shape)
            outs["conv"][si].append(cnew)

    return (streams[0]["x"], streams[1]["x"],
            jnp.stack(outs["ssm_re"][0]), jnp.stack(outs["ssm_im"][0]),
            jnp.stack(outs["ssm_re"][1]), jnp.stack(outs["ssm_im"][1]),
            jnp.stack(outs["ret"][0]), jnp.stack(outs["ret"][1]),
            _kv_unrows(mem_k, MEM_HEADS), _kv_unrows(mem_v, MEM_HEADS),
            jnp.stack(outs["conv"][0]), jnp.stack(outs["conv"][1]))
```
